```python
import math
import jax, jax.numpy as jnp
from jax import lax
import numpy as np

D_MODEL = 1024
BATCH = 2
SEQ = 8192
DEPTH = 1

GRID_W = 64
CTX_LEN = 256

N_HEADS = 8
Q_LORA = 512
KV_LORA = 256
QK_NOPE = 64
QK_ROPE = 32
V_HEAD = 64
ROPE_FREQS = QK_ROPE // 4
ROPE_THETA = 10000.0
ATTN_SCALE = 1.0 / math.sqrt(QK_NOPE + QK_ROPE)
ATTN_WIDTH = N_HEADS * V_HEAD
Q_BLOCK = 128

POOL_GROUPS = 4
POOL_WINDOWS = (2, 4, 8, 16)
POOL_WIDTH = D_MODEL // 2
POOL_GC = POOL_WIDTH // POOL_GROUPS

MIX_WIDTH = ATTN_WIDTH + POOL_WIDTH
IN_WIDTH = Q_LORA + KV_LORA + QK_ROPE + POOL_WIDTH

N_EXPERTS = 64
N_EXPERT_GROUPS = 8
TOPK_GROUPS = 4
TOP_K = 8
D_EXPERT = 256
D_SHARED = 256
ROUTED_SCALE = 2.5
MOE_BLOCK = 128

LN_EPS = 1e-5
RMS_EPS = 1e-6

kernel_name = "hybrid_mla_pool_moe_flow_block"


def layer_norm(x, g, b):
    xf = x.astype(jnp.float32)
    mu = xf.mean(-1, keepdims=True)
    var = jnp.square(xf - mu).mean(-1, keepdims=True)
    y = (xf - mu) * lax.rsqrt(var + LN_EPS)
    return (y * g.astype(jnp.float32) + b.astype(jnp.float32)).astype(x.dtype)


def rms_norm(x, g):
    xf = x.astype(jnp.float32)
    y = xf * lax.rsqrt(jnp.square(xf).mean(-1, keepdims=True) + RMS_EPS)
    return (y * g.astype(jnp.float32)).astype(x.dtype)


def axial_rope_tables(rows, dtype):
    t = jnp.arange(rows * GRID_W)
    pos = jnp.stack([t // GRID_W, t % GRID_W], axis=-1).astype(jnp.float32)
    inv_freq = ROPE_THETA ** (-jnp.arange(ROPE_FREQS, dtype=jnp.float32) / ROPE_FREQS)
    ang = pos[:, :, None] * inv_freq
    return (jnp.cos(ang)[:, :, None, :].astype(dtype),
            jnp.sin(ang)[:, :, None, :].astype(dtype))


def apply_axial_rope(x, cos, sin):
    xs = x.reshape(x.shape[:-1] + (2, 2, ROPE_FREQS))
    rot = jnp.concatenate([-xs[..., 1:, :], xs[..., :1, :]], axis=-2)
    return (xs * cos + rot * sin).reshape(x.shape)


def mla_keys(p, kv_norm_g, w_ukv, cos, sin):
    b_, n = p.shape[:2]
    kv_lat = p[..., Q_LORA:Q_LORA + KV_LORA]
    k_rope = p[..., Q_LORA + KV_LORA:Q_LORA + KV_LORA + QK_ROPE]
    kv = (rms_norm(kv_lat, kv_norm_g) @ w_ukv).reshape(b_, n, N_HEADS, QK_NOPE + V_HEAD)
    k_nope, v = kv[..., :QK_NOPE], kv[..., QK_NOPE:]
    if cos is not None:
        k_rope = apply_axial_rope(k_rope, cos, sin)
    return k_nope, k_rope, v


def mla_queries(p, q_norm_g, w_uq, cos, sin):
    b_, n = p.shape[:2]
    q = (rms_norm(p[..., :Q_LORA], q_norm_g) @ w_uq).reshape(b_, n, N_HEADS, QK_NOPE + QK_ROPE)
    q_nope, q_rope = q[..., :QK_NOPE], q[..., QK_NOPE:]
    if cos is not None:
        q_rope = apply_axial_rope(q_rope, cos[:, None], sin[:, None])
    return q_nope, q_rope


def attend(q_nope, q_rope, k_nope, k_rope, v):
    s = (jnp.einsum('bqhn,bkhn->bhqk', q_nope, k_nope)
         + jnp.einsum('bqhr,bkr->bhqk', q_rope, k_rope)).astype(jnp.float32) * ATTN_SCALE
    prob = jax.nn.softmax(s, axis=-1).astype(v.dtype)
    return jnp.einsum('bhqk,bkhv->bqhv', prob, v)


def latent_attention(q_nope, q_rope, k_nope, k_rope, v):
    b_, s = q_nope.shape[:2]
    nb = s // Q_BLOCK

    def to_blocks(a):
        return a.reshape((b_, nb, Q_BLOCK) + a.shape[2:]).swapaxes(0, 1)

    out = lax.map(lambda qs: attend(qs[0], qs[1], k_nope, k_rope, v),
                  (to_blocks(q_nope), to_blocks(q_rope)))
    return out.swapaxes(0, 1).reshape(b_, s, ATTN_WIDTH)


def multiscale_pool(u, w_pool, pool_scale):
    b_, n = u.shape[:2]
    ug = u.reshape(b_, n, POOL_GROUPS, POOL_GC)
    cs = jnp.concatenate([jnp.zeros((b_, 1, POOL_GROUPS, POOL_GC), jnp.float32),
                          jnp.cumsum(ug.astype(jnp.float32), axis=1)], axis=1)
    t = jnp.arange(n)
    means = []
    for g, w in enumerate(POOL_WINDOWS):
        lo = jnp.clip(t - w // 2, 0, n)
        hi = jnp.clip(t - w // 2 + w, 0, n)
        csg = cs[:, :, g]
        means.append((csg[:, hi] - csg[:, lo]) / (hi - lo).astype(jnp.float32)[:, None])
    pooled = jnp.stack(means, axis=2).astype(u.dtype) - ug
    out = jnp.einsum('bngc,gcd->bngd', pooled, w_pool).reshape(b_, n, POOL_WIDTH)
    return out * pool_scale


def moe_ffn(h, w_router, router_bias, w_e_gate, w_e_up, w_e_down, w_s_gate, w_s_up, w_s_down):
    lead = h.shape[:-1]
    hf = h.reshape(-1, D_MODEL)
    n = hf.shape[0]
    scores = jax.nn.sigmoid((hf @ w_router).astype(jnp.float32))
    biased = scores + router_bias.astype(jnp.float32)
    grp = biased.reshape(n, N_EXPERT_GROUPS, N_EXPERTS // N_EXPERT_GROUPS)
    grp_score = lax.top_k(grp, 2)[0].sum(-1)
    _, top_groups = lax.top_k(grp_score, TOPK_GROUPS)
    group_mask = jax.nn.one_hot(top_groups, N_EXPERT_GROUPS, dtype=jnp.float32).sum(-2)
    expert_mask = jnp.repeat(group_mask, N_EXPERTS // N_EXPERT_GROUPS, axis=-1)
    masked = jnp.where(expert_mask > 0, biased, -jnp.inf)
    _, idx = lax.top_k(masked, TOP_K)
    wsel = jnp.take_along_axis(scores, idx, axis=-1)
    wsel = wsel / wsel.sum(-1, keepdims=True) * ROUTED_SCALE
    gates = (jax.nn.one_hot(idx, N_EXPERTS, dtype=jnp.float32) * wsel[..., None]).sum(-2).astype(h.dtype)

    def expert_block(args):
        hb, gb = args
        a = jax.nn.silu(jnp.einsum('td,edf->tef', hb, w_e_gate)) * jnp.einsum('td,edf->tef', hb, w_e_up)
        return jnp.einsum('tef,efd->td', a * gb[:, :, None], w_e_down)

    nb = n // MOE_BLOCK
    routed = lax.map(expert_block, (hf.reshape(nb, MOE_BLOCK, D_MODEL),
                                    gates.reshape(nb, MOE_BLOCK, N_EXPERTS))).reshape(n, D_MODEL)
    shared = (jax.nn.silu(hf @ w_s_gate) * (hf @ w_s_up)) @ w_s_down
    return (routed + shared).reshape(lead + (D_MODEL,))


def setup_inputs(seed: int = 0) -> dict:
    key = jax.random.key(seed)
    ks = jax.random.split(key, 32)
    L = DEPTH
    beta = (8.0 * DEPTH) ** -0.25

    def nrm(k, shape, scale):
        return jax.random.normal(k, shape, jnp.float32) * scale

    return {
        "x": nrm(ks[0], (BATCH, SEQ, D_MODEL), 1.0),
        "c": nrm(ks[1], (BATCH, D_MODEL), 1.0),
        "ctx": nrm(ks[2], (BATCH, CTX_LEN, D_MODEL), 1.0),
        "c_ctx": nrm(ks[3], (D_MODEL,), 1.0),
        "w_ada": nrm(ks[4], (L, D_MODEL, 6 * D_MODEL), 0.5 * D_MODEL ** -0.5),
        "b_ada": nrm(ks[5], (L, 6 * D_MODEL), 0.02),
        "w_in": nrm(ks[6], (L, D_MODEL, IN_WIDTH), D_MODEL ** -0.5),
        "q_norm_g": 1.0 + nrm(ks[7], (L, Q_LORA), 0.02),
        "w_uq": nrm(ks[8], (L, Q_LORA, N_HEADS * (QK_NOPE + QK_ROPE)), Q_LORA ** -0.5),
        "kv_norm_g": 1.0 + nrm(ks[9], (L, KV_LORA), 0.02),
        "w_ukv": nrm(ks[10], (L, KV_LORA, N_HEADS * (QK_NOPE + V_HEAD)), KV_LORA ** -0.5),
        "w_pool": nrm(ks[11], (L, POOL_GROUPS, POOL_GC, POOL_GC), POOL_GC ** -0.5),
        "pool_scale": 1.0 + nrm(ks[12], (L, POOL_WIDTH), 0.02),
        "w_out": nrm(ks[13], (L, MIX_WIDTH, D_MODEL), beta * MIX_WIDTH ** -0.5),
        "ln1_g": 1.0 + nrm(ks[14], (L, D_MODEL), 0.02),
        "ln1_b": nrm(ks[15], (L, D_MODEL), 0.02),
        "w_router": nrm(ks[16], (L, D_MODEL, N_EXPERTS), D_MODEL ** -0.5),
        "router_bias": nrm(ks[17], (L, N_EXPERTS), 0.01),
        "w_e_gate": nrm(ks[18], (L, N_EXPERTS, D_MODEL, D_EXPERT), D_MODEL ** -0.5),
        "w_e_up": nrm(ks[19], (L, N_EXPERTS, D_MODEL, D_EXPERT), D_MODEL ** -0.5),
        "w_e_down": nrm(ks[20], (L, N_EXPERTS, D_EXPERT, D_MODEL), beta * D_EXPERT ** -0.5),
        "w_s_gate": nrm(ks[21], (L, D_MODEL, D_SHARED), D_MODEL ** -0.5),
        "w_s_up": nrm(ks[22], (L, D_MODEL, D_SHARED), D_MODEL ** -0.5),
        "w_s_down": nrm(ks[23], (L, D_SHARED, D_MODEL), beta * D_SHARED ** -0.5),
        "ln2_g": 1.0 + nrm(ks[24], (L, D_MODEL), 0.02),
        "ln2_b": nrm(ks[25], (L, D_MODEL), 0.02),
    }


def reference(x, c, ctx, c_ctx, w_ada, b_ada, w_in, q_norm_g, w_uq, kv_norm_g, w_ukv,
              w_pool, pool_scale, w_out, ln1_g, ln1_b, w_router, router_bias,
              w_e_gate, w_e_up, w_e_down, w_s_gate, w_s_up, w_s_down, ln2_g, ln2_b):
    b_, s = x.shape[:2]
    rows = s // GRID_W
    cos, sin = axial_rope_tables(rows, x.dtype)
    alpha = (2.0 * DEPTH) ** 0.25
    silu_c = jax.nn.silu(c)
    silu_cc = jax.nn.silu(c_ctx)

    for l in range(DEPTH):
        last = l == DEPTH - 1
        moe_args = (w_router[l], router_bias[l], w_e_gate[l], w_e_up[l], w_e_down[l],
                    w_s_gate[l], w_s_up[l], w_s_down[l])
        sh1, sc1, g1, sh2, sc2, g2 = jnp.split(silu_c @ w_ada[l] + b_ada[l], 6, axis=-1)
        sh1c, sc1c, g1c, sh2c, sc2c, g2c = jnp.split(silu_cc @ w_ada[l] + b_ada[l], 6, axis=-1)

        h = x * (1 + sc1[:, None]) + sh1[:, None]
        hc = ctx * (1 + sc1c) + sh1c
        p = h @ w_in[l]
        pc = hc @ w_in[l]

        q_nope, q_rope = mla_queries(p, q_norm_g[l], w_uq[l], cos, sin)
        k_nope, k_rope, v = mla_keys(p, kv_norm_g[l], w_ukv[l], cos, sin)
        kc_nope, kc_rope, vc = mla_keys(pc, kv_norm_g[l], w_ukv[l], None, None)
        attn = latent_attention(q_nope, q_rope,
                                jnp.concatenate([kc_nope, k_nope], axis=1),
                                jnp.concatenate([kc_rope, k_rope], axis=1),
                                jnp.concatenate([vc, v], axis=1))
        pooled = multiscale_pool(p[..., IN_WIDTH - POOL_WIDTH:], w_pool[l], pool_scale[l])

        y = jnp.concatenate([attn, pooled], axis=-1) @ w_out[l]
        x1 = layer_norm(alpha * x + g1[:, None] * y, ln1_g[l], ln1_b[l])
        h2 = x1 * (1 + sc2[:, None]) + sh2[:, None]
        x = layer_norm(alpha * x1 + g2[:, None] * moe_ffn(h2, *moe_args), ln2_g[l], ln2_b[l])

        if not last:
            qc_nope, qc_rope = mla_queries(pc, q_norm_g[l], w_uq[l], None, None)
            attn_c = attend(qc_nope, qc_rope, kc_nope, kc_rope, vc).reshape(b_, CTX_LEN, ATTN_WIDTH)
            pooled_c = multiscale_pool(pc[..., IN_WIDTH - POOL_WIDTH:], w_pool[l], pool_scale[l])
            yc = jnp.concatenate([attn_c, pooled_c], axis=-1) @ w_out[l]
            ctx1 = layer_norm(alpha * ctx + g1c * yc, ln1_g[l], ln1_b[l])
            hc2 = ctx1 * (1 + sc2c) + sh2c
            ctx = layer_norm(alpha * ctx1 + g2c * moe_ffn(hc2, *moe_args), ln2_g[l], ln2_b[l])

    return x
```

```python
import functools
import math

import jax
import jax.numpy as jnp
from jax import lax
from jax.experimental import pallas as pl
from jax.experimental.pallas import tpu as pltpu

F32 = jnp.float32
BF16 = jnp.bfloat16

D_MODEL = 1024
GRID_W = 64
N_HEADS = 8
Q_LORA = 512
KV_LORA = 256
QK_NOPE = 64
QK_ROPE = 32
V_HEAD = 64
ROPE_FREQS = QK_ROPE // 4
ROPE_THETA = 10000.0
ATTN_SCALE = 1.0 / math.sqrt(QK_NOPE + QK_ROPE)
POOL_GROUPS = 4
POOL_WINDOWS = (2, 4, 8, 16)
POOL_WIDTH = 512
POOL_GC = POOL_WIDTH // POOL_GROUPS
POOL_HALO = 8
N_EXPERTS = 64
N_EXPERT_GROUPS = 8
GROUP_SIZE = N_EXPERTS // N_EXPERT_GROUPS
TOPK_GROUPS = 4
TOP_K = 8
D_EXPERT = 256
ROUTED_SCALE = 2.5
LN_EPS = 1e-5
RMS_EPS = 1e-6
ALPHA = 2.0 ** 0.25

LANES = 128
HEAD_PAD = LANES
ONES_LANE = V_HEAD
IN_PAD = Q_LORA + KV_LORA + POOL_WIDTH + LANES

PROJ_TILE = 512
ATTN_TQ = 512
ATTN_TK = 512
MIX_TILE = 512
MOE_TILE = 1024
MOE_EC = 2
VMEM_LIMIT = 48 * 1024 * 1024
NEG_BIG = -1e30


def _silu(v):
    return v * jax.nn.sigmoid(v)


def _layer_norm(z, g, b):
    mu = jnp.mean(z, axis=-1, keepdims=True)
    zc = z - mu
    var = jnp.mean(zc * zc, axis=-1, keepdims=True)
    return zc * lax.rsqrt(var + LN_EPS) * g + b


def _rms_norm(v, g):
    return v * lax.rsqrt(jnp.mean(v * v, axis=-1, keepdims=True) + RMS_EPS) * g


def _ada_kernel(c_ref, w_ref, b_ref, o_ref):
    cv = _silu(c_ref[...])
    o_ref[...] = jnp.dot(cv, w_ref[...], preferred_element_type=F32,
                         precision=lax.Precision.HIGHEST) + b_ref[...]


def _ada(cvec, w_ada, b_ada):
    rows, d = cvec.shape
    n = w_ada.shape[1]
    tn = 1024
    return pl.pallas_call(
        _ada_kernel,
        grid=(n // tn,),
        in_specs=[pl.BlockSpec((rows, d), lambda j: (0, 0)),
                  pl.BlockSpec((d, tn), lambda j: (0, j)),
                  pl.BlockSpec((1, tn), lambda j: (0, j))],
        out_specs=pl.BlockSpec((rows, tn), lambda j: (0, j)),
        out_shape=jax.ShapeDtypeStruct((rows, n), F32),
        compiler_params=pltpu.CompilerParams(dimension_semantics=("arbitrary",),
                                             vmem_limit_bytes=VMEM_LIMIT),
        name="ada",
    )(cvec, w_ada, b_ada)


def _rope(v, cos, sin_lo, sin_hi):
    return v * cos + pltpu.roll(v, LANES - 8, axis=1) * sin_lo + pltpu.roll(v, 8, axis=1) * sin_hi


def _proj_kernel(*refs, with_q):
    if with_q:
        (x_ref, sc_ref, sh_ref, cos_ref, slo_ref, shi_ref, win_ref, qg_ref, wuq_ref, kvg_ref,
         wuk_ref, wuv_ref, q_ref, k_ref, v_ref, u_ref) = refs
    else:
        (x_ref, sc_ref, sh_ref, win_ref, kvg_ref, wuk_ref, wuv_ref, k_ref, v_ref) = refs
    h = (x_ref[0] * (1.0 + sc_ref[0]) + sh_ref[0]).astype(BF16)
    p = jnp.dot(h, win_ref[...], preferred_element_type=F32)
    kv_lat = p[:, Q_LORA:Q_LORA + KV_LORA]
    kr = p[:, IN_PAD - LANES:]
    kvn = _rms_norm(kv_lat, kvg_ref[...]).astype(BF16)
    kfull = jnp.dot(kvn, wuk_ref[...], preferred_element_type=F32)
    vfull = jnp.dot(kvn, wuv_ref[...], preferred_element_type=F32)
    ones_lane = (lax.broadcasted_iota(jnp.int32, (1, LANES), 1) == ONES_LANE).astype(F32)
    if with_q:
        cos, slo, shi = cos_ref[...], slo_ref[...], shi_ref[...]
        kr = _rope(kr, cos, slo, shi)
        u_ref[0] = p[:, Q_LORA + KV_LORA:Q_LORA + KV_LORA + POOL_WIDTH]
        qn = _rms_norm(p[:, :Q_LORA], qg_ref[...]).astype(BF16)
        qfull = jnp.dot(qn, wuq_ref[...], preferred_element_type=F32)
    for hd in range(N_HEADS):
        sl = slice(hd * HEAD_PAD, (hd + 1) * HEAD_PAD)
        k_ref[0, hd] = (kfull[:, sl] + kr).astype(BF16)
        v_ref[0, hd] = (vfull[:, sl] + ones_lane).astype(BF16)
        if with_q:
            q_ref[0, hd] = (_rope(qfull[:, sl], cos, slo, shi) * ATTN_SCALE).astype(BF16)


def _proj(x, sc, sh, tables, w_in_r, q_g, w_uq_p, kv_g, w_uk_p, w_uv_p, tile):
    b, s, d = x.shape
    with_q = tables is not None
    grid = (b, s // tile)
    row = lambda bi, i: (bi, i, 0)
    vec = lambda bi, i: (bi, 0, 0)
    const2 = lambda bi, i: (0, 0)
    head_out = pl.BlockSpec((1, N_HEADS, tile, HEAD_PAD), lambda bi, i: (bi, 0, i, 0))
    head_shape = jax.ShapeDtypeStruct((b, N_HEADS, s, HEAD_PAD), BF16)
    in_specs = [pl.BlockSpec((1, tile, d), row),
                pl.BlockSpec((1, 1, d), vec), pl.BlockSpec((1, 1, d), vec)]
    args = [x, sc, sh]
    if with_q:
        in_specs += [pl.BlockSpec((tile, LANES), lambda bi, i: (i, 0))] * 3
        args += list(tables)
    in_specs.append(pl.BlockSpec(w_in_r.shape, const2)); args.append(w_in_r)
    if with_q:
        in_specs += [pl.BlockSpec(q_g.shape, const2), pl.BlockSpec(w_uq_p.shape, const2)]
        args += [q_g, w_uq_p]
    in_specs += [pl.BlockSpec(kv_g.shape, const2), pl.BlockSpec(w_uk_p.shape, const2),
                 pl.BlockSpec(w_uv_p.shape, const2)]
    args += [kv_g, w_uk_p, w_uv_p]
    if with_q:
        out_specs = [head_out, head_out, head_out, pl.BlockSpec((1, tile, POOL_WIDTH), row)]
        out_shape = [head_shape, head_shape, head_shape,
                     jax.ShapeDtypeStruct((b, s, POOL_WIDTH), F32)]
    else:
        out_specs = [head_out, head_out]
        out_shape = [head_shape, head_shape]
    return pl.pallas_call(
        functools.partial(_proj_kernel, with_q=with_q),
        grid=grid, in_specs=in_specs, out_specs=out_specs, out_shape=out_shape,
        compiler_params=pltpu.CompilerParams(dimension_semantics=("arbitrary", "arbitrary"),
                                             vmem_limit_bytes=VMEM_LIMIT),
        name="proj" if with_q else "proj_ctx",
    )(*args)


def _attn_kernel(q_ref, kc_ref, vc_ref, k_ref, v_ref, o_ref, *, n_kblk, tk):
    tq = q_ref.shape[2]
    outs = []
    for hh in range(2):
        q = q_ref[0, hh]

        def block(kb, vb, m, acc):
            s = lax.dot_general(q, kb, (((1,), (1,)), ((), ())), preferred_element_type=F32)
            m_new = jnp.maximum(m, jnp.max(s, axis=1, keepdims=True))
            p = jnp.exp(s - m_new)
            acc = jnp.exp(m - m_new) * acc + jnp.dot(p.astype(BF16), vb, preferred_element_type=F32)
            return m_new, acc

        m0 = jnp.full((tq, 1), NEG_BIG, F32)
        acc0 = jnp.zeros((tq, HEAD_PAD), F32)
        carry = block(kc_ref[0, hh], vc_ref[0, hh], m0, acc0)

        def body(i, carry):
            off = pl.multiple_of(i * tk, tk)
            return block(k_ref[0, hh, pl.ds(off, tk), :], v_ref[0, hh, pl.ds(off, tk), :], *carry)

        _, acc = lax.fori_loop(0, n_kblk, body, carry)
        outs.append(acc[:, :V_HEAD] / acc[:, ONES_LANE:ONES_LANE + 1])
    o_ref[0] = jnp.concatenate(outs, axis=1).astype(o_ref.dtype)


def _attention(q, kc, vc, k, v):
    b, nh, s, dp = q.shape
    c = kc.shape[2]
    tq, tk = ATTN_TQ, ATTN_TK
    kern = functools.partial(_attn_kernel, n_kblk=s // tk, tk=tk)
    return pl.pallas_call(
        kern,
        grid=(b, nh // 2, s // tq),
        in_specs=[pl.BlockSpec((1, 2, tq, dp), lambda bi, hp, qi: (bi, hp, qi, 0)),
                  pl.BlockSpec((1, 2, c, dp), lambda bi, hp, qi: (bi, hp, 0, 0)),
                  pl.BlockSpec((1, 2, c, dp), lambda bi, hp, qi: (bi, hp, 0, 0)),
                  pl.BlockSpec((1, 2, s, dp), lambda bi, hp, qi: (bi, hp, 0, 0)),
                  pl.BlockSpec((1, 2, s, dp), lambda bi, hp, qi: (bi, hp, 0, 0))],
        out_specs=pl.BlockSpec((1, tq, 2 * V_HEAD), lambda bi, hp, qi: (bi, qi, hp)),
        out_shape=jax.ShapeDtypeStruct((b, s, nh * V_HEAD), BF16),
        compiler_params=pltpu.CompilerParams(
            dimension_semantics=("arbitrary", "arbitrary", "arbitrary"),
            vmem_limit_bytes=VMEM_LIMIT),
        name="attn",
    )(q, kc, vc, k, v)


def _route(logits_t, bias_t):
    e, t = logits_t.shape
    scores = jax.nn.sigmoid(logits_t)
    biased = scores + bias_t
    neg_inf = F32(-jnp.inf)
    gscore = []
    for g in range(N_EXPERT_GROUPS):
        v = biased[g * GROUP_SIZE:(g + 1) * GROUP_SIZE]
        m1 = jnp.max(v, axis=0, keepdims=True)
        at_max = v == m1
        n_max = jnp.sum(at_max.astype(F32), axis=0, keepdims=True)
        m2 = jnp.max(jnp.where(at_max, neg_inf, v), axis=0, keepdims=True)
        gscore.append(m1 + jnp.where(n_max >= 2.0, m1, m2))
    masked = []
    for g in range(N_EXPERT_GROUPS):
        rank = jnp.zeros((1, t), F32)
        for o in range(N_EXPERT_GROUPS):
            if o == g:
                continue
            beats = (gscore[o] >= gscore[g]) if o < g else (gscore[o] > gscore[g])
            rank = rank + beats.astype(F32)
        keep = rank < float(TOPK_GROUPS)
        masked.append(jnp.where(keep, biased[g * GROUP_SIZE:(g + 1) * GROUP_SIZE], neg_inf))
    work = jnp.concatenate(masked, axis=0)
    rows = lax.broadcasted_iota(jnp.int32, (e, t), 0)
    sel = jnp.zeros((e, t), F32)
    for _ in range(TOP_K):
        m = jnp.max(work, axis=0, keepdims=True)
        first = jnp.min(jnp.where(work == m, rows, e), axis=0, keepdims=True)
        pick = rows == first
        sel = jnp.where(pick, 1.0, sel)
        work = jnp.where(pick, neg_inf, work)
    w = sel * scores
    return w / jnp.sum(w, axis=0, keepdims=True) * ROUTED_SCALE


def _mix_kernel(attn_ref, u_ref, up_ref, un_ref, x_ref, g1_ref, sc2_ref, sh2_ref, wpool_ref,
                pscale_ref, wout_ref, ln_g_ref, ln_b_ref, wr_ref, rb_ref,
                x1_ref, h2_ref, gates_ref, uext_ref, *, seq):
    i = pl.program_id(1)
    tile = u_ref.shape[1]
    u = u_ref[0]
    uext_ref[0:POOL_HALO] = jnp.where(i == 0, 0.0, up_ref[0])
    uext_ref[POOL_HALO:POOL_HALO + tile] = u
    uext_ref[POOL_HALO + tile:] = jnp.where(i == pl.num_programs(1) - 1, 0.0, un_ref[0])
    t = i * tile + lax.broadcasted_iota(jnp.int32, (tile, POOL_GC), 0)
    pooled = []
    for g, w in enumerate(POOL_WINDOWS):
        lanes = slice(g * POOL_GC, (g + 1) * POOL_GC)
        tot = uext_ref[POOL_HALO - w // 2:POOL_HALO - w // 2 + tile, lanes]
        for dlt in range(1, w):
            start = POOL_HALO - w // 2 + dlt
            tot = tot + uext_ref[start:start + tile, lanes]
        cnt = (jnp.minimum(t - w // 2 + w, seq) - jnp.maximum(t - w // 2, 0)).astype(F32)
        pg = (tot / cnt - u[:, lanes]).astype(BF16)
        po = jnp.dot(pg, wpool_ref[g], preferred_element_type=F32) * pscale_ref[:, lanes]
        pooled.append(po.astype(BF16))
    mixed = jnp.concatenate([attn_ref[0]] + pooled, axis=1)
    y = jnp.dot(mixed, wout_ref[...], preferred_element_type=F32)
    x1 = _layer_norm(ALPHA * x_ref[0] + g1_ref[0] * y, ln_g_ref[...], ln_b_ref[...])
    x1_ref[0] = x1
    h2 = x1 * (1.0 + sc2_ref[0]) + sh2_ref[0]
    h2_ref[0] = h2.astype(BF16)
    logits_t = lax.dot_general(wr_ref[...], h2, (((1,), (1,)), ((), ())),
                               preferred_element_type=F32, precision=lax.Precision.HIGHEST)
    gates_t = _route(logits_t, rb_ref[...])
    gates_t = jnp.concatenate([gates_t, jnp.zeros((LANES - N_EXPERTS, tile), F32)], axis=0)
    gates_ref[0] = gates_t.T


def _mix(attn, u, x, g1, sc2, sh2, w_pool, pool_scale, w_out, ln_g, ln_b, w_r_t, rb_t):
    b, s, d = x.shape
    tile = MIX_TILE
    hb = tile // POOL_HALO
    row = lambda bi, i: (bi, i, 0)
    vec = lambda bi, i: (bi, 0, 0)
    c2 = lambda bi, i: (0, 0)
    return pl.pallas_call(
        functools.partial(_mix_kernel, seq=s),
        grid=(b, s // tile),
        in_specs=[pl.BlockSpec((1, tile, POOL_WIDTH), row),
                  pl.BlockSpec((1, tile, POOL_WIDTH), row),
                  pl.BlockSpec((1, POOL_HALO, POOL_WIDTH),
                               lambda bi, i: (bi, jnp.maximum(i * hb - 1, 0), 0)),
                  pl.BlockSpec((1, POOL_HALO, POOL_WIDTH),
                               lambda bi, i: (bi, jnp.minimum((i + 1) * hb, s // POOL_HALO - 1), 0)),
                  pl.BlockSpec((1, tile, d), row),
                  pl.BlockSpec((1, 1, d), vec), pl.BlockSpec((1, 1, d), vec),
                  pl.BlockSpec((1, 1, d), vec),
                  pl.BlockSpec(w_pool.shape, lambda bi, i: (0, 0, 0)),
                  pl.BlockSpec(pool_scale.shape, c2),
                  pl.BlockSpec(w_out.shape, c2),
                  pl.BlockSpec(ln_g.shape, c2), pl.BlockSpec(ln_b.shape, c2),
                  pl.BlockSpec(w_r_t.shape, c2), pl.BlockSpec(rb_t.shape, c2)],
        out_specs=[pl.BlockSpec((1, tile, d), row),
                   pl.BlockSpec((1, tile, d), row),
                   pl.BlockSpec((1, tile, LANES), row)],
        out_shape=[jax.ShapeDtypeStruct((b, s, d), F32),
                   jax.ShapeDtypeStruct((b, s, d), BF16),
                   jax.ShapeDtypeStruct((b, s, LANES), F32)],
        scratch_shapes=[pltpu.VMEM((tile + 2 * POOL_HALO, POOL_WIDTH), F32)],
        compiler_params=pltpu.CompilerParams(dimension_semantics=("arbitrary", "arbitrary"),
                                             vmem_limit_bytes=VMEM_LIMIT),
        name="mix",
    )(attn, u, u, u, x, g1, sc2, sh2, w_pool, pool_scale, w_out, ln_g, ln_b, w_r_t, rb_t)


def _moe_kernel(h2_ref, gates_ref, x1_ref, g2_ref, wg_ref, wu_ref, wd_ref, wsg_ref, wsu_ref,
                wsd_ref, ln_g_ref, ln_b_ref, o_ref, acc_ref):
    c = pl.program_id(2)
    h = h2_ref[0]

    @pl.when(c == 0)
    def _():
        a = _silu(jnp.dot(h, wsg_ref[...], preferred_element_type=F32)) * jnp.dot(
            h, wsu_ref[...], preferred_element_type=F32)
        acc_ref[...] = jnp.dot(a.astype(BF16), wsd_ref[...], preferred_element_type=F32)

    gates = pltpu.roll(gates_ref[0], (LANES - c * MOE_EC) % LANES, axis=1)
    for j in range(MOE_EC):
        a = _silu(jnp.dot(h, wg_ref[j].astype(BF16), preferred_element_type=F32)) * jnp.dot(
            h, wu_ref[j].astype(BF16), preferred_element_type=F32)
        a = a * gates[:, j:j + 1]
        acc_ref[...] += jnp.dot(a.astype(BF16), wd_ref[j].astype(BF16), preferred_element_type=F32)

    @pl.when(c == pl.num_programs(2) - 1)
    def _():
        z = ALPHA * x1_ref[0] + g2_ref[0] * acc_ref[...]
        o_ref[0] = _layer_norm(z, ln_g_ref[...], ln_b_ref[...])


def _moe(h2, gates, x1, g2, w_e_gate, w_e_up, w_e_down, w_s_gate, w_s_up, w_s_down, ln_g, ln_b):
    b, s, d = x1.shape
    tile = MOE_TILE
    n_e, _, f = w_e_gate.shape
    row = lambda bi, i, c: (bi, i, 0)
    vec = lambda bi, i, c: (bi, 0, 0)
    c2 = lambda bi, i, c: (0, 0)
    ew = lambda bi, i, c: (c, 0, 0)
    return pl.pallas_call(
        _moe_kernel,
        grid=(b, s // tile, n_e // MOE_EC),
        in_specs=[pl.BlockSpec((1, tile, d), row),
                  pl.BlockSpec((1, tile, LANES), row),
                  pl.BlockSpec((1, tile, d), row),
                  pl.BlockSpec((1, 1, d), vec),
                  pl.BlockSpec((MOE_EC, d, f), ew),
                  pl.BlockSpec((MOE_EC, d, f), ew),
                  pl.BlockSpec((MOE_EC, f, d), ew),
                  pl.BlockSpec(w_s_gate.shape, c2), pl.BlockSpec(w_s_up.shape, c2),
                  pl.BlockSpec(w_s_down.shape, c2),
                  pl.BlockSpec(ln_g.shape, c2), pl.BlockSpec(ln_b.shape, c2)],
        out_specs=pl.BlockSpec((1, tile, d), row),
        out_shape=jax.ShapeDtypeStruct((b, s, d), F32),
        scratch_shapes=[pltpu.VMEM((tile, d), F32)],
        compiler_params=pltpu.CompilerParams(
            dimension_semantics=("arbitrary", "arbitrary", "arbitrary"),
            vmem_limit_bytes=VMEM_LIMIT),
        name="moe",
    )(h2, gates, x1, g2, w_e_gate, w_e_up, w_e_down, w_s_gate, w_s_up, w_s_down, ln_g, ln_b)


def _rope_tables(seq):
    t = jnp.arange(seq)
    pos = jnp.stack([t // GRID_W, t % GRID_W], axis=-1).astype(F32)
    inv_freq = ROPE_THETA ** (-jnp.arange(ROPE_FREQS, dtype=F32) / ROPE_FREQS)
    ang = pos[:, :, None] * inv_freq
    cos, sin = jnp.cos(ang), jnp.sin(ang)
    zero = jnp.zeros_like(sin)
    cos_r = jnp.stack([cos, cos], axis=2).reshape(seq, QK_ROPE)
    sin_lo = jnp.stack([-sin, zero], axis=2).reshape(seq, QK_ROPE)
    sin_hi = jnp.stack([zero, sin], axis=2).reshape(seq, QK_ROPE)
    pad_l, pad_r = QK_NOPE, HEAD_PAD - QK_NOPE - QK_ROPE
    cos_t = jnp.pad(cos_r, ((0, 0), (pad_l, pad_r)), constant_values=1.0)
    return (cos_t, jnp.pad(sin_lo, ((0, 0), (pad_l, pad_r))), jnp.pad(sin_hi, ((0, 0), (pad_l, pad_r))))


def _pad_heads(w, width, start):
    k = w.shape[0]
    w = w.reshape(k, N_HEADS, width)
    w = jnp.pad(w, ((0, 0), (0, 0), (start, HEAD_PAD - width - start)))
    return w.reshape(k, N_HEADS * HEAD_PAD)


def kernel(x, c, ctx, c_ctx, w_ada, b_ada, w_in, q_norm_g, w_uq, kv_norm_g, w_ukv, w_pool, pool_scale, w_out, ln1_g, ln1_b, w_router, router_bias, w_e_gate, w_e_up, w_e_down, w_s_gate, w_s_up, w_s_down, ln2_g, ln2_b):
    assert w_ada.shape[0] == 1, "single-layer block"
    b, s, d = x.shape

    cvec = jnp.concatenate([c, c_ctx[None], jnp.zeros((8 - b - 1, d), F32)], axis=0)
    mod = _ada(cvec, w_ada[0], b_ada)
    sh1, sc1, g1, sh2, sc2, g2 = [mod[:b, k * d:(k + 1) * d][:, None, :] for k in range(6)]
    sh1c, sc1c = [jnp.broadcast_to(mod[b, k * d:(k + 1) * d], (b, 1, d)) for k in range(2)]

    wi = w_in[0]
    kr_cols = jnp.pad(wi[:, Q_LORA + KV_LORA:Q_LORA + KV_LORA + QK_ROPE],
                      ((0, 0), (QK_NOPE, HEAD_PAD - QK_NOPE - QK_ROPE)))
    w_in_r = jnp.concatenate([wi[:, :Q_LORA + KV_LORA], wi[:, Q_LORA + KV_LORA + QK_ROPE:], kr_cols],
                             axis=1).astype(BF16)
    w_uq_p = _pad_heads(w_uq[0], QK_NOPE + QK_ROPE, 0).astype(BF16)
    wkv = w_ukv[0].reshape(KV_LORA, N_HEADS, QK_NOPE + V_HEAD)
    w_uk_p = _pad_heads(wkv[:, :, :QK_NOPE].reshape(KV_LORA, -1), QK_NOPE, 0).astype(BF16)
    w_uv_p = _pad_heads(wkv[:, :, QK_NOPE:].reshape(KV_LORA, -1), V_HEAD, 0).astype(BF16)
    tables = _rope_tables(s)

    q, k, v, u = _proj(x, sc1, sh1, tables, w_in_r, q_norm_g, w_uq_p, kv_norm_g, w_uk_p, w_uv_p,
                       PROJ_TILE)
    kc, vc = _proj(ctx, sc1c, sh1c, None, w_in_r, None, None, kv_norm_g, w_uk_p, w_uv_p,
                   ctx.shape[1])
    attn = _attention(q, kc, vc, k, v)

    x1, h2, gates = _mix(attn, u, x, g1, sc2, sh2, w_pool[0].astype(BF16), pool_scale,
                         w_out[0].astype(BF16), ln1_g, ln1_b, w_router[0].T,
                         router_bias[0][:, None])
    return _moe(h2, gates, x1, g2, w_e_gate[0], w_e_up[0], w_e_down[0],
                w_s_gate[0].astype(BF16), w_s_up[0].astype(BF16), w_s_down[0].astype(BF16),
                ln2_g, ln2_b)
```

```python
import functools
import math

import jax
import jax.numpy as jnp
from jax import lax
from jax.experimental import pallas as pl
from jax.experimental.pallas import tpu as pltpu

F32 = jnp.float32
BF16 = jnp.bfloat16

D_MODEL = 1024
GRID_W = 64
N_HEADS = 8
Q_LORA = 512
KV_LORA = 256
QK_NOPE = 64
QK_ROPE = 32
V_HEAD = 64
ROPE_FREQS = QK_ROPE // 4
ROPE_THETA = 10000.0
ATTN_SCALE = 1.0 / math.sqrt(QK_NOPE + QK_ROPE)
LOG2_E = math.log2(math.e)
POOL_GROUPS = 4
POOL_WINDOWS = (2, 4, 8, 16)
POOL_WIDTH = 512
POOL_GC = POOL_WIDTH // POOL_GROUPS
POOL_HALO = 8
N_EXPERTS = 64
N_EXPERT_GROUPS = 8
GROUP_SIZE = N_EXPERTS // N_EXPERT_GROUPS
TOPK_GROUPS = 4
TOP_K = 8
D_EXPERT = 256
ROUTED_SCALE = 2.5
LN_EPS = 1e-5
RMS_EPS = 1e-6
ALPHA = 2.0 ** 0.25

LANES = 128
SUBLANES = 8
HEAD_PAD = LANES
V_ROWS = 80
ONES_ROW = V_HEAD
IN_PAD = Q_LORA + KV_LORA + POOL_WIDTH + LANES

PROJ_TILE = 512
ATTN_TQ = 512
ATTN_TK = PROJ_TILE
ATTN_SUB = 256
ATTN_AHEAD = 2
ATTN_UNROLL = 16
MIX_TILE = 512
MOE_TILE = 1024
MOE_EC = 2
VMEM_LIMIT = 48 * 1024 * 1024
NEG_BIG = -1e30


def _silu(v):
    return v * jax.nn.sigmoid(v)


def _layer_norm(z, g, b):
    mu = jnp.mean(z, axis=-1, keepdims=True)
    zc = z - mu
    var = jnp.mean(zc * zc, axis=-1, keepdims=True)
    return zc * lax.rsqrt(var + LN_EPS) * g + b


def _rms_norm(v, g):
    return v * lax.rsqrt(jnp.mean(v * v, axis=-1, keepdims=True) + RMS_EPS) * g


def _dot_nt(a, b):
    return lax.dot_general(a, b, (((1,), (1,)), ((), ())), preferred_element_type=F32)


def _ada_kernel(c_ref, w_ref, b_ref, o_ref):
    cv = _silu(c_ref[...])
    o_ref[...] = jnp.dot(cv, w_ref[...], preferred_element_type=F32,
                         precision=lax.Precision.HIGHEST) + b_ref[...]


def _ada(cvec, w_ada, b_ada):
    rows, d = cvec.shape
    n = w_ada.shape[1]
    tn = 1024
    return pl.pallas_call(
        _ada_kernel,
        grid=(n // tn,),
        in_specs=[pl.BlockSpec((rows, d), lambda j: (0, 0)),
                  pl.BlockSpec((d, tn), lambda j: (0, j)),
                  pl.BlockSpec((1, tn), lambda j: (0, j))],
        out_specs=pl.BlockSpec((rows, tn), lambda j: (0, j)),
        out_shape=jax.ShapeDtypeStruct((rows, n), F32),
        compiler_params=pltpu.CompilerParams(dimension_semantics=("arbitrary",),
                                             vmem_limit_bytes=VMEM_LIMIT),
        name="ada",
    )(cvec, w_ada, b_ada)


def _rope_lanes(v, cos, sin_lo, sin_hi):
    return v * cos + pltpu.roll(v, LANES - 8, axis=1) * sin_lo + pltpu.roll(v, 8, axis=1) * sin_hi


def _proj_kernel(*refs, with_q):
    if with_q:
        (x_ref, sc_ref, sh_ref, cos_ref, slo_ref, shi_ref, cos_t_ref, sin_t_ref, win_ref, qg_ref,
         wuq_ref, kvg_ref, wuk_ref, wuv_ref, q_ref, k_ref, v_ref, u_ref) = refs
    else:
        (x_ref, sc_ref, sh_ref, win_ref, kvg_ref, wuk_ref, wuv_ref, k_ref, v_ref) = refs
    h = (x_ref[0] * (1.0 + sc_ref[0]) + sh_ref[0]).astype(BF16)
    p = jnp.dot(h, win_ref[...], preferred_element_type=F32)
    tile = p.shape[0]
    kr = p[:, IN_PAD - LANES:]
    kvn = _rms_norm(p[:, Q_LORA:Q_LORA + KV_LORA], kvg_ref[...]).astype(BF16)
    kfull = jnp.dot(kvn, wuk_ref[...], preferred_element_type=F32)
    v_t = _dot_nt(wuv_ref[...], kvn)
    row = lax.broadcasted_iota(jnp.int32, (N_HEADS * V_ROWS, 1), 0)
    v_t = v_t + (row % V_ROWS == ONES_ROW).astype(F32)
    if with_q:
        kr = _rope_lanes(kr, cos_ref[...], slo_ref[...], shi_ref[...])
        u_ref[0] = p[:, Q_LORA + KV_LORA:Q_LORA + KV_LORA + POOL_WIDTH]
        qn = _rms_norm(p[:, :Q_LORA], qg_ref[...]).astype(BF16)
        q_t = _dot_nt(wuq_ref[...], qn) * (ATTN_SCALE * LOG2_E)
        cos_t, sin_t = cos_t_ref[...], sin_t_ref[...]
    for hd in range(N_HEADS):
        k_ref[0, hd] = (kfull[:, hd * HEAD_PAD:(hd + 1) * HEAD_PAD] + kr).astype(BF16)
        v_ref[0, hd, 0] = v_t[hd * V_ROWS:(hd + 1) * V_ROWS].astype(BF16)
        if with_q:
            base = hd * HEAD_PAD
            q_ref[0, hd, 0:QK_NOPE, :] = q_t[base:base + QK_NOPE].astype(BF16)
            rope = []
            for ax in range(2):
                lo = q_t[base + QK_NOPE + 16 * ax:base + QK_NOPE + 16 * ax + 8]
                hi = q_t[base + QK_NOPE + 16 * ax + 8:base + QK_NOPE + 16 * ax + 16]
                cs, sn = cos_t[8 * ax:8 * ax + 8], sin_t[8 * ax:8 * ax + 8]
                rope += [lo * cs - hi * sn, hi * cs + lo * sn]
            rope.append(jnp.zeros((HEAD_PAD - QK_NOPE - QK_ROPE, tile), F32))
            q_ref[0, hd, QK_NOPE:, :] = jnp.concatenate(rope, axis=0).astype(BF16)


def _proj(x, sc, sh, tables, w_in_r, q_g, w_uq_t, kv_g, w_uk_p, w_uv_t, tile):
    b, s, d = x.shape
    with_q = tables is not None
    grid = (b, s // tile)
    row = lambda bi, i: (bi, i, 0)
    vec = lambda bi, i: (bi, 0, 0)
    const2 = lambda bi, i: (0, 0)
    k_out = pl.BlockSpec((1, N_HEADS, tile, HEAD_PAD), lambda bi, i: (bi, 0, i, 0))
    k_shape = jax.ShapeDtypeStruct((b, N_HEADS, s, HEAD_PAD), BF16)
    v_out = pl.BlockSpec((1, N_HEADS, 1, V_ROWS, tile), lambda bi, i: (bi, 0, i, 0, 0))
    v_shape = jax.ShapeDtypeStruct((b, N_HEADS, s // tile, V_ROWS, tile), BF16)
    in_specs = [pl.BlockSpec((1, tile, d), row),
                pl.BlockSpec((1, 1, d), vec), pl.BlockSpec((1, 1, d), vec)]
    args = [x, sc, sh]
    if with_q:
        lane_tabs, row_tabs = tables
        in_specs += [pl.BlockSpec((tile, LANES), lambda bi, i: (i, 0))] * 3
        in_specs += [pl.BlockSpec((2 * ROPE_FREQS, tile), lambda bi, i: (0, i))] * 2
        args += list(lane_tabs) + list(row_tabs)
    in_specs.append(pl.BlockSpec(w_in_r.shape, const2)); args.append(w_in_r)
    if with_q:
        in_specs += [pl.BlockSpec(q_g.shape, const2), pl.BlockSpec(w_uq_t.shape, const2)]
        args += [q_g, w_uq_t]
    in_specs += [pl.BlockSpec(kv_g.shape, const2), pl.BlockSpec(w_uk_p.shape, const2),
                 pl.BlockSpec(w_uv_t.shape, const2)]
    args += [kv_g, w_uk_p, w_uv_t]
    if with_q:
        q_out = pl.BlockSpec((1, N_HEADS, HEAD_PAD, tile), lambda bi, i: (bi, 0, 0, i))
        q_shape = jax.ShapeDtypeStruct((b, N_HEADS, HEAD_PAD, s), BF16)
        out_specs = [q_out, k_out, v_out, pl.BlockSpec((1, tile, POOL_WIDTH), row)]
        out_shape = [q_shape, k_shape, v_shape, jax.ShapeDtypeStruct((b, s, POOL_WIDTH), F32)]
    else:
        out_specs = [k_out, v_out]
        out_shape = [k_shape, v_shape]
    return pl.pallas_call(
        functools.partial(_proj_kernel, with_q=with_q),
        grid=grid, in_specs=in_specs, out_specs=out_specs, out_shape=out_shape,
        compiler_params=pltpu.CompilerParams(dimension_semantics=("arbitrary", "arbitrary"),
                                             vmem_limit_bytes=VMEM_LIMIT),
        name="proj" if with_q else "proj_ctx",
    )(*args)


def _attn_kernel(q_ref, kc_ref, vc_ref, k_ref, v_ref, o_ref, *, n_kblk, tk):
    tq = q_ref.shape[3]
    qs = [q_ref[0, hh] for hh in range(2)]

    def scores(hh, kb):
        return jnp.dot(kb, qs[hh], preferred_element_type=F32)

    def update(s_t, vb_t, m, acc):
        m_new = jnp.maximum(m, jnp.max(s_t, axis=0, keepdims=True))
        p_t = jnp.exp2(s_t - m_new).astype(BF16)
        acc = jnp.exp2(m - m_new) * acc + jnp.dot(vb_t, p_t, preferred_element_type=F32)
        return m_new, acc

    def run_items(items, state):
        pending = [scores(hh, kb()) for hh, kb, _ in items[:ATTN_AHEAD]]
        for j, (hh, _, vb) in enumerate(items):
            if j + ATTN_AHEAD < len(items):
                nh, nkb, _ = items[j + ATTN_AHEAD]
                pending.append(scores(nh, nkb()))
            state[hh] = update(pending.pop(0), vb(), *state[hh])
        return state

    def block_items(blk, off):
        out = []
        for sub in range(tk // ATTN_SUB):
            for hh in range(2):
                lo = sub * ATTN_SUB
                out.append((hh,
                            lambda hh=hh, lo=lo: k_ref[0, hh, pl.ds(off + lo, ATTN_SUB), :],
                            lambda hh=hh, lo=lo: v_ref[0, hh, blk, :, lo:lo + ATTN_SUB]))
        return out

    ctx_items = [(hh, lambda hh=hh: kc_ref[0, hh], lambda hh=hh: vc_ref[0, hh, 0])
                 for hh in range(2)]
    state = [(jnp.full((1, tq), NEG_BIG, F32), jnp.zeros((V_ROWS, tq), F32)) for _ in range(2)]
    n_iter = n_kblk // ATTN_UNROLL
    if n_iter == 1:
        items = ctx_items
        for blk in range(n_kblk):
            items = items + block_items(blk, blk * tk)
        state = run_items(items, state)
    else:
        state = run_items(ctx_items, state)

        def body(i, carry):
            items = []
            for r in range(ATTN_UNROLL):
                blk = i * ATTN_UNROLL + r
                items += block_items(blk, pl.multiple_of(blk * tk, tk))
            st = run_items(items, [(carry[0], carry[1]), (carry[2], carry[3])])
            return st[0] + st[1]

        carry = lax.fori_loop(0, n_iter, body, state[0] + state[1])
        state = [(carry[0], carry[1]), (carry[2], carry[3])]
    carry = state[0] + state[1]
    outs = [carry[2 * hh + 1][:V_HEAD] / carry[2 * hh + 1][ONES_ROW:ONES_ROW + 1] for hh in range(2)]
    o_ref[0] = jnp.concatenate(outs, axis=0).T.astype(o_ref.dtype)


def _attention(q_t, kc, vc_t, k, v_t):
    b, nh, dp, s = q_t.shape
    c = kc.shape[2]
    tq, tk = ATTN_TQ, ATTN_TK
    n_kblk = s // tk
    kern = functools.partial(_attn_kernel, n_kblk=n_kblk, tk=tk)
    return pl.pallas_call(
        kern,
        grid=(b, nh // 2, s // tq),
        in_specs=[pl.BlockSpec((1, 2, dp, tq), lambda bi, hp, qi: (bi, hp, 0, qi)),
                  pl.BlockSpec((1, 2, c, dp), lambda bi, hp, qi: (bi, hp, 0, 0)),
                  pl.BlockSpec((1, 2, 1, V_ROWS, c), lambda bi, hp, qi: (bi, hp, 0, 0, 0)),
                  pl.BlockSpec((1, 2, s, dp), lambda bi, hp, qi: (bi, hp, 0, 0)),
                  pl.BlockSpec((1, 2, n_kblk, V_ROWS, tk), lambda bi, hp, qi: (bi, hp, 0, 0, 0))],
        out_specs=pl.BlockSpec((1, tq, 2 * V_HEAD), lambda bi, hp, qi: (bi, qi, hp)),
        out_shape=jax.ShapeDtypeStruct((b, s, nh * V_HEAD), BF16),
        compiler_params=pltpu.CompilerParams(
            dimension_semantics=("arbitrary", "arbitrary", "arbitrary"),
            vmem_limit_bytes=VMEM_LIMIT),
        name="attn",
    )(q_t, kc, vc_t, k, v_t)


def _route(logits_t, bias_t):
    e, t = logits_t.shape
    scores = jax.nn.sigmoid(logits_t)
    biased = scores + bias_t
    neg_inf = F32(-jnp.inf)
    gscore = []
    for g in range(N_EXPERT_GROUPS):
        v = biased[g * GROUP_SIZE:(g + 1) * GROUP_SIZE]
        m1 = jnp.max(v, axis=0, keepdims=True)
        at_max = v == m1
        n_max = jnp.sum(at_max.astype(F32), axis=0, keepdims=True)
        m2 = jnp.max(jnp.where(at_max, neg_inf, v), axis=0, keepdims=True)
        gscore.append(m1 + jnp.where(n_max >= 2.0, m1, m2))
    masked = []
    for g in range(N_EXPERT_GROUPS):
        rank = jnp.zeros((1, t), F32)
        for o in range(N_EXPERT_GROUPS):
            if o == g:
                continue
            beats = (gscore[o] >= gscore[g]) if o < g else (gscore[o] > gscore[g])
            rank = rank + beats.astype(F32)
        keep = rank < float(TOPK_GROUPS)
        masked.append(jnp.where(keep, biased[g * GROUP_SIZE:(g + 1) * GROUP_SIZE], neg_inf))
    work = jnp.concatenate(masked, axis=0)
    rows = lax.broadcasted_iota(jnp.int32, (e, t), 0)
    sel = jnp.zeros((e, t), F32)
    for _ in range(TOP_K):
        m = jnp.max(work, axis=0, keepdims=True)
        first = jnp.min(jnp.where(work == m, rows, e), axis=0, keepdims=True)
        pick = rows == first
        sel = jnp.where(pick, 1.0, sel)
        work = jnp.where(pick, neg_inf, work)
    w = sel * scores
    return w / jnp.sum(w, axis=0, keepdims=True) * ROUTED_SCALE


def _mix_kernel(attn_ref, u_ref, up_ref, un_ref, x_ref, g1_ref, sc2_ref, sh2_ref, wpool_ref,
                pscale_ref, wout_ref, ln_g_ref, ln_b_ref, wr_ref, rb_ref,
                x1_ref, h2_ref, gates_ref, uext_ref, *, seq):
    i = pl.program_id(1)
    tile = u_ref.shape[1]
    u = u_ref[0]
    uext_ref[0:POOL_HALO] = jnp.where(i == 0, 0.0, up_ref[0])
    uext_ref[POOL_HALO:POOL_HALO + tile] = u
    uext_ref[POOL_HALO + tile:] = jnp.where(i == pl.num_programs(1) - 1, 0.0, un_ref[0])
    t = i * tile + lax.broadcasted_iota(jnp.int32, (tile, POOL_GC), 0)
    pooled = []
    for g, w in enumerate(POOL_WINDOWS):
        lanes = slice(g * POOL_GC, (g + 1) * POOL_GC)
        tot = uext_ref[POOL_HALO - w // 2:POOL_HALO - w // 2 + tile, lanes]
        for dlt in range(1, w):
            start = POOL_HALO - w // 2 + dlt
            tot = tot + uext_ref[start:start + tile, lanes]
        cnt = (jnp.minimum(t - w // 2 + w, seq) - jnp.maximum(t - w // 2, 0)).astype(F32)
        pg = (tot / cnt - u[:, lanes]).astype(BF16)
        po = jnp.dot(pg, wpool_ref[g], preferred_element_type=F32) * pscale_ref[:, lanes]
        pooled.append(po.astype(BF16))
    mixed = jnp.concatenate([attn_ref[0]] + pooled, axis=1)
    y = jnp.dot(mixed, wout_ref[...], preferred_element_type=F32)
    x1 = _layer_norm(ALPHA * x_ref[0] + g1_ref[0] * y, ln_g_ref[...], ln_b_ref[...])
    x1_ref[0] = x1
    h2 = x1 * (1.0 + sc2_ref[0]) + sh2_ref[0]
    h2_ref[0] = h2.astype(BF16)
    logits_t = lax.dot_general(wr_ref[...], h2, (((1,), (1,)), ((), ())),
                               preferred_element_type=F32, precision=lax.Precision.HIGHEST)
    gates_t = _route(logits_t, rb_ref[...])
    gates_t = jnp.concatenate([gates_t, jnp.zeros((LANES - N_EXPERTS, tile), F32)], axis=0)
    gates_ref[0] = gates_t.T


def _mix(attn, u, x, g1, sc2, sh2, w_pool, pool_scale, w_out, ln_g, ln_b, w_r_t, rb_t):
    b, s, d = x.shape
    tile = MIX_TILE
    hb = tile // POOL_HALO
    row = lambda bi, i: (bi, i, 0)
    vec = lambda bi, i: (bi, 0, 0)
    c2 = lambda bi, i: (0, 0)
    return pl.pallas_call(
        functools.partial(_mix_kernel, seq=s),
        grid=(b, s // tile),
        in_specs=[pl.BlockSpec((1, tile, POOL_WIDTH), row),
                  pl.BlockSpec((1, tile, POOL_WIDTH), row),
                  pl.BlockSpec((1, POOL_HALO, POOL_WIDTH),
                               lambda bi, i: (bi, jnp.maximum(i * hb - 1, 0), 0)),
                  pl.BlockSpec((1, POOL_HALO, POOL_WIDTH),
                               lambda bi, i: (bi, jnp.minimum((i + 1) * hb, s // POOL_HALO - 1), 0)),
                  pl.BlockSpec((1, tile, d), row),
                  pl.BlockSpec((1, 1, d), vec), pl.BlockSpec((1, 1, d), vec),
                  pl.BlockSpec((1, 1, d), vec),
                  pl.BlockSpec(w_pool.shape, lambda bi, i: (0, 0, 0)),
                  pl.BlockSpec(pool_scale.shape, c2),
                  pl.BlockSpec(w_out.shape, c2),
                  pl.BlockSpec(ln_g.shape, c2), pl.BlockSpec(ln_b.shape, c2),
                  pl.BlockSpec(w_r_t.shape, c2), pl.BlockSpec(rb_t.shape, c2)],
        out_specs=[pl.BlockSpec((1, tile, d), row),
                   pl.BlockSpec((1, tile, d), row),
                   pl.BlockSpec((1, tile, LANES), row)],
        out_shape=[jax.ShapeDtypeStruct((b, s, d), F32),
                   jax.ShapeDtypeStruct((b, s, d), BF16),
                   jax.ShapeDtypeStruct((b, s, LANES), F32)],
        scratch_shapes=[pltpu.VMEM((tile + 2 * POOL_HALO, POOL_WIDTH), F32)],
        compiler_params=pltpu.CompilerParams(dimension_semantics=("arbitrary", "arbitrary"),
                                             vmem_limit_bytes=VMEM_LIMIT),
        name="mix",
    )(attn, u, u, u, x, g1, sc2, sh2, w_pool, pool_scale, w_out, ln_g, ln_b, w_r_t, rb_t)


def _moe_kernel(h2_ref, gates_ref, x1_ref, g2_ref, wg_ref, wu_ref, wd_ref, wsg_ref, wsu_ref,
                wsd_ref, ln_g_ref, ln_b_ref, o_ref, acc_ref):
    c = pl.program_id(2)
    h = h2_ref[0]

    @pl.when(c == 0)
    def _():
        a = _silu(jnp.dot(h, wsg_ref[...], preferred_element_type=F32)) * jnp.dot(
            h, wsu_ref[...], preferred_element_type=F32)
        acc_ref[...] = jnp.dot(a.astype(BF16), wsd_ref[...], preferred_element_type=F32)

    gates = pltpu.roll(gates_ref[0], (LANES - c * MOE_EC) % LANES, axis=1)
    for j in range(MOE_EC):
        a = _silu(jnp.dot(h, wg_ref[j].astype(BF16), preferred_element_type=F32)) * jnp.dot(
            h, wu_ref[j].astype(BF16), preferred_element_type=F32)
        a = a * gates[:, j:j + 1]
        acc_ref[...] += jnp.dot(a.astype(BF16), wd_ref[j].astype(BF16), preferred_element_type=F32)

    @pl.when(c == pl.num_programs(2) - 1)
    def _():
        z = ALPHA * x1_ref[0] + g2_ref[0] * acc_ref[...]
        o_ref[0] = _layer_norm(z, ln_g_ref[...], ln_b_ref[...])


def _moe(h2, gates, x1, g2, w_e_gate, w_e_up, w_e_down, w_s_gate, w_s_up, w_s_down, ln_g, ln_b):
    b, s, d = x1.shape
    tile = MOE_TILE
    n_e, _, f = w_e_gate.shape
    row = lambda bi, i, c: (bi, i, 0)
    vec = lambda bi, i, c: (bi, 0, 0)
    c2 = lambda bi, i, c: (0, 0)
    ew = lambda bi, i, c: (c, 0, 0)
    return pl.pallas_call(
        _moe_kernel,
        grid=(b, s // tile, n_e // MOE_EC),
        in_specs=[pl.BlockSpec((1, tile, d), row),
                  pl.BlockSpec((1, tile, LANES), row),
                  pl.BlockSpec((1, tile, d), row),
                  pl.BlockSpec((1, 1, d), vec),
                  pl.BlockSpec((MOE_EC, d, f), ew),
                  pl.BlockSpec((MOE_EC, d, f), ew),
                  pl.BlockSpec((MOE_EC, f, d), ew),
                  pl.BlockSpec(w_s_gate.shape, c2), pl.BlockSpec(w_s_up.shape, c2),
                  pl.BlockSpec(w_s_down.shape, c2),
                  pl.BlockSpec(ln_g.shape, c2), pl.BlockSpec(ln_b.shape, c2)],
        out_specs=pl.BlockSpec((1, tile, d), row),
        out_shape=jax.ShapeDtypeStruct((b, s, d), F32),
        scratch_shapes=[pltpu.VMEM((tile, d), F32)],
        compiler_params=pltpu.CompilerParams(
            dimension_semantics=("arbitrary", "arbitrary", "arbitrary"),
            vmem_limit_bytes=VMEM_LIMIT),
        name="moe",
    )(h2, gates, x1, g2, w_e_gate, w_e_up, w_e_down, w_s_gate, w_s_up, w_s_down, ln_g, ln_b)


def _rope_tables(seq):
    t = jnp.arange(seq)
    pos = jnp.stack([t // GRID_W, t % GRID_W], axis=-1).astype(F32)
    inv_freq = ROPE_THETA ** (-jnp.arange(ROPE_FREQS, dtype=F32) / ROPE_FREQS)
    ang = pos[:, :, None] * inv_freq
    cos, sin = jnp.cos(ang), jnp.sin(ang)
    zero = jnp.zeros_like(sin)
    cos_r = jnp.stack([cos, cos], axis=2).reshape(seq, QK_ROPE)
    sin_lo = jnp.stack([-sin, zero], axis=2).reshape(seq, QK_ROPE)
    sin_hi = jnp.stack([zero, sin], axis=2).reshape(seq, QK_ROPE)
    pads = ((0, 0), (QK_NOPE, HEAD_PAD - QK_NOPE - QK_ROPE))
    lane_tabs = (jnp.pad(cos_r, pads, constant_values=1.0), jnp.pad(sin_lo, pads),
                 jnp.pad(sin_hi, pads))
    row_tabs = (cos.reshape(seq, 2 * ROPE_FREQS).T, sin.reshape(seq, 2 * ROPE_FREQS).T)
    return lane_tabs, row_tabs


def _pad_heads(w, width, padded):
    k = w.shape[0]
    w = jnp.pad(w.reshape(k, N_HEADS, width), ((0, 0), (0, 0), (0, padded - width)))
    return w.reshape(k, N_HEADS * padded)


def kernel(x, c, ctx, c_ctx, w_ada, b_ada, w_in, q_norm_g, w_uq, kv_norm_g, w_ukv, w_pool, pool_scale, w_out, ln1_g, ln1_b, w_router, router_bias, w_e_gate, w_e_up, w_e_down, w_s_gate, w_s_up, w_s_down, ln2_g, ln2_b):
    assert w_ada.shape[0] == 1, "single-layer block"
    b, s, d = x.shape

    cvec = jnp.concatenate([c, c_ctx[None], jnp.zeros((SUBLANES - b - 1, d), F32)], axis=0)
    mod = _ada(cvec, w_ada[0], b_ada)
    sh1, sc1, g1, sh2, sc2, g2 = [mod[:b, k * d:(k + 1) * d][:, None, :] for k in range(6)]
    sh1c, sc1c = [jnp.broadcast_to(mod[b, k * d:(k + 1) * d], (b, 1, d)) for k in range(2)]

    wi = w_in[0]
    kr_cols = jnp.pad(wi[:, Q_LORA + KV_LORA:Q_LORA + KV_LORA + QK_ROPE],
                      ((0, 0), (QK_NOPE, HEAD_PAD - QK_NOPE - QK_ROPE)))
    w_in_r = jnp.concatenate([wi[:, :Q_LORA + KV_LORA], wi[:, Q_LORA + KV_LORA + QK_ROPE:], kr_cols],
                             axis=1).astype(BF16)
    w_uq_t = _pad_heads(w_uq[0], QK_NOPE + QK_ROPE, HEAD_PAD).T.astype(BF16)
    wkv = w_ukv[0].reshape(KV_LORA, N_HEADS, QK_NOPE + V_HEAD)
    w_uk_p = _pad_heads(wkv[:, :, :QK_NOPE].reshape(KV_LORA, -1), QK_NOPE, HEAD_PAD).astype(BF16)
    w_uv_t = _pad_heads(wkv[:, :, QK_NOPE:].reshape(KV_LORA, -1), V_HEAD, V_ROWS).T.astype(BF16)
    tables = _rope_tables(s)

    q_t, k, v_t, u = _proj(x, sc1, sh1, tables, w_in_r, q_norm_g, w_uq_t, kv_norm_g, w_uk_p,
                           w_uv_t, PROJ_TILE)
    kc, vc_t = _proj(ctx, sc1c, sh1c, None, w_in_r, None, None, kv_norm_g, w_uk_p, w_uv_t,
                     ctx.shape[1])
    attn = _attention(q_t, kc, vc_t, k, v_t)

    x1, h2, gates = _mix(attn, u, x, g1, sc2, sh2, w_pool[0].astype(BF16), pool_scale,
                         w_out[0].astype(BF16), ln1_g, ln1_b, w_router[0].T,
                         router_bias[0][:, None])
    return _moe(h2, gates, x1, g2, w_e_gate[0], w_e_up[0], w_e_down[0],
                w_s_gate[0].astype(BF16), w_s_up[0].astype(BF16), w_s_down[0].astype(BF16),
                ln2_g, ln2_b)
```

```python
import functools
import math

import jax
import jax.numpy as jnp
from jax import lax
from jax.experimental import pallas as pl
from jax.experimental.pallas import tpu as pltpu

F32 = jnp.float32
BF16 = jnp.bfloat16

D_MODEL = 1024
GRID_W = 64
N_HEADS = 8
Q_LORA = 512
KV_LORA = 256
QK_NOPE = 64
QK_ROPE = 32
V_HEAD = 64
ROPE_FREQS = QK_ROPE // 4
ROPE_THETA = 10000.0
ATTN_SCALE = 1.0 / math.sqrt(QK_NOPE + QK_ROPE)
LOG2_E = math.log2(math.e)
POOL_GROUPS = 4
POOL_WINDOWS = (2, 4, 8, 16)
POOL_WIDTH = 512
POOL_GC = POOL_WIDTH // POOL_GROUPS
POOL_HALO = 8
N_EXPERTS = 64
N_EXPERT_GROUPS = 8
GROUP_SIZE = N_EXPERTS // N_EXPERT_GROUPS
TOPK_GROUPS = 4
TOP_K = 8
D_EXPERT = 256
ROUTED_SCALE = 2.5
LN_EPS = 1e-5
RMS_EPS = 1e-6
ALPHA = 2.0 ** 0.25

LANES = 128
SUBLANES = 8
HEAD_PAD = LANES
V_ROWS = 80
ONES_ROW = V_HEAD
IN_PAD = Q_LORA + KV_LORA + POOL_WIDTH + LANES

PROJ_TILE = 512
ATTN_TQ = 512
ATTN_TK = PROJ_TILE
ATTN_SUB = 256
ATTN_AHEAD = 2
ATTN_UNROLL = 16
MIX_TILE = 512
MOE_TM = 512
DISPATCH_CHUNK = 256
COMBINE_TILE = 256
VMEM_LIMIT = 48 * 1024 * 1024
NEG_BIG = -1e30


def _silu(v):
    return v * jax.nn.sigmoid(v)


def _layer_norm(z, g, b):
    mu = jnp.mean(z, axis=-1, keepdims=True)
    zc = z - mu
    var = jnp.mean(zc * zc, axis=-1, keepdims=True)
    return zc * lax.rsqrt(var + LN_EPS) * g + b


def _rms_norm(v, g):
    return v * lax.rsqrt(jnp.mean(v * v, axis=-1, keepdims=True) + RMS_EPS) * g


def _dot_nt(a, b):
    return lax.dot_general(a, b, (((1,), (1,)), ((), ())), preferred_element_type=F32)


def _ada_kernel(c_ref, w_ref, b_ref, o_ref):
    cv = _silu(c_ref[...])
    o_ref[...] = jnp.dot(cv, w_ref[...], preferred_element_type=F32,
                         precision=lax.Precision.HIGHEST) + b_ref[...]


def _ada(cvec, w_ada, b_ada):
    rows, d = cvec.shape
    n = w_ada.shape[1]
    tn = 1024
    return pl.pallas_call(
        _ada_kernel,
        grid=(n // tn,),
        in_specs=[pl.BlockSpec((rows, d), lambda j: (0, 0)),
                  pl.BlockSpec((d, tn), lambda j: (0, j)),
                  pl.BlockSpec((1, tn), lambda j: (0, j))],
        out_specs=pl.BlockSpec((rows, tn), lambda j: (0, j)),
        out_shape=jax.ShapeDtypeStruct((rows, n), F32),
        compiler_params=pltpu.CompilerParams(dimension_semantics=("arbitrary",),
                                             vmem_limit_bytes=VMEM_LIMIT),
        name="ada",
    )(cvec, w_ada, b_ada)


def _rope_lanes(v, cos, sin_lo, sin_hi):
    return v * cos + pltpu.roll(v, LANES - 8, axis=1) * sin_lo + pltpu.roll(v, 8, axis=1) * sin_hi


def _proj_kernel(*refs, with_q):
    if with_q:
        (x_ref, sc_ref, sh_ref, cos_ref, slo_ref, shi_ref, cos_t_ref, sin_t_ref, win_ref, qg_ref,
         wuq_ref, kvg_ref, wuk_ref, wuv_ref, q_ref, k_ref, v_ref, u_ref) = refs
    else:
        (x_ref, sc_ref, sh_ref, win_ref, kvg_ref, wuk_ref, wuv_ref, k_ref, v_ref) = refs
    h = (x_ref[0] * (1.0 + sc_ref[0]) + sh_ref[0]).astype(BF16)
    p = jnp.dot(h, win_ref[...], preferred_element_type=F32)
    tile = p.shape[0]
    kr = p[:, IN_PAD - LANES:]
    kvn = _rms_norm(p[:, Q_LORA:Q_LORA + KV_LORA], kvg_ref[...]).astype(BF16)
    kfull = jnp.dot(kvn, wuk_ref[...], preferred_element_type=F32)
    v_t = _dot_nt(wuv_ref[...], kvn)
    row = lax.broadcasted_iota(jnp.int32, (N_HEADS * V_ROWS, 1), 0)
    v_t = v_t + (row % V_ROWS == ONES_ROW).astype(F32)
    if with_q:
        kr = _rope_lanes(kr, cos_ref[...], slo_ref[...], shi_ref[...])
        u_ref[0] = p[:, Q_LORA + KV_LORA:Q_LORA + KV_LORA + POOL_WIDTH]
        qn = _rms_norm(p[:, :Q_LORA], qg_ref[...]).astype(BF16)
        q_t = _dot_nt(wuq_ref[...], qn) * (ATTN_SCALE * LOG2_E)
        cos_t, sin_t = cos_t_ref[...], sin_t_ref[...]
    for hd in range(N_HEADS):
        k_ref[0, hd] = (kfull[:, hd * HEAD_PAD:(hd + 1) * HEAD_PAD] + kr).astype(BF16)
        v_ref[0, hd, 0] = v_t[hd * V_ROWS:(hd + 1) * V_ROWS].astype(BF16)
        if with_q:
            base = hd * HEAD_PAD
            q_ref[0, hd, 0:QK_NOPE, :] = q_t[base:base + QK_NOPE].astype(BF16)
            rope = []
            for ax in range(2):
                lo = q_t[base + QK_NOPE + 16 * ax:base + QK_NOPE + 16 * ax + 8]
                hi = q_t[base + QK_NOPE + 16 * ax + 8:base + QK_NOPE + 16 * ax + 16]
                cs, sn = cos_t[8 * ax:8 * ax + 8], sin_t[8 * ax:8 * ax + 8]
                rope += [lo * cs - hi * sn, hi * cs + lo * sn]
            rope.append(jnp.zeros((HEAD_PAD - QK_NOPE - QK_ROPE, tile), F32))
            q_ref[0, hd, QK_NOPE:, :] = jnp.concatenate(rope, axis=0).astype(BF16)


def _proj(x, sc, sh, tables, w_in_r, q_g, w_uq_t, kv_g, w_uk_p, w_uv_t, tile):
    b, s, d = x.shape
    with_q = tables is not None
    grid = (b, s // tile)
    row = lambda bi, i: (bi, i, 0)
    vec = lambda bi, i: (bi, 0, 0)
    const2 = lambda bi, i: (0, 0)
    k_out = pl.BlockSpec((1, N_HEADS, tile, HEAD_PAD), lambda bi, i: (bi, 0, i, 0))
    k_shape = jax.ShapeDtypeStruct((b, N_HEADS, s, HEAD_PAD), BF16)
    v_out = pl.BlockSpec((1, N_HEADS, 1, V_ROWS, tile), lambda bi, i: (bi, 0, i, 0, 0))
    v_shape = jax.ShapeDtypeStruct((b, N_HEADS, s // tile, V_ROWS, tile), BF16)
    in_specs = [pl.BlockSpec((1, tile, d), row),
                pl.BlockSpec((1, 1, d), vec), pl.BlockSpec((1, 1, d), vec)]
    args = [x, sc, sh]
    if with_q:
        lane_tabs, row_tabs = tables
        in_specs += [pl.BlockSpec((tile, LANES), lambda bi, i: (i, 0))] * 3
        in_specs += [pl.BlockSpec((2 * ROPE_FREQS, tile), lambda bi, i: (0, i))] * 2
        args += list(lane_tabs) + list(row_tabs)
    in_specs.append(pl.BlockSpec(w_in_r.shape, const2)); args.append(w_in_r)
    if with_q:
        in_specs += [pl.BlockSpec(q_g.shape, const2), pl.BlockSpec(w_uq_t.shape, const2)]
        args += [q_g, w_uq_t]
    in_specs += [pl.BlockSpec(kv_g.shape, const2), pl.BlockSpec(w_uk_p.shape, const2),
                 pl.BlockSpec(w_uv_t.shape, const2)]
    args += [kv_g, w_uk_p, w_uv_t]
    if with_q:
        q_out = pl.BlockSpec((1, N_HEADS, HEAD_PAD, tile), lambda bi, i: (bi, 0, 0, i))
        q_shape = jax.ShapeDtypeStruct((b, N_HEADS, HEAD_PAD, s), BF16)
        out_specs = [q_out, k_out, v_out, pl.BlockSpec((1, tile, POOL_WIDTH), row)]
        out_shape = [q_shape, k_shape, v_shape, jax.ShapeDtypeStruct((b, s, POOL_WIDTH), F32)]
    else:
        out_specs = [k_out, v_out]
        out_shape = [k_shape, v_shape]
    return pl.pallas_call(
        functools.partial(_proj_kernel, with_q=with_q),
        grid=grid, in_specs=in_specs, out_specs=out_specs, out_shape=out_shape,
        compiler_params=pltpu.CompilerParams(dimension_semantics=("arbitrary", "arbitrary"),
                                             vmem_limit_bytes=VMEM_LIMIT),
        name="proj" if with_q else "proj_ctx",
    )(*args)


def _attn_kernel(q_ref, kc_ref, vc_ref, k_ref, v_ref, o_ref, *, n_kblk, tk):
    tq = q_ref.shape[3]
    qs = [q_ref[0, hh] for hh in range(2)]

    def scores(hh, kb):
        return jnp.dot(kb, qs[hh], preferred_element_type=F32)

    def update(s_t, vb_t, m, acc):
        m_new = jnp.maximum(m, jnp.max(s_t, axis=0, keepdims=True))
        p_t = jnp.exp2(s_t - m_new).astype(BF16)
        acc = jnp.exp2(m - m_new) * acc + jnp.dot(vb_t, p_t, preferred_element_type=F32)
        return m_new, acc

    def run_items(items, state):
        pending = [scores(hh, kb()) for hh, kb, _ in items[:ATTN_AHEAD]]
        for j, (hh, _, vb) in enumerate(items):
            if j + ATTN_AHEAD < len(items):
                nh, nkb, _ = items[j + ATTN_AHEAD]
                pending.append(scores(nh, nkb()))
            state[hh] = update(pending.pop(0), vb(), *state[hh])
        return state

    def block_items(blk, off):
        out = []
        for sub in range(tk // ATTN_SUB):
            for hh in range(2):
                lo = sub * ATTN_SUB
                out.append((hh,
                            lambda hh=hh, lo=lo: k_ref[0, hh, pl.ds(off + lo, ATTN_SUB), :],
                            lambda hh=hh, lo=lo: v_ref[0, hh, blk, :, lo:lo + ATTN_SUB]))
        return out

    ctx_items = [(hh, lambda hh=hh: kc_ref[0, hh], lambda hh=hh: vc_ref[0, hh, 0])
                 for hh in range(2)]
    state = [(jnp.full((1, tq), NEG_BIG, F32), jnp.zeros((V_ROWS, tq), F32)) for _ in range(2)]
    n_iter = n_kblk // ATTN_UNROLL
    if n_iter == 1:
        items = ctx_items
        for blk in range(n_kblk):
            items = items + block_items(blk, blk * tk)
        state = run_items(items, state)
    else:
        state = run_items(ctx_items, state)

        def body(i, carry):
            items = []
            for r in range(ATTN_UNROLL):
                blk = i * ATTN_UNROLL + r
                items += block_items(blk, pl.multiple_of(blk * tk, tk))
            st = run_items(items, [(carry[0], carry[1]), (carry[2], carry[3])])
            return st[0] + st[1]

        carry = lax.fori_loop(0, n_iter, body, state[0] + state[1])
        state = [(carry[0], carry[1]), (carry[2], carry[3])]
    carry = state[0] + state[1]
    outs = [carry[2 * hh + 1][:V_HEAD] / carry[2 * hh + 1][ONES_ROW:ONES_ROW + 1] for hh in range(2)]
    o_ref[0] = jnp.concatenate(outs, axis=0).T.astype(o_ref.dtype)


def _attention(q_t, kc, vc_t, k, v_t):
    b, nh, dp, s = q_t.shape
    c = kc.shape[2]
    tq, tk = ATTN_TQ, ATTN_TK
    n_kblk = s // tk
    kern = functools.partial(_attn_kernel, n_kblk=n_kblk, tk=tk)
    return pl.pallas_call(
        kern,
        grid=(b, nh // 2, s // tq),
        in_specs=[pl.BlockSpec((1, 2, dp, tq), lambda bi, hp, qi: (bi, hp, 0, qi)),
                  pl.BlockSpec((1, 2, c, dp), lambda bi, hp, qi: (bi, hp, 0, 0)),
                  pl.BlockSpec((1, 2, 1, V_ROWS, c), lambda bi, hp, qi: (bi, hp, 0, 0, 0)),
                  pl.BlockSpec((1, 2, s, dp), lambda bi, hp, qi: (bi, hp, 0, 0)),
                  pl.BlockSpec((1, 2, n_kblk, V_ROWS, tk), lambda bi, hp, qi: (bi, hp, 0, 0, 0))],
        out_specs=pl.BlockSpec((1, tq, 2 * V_HEAD), lambda bi, hp, qi: (bi, qi, hp)),
        out_shape=jax.ShapeDtypeStruct((b, s, nh * V_HEAD), BF16),
        compiler_params=pltpu.CompilerParams(
            dimension_semantics=("arbitrary", "arbitrary", "arbitrary"),
            vmem_limit_bytes=VMEM_LIMIT),
        name="attn",
    )(q_t, kc, vc_t, k, v_t)


def _route(logits_t, bias_t):
    e, t = logits_t.shape
    scores = jax.nn.sigmoid(logits_t)
    biased = scores + bias_t
    neg_inf = F32(-jnp.inf)
    gscore = []
    for g in range(N_EXPERT_GROUPS):
        v = biased[g * GROUP_SIZE:(g + 1) * GROUP_SIZE]
        m1 = jnp.max(v, axis=0, keepdims=True)
        at_max = v == m1
        n_max = jnp.sum(at_max.astype(F32), axis=0, keepdims=True)
        m2 = jnp.max(jnp.where(at_max, neg_inf, v), axis=0, keepdims=True)
        gscore.append(m1 + jnp.where(n_max >= 2.0, m1, m2))
    masked = []
    for g in range(N_EXPERT_GROUPS):
        rank = jnp.zeros((1, t), F32)
        for o in range(N_EXPERT_GROUPS):
            if o == g:
                continue
            beats = (gscore[o] >= gscore[g]) if o < g else (gscore[o] > gscore[g])
            rank = rank + beats.astype(F32)
        keep = rank < float(TOPK_GROUPS)
        masked.append(jnp.where(keep, biased[g * GROUP_SIZE:(g + 1) * GROUP_SIZE], neg_inf))
    work = jnp.concatenate(masked, axis=0)
    rows = lax.broadcasted_iota(jnp.int32, (e, t), 0)
    sel = jnp.zeros((e, t), F32)
    picked = []
    for _ in range(TOP_K):
        m = jnp.max(work, axis=0, keepdims=True)
        first = jnp.min(jnp.where(work == m, rows, e), axis=0, keepdims=True)
        pick = rows == first
        picked.append(first)
        sel = jnp.where(pick, 1.0, sel)
        work = jnp.where(pick, neg_inf, work)
    w = sel * scores
    gates = w / jnp.sum(w, axis=0, keepdims=True) * ROUTED_SCALE
    return picked, sel, gates


def _mix_kernel(attn_ref, u_ref, up_ref, un_ref, x_ref, g1_ref, sc2_ref, sh2_ref, wpool_ref,
                pscale_ref, wout_ref, ln_g_ref, ln_b_ref, wr_ref, rb_ref, tri_ref, ones_ref,
                x1_ref, h2_ref, idx_ref, rank_ref, w_ref, cnt_ref, uext_ref, *, seq):
    i = pl.program_id(1)

    @pl.when((pl.program_id(0) == 0) & (i == 0))
    def _():
        cnt_ref[...] = jnp.zeros_like(cnt_ref)

    tile = u_ref.shape[1]
    u = u_ref[0]
    uext_ref[0:POOL_HALO] = jnp.where(i == 0, 0.0, up_ref[0])
    uext_ref[POOL_HALO:POOL_HALO + tile] = u
    uext_ref[POOL_HALO + tile:] = jnp.where(i == pl.num_programs(1) - 1, 0.0, un_ref[0])
    t = i * tile + lax.broadcasted_iota(jnp.int32, (tile, POOL_GC), 0)
    pooled = []
    for g, w in enumerate(POOL_WINDOWS):
        lanes = slice(g * POOL_GC, (g + 1) * POOL_GC)
        tot = uext_ref[POOL_HALO - w // 2:POOL_HALO - w // 2 + tile, lanes]
        for dlt in range(1, w):
            start = POOL_HALO - w // 2 + dlt
            tot = tot + uext_ref[start:start + tile, lanes]
        cnt = (jnp.minimum(t - w // 2 + w, seq) - jnp.maximum(t - w // 2, 0)).astype(F32)
        pg = (tot / cnt - u[:, lanes]).astype(BF16)
        po = jnp.dot(pg, wpool_ref[g], preferred_element_type=F32) * pscale_ref[:, lanes]
        pooled.append(po.astype(BF16))
    mixed = jnp.concatenate([attn_ref[0]] + pooled, axis=1)
    y = jnp.dot(mixed, wout_ref[...], preferred_element_type=F32)
    x1 = _layer_norm(ALPHA * x_ref[0] + g1_ref[0] * y, ln_g_ref[...], ln_b_ref[...])
    x1_ref[0] = x1
    h2 = x1 * (1.0 + sc2_ref[0]) + sh2_ref[0]
    h2_ref[0] = h2
    logits_t = lax.dot_general(wr_ref[...], h2, (((1,), (1,)), ((), ())),
                               preferred_element_type=F32, precision=lax.Precision.HIGHEST)
    picked, sel, gates_t = _route(logits_t, rb_ref[...])
    sel_b = sel.astype(BF16)
    pos_t = jnp.dot(sel_b, tri_ref[...], preferred_element_type=F32) + cnt_ref[:, 0:1]
    cnt_ref[...] += jnp.dot(sel_b, ones_ref[...], preferred_element_type=F32)
    rows = lax.broadcasted_iota(jnp.int32, sel.shape, 0)
    ranks, weights = [], []
    for first in picked:
        hit = rows == first
        ranks.append(jnp.sum(jnp.where(hit, pos_t, 0.0), axis=0, keepdims=True))
        weights.append(jnp.sum(jnp.where(hit, gates_t, 0.0), axis=0, keepdims=True))
    idx_ref[0] = jnp.concatenate(picked, axis=0)
    rank_ref[0] = jnp.concatenate(ranks, axis=0).astype(jnp.int32)
    w_t = jnp.concatenate(weights + [jnp.zeros((LANES - TOP_K, tile), F32)], axis=0)
    w_ref[0] = w_t.T


def _mix(attn, u, x, g1, sc2, sh2, w_pool, pool_scale, w_out, ln_g, ln_b, w_r_t, rb_t):
    b, s, d = x.shape
    tile = MIX_TILE
    hb = tile // POOL_HALO
    row = lambda bi, i: (bi, i, 0)
    vec = lambda bi, i: (bi, 0, 0)
    c2 = lambda bi, i: (0, 0)
    lane_row = lambda bi, i: (bi, 0, i)
    tri = (lax.broadcasted_iota(jnp.int32, (tile, tile), 0)
           < lax.broadcasted_iota(jnp.int32, (tile, tile), 1)).astype(BF16)
    ones = jnp.ones((tile, LANES), BF16)
    return pl.pallas_call(
        functools.partial(_mix_kernel, seq=s),
        grid=(b, s // tile),
        in_specs=[pl.BlockSpec((1, tile, POOL_WIDTH), row),
                  pl.BlockSpec((1, tile, POOL_WIDTH), row),
                  pl.BlockSpec((1, POOL_HALO, POOL_WIDTH),
                               lambda bi, i: (bi, jnp.maximum(i * hb - 1, 0), 0)),
                  pl.BlockSpec((1, POOL_HALO, POOL_WIDTH),
                               lambda bi, i: (bi, jnp.minimum((i + 1) * hb, s // POOL_HALO - 1), 0)),
                  pl.BlockSpec((1, tile, d), row),
                  pl.BlockSpec((1, 1, d), vec), pl.BlockSpec((1, 1, d), vec),
                  pl.BlockSpec((1, 1, d), vec),
                  pl.BlockSpec(w_pool.shape, lambda bi, i: (0, 0, 0)),
                  pl.BlockSpec(pool_scale.shape, c2),
                  pl.BlockSpec(w_out.shape, c2),
                  pl.BlockSpec(ln_g.shape, c2), pl.BlockSpec(ln_b.shape, c2),
                  pl.BlockSpec(w_r_t.shape, c2), pl.BlockSpec(rb_t.shape, c2),
                  pl.BlockSpec(tri.shape, c2), pl.BlockSpec(ones.shape, c2)],
        out_specs=[pl.BlockSpec((1, tile, d), row),
                   pl.BlockSpec((1, tile, d), row),
                   pl.BlockSpec((1, TOP_K, tile), lane_row),
                   pl.BlockSpec((1, TOP_K, tile), lane_row),
                   pl.BlockSpec((1, tile, LANES), row),
                   pl.BlockSpec((N_EXPERTS, LANES), c2)],
        out_shape=[jax.ShapeDtypeStruct((b, s, d), F32),
                   jax.ShapeDtypeStruct((b, s, d), F32),
                   jax.ShapeDtypeStruct((b, TOP_K, s), jnp.int32),
                   jax.ShapeDtypeStruct((b, TOP_K, s), jnp.int32),
                   jax.ShapeDtypeStruct((b, s, LANES), F32),
                   jax.ShapeDtypeStruct((N_EXPERTS, LANES), F32)],
        scratch_shapes=[pltpu.VMEM((tile + 2 * POOL_HALO, POOL_WIDTH), F32)],
        compiler_params=pltpu.CompilerParams(dimension_semantics=("arbitrary", "arbitrary"),
                                             vmem_limit_bytes=VMEM_LIMIT),
        name="mix",
    )(attn, u, u, u, x, g1, sc2, sh2, w_pool, pool_scale, w_out, ln_g, ln_b, w_r_t, rb_t,
      tri, ones)


def _slots_kernel(base_ref, idx_ref, rank_ref, dst_ref):
    idx = idx_ref[0]
    dst = rank_ref[0]
    for e in range(N_EXPERTS):
        dst = dst + jnp.where(idx == e, base_ref[e], 0)
    dst_ref[...] = dst


def _slots(base, idx, rank):
    b, k, s = idx.shape
    return pl.pallas_call(
        _slots_kernel,
        grid_spec=pltpu.PrefetchScalarGridSpec(
            num_scalar_prefetch=1, grid=(b,),
            in_specs=[pl.BlockSpec((1, k, s), lambda bi, base: (bi, 0, 0)),
                      pl.BlockSpec((1, k, s), lambda bi, base: (bi, 0, 0))],
            out_specs=pl.BlockSpec((k, s), lambda bi, base: (0, bi))),
        out_shape=jax.ShapeDtypeStruct((k, b * s), jnp.int32),
        compiler_params=pltpu.CompilerParams(dimension_semantics=("arbitrary",),
                                             vmem_limit_bytes=VMEM_LIMIT),
        name="slots",
    )(base, idx, rank)


def _row_copy(src_hbm, src_row, dst_ref, dst_row, sem):
    return pltpu.make_async_copy(src_hbm.at[pl.ds(src_row, 1)], dst_ref.at[pl.ds(dst_row, 1)], sem)


def _dispatch_kernel(dst_ref, cnt_ref, pcnt_ref, base_ref, nu_ref, h2_hbm, xs_hbm, zero_ref, sem):
    i = pl.program_id(0)
    n_tiles = xs_hbm.shape[0] // MOE_TM

    def pad_blocks(e):
        first = (cnt_ref[e] // SUBLANES) * SUBLANES
        return base_ref[e] + first, (pcnt_ref[e] - first) // SUBLANES

    def pad_copy(row):
        return pltpu.make_async_copy(zero_ref.at[pl.ds(0, SUBLANES)],
                                     xs_hbm.at[pl.ds(pl.multiple_of(row, SUBLANES), SUBLANES)],
                                     sem.at[1])

    def tail_copy(tile):
        return pltpu.make_async_copy(
            zero_ref, xs_hbm.at[pl.ds(pl.multiple_of(tile * MOE_TM, MOE_TM), MOE_TM)], sem.at[1])

    @pl.when(i == 0)
    def _():
        zero_ref[...] = jnp.zeros_like(zero_ref)
        for wait in (False, True):
            def per_tail(j, c):
                cp = tail_copy(j)
                cp.wait() if wait else cp.start()
                return c
            lax.fori_loop(nu_ref[0], n_tiles, per_tail, 0)

            def per_expert(e, c):
                row0, n_blk = pad_blocks(e)

                def per_block(j, c2):
                    cp = pad_copy(row0 + j * SUBLANES)
                    cp.wait() if wait else cp.start()
                    return c2
                return lax.fori_loop(0, n_blk, per_block, c)
            lax.fori_loop(0, N_EXPERTS, per_expert, 0)

    def per_token(t, c):
        tok = i * DISPATCH_CHUNK + t
        for k in range(TOP_K):
            _row_copy(h2_hbm, tok, xs_hbm, dst_ref[k, tok], sem.at[0]).start()
        return c
    lax.fori_loop(0, DISPATCH_CHUNK, per_token, 0)
    rows = TOP_K * DISPATCH_CHUNK
    pltpu.make_async_copy(xs_hbm.at[pl.ds(0, rows)], xs_hbm.at[pl.ds(0, rows)], sem.at[0]).wait()


def _dispatch(dst, cnt, pcnt, base, n_used, h2, n_slots):
    t, d = h2.shape
    return pl.pallas_call(
        _dispatch_kernel,
        grid_spec=pltpu.PrefetchScalarGridSpec(
            num_scalar_prefetch=5, grid=(t // DISPATCH_CHUNK,),
            in_specs=[pl.BlockSpec(memory_space=pl.ANY)],
            out_specs=pl.BlockSpec(memory_space=pl.ANY),
            scratch_shapes=[pltpu.VMEM((MOE_TM, d), F32), pltpu.SemaphoreType.DMA((2,))]),
        out_shape=jax.ShapeDtypeStruct((n_slots, d), F32),
        compiler_params=pltpu.CompilerParams(dimension_semantics=("arbitrary",),
                                             vmem_limit_bytes=VMEM_LIMIT),
        name="dispatch",
    )(dst, cnt, pcnt, base, n_used, h2)


def _experts_kernel(te_ref, nu_ref, xs_ref, wg_ref, wu_ref, wd_ref, ys_ref, wg_b, wu_b, wd_b):
    i = pl.program_id(0)

    @pl.when(i < nu_ref[0])
    def _():
        @pl.when((i == 0) | (te_ref[i] != te_ref[jnp.maximum(i - 1, 0)]))
        def _():
            wg_b[...] = wg_ref[0].astype(BF16)
            wu_b[...] = wu_ref[0].astype(BF16)
            wd_b[...] = wd_ref[0].astype(BF16)

        x = xs_ref[...].astype(BF16)
        a = _silu(jnp.dot(x, wg_b[...], preferred_element_type=F32)) * jnp.dot(
            x, wu_b[...], preferred_element_type=F32)
        ys_ref[...] = jnp.dot(a.astype(BF16), wd_b[...], preferred_element_type=F32)

    @pl.when(i >= nu_ref[0])
    def _():
        ys_ref[...] = jnp.zeros_like(ys_ref)


def _experts(tile_expert, n_used, xs, w_e_gate, w_e_up, w_e_down):
    n_slots, d = xs.shape
    _, _, f = w_e_gate.shape
    slot_tile = lambda i, te, nu: (jnp.minimum(i, nu[0] - 1), 0)
    expert = lambda i, te, nu: (te[i], 0, 0)
    return pl.pallas_call(
        _experts_kernel,
        grid_spec=pltpu.PrefetchScalarGridSpec(
            num_scalar_prefetch=2, grid=(n_slots // MOE_TM,),
            in_specs=[pl.BlockSpec((MOE_TM, d), slot_tile),
                      pl.BlockSpec((1, d, f), expert), pl.BlockSpec((1, d, f), expert),
                      pl.BlockSpec((1, f, d), expert)],
            out_specs=pl.BlockSpec((MOE_TM, d), lambda i, te, nu: (i, 0)),
            scratch_shapes=[pltpu.VMEM((d, f), BF16), pltpu.VMEM((d, f), BF16),
                            pltpu.VMEM((f, d), BF16)]),
        out_shape=jax.ShapeDtypeStruct((n_slots, d), F32),
        compiler_params=pltpu.CompilerParams(dimension_semantics=("arbitrary",),
                                             vmem_limit_bytes=VMEM_LIMIT),
        name="experts",
    )(tile_expert, n_used, xs, w_e_gate, w_e_up, w_e_down)


def _combine_kernel(dst_ref, w_ref, x1_ref, h2_ref, g2_ref, wsg_ref, wsu_ref, wsd_ref, ln_g_ref,
                    ln_b_ref, ys_hbm, o_ref, buf, sem):
    i = pl.program_id(0)
    n = pl.num_programs(0)

    def gather(step, slot):
        def per_token(t, c):
            tok = step * COMBINE_TILE + t
            for k in range(TOP_K):
                _row_copy(ys_hbm, dst_ref[k, tok], buf.at[slot, k], t, sem.at[slot]).start()
            return c
        lax.fori_loop(0, COMBINE_TILE, per_token, 0)

    @pl.when(i == 0)
    def _():
        gather(0, 0)

    @pl.when(i + 1 < n)
    def _():
        gather(i + 1, (i + 1) % 2)

    slot = i % 2
    h = h2_ref[...].astype(BF16)
    a = _silu(jnp.dot(h, wsg_ref[...], preferred_element_type=F32)) * jnp.dot(
        h, wsu_ref[...], preferred_element_type=F32)
    moe = jnp.dot(a.astype(BF16), wsd_ref[...], preferred_element_type=F32)
    pltpu.make_async_copy(buf.at[slot], buf.at[slot], sem.at[slot]).wait()
    w = w_ref[...]
    for k in range(TOP_K):
        moe = moe + w[:, k:k + 1] * buf[slot, k]
    z = ALPHA * x1_ref[...] + g2_ref[0] * moe
    o_ref[...] = _layer_norm(z, ln_g_ref[...], ln_b_ref[...])


def _combine(dst, w, x1, h2, g2, w_s_gate, w_s_up, w_s_down, ln_g, ln_b, ys, seq):
    t, d = x1.shape
    tile = COMBINE_TILE
    row = lambda i, dst: (i, 0)
    c2 = lambda i, dst: (0, 0)
    return pl.pallas_call(
        _combine_kernel,
        grid_spec=pltpu.PrefetchScalarGridSpec(
            num_scalar_prefetch=1, grid=(t // tile,),
            in_specs=[pl.BlockSpec((tile, LANES), row),
                      pl.BlockSpec((tile, d), row),
                      pl.BlockSpec((tile, d), row),
                      pl.BlockSpec((1, 1, d), lambda i, dst: (i * tile // seq, 0, 0)),
                      pl.BlockSpec(w_s_gate.shape, c2), pl.BlockSpec(w_s_up.shape, c2),
                      pl.BlockSpec(w_s_down.shape, c2),
                      pl.BlockSpec(ln_g.shape, c2), pl.BlockSpec(ln_b.shape, c2),
                      pl.BlockSpec(memory_space=pl.ANY)],
            out_specs=pl.BlockSpec((tile, d), row),
            scratch_shapes=[pltpu.VMEM((2, TOP_K, tile, d), F32), pltpu.SemaphoreType.DMA((2,))]),
        out_shape=jax.ShapeDtypeStruct((t, d), F32),
        compiler_params=pltpu.CompilerParams(dimension_semantics=("arbitrary",),
                                             vmem_limit_bytes=VMEM_LIMIT),
        name="combine",
    )(dst, w, x1, h2, g2, w_s_gate, w_s_up, w_s_down, ln_g, ln_b, ys)


def _moe(h2, idx, rank, w, counts, x1, g2, w_e_gate, w_e_up, w_e_down, w_s_gate, w_s_up, w_s_down,
         ln_g, ln_b):
    b, s, d = x1.shape
    t = b * s
    n_tiles = (t * TOP_K) // MOE_TM + N_EXPERTS
    cnt = jnp.round(counts[:, 0]).astype(jnp.int32)
    pcnt = (cnt + MOE_TM - 1) // MOE_TM * MOE_TM
    ends = jnp.cumsum(pcnt)
    base = ends - pcnt
    n_used = (ends[-1] // MOE_TM).reshape(1)
    tile_ids = jnp.arange(n_tiles, dtype=jnp.int32)
    tile_expert = jnp.sum((ends[None, :] <= tile_ids[:, None] * MOE_TM).astype(jnp.int32), axis=1)
    tile_expert = jnp.minimum(tile_expert, N_EXPERTS - 1)
    tile_expert = jnp.where(tile_ids < n_used, tile_expert, tile_expert[n_used[0] - 1])

    dst = _slots(base, idx, rank)
    h2f = h2.reshape(t, d)
    xs = _dispatch(dst, cnt, pcnt, base, n_used, h2f, n_tiles * MOE_TM)
    ys = _experts(tile_expert, n_used, xs, w_e_gate, w_e_up, w_e_down)
    out = _combine(dst, w.reshape(t, LANES), x1.reshape(t, d), h2f, g2, w_s_gate, w_s_up,
                   w_s_down, ln_g, ln_b, ys, s)
    return out.reshape(b, s, d)


def _rope_tables(seq):
    t = jnp.arange(seq)
    pos = jnp.stack([t // GRID_W, t % GRID_W], axis=-1).astype(F32)
    inv_freq = ROPE_THETA ** (-jnp.arange(ROPE_FREQS, dtype=F32) / ROPE_FREQS)
    ang = pos[:, :, None] * inv_freq
    cos, sin = jnp.cos(ang), jnp.sin(ang)
    zero = jnp.zeros_like(sin)
    cos_r = jnp.stack([cos, cos], axis=2).reshape(seq, QK_ROPE)
    sin_lo = jnp.stack([-sin, zero], axis=2).reshape(seq, QK_ROPE)
    sin_hi = jnp.stack([zero, sin], axis=2).reshape(seq, QK_ROPE)
    pads = ((0, 0), (QK_NOPE, HEAD_PAD - QK_NOPE - QK_ROPE))
    lane_tabs = (jnp.pad(cos_r, pads, constant_values=1.0), jnp.pad(sin_lo, pads),
                 jnp.pad(sin_hi, pads))
    row_tabs = (cos.reshape(seq, 2 * ROPE_FREQS).T, sin.reshape(seq, 2 * ROPE_FREQS).T)
    return lane_tabs, row_tabs


def _pad_heads(w, width, padded):
    k = w.shape[0]
    w = jnp.pad(w.reshape(k, N_HEADS, width), ((0, 0), (0, 0), (0, padded - width)))
    return w.reshape(k, N_HEADS * padded)


def kernel(x, c, ctx, c_ctx, w_ada, b_ada, w_in, q_norm_g, w_uq, kv_norm_g, w_ukv, w_pool, pool_scale, w_out, ln1_g, ln1_b, w_router, router_bias, w_e_gate, w_e_up, w_e_down, w_s_gate, w_s_up, w_s_down, ln2_g, ln2_b):
    assert w_ada.shape[0] == 1, "single-layer block"
    b, s, d = x.shape

    cvec = jnp.concatenate([c, c_ctx[None], jnp.zeros((SUBLANES - b - 1, d), F32)], axis=0)
    mod = _ada(cvec, w_ada[0], b_ada)
    sh1, sc1, g1, sh2, sc2, g2 = [mod[:b, k * d:(k + 1) * d][:, None, :] for k in range(6)]
    sh1c, sc1c = [jnp.broadcast_to(mod[b, k * d:(k + 1) * d], (b, 1, d)) for k in range(2)]

    wi = w_in[0]
    kr_cols = jnp.pad(wi[:, Q_LORA + KV_LORA:Q_LORA + KV_LORA + QK_ROPE],
                      ((0, 0), (QK_NOPE, HEAD_PAD - QK_NOPE - QK_ROPE)))
    w_in_r = jnp.concatenate([wi[:, :Q_LORA + KV_LORA], wi[:, Q_LORA + KV_LORA + QK_ROPE:], kr_cols],
                             axis=1).astype(BF16)
    w_uq_t = _pad_heads(w_uq[0], QK_NOPE + QK_ROPE, HEAD_PAD).T.astype(BF16)
    wkv = w_ukv[0].reshape(KV_LORA, N_HEADS, QK_NOPE + V_HEAD)
    w_uk_p = _pad_heads(wkv[:, :, :QK_NOPE].reshape(KV_LORA, -1), QK_NOPE, HEAD_PAD).astype(BF16)
    w_uv_t = _pad_heads(wkv[:, :, QK_NOPE:].reshape(KV_LORA, -1), V_HEAD, V_ROWS).T.astype(BF16)
    tables = _rope_tables(s)

    q_t, k, v_t, u = _proj(x, sc1, sh1, tables, w_in_r, q_norm_g, w_uq_t, kv_norm_g, w_uk_p,
                           w_uv_t, PROJ_TILE)
    kc, vc_t = _proj(ctx, sc1c, sh1c, None, w_in_r, None, None, kv_norm_g, w_uk_p, w_uv_t,
                     ctx.shape[1])
    attn = _attention(q_t, kc, vc_t, k, v_t)

    x1, h2, idx, rank, w, counts = _mix(attn, u, x, g1, sc2, sh2, w_pool[0].astype(BF16),
                                        pool_scale, w_out[0].astype(BF16), ln1_g, ln1_b,
                                        w_router[0].T, router_bias[0][:, None])
    return _moe(h2, idx, rank, w, counts, x1, g2, w_e_gate[0], w_e_up[0], w_e_down[0],
                w_s_gate[0].astype(BF16), w_s_up[0].astype(BF16), w_s_down[0].astype(BF16),
                ln2_g, ln2_b)
```

```python
import functools
import math

import jax
import jax.numpy as jnp
from jax import lax
from jax.experimental import pallas as pl
from jax.experimental.pallas import tpu as pltpu

F32 = jnp.float32
BF16 = jnp.bfloat16

D_MODEL = 1024
GRID_W = 64
N_HEADS = 8
Q_LORA = 512
KV_LORA = 256
QK_NOPE = 64
QK_ROPE = 32
V_HEAD = 64
ROPE_FREQS = QK_ROPE // 4
ROPE_THETA = 10000.0
ATTN_SCALE = 1.0 / math.sqrt(QK_NOPE + QK_ROPE)
LOG2_E = math.log2(math.e)
POOL_GROUPS = 4
POOL_WINDOWS = (2, 4, 8, 16)
POOL_WIDTH = 512
POOL_GC = POOL_WIDTH // POOL_GROUPS
POOL_HALO = 8
N_EXPERTS = 64
N_EXPERT_GROUPS = 8
GROUP_SIZE = N_EXPERTS // N_EXPERT_GROUPS
TOPK_GROUPS = 4
TOP_K = 8
D_EXPERT = 256
ROUTED_SCALE = 2.5
LN_EPS = 1e-5
RMS_EPS = 1e-6
ALPHA = 2.0 ** 0.25

LANES = 128
SUBLANES = 8
HEAD_PAD = LANES
V_ROWS = 80
ONES_ROW = V_HEAD
IN_PAD = Q_LORA + KV_LORA + POOL_WIDTH + LANES

PROJ_TILE = 512
ATTN_TQ = 512
ATTN_TK = PROJ_TILE
ATTN_SUB = 256
ATTN_AHEAD = 2
ATTN_UNROLL = 16
TOK_TILE = 256
MIX_TILE = TOK_TILE
MOE_TM = 512
SEG_ROWS = 16
SEG_WIN = 64
WIN_GROUP = 8
HX_WIDTH = D_MODEL + 2 * N_EXPERTS
VMEM_LIMIT = 48 * 1024 * 1024
NEG_BIG = -1e30


def _silu(v):
    return v * jax.nn.sigmoid(v)


def _layer_norm(z, g, b):
    mu = jnp.mean(z, axis=-1, keepdims=True)
    zc = z - mu
    var = jnp.mean(zc * zc, axis=-1, keepdims=True)
    return zc * lax.rsqrt(var + LN_EPS) * g + b


def _rms_norm(v, g):
    return v * lax.rsqrt(jnp.mean(v * v, axis=-1, keepdims=True) + RMS_EPS) * g


def _dot_nt(a, b):
    return lax.dot_general(a, b, (((1,), (1,)), ((), ())), preferred_element_type=F32)


def _ada_kernel(c_ref, w_ref, b_ref, o_ref):
    cv = _silu(c_ref[...])
    o_ref[...] = jnp.dot(cv, w_ref[...], preferred_element_type=F32,
                         precision=lax.Precision.HIGHEST) + b_ref[...]


def _ada(cvec, w_ada, b_ada):
    rows, d = cvec.shape
    n = w_ada.shape[1]
    tn = 1024
    return pl.pallas_call(
        _ada_kernel,
        grid=(n // tn,),
        in_specs=[pl.BlockSpec((rows, d), lambda j: (0, 0)),
                  pl.BlockSpec((d, tn), lambda j: (0, j)),
                  pl.BlockSpec((1, tn), lambda j: (0, j))],
        out_specs=pl.BlockSpec((rows, tn), lambda j: (0, j)),
        out_shape=jax.ShapeDtypeStruct((rows, n), F32),
        compiler_params=pltpu.CompilerParams(dimension_semantics=("arbitrary",),
                                             vmem_limit_bytes=VMEM_LIMIT),
        name="ada",
    )(cvec, w_ada, b_ada)


def _rope_lanes(v, cos, sin_lo, sin_hi):
    return v * cos + pltpu.roll(v, LANES - 8, axis=1) * sin_lo + pltpu.roll(v, 8, axis=1) * sin_hi


def _proj_kernel(*refs, with_q):
    if with_q:
        (x_ref, sc_ref, sh_ref, cos_ref, slo_ref, shi_ref, cos_t_ref, sin_t_ref, win_ref, qg_ref,
         wuq_ref, kvg_ref, wuk_ref, wuv_ref, q_ref, k_ref, v_ref, u_ref) = refs
    else:
        (x_ref, sc_ref, sh_ref, win_ref, kvg_ref, wuk_ref, wuv_ref, k_ref, v_ref) = refs
    h = (x_ref[0] * (1.0 + sc_ref[0]) + sh_ref[0]).astype(BF16)
    p = jnp.dot(h, win_ref[...], preferred_element_type=F32)
    tile = p.shape[0]
    kr = p[:, IN_PAD - LANES:]
    kvn = _rms_norm(p[:, Q_LORA:Q_LORA + KV_LORA], kvg_ref[...]).astype(BF16)
    kfull = jnp.dot(kvn, wuk_ref[...], preferred_element_type=F32)
    v_t = _dot_nt(wuv_ref[...], kvn)
    row = lax.broadcasted_iota(jnp.int32, (N_HEADS * V_ROWS, 1), 0)
    v_t = v_t + (row % V_ROWS == ONES_ROW).astype(F32)
    if with_q:
        kr = _rope_lanes(kr, cos_ref[...], slo_ref[...], shi_ref[...])
        u_ref[0] = p[:, Q_LORA + KV_LORA:Q_LORA + KV_LORA + POOL_WIDTH]
        qn = _rms_norm(p[:, :Q_LORA], qg_ref[...]).astype(BF16)
        q_t = _dot_nt(wuq_ref[...], qn) * (ATTN_SCALE * LOG2_E)
        cos_t, sin_t = cos_t_ref[...], sin_t_ref[...]
    for hd in range(N_HEADS):
        k_ref[0, hd] = (kfull[:, hd * HEAD_PAD:(hd + 1) * HEAD_PAD] + kr).astype(BF16)
        v_ref[0, hd, 0] = v_t[hd * V_ROWS:(hd + 1) * V_ROWS].astype(BF16)
        if with_q:
            base = hd * HEAD_PAD
            q_ref[0, hd, 0:QK_NOPE, :] = q_t[base:base + QK_NOPE].astype(BF16)
            rope = []
            for ax in range(2):
                lo = q_t[base + QK_NOPE + 16 * ax:base + QK_NOPE + 16 * ax + 8]
                hi = q_t[base + QK_NOPE + 16 * ax + 8:base + QK_NOPE + 16 * ax + 16]
                cs, sn = cos_t[8 * ax:8 * ax + 8], sin_t[8 * ax:8 * ax + 8]
                rope += [lo * cs - hi * sn, hi * cs + lo * sn]
            rope.append(jnp.zeros((HEAD_PAD - QK_NOPE - QK_ROPE, tile), F32))
            q_ref[0, hd, QK_NOPE:, :] = jnp.concatenate(rope, axis=0).astype(BF16)


def _proj(x, sc, sh, tables, w_in_r, q_g, w_uq_t, kv_g, w_uk_p, w_uv_t, tile):
    b, s, d = x.shape
    with_q = tables is not None
    grid = (b, s // tile)
    row = lambda bi, i: (bi, i, 0)
    vec = lambda bi, i: (bi, 0, 0)
    const2 = lambda bi, i: (0, 0)
    k_out = pl.BlockSpec((1, N_HEADS, tile, HEAD_PAD), lambda bi, i: (bi, 0, i, 0))
    k_shape = jax.ShapeDtypeStruct((b, N_HEADS, s, HEAD_PAD), BF16)
    v_out = pl.BlockSpec((1, N_HEADS, 1, V_ROWS, tile), lambda bi, i: (bi, 0, i, 0, 0))
    v_shape = jax.ShapeDtypeStruct((b, N_HEADS, s // tile, V_ROWS, tile), BF16)
    in_specs = [pl.BlockSpec((1, tile, d), row),
                pl.BlockSpec((1, 1, d), vec), pl.BlockSpec((1, 1, d), vec)]
    args = [x, sc, sh]
    if with_q:
        lane_tabs, row_tabs = tables
        in_specs += [pl.BlockSpec((tile, LANES), lambda bi, i: (i, 0))] * 3
        in_specs += [pl.BlockSpec((2 * ROPE_FREQS, tile), lambda bi, i: (0, i))] * 2
        args += list(lane_tabs) + list(row_tabs)
    in_specs.append(pl.BlockSpec(w_in_r.shape, const2)); args.append(w_in_r)
    if with_q:
        in_specs += [pl.BlockSpec(q_g.shape, const2), pl.BlockSpec(w_uq_t.shape, const2)]
        args += [q_g, w_uq_t]
    in_specs += [pl.BlockSpec(kv_g.shape, const2), pl.BlockSpec(w_uk_p.shape, const2),
                 pl.BlockSpec(w_uv_t.shape, const2)]
    args += [kv_g, w_uk_p, w_uv_t]
    if with_q:
        q_out = pl.BlockSpec((1, N_HEADS, HEAD_PAD, tile), lambda bi, i: (bi, 0, 0, i))
        q_shape = jax.ShapeDtypeStruct((b, N_HEADS, HEAD_PAD, s), BF16)
        out_specs = [q_out, k_out, v_out, pl.BlockSpec((1, tile, POOL_WIDTH), row)]
        out_shape = [q_shape, k_shape, v_shape, jax.ShapeDtypeStruct((b, s, POOL_WIDTH), F32)]
    else:
        out_specs = [k_out, v_out]
        out_shape = [k_shape, v_shape]
    return pl.pallas_call(
        functools.partial(_proj_kernel, with_q=with_q),
        grid=grid, in_specs=in_specs, out_specs=out_specs, out_shape=out_shape,
        compiler_params=pltpu.CompilerParams(dimension_semantics=("arbitrary", "arbitrary"),
                                             vmem_limit_bytes=VMEM_LIMIT),
        name="proj" if with_q else "proj_ctx",
    )(*args)


def _attn_kernel(q_ref, kc_ref, vc_ref, k_ref, v_ref, o_ref, *, n_kblk, tk):
    tq = q_ref.shape[3]
    qs = [q_ref[0, hh] for hh in range(2)]

    def scores(hh, kb):
        return jnp.dot(kb, qs[hh], preferred_element_type=F32)

    def update(s_t, vb_t, m, acc):
        m_new = jnp.maximum(m, jnp.max(s_t, axis=0, keepdims=True))
        p_t = jnp.exp2(s_t - m_new).astype(BF16)
        acc = jnp.exp2(m - m_new) * acc + jnp.dot(vb_t, p_t, preferred_element_type=F32)
        return m_new, acc

    def run_items(items, state):
        pending = [scores(hh, kb()) for hh, kb, _ in items[:ATTN_AHEAD]]
        for j, (hh, _, vb) in enumerate(items):
            if j + ATTN_AHEAD < len(items):
                nh, nkb, _ = items[j + ATTN_AHEAD]
                pending.append(scores(nh, nkb()))
            state[hh] = update(pending.pop(0), vb(), *state[hh])
        return state

    def block_items(blk, off):
        out = []
        for sub in range(tk // ATTN_SUB):
            for hh in range(2):
                lo = sub * ATTN_SUB
                out.append((hh,
                            lambda hh=hh, lo=lo: k_ref[0, hh, pl.ds(off + lo, ATTN_SUB), :],
                            lambda hh=hh, lo=lo: v_ref[0, hh, blk, :, lo:lo + ATTN_SUB]))
        return out

    ctx_items = [(hh, lambda hh=hh: kc_ref[0, hh], lambda hh=hh: vc_ref[0, hh, 0])
                 for hh in range(2)]
    state = [(jnp.full((1, tq), NEG_BIG, F32), jnp.zeros((V_ROWS, tq), F32)) for _ in range(2)]
    n_iter = n_kblk // ATTN_UNROLL
    if n_iter == 1:
        items = ctx_items
        for blk in range(n_kblk):
            items = items + block_items(blk, blk * tk)
        state = run_items(items, state)
    else:
        state = run_items(ctx_items, state)

        def body(i, carry):
            items = []
            for r in range(ATTN_UNROLL):
                blk = i * ATTN_UNROLL + r
                items += block_items(blk, pl.multiple_of(blk * tk, tk))
            st = run_items(items, [(carry[0], carry[1]), (carry[2], carry[3])])
            return st[0] + st[1]

        carry = lax.fori_loop(0, n_iter, body, state[0] + state[1])
        state = [(carry[0], carry[1]), (carry[2], carry[3])]
    carry = state[0] + state[1]
    outs = [carry[2 * hh + 1][:V_HEAD] / carry[2 * hh + 1][ONES_ROW:ONES_ROW + 1] for hh in range(2)]
    o_ref[0] = jnp.concatenate(outs, axis=0).T.astype(o_ref.dtype)


def _attention(q_t, kc, vc_t, k, v_t):
    b, nh, dp, s = q_t.shape
    c = kc.shape[2]
    tq, tk = ATTN_TQ, ATTN_TK
    n_kblk = s // tk
    kern = functools.partial(_attn_kernel, n_kblk=n_kblk, tk=tk)
    return pl.pallas_call(
        kern,
        grid=(b, nh // 2, s // tq),
        in_specs=[pl.BlockSpec((1, 2, dp, tq), lambda bi, hp, qi: (bi, hp, 0, qi)),
                  pl.BlockSpec((1, 2, c, dp), lambda bi, hp, qi: (bi, hp, 0, 0)),
                  pl.BlockSpec((1, 2, 1, V_ROWS, c), lambda bi, hp, qi: (bi, hp, 0, 0, 0)),
                  pl.BlockSpec((1, 2, s, dp), lambda bi, hp, qi: (bi, hp, 0, 0)),
                  pl.BlockSpec((1, 2, n_kblk, V_ROWS, tk), lambda bi, hp, qi: (bi, hp, 0, 0, 0))],
        out_specs=pl.BlockSpec((1, tq, 2 * V_HEAD), lambda bi, hp, qi: (bi, qi, hp)),
        out_shape=jax.ShapeDtypeStruct((b, s, nh * V_HEAD), BF16),
        compiler_params=pltpu.CompilerParams(
            dimension_semantics=("arbitrary", "arbitrary", "arbitrary"),
            vmem_limit_bytes=VMEM_LIMIT),
        name="attn",
    )(q_t, kc, vc_t, k, v_t)


def _route(logits_t, bias_t):
    e, t = logits_t.shape
    scores = jax.nn.sigmoid(logits_t)
    biased = scores + bias_t
    neg_inf = F32(-jnp.inf)
    gscore = []
    for g in range(N_EXPERT_GROUPS):
        v = biased[g * GROUP_SIZE:(g + 1) * GROUP_SIZE]
        m1 = jnp.max(v, axis=0, keepdims=True)
        at_max = v == m1
        n_max = jnp.sum(at_max.astype(F32), axis=0, keepdims=True)
        m2 = jnp.max(jnp.where(at_max, neg_inf, v), axis=0, keepdims=True)
        gscore.append(m1 + jnp.where(n_max >= 2.0, m1, m2))
    masked = []
    for g in range(N_EXPERT_GROUPS):
        rank = jnp.zeros((1, t), F32)
        for o in range(N_EXPERT_GROUPS):
            if o == g:
                continue
            beats = (gscore[o] >= gscore[g]) if o < g else (gscore[o] > gscore[g])
            rank = rank + beats.astype(F32)
        keep = rank < float(TOPK_GROUPS)
        masked.append(jnp.where(keep, biased[g * GROUP_SIZE:(g + 1) * GROUP_SIZE], neg_inf))
    work = jnp.concatenate(masked, axis=0)
    rows = lax.broadcasted_iota(jnp.int32, (e, t), 0)
    sel = jnp.zeros((e, t), F32)
    for _ in range(TOP_K):
        m = jnp.max(work, axis=0, keepdims=True)
        first = jnp.min(jnp.where(work == m, rows, e), axis=0, keepdims=True)
        pick = rows == first
        sel = jnp.where(pick, 1.0, sel)
        work = jnp.where(pick, neg_inf, work)
    w = sel * scores
    gates = w / jnp.sum(w, axis=0, keepdims=True) * ROUTED_SCALE
    return sel, gates


def _mix_kernel(attn_ref, u_ref, up_ref, un_ref, x_ref, g1_ref, sc2_ref, sh2_ref, wpool_ref,
                pscale_ref, wout_ref, ln_g_ref, ln_b_ref, wr_ref, rb_ref, tri_ref, ones_ref,
                x1_ref, hx_ref, pos_ref, seg_ref, cnt_ref, uext_ref, *, seq):
    i = pl.program_id(1)

    @pl.when((pl.program_id(0) == 0) & (i == 0))
    def _():
        cnt_ref[...] = jnp.zeros_like(cnt_ref)

    tile = u_ref.shape[1]
    u = u_ref[0]
    uext_ref[0:POOL_HALO] = jnp.where(i == 0, 0.0, up_ref[0])
    uext_ref[POOL_HALO:POOL_HALO + tile] = u
    uext_ref[POOL_HALO + tile:] = jnp.where(i == pl.num_programs(1) - 1, 0.0, un_ref[0])
    t = i * tile + lax.broadcasted_iota(jnp.int32, (tile, POOL_GC), 0)
    pooled = []
    for g, w in enumerate(POOL_WINDOWS):
        lanes = slice(g * POOL_GC, (g + 1) * POOL_GC)
        tot = uext_ref[POOL_HALO - w // 2:POOL_HALO - w // 2 + tile, lanes]
        for dlt in range(1, w):
            start = POOL_HALO - w // 2 + dlt
            tot = tot + uext_ref[start:start + tile, lanes]
        cnt = (jnp.minimum(t - w // 2 + w, seq) - jnp.maximum(t - w // 2, 0)).astype(F32)
        pg = (tot / cnt - u[:, lanes]).astype(BF16)
        po = jnp.dot(pg, wpool_ref[g], preferred_element_type=F32) * pscale_ref[:, lanes]
        pooled.append(po.astype(BF16))
    mixed = jnp.concatenate([attn_ref[0]] + pooled, axis=1)
    y = jnp.dot(mixed, wout_ref[...], preferred_element_type=F32)
    x1 = _layer_norm(ALPHA * x_ref[0] + g1_ref[0] * y, ln_g_ref[...], ln_b_ref[...])
    x1_ref[0] = x1
    h2 = x1 * (1.0 + sc2_ref[0]) + sh2_ref[0]
    logits_t = lax.dot_general(wr_ref[...], h2, (((1,), (1,)), ((), ())),
                               preferred_element_type=F32, precision=lax.Precision.HIGHEST)
    sel, gates_t = _route(logits_t, rb_ref[...])
    g_hi = gates_t.astype(BF16)
    g_lo = (gates_t - g_hi.astype(F32)).astype(BF16)
    g_tok = jnp.concatenate([g_hi.astype(F32), g_lo.astype(F32)], axis=0).T
    hx_ref[0] = jnp.concatenate([h2.astype(BF16), g_tok.astype(BF16)], axis=1)
    sel_b = sel.astype(BF16)
    start = cnt_ref[...]
    pos_t = jnp.dot(sel_b, tri_ref[...], preferred_element_type=F32) + start[:, 0:1]
    pos_ref[0] = jnp.where(sel > 0.0, pos_t, -1.0).astype(jnp.int32)
    n_tok = jnp.dot(sel_b, ones_ref[...], preferred_element_type=F32)
    n_chunk = jnp.floor((n_tok + (SEG_ROWS - 1)) * (1.0 / SEG_ROWS))
    seg_ref[0, 0] = start.astype(jnp.int32)
    seg_ref[0, 1] = n_chunk.astype(jnp.int32)
    cnt_ref[...] = start + n_chunk * SEG_ROWS


def _mix(attn, u, x, g1, sc2, sh2, w_pool, pool_scale, w_out, ln_g, ln_b, w_r_t, rb_t):
    b, s, d = x.shape
    tile = MIX_TILE
    hb = tile // POOL_HALO
    row = lambda bi, i: (bi, i, 0)
    vec = lambda bi, i: (bi, 0, 0)
    c2 = lambda bi, i: (0, 0)
    lane_row = lambda bi, i: (bi, 0, i)
    tri = (lax.broadcasted_iota(jnp.int32, (tile, tile), 0)
           < lax.broadcasted_iota(jnp.int32, (tile, tile), 1)).astype(BF16)
    ones = jnp.ones((tile, LANES), BF16)
    return pl.pallas_call(
        functools.partial(_mix_kernel, seq=s),
        grid=(b, s // tile),
        in_specs=[pl.BlockSpec((1, tile, POOL_WIDTH), row),
                  pl.BlockSpec((1, tile, POOL_WIDTH), row),
                  pl.BlockSpec((1, POOL_HALO, POOL_WIDTH),
                               lambda bi, i: (bi, jnp.maximum(i * hb - 1, 0), 0)),
                  pl.BlockSpec((1, POOL_HALO, POOL_WIDTH),
                               lambda bi, i: (bi, jnp.minimum((i + 1) * hb, s // POOL_HALO - 1), 0)),
                  pl.BlockSpec((1, tile, d), row),
                  pl.BlockSpec((1, 1, d), vec), pl.BlockSpec((1, 1, d), vec),
                  pl.BlockSpec((1, 1, d), vec),
                  pl.BlockSpec(w_pool.shape, lambda bi, i: (0, 0, 0)),
                  pl.BlockSpec(pool_scale.shape, c2),
                  pl.BlockSpec(w_out.shape, c2),
                  pl.BlockSpec(ln_g.shape, c2), pl.BlockSpec(ln_b.shape, c2),
                  pl.BlockSpec(w_r_t.shape, c2), pl.BlockSpec(rb_t.shape, c2),
                  pl.BlockSpec(tri.shape, c2), pl.BlockSpec(ones.shape, c2)],
        out_specs=[pl.BlockSpec((1, tile, d), row),
                   pl.BlockSpec((1, tile, HX_WIDTH), row),
                   pl.BlockSpec((1, N_EXPERTS, tile), lane_row),
                   pl.BlockSpec((1, 2, N_EXPERTS, LANES),
                                lambda bi, i: (bi * (s // tile) + i, 0, 0, 0)),
                   pl.BlockSpec((N_EXPERTS, LANES), c2)],
        out_shape=[jax.ShapeDtypeStruct((b, s, d), F32),
                   jax.ShapeDtypeStruct((b, s, HX_WIDTH), BF16),
                   jax.ShapeDtypeStruct((b, N_EXPERTS, s), jnp.int32),
                   jax.ShapeDtypeStruct((b * (s // tile), 2, N_EXPERTS, LANES), jnp.int32),
                   jax.ShapeDtypeStruct((N_EXPERTS, LANES), F32)],
        scratch_shapes=[pltpu.VMEM((tile + 2 * POOL_HALO, POOL_WIDTH), F32)],
        compiler_params=pltpu.CompilerParams(dimension_semantics=("arbitrary", "arbitrary"),
                                             vmem_limit_bytes=VMEM_LIMIT),
        name="mix",
    )(attn, u, u, u, x, g1, sc2, sh2, w_pool, pool_scale, w_out, ln_g, ln_b, w_r_t, rb_t,
      tri, ones)


def _one_hot(pos_row, start, window, n_tok):
    rows = lax.broadcasted_iota(jnp.int32, (SEG_WIN, n_tok), 0)
    hit = rows == (pos_row - (start + window * SEG_WIN))
    return jnp.where(hit, 1.0, 0.0).astype(BF16)


def _window_rows(base_ref, start_ref, step, e, window):
    row = base_ref[e] + start_ref[step * N_EXPERTS + e] + window * SEG_WIN
    return pl.ds(pl.multiple_of(row, SEG_ROWS), SEG_WIN)


def _extra_windows(nchunk_ref, step, e):
    rows = nchunk_ref[step * N_EXPERTS + e] * SEG_ROWS
    return jnp.maximum((rows + SEG_WIN - 1) // SEG_WIN, 1)


def _dispatch_kernel(base_ref, start_ref, nchunk_ref, total_ref, pcnt_ref, nu_ref,
                     hx_ref, pos_ref, xs_hbm, stage, extra, zero_ref, sem):
    i = pl.program_id(0)
    n_tiles = xs_hbm.shape[0] // MOE_TM
    n_tok = hx_ref.shape[0]

    def zero_copy(row, n_rows):
        return pltpu.make_async_copy(zero_ref.at[pl.ds(0, n_rows)],
                                     xs_hbm.at[pl.ds(pl.multiple_of(row, SEG_ROWS), n_rows)],
                                     sem.at[1])

    @pl.when(i == 0)
    def _():
        zero_ref[...] = jnp.zeros_like(zero_ref)
        for wait in (False, True):
            def per_tail(j, c):
                cp = zero_copy(j * MOE_TM, MOE_TM)
                cp.wait() if wait else cp.start()
                return c
            lax.fori_loop(nu_ref[0], n_tiles, per_tail, 0)

            def per_expert(e, c):
                row0 = base_ref[e] + total_ref[e]

                def per_block(j, c2):
                    cp = zero_copy(row0 + j * SEG_ROWS, SEG_ROWS)
                    cp.wait() if wait else cp.start()
                    return c2
                return lax.fori_loop(0, (pcnt_ref[e] - total_ref[e]) // SEG_ROWS, per_block, c)
            lax.fori_loop(0, N_EXPERTS, per_expert, 0)

    hx = hx_ref[...]
    for g in range(N_EXPERTS // WIN_GROUP):
        oh = jnp.concatenate(
            [_one_hot(pos_ref[0, e:e + 1, :], start_ref[i * N_EXPERTS + e], 0, n_tok)
             for e in range(g * WIN_GROUP, (g + 1) * WIN_GROUP)], axis=0)
        rows = slice(g * WIN_GROUP * SEG_WIN, (g + 1) * WIN_GROUP * SEG_WIN)
        stage[rows, :] = jnp.dot(oh, hx, preferred_element_type=F32).astype(BF16)

    def first_window(e, c):
        pltpu.make_async_copy(stage.at[pl.ds(pl.multiple_of(e * SEG_WIN, SEG_WIN), SEG_WIN)],
                              xs_hbm.at[_window_rows(base_ref, start_ref, i, e, 0)],
                              sem.at[0]).start()
        return c
    lax.fori_loop(0, N_EXPERTS, first_window, 0)
    pltpu.make_async_copy(stage, xs_hbm.at[pl.ds(0, N_EXPERTS * SEG_WIN)], sem.at[0]).wait()

    def more_windows(e, c):
        def one(window, c2):
            oh = _one_hot(pos_ref[0, pl.ds(e, 1), :], start_ref[i * N_EXPERTS + e], window, n_tok)
            extra[...] = jnp.dot(oh, hx, preferred_element_type=F32).astype(BF16)
            cp = pltpu.make_async_copy(extra, xs_hbm.at[_window_rows(base_ref, start_ref, i, e, window)],
                                       sem.at[0])
            cp.start()
            cp.wait()
            return c2
        return lax.fori_loop(1, _extra_windows(nchunk_ref, i, e), one, c)
    lax.fori_loop(0, N_EXPERTS, more_windows, 0)


def _dispatch(base, seg_start, seg_chunks, total, pcnt, n_used, hx, pos, n_slots):
    t, width = hx.shape
    tpb = pos.shape[2] // TOK_TILE
    return pl.pallas_call(
        _dispatch_kernel,
        grid_spec=pltpu.PrefetchScalarGridSpec(
            num_scalar_prefetch=6, grid=(t // TOK_TILE,),
            in_specs=[pl.BlockSpec((TOK_TILE, width), lambda i, *_: (i, 0)),
                      pl.BlockSpec((1, N_EXPERTS, TOK_TILE), lambda i, *_: (i // tpb, 0, i % tpb))],
            out_specs=pl.BlockSpec(memory_space=pl.ANY),
            scratch_shapes=[pltpu.VMEM((N_EXPERTS * SEG_WIN, width), BF16),
                            pltpu.VMEM((SEG_WIN, width), BF16),
                            pltpu.VMEM((MOE_TM, width), BF16),
                            pltpu.SemaphoreType.DMA((2,))]),
        out_shape=jax.ShapeDtypeStruct((n_slots, width), BF16),
        compiler_params=pltpu.CompilerParams(dimension_semantics=("arbitrary",),
                                             vmem_limit_bytes=VMEM_LIMIT),
        name="dispatch",
    )(base, seg_start, seg_chunks, total, pcnt, n_used, hx, pos)


def _experts_kernel(te_ref, nu_ref, xs_ref, wg_ref, wu_ref, wd_ref, ys_ref, wg_b, wu_b, wd_b):
    i = pl.program_id(0)

    @pl.when(i < nu_ref[0])
    def _():
        @pl.when((i == 0) | (te_ref[i] != te_ref[jnp.maximum(i - 1, 0)]))
        def _():
            wg_b[...] = wg_ref[0].astype(BF16)
            wu_b[...] = wu_ref[0].astype(BF16)
            wd_b[...] = wd_ref[0].astype(BF16)

        x = xs_ref[:, :D_MODEL]
        g = xs_ref[:, D_MODEL:].astype(F32)
        lane = lax.broadcasted_iota(jnp.int32, g.shape, 1)
        mine = (lane == te_ref[i]) | (lane == te_ref[i] + N_EXPERTS)
        gate = jnp.sum(jnp.where(mine, g, 0.0), axis=1, keepdims=True)
        a = _silu(jnp.dot(x, wg_b[...], preferred_element_type=F32)) * jnp.dot(
            x, wu_b[...], preferred_element_type=F32)
        ys_ref[...] = jnp.dot((a * gate).astype(BF16), wd_b[...],
                              preferred_element_type=F32).astype(ys_ref.dtype)

    @pl.when(i >= nu_ref[0])
    def _():
        ys_ref[...] = jnp.zeros_like(ys_ref)


def _experts(tile_expert, n_used, xs, w_e_gate, w_e_up, w_e_down):
    n_slots, width = xs.shape
    _, d, f = w_e_gate.shape
    slot_tile = lambda i, te, nu: (jnp.minimum(i, nu[0] - 1), 0)
    expert = lambda i, te, nu: (te[i], 0, 0)
    return pl.pallas_call(
        _experts_kernel,
        grid_spec=pltpu.PrefetchScalarGridSpec(
            num_scalar_prefetch=2, grid=(n_slots // MOE_TM,),
            in_specs=[pl.BlockSpec((MOE_TM, width), slot_tile),
                      pl.BlockSpec((1, d, f), expert), pl.BlockSpec((1, d, f), expert),
                      pl.BlockSpec((1, f, d), expert)],
            out_specs=pl.BlockSpec((MOE_TM, d), lambda i, te, nu: (i, 0)),
            scratch_shapes=[pltpu.VMEM((d, f), BF16), pltpu.VMEM((d, f), BF16),
                            pltpu.VMEM((f, d), BF16)]),
        out_shape=jax.ShapeDtypeStruct((n_slots, d), BF16),
        compiler_params=pltpu.CompilerParams(dimension_semantics=("arbitrary",),
                                             vmem_limit_bytes=VMEM_LIMIT),
        name="experts",
    )(tile_expert, n_used, xs, w_e_gate, w_e_up, w_e_down)


def _dot_tn(a, b):
    return lax.dot_general(a, b, (((0,), (0,)), ((), ())), preferred_element_type=F32)


def _combine_kernel(base_ref, start_ref, nchunk_ref, hx_ref, pos_ref, x1_ref, g2_ref, wsg_ref,
                    wsu_ref, wsd_ref, ln_g_ref, ln_b_ref, ys_hbm, o_ref, win, extra, acc_ref, sem):
    i = pl.program_id(0)
    n = pl.num_programs(0)
    n_tok = hx_ref.shape[0]

    def fetch(step, slot):
        def one(e, c):
            pltpu.make_async_copy(
                ys_hbm.at[_window_rows(base_ref, start_ref, step, e, 0)],
                win.at[slot, pl.ds(pl.multiple_of(e * SEG_WIN, SEG_WIN), SEG_WIN)],
                sem.at[slot]).start()
            return c
        lax.fori_loop(0, N_EXPERTS, one, 0)

    @pl.when(i == 0)
    def _():
        fetch(0, 0)

    @pl.when(i + 1 < n)
    def _():
        fetch(i + 1, (i + 1) % 2)

    slot = i % 2
    h = hx_ref[:, :D_MODEL]
    a = _silu(jnp.dot(h, wsg_ref[...], preferred_element_type=F32)) * jnp.dot(
        h, wsu_ref[...], preferred_element_type=F32)
    moe = jnp.dot(a.astype(BF16), wsd_ref[...], preferred_element_type=F32)
    pltpu.make_async_copy(ys_hbm.at[pl.ds(0, N_EXPERTS * SEG_WIN)], win.at[slot], sem.at[slot]).wait()
    for g in range(N_EXPERTS // WIN_GROUP):
        oh = jnp.concatenate(
            [_one_hot(pos_ref[0, e:e + 1, :], start_ref[i * N_EXPERTS + e], 0, n_tok)
             for e in range(g * WIN_GROUP, (g + 1) * WIN_GROUP)], axis=0)
        rows = pl.ds(g * WIN_GROUP * SEG_WIN, WIN_GROUP * SEG_WIN)
        moe = moe + _dot_tn(oh, win[slot, rows, :])
    acc_ref[...] = moe

    def more_windows(e, c):
        def one(window, c2):
            cp = pltpu.make_async_copy(ys_hbm.at[_window_rows(base_ref, start_ref, i, e, window)],
                                       extra, sem.at[2])
            cp.start()
            cp.wait()
            oh = _one_hot(pos_ref[0, pl.ds(e, 1), :], start_ref[i * N_EXPERTS + e], window, n_tok)
            acc_ref[...] += _dot_tn(oh, extra[...])
            return c2
        return lax.fori_loop(1, _extra_windows(nchunk_ref, i, e), one, c)
    lax.fori_loop(0, N_EXPERTS, more_windows, 0)

    z = ALPHA * x1_ref[...] + g2_ref[0] * acc_ref[...]
    o_ref[...] = _layer_norm(z, ln_g_ref[...], ln_b_ref[...])


def _combine(base, seg_start, seg_chunks, hx, pos, x1, g2, w_s_gate, w_s_up, w_s_down, ln_g, ln_b,
             ys):
    t, d = x1.shape
    tpb = pos.shape[2] // TOK_TILE
    row = lambda i, *_: (i, 0)
    c2 = lambda i, *_: (0, 0)
    return pl.pallas_call(
        _combine_kernel,
        grid_spec=pltpu.PrefetchScalarGridSpec(
            num_scalar_prefetch=3, grid=(t // TOK_TILE,),
            in_specs=[pl.BlockSpec((TOK_TILE, hx.shape[1]), row),
                      pl.BlockSpec((1, N_EXPERTS, TOK_TILE), lambda i, *_: (i // tpb, 0, i % tpb)),
                      pl.BlockSpec((TOK_TILE, d), row),
                      pl.BlockSpec((1, 1, d), lambda i, *_: (i // tpb, 0, 0)),
                      pl.BlockSpec(w_s_gate.shape, c2), pl.BlockSpec(w_s_up.shape, c2),
                      pl.BlockSpec(w_s_down.shape, c2),
                      pl.BlockSpec(ln_g.shape, c2), pl.BlockSpec(ln_b.shape, c2),
                      pl.BlockSpec(memory_space=pl.ANY)],
            out_specs=pl.BlockSpec((TOK_TILE, d), row),
            scratch_shapes=[pltpu.VMEM((2, N_EXPERTS * SEG_WIN, d), BF16),
                            pltpu.VMEM((SEG_WIN, d), BF16),
                            pltpu.VMEM((TOK_TILE, d), F32),
                            pltpu.SemaphoreType.DMA((3,))]),
        out_shape=jax.ShapeDtypeStruct((t, d), F32),
        compiler_params=pltpu.CompilerParams(dimension_semantics=("arbitrary",),
                                             vmem_limit_bytes=VMEM_LIMIT),
        name="combine",
    )(base, seg_start, seg_chunks, hx, pos, x1, g2, w_s_gate, w_s_up, w_s_down, ln_g, ln_b, ys)


def _moe(hx, pos, seg, counts, x1, g2, w_e_gate, w_e_up, w_e_down, w_s_gate, w_s_up, w_s_down,
         ln_g, ln_b):
    b, s, d = x1.shape
    t = b * s
    n_seg = (t // TOK_TILE) * N_EXPERTS
    max_rows = t * TOP_K + n_seg * (SEG_ROWS - 1) + N_EXPERTS * (SEG_WIN + MOE_TM)
    n_tiles = -(-max_rows // MOE_TM)
    total = jnp.round(counts[:, 0]).astype(jnp.int32)
    pcnt = (total + SEG_WIN + MOE_TM - 1) // MOE_TM * MOE_TM
    ends = jnp.cumsum(pcnt)
    base = ends - pcnt
    n_used = (ends[-1] // MOE_TM).reshape(1)
    tile_ids = jnp.arange(n_tiles, dtype=jnp.int32)
    tile_expert = jnp.sum((ends[None, :] <= tile_ids[:, None] * MOE_TM).astype(jnp.int32), axis=1)
    tile_expert = jnp.minimum(tile_expert, N_EXPERTS - 1)
    tile_expert = jnp.where(tile_ids < n_used, tile_expert, tile_expert[n_used[0] - 1])
    seg_start = seg[:, 0, :, 0].reshape(n_seg)
    seg_chunks = seg[:, 1, :, 0].reshape(n_seg)

    hxf = hx.reshape(t, hx.shape[2])
    xs = _dispatch(base, seg_start, seg_chunks, total, pcnt, n_used, hxf, pos, n_tiles * MOE_TM)
    ys = _experts(tile_expert, n_used, xs, w_e_gate, w_e_up, w_e_down)
    out = _combine(base, seg_start, seg_chunks, hxf, pos, x1.reshape(t, d), g2, w_s_gate, w_s_up,
                   w_s_down, ln_g, ln_b, ys)
    return out.reshape(b, s, d)


def _rope_tables(seq):
    t = jnp.arange(seq)
    pos = jnp.stack([t // GRID_W, t % GRID_W], axis=-1).astype(F32)
    inv_freq = ROPE_THETA ** (-jnp.arange(ROPE_FREQS, dtype=F32) / ROPE_FREQS)
    ang = pos[:, :, None] * inv_freq
    cos, sin = jnp.cos(ang), jnp.sin(ang)
    zero = jnp.zeros_like(sin)
    cos_r = jnp.stack([cos, cos], axis=2).reshape(seq, QK_ROPE)
    sin_lo = jnp.stack([-sin, zero], axis=2).reshape(seq, QK_ROPE)
    sin_hi = jnp.stack([zero, sin], axis=2).reshape(seq, QK_ROPE)
    pads = ((0, 0), (QK_NOPE, HEAD_PAD - QK_NOPE - QK_ROPE))
    lane_tabs = (jnp.pad(cos_r, pads, constant_values=1.0), jnp.pad(sin_lo, pads),
                 jnp.pad(sin_hi, pads))
    row_tabs = (cos.reshape(seq, 2 * ROPE_FREQS).T, sin.reshape(seq, 2 * ROPE_FREQS).T)
    return lane_tabs, row_tabs


def _pad_heads(w, width, padded):
    k = w.shape[0]
    w = jnp.pad(w.reshape(k, N_HEADS, width), ((0, 0), (0, 0), (0, padded - width)))
    return w.reshape(k, N_HEADS * padded)


def kernel(x, c, ctx, c_ctx, w_ada, b_ada, w_in, q_norm_g, w_uq, kv_norm_g, w_ukv, w_pool, pool_scale, w_out, ln1_g, ln1_b, w_router, router_bias, w_e_gate, w_e_up, w_e_down, w_s_gate, w_s_up, w_s_down, ln2_g, ln2_b):
    assert w_ada.shape[0] == 1, "single-layer block"
    b, s, d = x.shape

    cvec = jnp.concatenate([c, c_ctx[None], jnp.zeros((SUBLANES - b - 1, d), F32)], axis=0)
    mod = _ada(cvec, w_ada[0], b_ada)
    sh1, sc1, g1, sh2, sc2, g2 = [mod[:b, k * d:(k + 1) * d][:, None, :] for k in range(6)]
    sh1c, sc1c = [jnp.broadcast_to(mod[b, k * d:(k + 1) * d], (b, 1, d)) for k in range(2)]

    wi = w_in[0]
    kr_cols = jnp.pad(wi[:, Q_LORA + KV_LORA:Q_LORA + KV_LORA + QK_ROPE],
                      ((0, 0), (QK_NOPE, HEAD_PAD - QK_NOPE - QK_ROPE)))
    w_in_r = jnp.concatenate([wi[:, :Q_LORA + KV_LORA], wi[:, Q_LORA + KV_LORA + QK_ROPE:], kr_cols],
                             axis=1).astype(BF16)
    w_uq_t = _pad_heads(w_uq[0], QK_NOPE + QK_ROPE, HEAD_PAD).T.astype(BF16)
    wkv = w_ukv[0].reshape(KV_LORA, N_HEADS, QK_NOPE + V_HEAD)
    w_uk_p = _pad_heads(wkv[:, :, :QK_NOPE].reshape(KV_LORA, -1), QK_NOPE, HEAD_PAD).astype(BF16)
    w_uv_t = _pad_heads(wkv[:, :, QK_NOPE:].reshape(KV_LORA, -1), V_HEAD, V_ROWS).T.astype(BF16)
    tables = _rope_tables(s)

    q_t, k, v_t, u = _proj(x, sc1, sh1, tables, w_in_r, q_norm_g, w_uq_t, kv_norm_g, w_uk_p,
                           w_uv_t, PROJ_TILE)
    kc, vc_t = _proj(ctx, sc1c, sh1c, None, w_in_r, None, None, kv_norm_g, w_uk_p, w_uv_t,
                     ctx.shape[1])
    attn = _attention(q_t, kc, vc_t, k, v_t)

    x1, hx, pos, seg, counts = _mix(attn, u, x, g1, sc2, sh2, w_pool[0].astype(BF16),
                                    pool_scale, w_out[0].astype(BF16), ln1_g, ln1_b,
                                    w_router[0].T, router_bias[0][:, None])
    return _moe(hx, pos, seg, counts, x1, g2, w_e_gate[0], w_e_up[0], w_e_down[0],
                w_s_gate[0].astype(BF16), w_s_up[0].astype(BF16), w_s_down[0].astype(BF16),
                ln2_g, ln2_b)
```

```python
import functools
import math

import jax
import jax.numpy as jnp
from jax import lax
from jax.experimental import pallas as pl
from jax.experimental.pallas import tpu as pltpu

F32 = jnp.float32
BF16 = jnp.bfloat16

D_MODEL = 1024
GRID_W = 64
N_HEADS = 8
Q_LORA = 512
KV_LORA = 256
QK_NOPE = 64
QK_ROPE = 32
V_HEAD = 64
ROPE_FREQS = QK_ROPE // 4
ROPE_THETA = 10000.0
ATTN_SCALE = 1.0 / math.sqrt(QK_NOPE + QK_ROPE)
LOG2_E = math.log2(math.e)
POOL_GROUPS = 4
POOL_WINDOWS = (2, 4, 8, 16)
POOL_WIDTH = 512
POOL_GC = POOL_WIDTH // POOL_GROUPS
POOL_HALO = 8
N_EXPERTS = 64
N_EXPERT_GROUPS = 8
GROUP_SIZE = N_EXPERTS // N_EXPERT_GROUPS
TOPK_GROUPS = 4
TOP_K = 8
D_EXPERT = 256
ROUTED_SCALE = 2.5
LN_EPS = 1e-5
RMS_EPS = 1e-6
ALPHA = 2.0 ** 0.25

LANES = 128
SUBLANES = 8
HEAD_PAD = LANES
V_ROWS = 80
ONES_ROW = V_HEAD
IN_PAD = Q_LORA + KV_LORA + POOL_WIDTH + LANES

PROJ_TILE = 512
ATTN_TQ = 512
ATTN_TK = PROJ_TILE
ATTN_SUB = 256
ATTN_AHEAD = 2
ATTN_UNROLL = 16
TOK_TILE = 256
MIX_TILE = TOK_TILE
MOE_TM = 1024
SEG_ROWS = 16
SEG_WIN = 48
WIN_GROUP = 16
HX_WIDTH = D_MODEL + 2 * N_EXPERTS
VMEM_LIMIT = 48 * 1024 * 1024
NEG_BIG = -1e30


def _silu(v):
    return v * jax.nn.sigmoid(v)


def _layer_norm(z, g, b):
    mu = jnp.mean(z, axis=-1, keepdims=True)
    zc = z - mu
    var = jnp.mean(zc * zc, axis=-1, keepdims=True)
    return zc * lax.rsqrt(var + LN_EPS) * g + b


def _rms_norm(v, g):
    return v * lax.rsqrt(jnp.mean(v * v, axis=-1, keepdims=True) + RMS_EPS) * g


def _dot_nt(a, b):
    return lax.dot_general(a, b, (((1,), (1,)), ((), ())), preferred_element_type=F32)


def _ada_kernel(c_ref, w_ref, b_ref, o_ref):
    cv = _silu(c_ref[...])
    o_ref[...] = jnp.dot(cv, w_ref[...], preferred_element_type=F32,
                         precision=lax.Precision.HIGHEST) + b_ref[...]


def _ada(cvec, w_ada, b_ada):
    rows, d = cvec.shape
    n = w_ada.shape[1]
    tn = 1024
    return pl.pallas_call(
        _ada_kernel,
        grid=(n // tn,),
        in_specs=[pl.BlockSpec((rows, d), lambda j: (0, 0)),
                  pl.BlockSpec((d, tn), lambda j: (0, j)),
                  pl.BlockSpec((1, tn), lambda j: (0, j))],
        out_specs=pl.BlockSpec((rows, tn), lambda j: (0, j)),
        out_shape=jax.ShapeDtypeStruct((rows, n), F32),
        compiler_params=pltpu.CompilerParams(dimension_semantics=("arbitrary",),
                                             vmem_limit_bytes=VMEM_LIMIT),
        name="ada",
    )(cvec, w_ada, b_ada)


def _rope_lanes(v, cos, sin_lo, sin_hi):
    return v * cos + pltpu.roll(v, LANES - 8, axis=1) * sin_lo + pltpu.roll(v, 8, axis=1) * sin_hi


def _proj_kernel(*refs, with_q):
    if with_q:
        (x_ref, sc_ref, sh_ref, cos_ref, slo_ref, shi_ref, cos_t_ref, sin_t_ref, win_ref, qg_ref,
         wuq_ref, kvg_ref, wuk_ref, wuv_ref, q_ref, k_ref, v_ref, u_ref) = refs
    else:
        (x_ref, sc_ref, sh_ref, win_ref, kvg_ref, wuk_ref, wuv_ref, k_ref, v_ref) = refs
    h = (x_ref[0] * (1.0 + sc_ref[0]) + sh_ref[0]).astype(BF16)
    p = jnp.dot(h, win_ref[...], preferred_element_type=F32)
    tile = p.shape[0]
    kr = p[:, IN_PAD - LANES:]
    kvn = _rms_norm(p[:, Q_LORA:Q_LORA + KV_LORA], kvg_ref[...]).astype(BF16)
    kfull = jnp.dot(kvn, wuk_ref[...], preferred_element_type=F32)
    v_t = _dot_nt(wuv_ref[...], kvn)
    row = lax.broadcasted_iota(jnp.int32, (N_HEADS * V_ROWS, 1), 0)
    v_t = v_t + (row % V_ROWS == ONES_ROW).astype(F32)
    if with_q:
        kr = _rope_lanes(kr, cos_ref[...], slo_ref[...], shi_ref[...])
        u_ref[0] = p[:, Q_LORA + KV_LORA:Q_LORA + KV_LORA + POOL_WIDTH]
        qn = _rms_norm(p[:, :Q_LORA], qg_ref[...]).astype(BF16)
        q_t = _dot_nt(wuq_ref[...], qn) * (ATTN_SCALE * LOG2_E)
        cos_t, sin_t = cos_t_ref[...], sin_t_ref[...]
    for hd in range(N_HEADS):
        k_ref[0, hd] = (kfull[:, hd * HEAD_PAD:(hd + 1) * HEAD_PAD] + kr).astype(BF16)
        v_ref[0, hd, 0] = v_t[hd * V_ROWS:(hd + 1) * V_ROWS].astype(BF16)
        if with_q:
            base = hd * HEAD_PAD
            q_ref[0, hd, 0:QK_NOPE, :] = q_t[base:base + QK_NOPE].astype(BF16)
            rope = []
            for ax in range(2):
                lo = q_t[base + QK_NOPE + 16 * ax:base + QK_NOPE + 16 * ax + 8]
                hi = q_t[base + QK_NOPE + 16 * ax + 8:base + QK_NOPE + 16 * ax + 16]
                cs, sn = cos_t[8 * ax:8 * ax + 8], sin_t[8 * ax:8 * ax + 8]
                rope += [lo * cs - hi * sn, hi * cs + lo * sn]
            rope.append(jnp.zeros((HEAD_PAD - QK_NOPE - QK_ROPE, tile), F32))
            q_ref[0, hd, QK_NOPE:, :] = jnp.concatenate(rope, axis=0).astype(BF16)


def _proj(x, sc, sh, tables, w_in_r, q_g, w_uq_t, kv_g, w_uk_p, w_uv_t, tile):
    b, s, d = x.shape
    with_q = tables is not None
    grid = (b, s // tile)
    row = lambda bi, i: (bi, i, 0)
    vec = lambda bi, i: (bi, 0, 0)
    const2 = lambda bi, i: (0, 0)
    k_out = pl.BlockSpec((1, N_HEADS, tile, HEAD_PAD), lambda bi, i: (bi, 0, i, 0))
    k_shape = jax.ShapeDtypeStruct((b, N_HEADS, s, HEAD_PAD), BF16)
    v_out = pl.BlockSpec((1, N_HEADS, 1, V_ROWS, tile), lambda bi, i: (bi, 0, i, 0, 0))
    v_shape = jax.ShapeDtypeStruct((b, N_HEADS, s // tile, V_ROWS, tile), BF16)
    in_specs = [pl.BlockSpec((1, tile, d), row),
                pl.BlockSpec((1, 1, d), vec), pl.BlockSpec((1, 1, d), vec)]
    args = [x, sc, sh]
    if with_q:
        lane_tabs, row_tabs = tables
        in_specs += [pl.BlockSpec((tile, LANES), lambda bi, i: (i, 0))] * 3
        in_specs += [pl.BlockSpec((2 * ROPE_FREQS, tile), lambda bi, i: (0, i))] * 2
        args += list(lane_tabs) + list(row_tabs)
    in_specs.append(pl.BlockSpec(w_in_r.shape, const2)); args.append(w_in_r)
    if with_q:
        in_specs += [pl.BlockSpec(q_g.shape, const2), pl.BlockSpec(w_uq_t.shape, const2)]
        args += [q_g, w_uq_t]
    in_specs += [pl.BlockSpec(kv_g.shape, const2), pl.BlockSpec(w_uk_p.shape, const2),
                 pl.BlockSpec(w_uv_t.shape, const2)]
    args += [kv_g, w_uk_p, w_uv_t]
    if with_q:
        q_out = pl.BlockSpec((1, N_HEADS, HEAD_PAD, tile), lambda bi, i: (bi, 0, 0, i))
        q_shape = jax.ShapeDtypeStruct((b, N_HEADS, HEAD_PAD, s), BF16)
        out_specs = [q_out, k_out, v_out, pl.BlockSpec((1, tile, POOL_WIDTH), row)]
        out_shape = [q_shape, k_shape, v_shape, jax.ShapeDtypeStruct((b, s, POOL_WIDTH), F32)]
    else:
        out_specs = [k_out, v_out]
        out_shape = [k_shape, v_shape]
    return pl.pallas_call(
        functools.partial(_proj_kernel, with_q=with_q),
        grid=grid, in_specs=in_specs, out_specs=out_specs, out_shape=out_shape,
        compiler_params=pltpu.CompilerParams(dimension_semantics=("arbitrary", "arbitrary"),
                                             vmem_limit_bytes=VMEM_LIMIT),
        name="proj" if with_q else "proj_ctx",
    )(*args)


def _attn_kernel(q_ref, kc_ref, vc_ref, k_ref, v_ref, o_ref, *, n_kblk, tk):
    tq = q_ref.shape[3]
    qs = [q_ref[0, hh] for hh in range(2)]

    def scores(hh, kb):
        return jnp.dot(kb, qs[hh], preferred_element_type=F32)

    def update(s_t, vb_t, m, acc):
        m_new = jnp.maximum(m, jnp.max(s_t, axis=0, keepdims=True))
        p_t = jnp.exp2(s_t - m_new).astype(BF16)
        acc = jnp.exp2(m - m_new) * acc + jnp.dot(vb_t, p_t, preferred_element_type=F32)
        return m_new, acc

    def run_items(items, state):
        pending = [scores(hh, kb()) for hh, kb, _ in items[:ATTN_AHEAD]]
        for j, (hh, _, vb) in enumerate(items):
            if j + ATTN_AHEAD < len(items):
                nh, nkb, _ = items[j + ATTN_AHEAD]
                pending.append(scores(nh, nkb()))
            state[hh] = update(pending.pop(0), vb(), *state[hh])
        return state

    def block_items(blk, off):
        out = []
        for sub in range(tk // ATTN_SUB):
            for hh in range(2):
                lo = sub * ATTN_SUB
                out.append((hh,
                            lambda hh=hh, lo=lo: k_ref[0, hh, pl.ds(off + lo, ATTN_SUB), :],
                            lambda hh=hh, lo=lo: v_ref[0, hh, blk, :, lo:lo + ATTN_SUB]))
        return out

    ctx_items = [(hh, lambda hh=hh: kc_ref[0, hh], lambda hh=hh: vc_ref[0, hh, 0])
                 for hh in range(2)]
    state = [(jnp.full((1, tq), NEG_BIG, F32), jnp.zeros((V_ROWS, tq), F32)) for _ in range(2)]
    n_iter = n_kblk // ATTN_UNROLL
    if n_iter == 1:
        items = ctx_items
        for blk in range(n_kblk):
            items = items + block_items(blk, blk * tk)
        state = run_items(items, state)
    else:
        state = run_items(ctx_items, state)

        def body(i, carry):
            items = []
            for r in range(ATTN_UNROLL):
                blk = i * ATTN_UNROLL + r
                items += block_items(blk, pl.multiple_of(blk * tk, tk))
            st = run_items(items, [(carry[0], carry[1]), (carry[2], carry[3])])
            return st[0] + st[1]

        carry = lax.fori_loop(0, n_iter, body, state[0] + state[1])
        state = [(carry[0], carry[1]), (carry[2], carry[3])]
    carry = state[0] + state[1]
    outs = [carry[2 * hh + 1][:V_HEAD] / carry[2 * hh + 1][ONES_ROW:ONES_ROW + 1] for hh in range(2)]
    o_ref[0] = jnp.concatenate(outs, axis=0).T.astype(o_ref.dtype)


def _attention(q_t, kc, vc_t, k, v_t):
    b, nh, dp, s = q_t.shape
    c = kc.shape[2]
    tq, tk = ATTN_TQ, ATTN_TK
    n_kblk = s // tk
    kern = functools.partial(_attn_kernel, n_kblk=n_kblk, tk=tk)
    return pl.pallas_call(
        kern,
        grid=(b, nh // 2, s // tq),
        in_specs=[pl.BlockSpec((1, 2, dp, tq), lambda bi, hp, qi: (bi, hp, 0, qi)),
                  pl.BlockSpec((1, 2, c, dp), lambda bi, hp, qi: (bi, hp, 0, 0)),
                  pl.BlockSpec((1, 2, 1, V_ROWS, c), lambda bi, hp, qi: (bi, hp, 0, 0, 0)),
                  pl.BlockSpec((1, 2, s, dp), lambda bi, hp, qi: (bi, hp, 0, 0)),
                  pl.BlockSpec((1, 2, n_kblk, V_ROWS, tk), lambda bi, hp, qi: (bi, hp, 0, 0, 0))],
        out_specs=pl.BlockSpec((1, tq, 2 * V_HEAD), lambda bi, hp, qi: (bi, qi, hp)),
        out_shape=jax.ShapeDtypeStruct((b, s, nh * V_HEAD), BF16),
        compiler_params=pltpu.CompilerParams(
            dimension_semantics=("arbitrary", "arbitrary", "arbitrary"),
            vmem_limit_bytes=VMEM_LIMIT),
        name="attn",
    )(q_t, kc, vc_t, k, v_t)


def _route(logits_t, bias_t):
    e, t = logits_t.shape
    scores = jax.nn.sigmoid(logits_t)
    biased = scores + bias_t
    neg_inf = F32(-jnp.inf)
    gscore = []
    for g in range(N_EXPERT_GROUPS):
        v = biased[g * GROUP_SIZE:(g + 1) * GROUP_SIZE]
        m1 = jnp.max(v, axis=0, keepdims=True)
        at_max = v == m1
        n_max = jnp.sum(at_max.astype(F32), axis=0, keepdims=True)
        m2 = jnp.max(jnp.where(at_max, neg_inf, v), axis=0, keepdims=True)
        gscore.append(m1 + jnp.where(n_max >= 2.0, m1, m2))
    masked = []
    for g in range(N_EXPERT_GROUPS):
        rank = jnp.zeros((1, t), F32)
        for o in range(N_EXPERT_GROUPS):
            if o == g:
                continue
            beats = (gscore[o] >= gscore[g]) if o < g else (gscore[o] > gscore[g])
            rank = rank + beats.astype(F32)
        keep = rank < float(TOPK_GROUPS)
        masked.append(jnp.where(keep, biased[g * GROUP_SIZE:(g + 1) * GROUP_SIZE], neg_inf))
    work = jnp.concatenate(masked, axis=0)
    rows = lax.broadcasted_iota(jnp.int32, (e, t), 0)
    sel = jnp.zeros((e, t), F32)
    for _ in range(TOP_K):
        m = jnp.max(work, axis=0, keepdims=True)
        first = jnp.min(jnp.where(work == m, rows, e), axis=0, keepdims=True)
        pick = rows == first
        sel = jnp.where(pick, 1.0, sel)
        work = jnp.where(pick, neg_inf, work)
    w = sel * scores
    gates = w / jnp.sum(w, axis=0, keepdims=True) * ROUTED_SCALE
    return sel, gates


def _mix_kernel(attn_ref, u_ref, up_ref, un_ref, x_ref, g1_ref, sc2_ref, sh2_ref, wpool_ref,
                pscale_ref, wout_ref, ln_g_ref, ln_b_ref, wr_ref, rb_ref, tri_ref, ones_ref,
                x1_ref, hx_ref, pos_ref, seg_ref, cnt_ref, uext_ref, *, seq):
    i = pl.program_id(1)

    @pl.when((pl.program_id(0) == 0) & (i == 0))
    def _():
        cnt_ref[...] = jnp.zeros_like(cnt_ref)

    tile = u_ref.shape[1]
    u = u_ref[0]
    uext_ref[0:POOL_HALO] = jnp.where(i == 0, 0.0, up_ref[0])
    uext_ref[POOL_HALO:POOL_HALO + tile] = u
    uext_ref[POOL_HALO + tile:] = jnp.where(i == pl.num_programs(1) - 1, 0.0, un_ref[0])
    t = i * tile + lax.broadcasted_iota(jnp.int32, (tile, POOL_GC), 0)
    pooled = []
    for g, w in enumerate(POOL_WINDOWS):
        lanes = slice(g * POOL_GC, (g + 1) * POOL_GC)
        tot = uext_ref[POOL_HALO - w // 2:POOL_HALO - w // 2 + tile, lanes]
        for dlt in range(1, w):
            start = POOL_HALO - w // 2 + dlt
            tot = tot + uext_ref[start:start + tile, lanes]
        cnt = (jnp.minimum(t - w // 2 + w, seq) - jnp.maximum(t - w // 2, 0)).astype(F32)
        pg = (tot / cnt - u[:, lanes]).astype(BF16)
        po = jnp.dot(pg, wpool_ref[g], preferred_element_type=F32) * pscale_ref[:, lanes]
        pooled.append(po.astype(BF16))
    mixed = jnp.concatenate([attn_ref[0]] + pooled, axis=1)
    y = jnp.dot(mixed, wout_ref[...], preferred_element_type=F32)
    x1 = _layer_norm(ALPHA * x_ref[0] + g1_ref[0] * y, ln_g_ref[...], ln_b_ref[...])
    x1_ref[0] = x1
    h2 = x1 * (1.0 + sc2_ref[0]) + sh2_ref[0]
    logits_t = lax.dot_general(wr_ref[...], h2, (((1,), (1,)), ((), ())),
                               preferred_element_type=F32, precision=lax.Precision.HIGHEST)
    sel, gates_t = _route(logits_t, rb_ref[...])
    g_hi = gates_t.astype(BF16)
    g_lo = (gates_t - g_hi.astype(F32)).astype(BF16)
    g_tok = jnp.concatenate([g_hi.astype(F32), g_lo.astype(F32)], axis=0).T
    hx_ref[0] = jnp.concatenate([h2.astype(BF16), g_tok.astype(BF16)], axis=1)
    sel_b = sel.astype(BF16)
    start = cnt_ref[...]
    pos_t = jnp.dot(sel_b, tri_ref[...], preferred_element_type=F32) + start[:, 0:1]
    pos_ref[0] = jnp.where(sel > 0.0, pos_t, -1.0).astype(jnp.int32)
    n_tok = jnp.dot(sel_b, ones_ref[...], preferred_element_type=F32)
    n_chunk = jnp.floor((n_tok + (SEG_ROWS - 1)) * (1.0 / SEG_ROWS))
    seg_ref[0, 0] = start.astype(jnp.int32)
    seg_ref[0, 1] = n_chunk.astype(jnp.int32)
    cnt_ref[...] = start + n_chunk * SEG_ROWS


def _mix(attn, u, x, g1, sc2, sh2, w_pool, pool_scale, w_out, ln_g, ln_b, w_r_t, rb_t):
    b, s, d = x.shape
    tile = MIX_TILE
    hb = tile // POOL_HALO
    row = lambda bi, i: (bi, i, 0)
    vec = lambda bi, i: (bi, 0, 0)
    c2 = lambda bi, i: (0, 0)
    lane_row = lambda bi, i: (bi, 0, i)
    tri = (lax.broadcasted_iota(jnp.int32, (tile, tile), 0)
           < lax.broadcasted_iota(jnp.int32, (tile, tile), 1)).astype(BF16)
    ones = jnp.ones((tile, LANES), BF16)
    return pl.pallas_call(
        functools.partial(_mix_kernel, seq=s),
        grid=(b, s // tile),
        in_specs=[pl.BlockSpec((1, tile, POOL_WIDTH), row),
                  pl.BlockSpec((1, tile, POOL_WIDTH), row),
                  pl.BlockSpec((1, POOL_HALO, POOL_WIDTH),
                               lambda bi, i: (bi, jnp.maximum(i * hb - 1, 0), 0)),
                  pl.BlockSpec((1, POOL_HALO, POOL_WIDTH),
                               lambda bi, i: (bi, jnp.minimum((i + 1) * hb, s // POOL_HALO - 1), 0)),
                  pl.BlockSpec((1, tile, d), row),
                  pl.BlockSpec((1, 1, d), vec), pl.BlockSpec((1, 1, d), vec),
                  pl.BlockSpec((1, 1, d), vec),
                  pl.BlockSpec(w_pool.shape, lambda bi, i: (0, 0, 0)),
                  pl.BlockSpec(pool_scale.shape, c2),
                  pl.BlockSpec(w_out.shape, c2),
                  pl.BlockSpec(ln_g.shape, c2), pl.BlockSpec(ln_b.shape, c2),
                  pl.BlockSpec(w_r_t.shape, c2), pl.BlockSpec(rb_t.shape, c2),
                  pl.BlockSpec(tri.shape, c2), pl.BlockSpec(ones.shape, c2)],
        out_specs=[pl.BlockSpec((1, tile, d), row),
                   pl.BlockSpec((1, tile, HX_WIDTH), row),
                   pl.BlockSpec((1, N_EXPERTS, tile), lane_row),
                   pl.BlockSpec((1, 2, N_EXPERTS, LANES),
                                lambda bi, i: (bi * (s // tile) + i, 0, 0, 0)),
                   pl.BlockSpec((N_EXPERTS, LANES), c2)],
        out_shape=[jax.ShapeDtypeStruct((b, s, d), F32),
                   jax.ShapeDtypeStruct((b, s, HX_WIDTH), BF16),
                   jax.ShapeDtypeStruct((b, N_EXPERTS, s), jnp.int32),
                   jax.ShapeDtypeStruct((b * (s // tile), 2, N_EXPERTS, LANES), jnp.int32),
                   jax.ShapeDtypeStruct((N_EXPERTS, LANES), F32)],
        scratch_shapes=[pltpu.VMEM((tile + 2 * POOL_HALO, POOL_WIDTH), F32)],
        compiler_params=pltpu.CompilerParams(dimension_semantics=("arbitrary", "arbitrary"),
                                             vmem_limit_bytes=VMEM_LIMIT),
        name="mix",
    )(attn, u, u, u, x, g1, sc2, sh2, w_pool, pool_scale, w_out, ln_g, ln_b, w_r_t, rb_t,
      tri, ones)


def _one_hot(pos_row, start, window, n_tok):
    rows = lax.broadcasted_iota(jnp.int32, (SEG_WIN, n_tok), 0)
    hit = rows == (pos_row - (start + window * SEG_WIN))
    return jnp.where(hit, 1.0, 0.0).astype(BF16)


def _window_rows(base_ref, start_ref, step, e, window):
    row = base_ref[e] + start_ref[step * N_EXPERTS + e] + window * SEG_WIN
    return pl.ds(pl.multiple_of(row, SEG_ROWS), SEG_WIN)


def _extra_windows(nchunk_ref, step, e):
    rows = nchunk_ref[step * N_EXPERTS + e] * SEG_ROWS
    return jnp.maximum((rows + SEG_WIN - 1) // SEG_WIN, 1)


def _dispatch_kernel(base_ref, start_ref, nchunk_ref, over_ref, total_ref, pcnt_ref, nu_ref,
                     hx_ref, pos_ref, xs_hbm, stage, extra, zero_ref, sem):
    i = pl.program_id(0)
    n_tiles = xs_hbm.shape[0] // MOE_TM
    n_tok = hx_ref.shape[0]

    def zero_copy(row, n_rows):
        return pltpu.make_async_copy(zero_ref.at[pl.ds(0, n_rows)],
                                     xs_hbm.at[pl.ds(pl.multiple_of(row, SEG_ROWS), n_rows)],
                                     sem.at[1])

    @pl.when(i == 0)
    def _():
        zero_ref[...] = jnp.zeros_like(zero_ref)
        for wait in (False, True):
            def per_tail(j, c):
                cp = zero_copy(j * MOE_TM, MOE_TM)
                cp.wait() if wait else cp.start()
                return c
            lax.fori_loop(nu_ref[0], n_tiles, per_tail, 0)

            def per_expert(e, c):
                row0 = base_ref[e] + total_ref[e]

                def per_block(j, c2):
                    cp = zero_copy(row0 + j * SEG_ROWS, SEG_ROWS)
                    cp.wait() if wait else cp.start()
                    return c2
                return lax.fori_loop(0, (pcnt_ref[e] - total_ref[e]) // SEG_ROWS, per_block, c)
            lax.fori_loop(0, N_EXPERTS, per_expert, 0)

    slot = i % 2
    hx = hx_ref[...]
    for g in range(N_EXPERTS // WIN_GROUP):
        oh = jnp.concatenate(
            [_one_hot(pos_ref[0, e:e + 1, :], start_ref[i * N_EXPERTS + e], 0, n_tok)
             for e in range(g * WIN_GROUP, (g + 1) * WIN_GROUP)], axis=0)
        rows = pl.ds(g * WIN_GROUP * SEG_WIN, WIN_GROUP * SEG_WIN)
        stage[slot, rows, :] = jnp.dot(oh, hx, preferred_element_type=F32).astype(BF16)

    def wait_windows(which):
        pltpu.make_async_copy(stage.at[which], xs_hbm.at[pl.ds(0, N_EXPERTS * SEG_WIN)],
                              sem.at[0]).wait()

    @pl.when(i > 0)
    def _():
        wait_windows(1 - slot)

    for e in range(N_EXPERTS):
        pltpu.make_async_copy(stage.at[slot, pl.ds(e * SEG_WIN, SEG_WIN)],
                              xs_hbm.at[_window_rows(base_ref, start_ref, i, e, 0)],
                              sem.at[0]).start(priority=e % 2)

    @pl.when(i == pl.num_programs(0) - 1)
    def _():
        wait_windows(slot)

    @pl.when(over_ref[i] > 0)
    def _():
        def more_windows(e, c):
            def one(window, c2):
                oh = _one_hot(pos_ref[0, pl.ds(e, 1), :], start_ref[i * N_EXPERTS + e], window,
                              n_tok)
                extra[...] = jnp.dot(oh, hx, preferred_element_type=F32).astype(BF16)
                cp = pltpu.make_async_copy(
                    extra, xs_hbm.at[_window_rows(base_ref, start_ref, i, e, window)], sem.at[2])
                cp.start()
                cp.wait()
                return c2
            return lax.fori_loop(1, _extra_windows(nchunk_ref, i, e), one, c)
        lax.fori_loop(0, N_EXPERTS, more_windows, 0)


def _dispatch(base, seg_start, seg_chunks, seg_over, total, pcnt, n_used, hx, pos, n_slots):
    t, width = hx.shape
    tpb = pos.shape[2] // TOK_TILE
    return pl.pallas_call(
        _dispatch_kernel,
        grid_spec=pltpu.PrefetchScalarGridSpec(
            num_scalar_prefetch=7, grid=(t // TOK_TILE,),
            in_specs=[pl.BlockSpec((TOK_TILE, width), lambda i, *_: (i, 0)),
                      pl.BlockSpec((1, N_EXPERTS, TOK_TILE), lambda i, *_: (i // tpb, 0, i % tpb))],
            out_specs=pl.BlockSpec(memory_space=pl.ANY),
            scratch_shapes=[pltpu.VMEM((2, N_EXPERTS * SEG_WIN, width), BF16),
                            pltpu.VMEM((SEG_WIN, width), BF16),
                            pltpu.VMEM((MOE_TM, width), BF16),
                            pltpu.SemaphoreType.DMA((3,))]),
        out_shape=jax.ShapeDtypeStruct((n_slots, width), BF16),
        compiler_params=pltpu.CompilerParams(dimension_semantics=("arbitrary",),
                                             vmem_limit_bytes=VMEM_LIMIT),
        name="dispatch",
    )(base, seg_start, seg_chunks, seg_over, total, pcnt, n_used, hx, pos)


def _experts_kernel(te_ref, nu_ref, xs_ref, wg_ref, wu_ref, wd_ref, ys_ref, wg_b, wu_b, wd_b):
    i = pl.program_id(0)

    @pl.when(i < nu_ref[0])
    def _():
        @pl.when((i == 0) | (te_ref[i] != te_ref[jnp.maximum(i - 1, 0)]))
        def _():
            wg_b[...] = wg_ref[0].astype(BF16)
            wu_b[...] = wu_ref[0].astype(BF16)
            wd_b[...] = wd_ref[0].astype(BF16)

        half = MOE_TM // 2
        halves = [pl.ds(k * half, half) for k in range(2)]
        hidden = []
        for rows in halves:
            x = xs_ref[rows, :D_MODEL]
            hidden.append((jnp.dot(x, wg_b[...], preferred_element_type=F32),
                           jnp.dot(x, wu_b[...], preferred_element_type=F32)))
        for rows, (hg, hu) in zip(halves, hidden):
            g = xs_ref[rows, D_MODEL:].astype(F32)
            lane = lax.broadcasted_iota(jnp.int32, g.shape, 1)
            mine = (lane == te_ref[i]) | (lane == te_ref[i] + N_EXPERTS)
            gate = jnp.sum(jnp.where(mine, g, 0.0), axis=1, keepdims=True)
            a = (_silu(hg) * hu * gate).astype(BF16)
            ys_ref[rows, :] = jnp.dot(a, wd_b[...], preferred_element_type=F32).astype(ys_ref.dtype)

    @pl.when(i >= nu_ref[0])
    def _():
        ys_ref[...] = jnp.zeros_like(ys_ref)


def _experts(tile_expert, n_used, xs, w_e_gate, w_e_up, w_e_down):
    n_slots, width = xs.shape
    _, d, f = w_e_gate.shape
    slot_tile = lambda i, te, nu: (jnp.minimum(i, nu[0] - 1), 0)
    expert = lambda i, te, nu: (te[i], 0, 0)
    return pl.pallas_call(
        _experts_kernel,
        grid_spec=pltpu.PrefetchScalarGridSpec(
            num_scalar_prefetch=2, grid=(n_slots // MOE_TM,),
            in_specs=[pl.BlockSpec((MOE_TM, width), slot_tile),
                      pl.BlockSpec((1, d, f), expert), pl.BlockSpec((1, d, f), expert),
                      pl.BlockSpec((1, f, d), expert)],
            out_specs=pl.BlockSpec((MOE_TM, d), lambda i, te, nu: (i, 0)),
            scratch_shapes=[pltpu.VMEM((d, f), BF16), pltpu.VMEM((d, f), BF16),
                            pltpu.VMEM((f, d), BF16)]),
        out_shape=jax.ShapeDtypeStruct((n_slots, d), BF16),
        compiler_params=pltpu.CompilerParams(dimension_semantics=("arbitrary",),
                                             vmem_limit_bytes=VMEM_LIMIT),
        name="experts",
    )(tile_expert, n_used, xs, w_e_gate, w_e_up, w_e_down)


def _dot_tn(a, b):
    return lax.dot_general(a, b, (((0,), (0,)), ((), ())), preferred_element_type=F32)


def _combine_kernel(base_ref, start_ref, nchunk_ref, over_ref, hx_ref, pos_ref, x1_ref, g2_ref, wsg_ref,
                    wsu_ref, wsd_ref, ln_g_ref, ln_b_ref, ys_hbm, o_ref, win, extra, acc_ref, sem):
    i = pl.program_id(0)
    n = pl.num_programs(0)
    n_tok = hx_ref.shape[0]

    def fetch(step, slot):
        for e in range(N_EXPERTS):
            pltpu.make_async_copy(ys_hbm.at[_window_rows(base_ref, start_ref, step, e, 0)],
                                  win.at[slot, pl.ds(e * SEG_WIN, SEG_WIN)],
                                  sem.at[slot]).start(priority=e % 2)

    @pl.when(i == 0)
    def _():
        fetch(0, 0)

    @pl.when(i + 1 < n)
    def _():
        fetch(i + 1, (i + 1) % 2)

    slot = i % 2
    h = hx_ref[:, :D_MODEL]
    a = _silu(jnp.dot(h, wsg_ref[...], preferred_element_type=F32)) * jnp.dot(
        h, wsu_ref[...], preferred_element_type=F32)
    moe = jnp.dot(a.astype(BF16), wsd_ref[...], preferred_element_type=F32)
    pltpu.make_async_copy(ys_hbm.at[pl.ds(0, N_EXPERTS * SEG_WIN)], win.at[slot], sem.at[slot]).wait()
    for g in range(N_EXPERTS // WIN_GROUP):
        oh = jnp.concatenate(
            [_one_hot(pos_ref[0, e:e + 1, :], start_ref[i * N_EXPERTS + e], 0, n_tok)
             for e in range(g * WIN_GROUP, (g + 1) * WIN_GROUP)], axis=0)
        rows = pl.ds(g * WIN_GROUP * SEG_WIN, WIN_GROUP * SEG_WIN)
        moe = moe + _dot_tn(oh, win[slot, rows, :])
    acc_ref[...] = moe

    @pl.when(over_ref[i] > 0)
    def _():
        def more_windows(e, c):
            def one(window, c2):
                cp = pltpu.make_async_copy(
                    ys_hbm.at[_window_rows(base_ref, start_ref, i, e, window)], extra, sem.at[2])
                cp.start()
                cp.wait()
                oh = _one_hot(pos_ref[0, pl.ds(e, 1), :], start_ref[i * N_EXPERTS + e], window,
                              n_tok)
                acc_ref[...] += _dot_tn(oh, extra[...])
                return c2
            return lax.fori_loop(1, _extra_windows(nchunk_ref, i, e), one, c)
        lax.fori_loop(0, N_EXPERTS, more_windows, 0)

    z = ALPHA * x1_ref[...] + g2_ref[0] * acc_ref[...]
    o_ref[...] = _layer_norm(z, ln_g_ref[...], ln_b_ref[...])


def _combine(base, seg_start, seg_chunks, seg_over, hx, pos, x1, g2, w_s_gate, w_s_up, w_s_down,
             ln_g, ln_b, ys):
    t, d = x1.shape
    tpb = pos.shape[2] // TOK_TILE
    row = lambda i, *_: (i, 0)
    c2 = lambda i, *_: (0, 0)
    return pl.pallas_call(
        _combine_kernel,
        grid_spec=pltpu.PrefetchScalarGridSpec(
            num_scalar_prefetch=4, grid=(t // TOK_TILE,),
            in_specs=[pl.BlockSpec((TOK_TILE, hx.shape[1]), row),
                      pl.BlockSpec((1, N_EXPERTS, TOK_TILE), lambda i, *_: (i // tpb, 0, i % tpb)),
                      pl.BlockSpec((TOK_TILE, d), row),
                      pl.BlockSpec((1, 1, d), lambda i, *_: (i // tpb, 0, 0)),
                      pl.BlockSpec(w_s_gate.shape, c2), pl.BlockSpec(w_s_up.shape, c2),
                      pl.BlockSpec(w_s_down.shape, c2),
                      pl.BlockSpec(ln_g.shape, c2), pl.BlockSpec(ln_b.shape, c2),
                      pl.BlockSpec(memory_space=pl.ANY)],
            out_specs=pl.BlockSpec((TOK_TILE, d), row),
            scratch_shapes=[pltpu.VMEM((2, N_EXPERTS * SEG_WIN, d), BF16),
                            pltpu.VMEM((SEG_WIN, d), BF16),
                            pltpu.VMEM((TOK_TILE, d), F32),
                            pltpu.SemaphoreType.DMA((3,))]),
        out_shape=jax.ShapeDtypeStruct((t, d), F32),
        compiler_params=pltpu.CompilerParams(dimension_semantics=("arbitrary",),
                                             vmem_limit_bytes=VMEM_LIMIT),
        name="combine",
    )(base, seg_start, seg_chunks, seg_over, hx, pos, x1, g2, w_s_gate, w_s_up, w_s_down, ln_g,
      ln_b, ys)


def _moe(hx, pos, seg, counts, x1, g2, w_e_gate, w_e_up, w_e_down, w_s_gate, w_s_up, w_s_down,
         ln_g, ln_b):
    b, s, d = x1.shape
    t = b * s
    n_seg = (t // TOK_TILE) * N_EXPERTS
    max_rows = t * TOP_K + n_seg * (SEG_ROWS - 1) + N_EXPERTS * (SEG_WIN + MOE_TM)
    n_tiles = -(-max_rows // MOE_TM)
    total = jnp.round(counts[:, 0]).astype(jnp.int32)
    pcnt = (total + SEG_WIN + MOE_TM - 1) // MOE_TM * MOE_TM
    ends = jnp.cumsum(pcnt)
    base = ends - pcnt
    n_used = (ends[-1] // MOE_TM).reshape(1)
    tile_ids = jnp.arange(n_tiles, dtype=jnp.int32)
    tile_expert = jnp.sum((ends[None, :] <= tile_ids[:, None] * MOE_TM).astype(jnp.int32), axis=1)
    tile_expert = jnp.minimum(tile_expert, N_EXPERTS - 1)
    tile_expert = jnp.where(tile_ids < n_used, tile_expert, tile_expert[n_used[0] - 1])
    seg_start = seg[:, 0, :, 0].reshape(n_seg)
    seg_chunks = seg[:, 1, :, 0]
    seg_over = (jnp.max(seg_chunks, axis=1) * SEG_ROWS > SEG_WIN).astype(jnp.int32)
    seg_chunks = seg_chunks.reshape(n_seg)

    hxf = hx.reshape(t, hx.shape[2])
    xs = _dispatch(base, seg_start, seg_chunks, seg_over, total, pcnt, n_used, hxf, pos,
                   n_tiles * MOE_TM)
    ys = _experts(tile_expert, n_used, xs, w_e_gate, w_e_up, w_e_down)
    out = _combine(base, seg_start, seg_chunks, seg_over, hxf, pos, x1.reshape(t, d), g2, w_s_gate,
                   w_s_up, w_s_down, ln_g, ln_b, ys)
    return out.reshape(b, s, d)


def _rope_tables(seq):
    t = jnp.arange(seq)
    pos = jnp.stack([t // GRID_W, t % GRID_W], axis=-1).astype(F32)
    inv_freq = ROPE_THETA ** (-jnp.arange(ROPE_FREQS, dtype=F32) / ROPE_FREQS)
    ang = pos[:, :, None] * inv_freq
    cos, sin = jnp.cos(ang), jnp.sin(ang)
    zero = jnp.zeros_like(sin)
    cos_r = jnp.stack([cos, cos], axis=2).reshape(seq, QK_ROPE)
    sin_lo = jnp.stack([-sin, zero], axis=2).reshape(seq, QK_ROPE)
    sin_hi = jnp.stack([zero, sin], axis=2).reshape(seq, QK_ROPE)
    pads = ((0, 0), (QK_NOPE, HEAD_PAD - QK_NOPE - QK_ROPE))
    lane_tabs = (jnp.pad(cos_r, pads, constant_values=1.0), jnp.pad(sin_lo, pads),
                 jnp.pad(sin_hi, pads))
    row_tabs = (cos.reshape(seq, 2 * ROPE_FREQS).T, sin.reshape(seq, 2 * ROPE_FREQS).T)
    return lane_tabs, row_tabs


def _pad_heads(w, width, padded):
    k = w.shape[0]
    w = jnp.pad(w.reshape(k, N_HEADS, width), ((0, 0), (0, 0), (0, padded - width)))
    return w.reshape(k, N_HEADS * padded)


def kernel(x, c, ctx, c_ctx, w_ada, b_ada, w_in, q_norm_g, w_uq, kv_norm_g, w_ukv, w_pool, pool_scale, w_out, ln1_g, ln1_b, w_router, router_bias, w_e_gate, w_e_up, w_e_down, w_s_gate, w_s_up, w_s_down, ln2_g, ln2_b):
    assert w_ada.shape[0] == 1, "single-layer block"
    b, s, d = x.shape

    cvec = jnp.concatenate([c, c_ctx[None], jnp.zeros((SUBLANES - b - 1, d), F32)], axis=0)
    mod = _ada(cvec, w_ada[0], b_ada)
    sh1, sc1, g1, sh2, sc2, g2 = [mod[:b, k * d:(k + 1) * d][:, None, :] for k in range(6)]
    sh1c, sc1c = [jnp.broadcast_to(mod[b, k * d:(k + 1) * d], (b, 1, d)) for k in range(2)]

    wi = w_in[0]
    kr_cols = jnp.pad(wi[:, Q_LORA + KV_LORA:Q_LORA + KV_LORA + QK_ROPE],
                      ((0, 0), (QK_NOPE, HEAD_PAD - QK_NOPE - QK_ROPE)))
    w_in_r = jnp.concatenate([wi[:, :Q_LORA + KV_LORA], wi[:, Q_LORA + KV_LORA + QK_ROPE:], kr_cols],
                             axis=1).astype(BF16)
    w_uq_t = _pad_heads(w_uq[0], QK_NOPE + QK_ROPE, HEAD_PAD).T.astype(BF16)
    wkv = w_ukv[0].reshape(KV_LORA, N_HEADS, QK_NOPE + V_HEAD)
    w_uk_p = _pad_heads(wkv[:, :, :QK_NOPE].reshape(KV_LORA, -1), QK_NOPE, HEAD_PAD).astype(BF16)
    w_uv_t = _pad_heads(wkv[:, :, QK_NOPE:].reshape(KV_LORA, -1), V_HEAD, V_ROWS).T.astype(BF16)
    tables = _rope_tables(s)

    q_t, k, v_t, u = _proj(x, sc1, sh1, tables, w_in_r, q_norm_g, w_uq_t, kv_norm_g, w_uk_p,
                           w_uv_t, PROJ_TILE)
    kc, vc_t = _proj(ctx, sc1c, sh1c, None, w_in_r, None, None, kv_norm_g, w_uk_p, w_uv_t,
                     ctx.shape[1])
    attn = _attention(q_t, kc, vc_t, k, v_t)

    x1, hx, pos, seg, counts = _mix(attn, u, x, g1, sc2, sh2, w_pool[0].astype(BF16),
                                    pool_scale, w_out[0].astype(BF16), ln1_g, ln1_b,
                                    w_router[0].T, router_bias[0][:, None])
    return _moe(hx, pos, seg, counts, x1, g2, w_e_gate[0], w_e_up[0], w_e_down[0],
                w_s_gate[0].astype(BF16), w_s_up[0].astype(BF16), w_s_down[0].astype(BF16),
                ln2_g, ln2_b)
```

```python
import functools
import math

import jax
import jax.numpy as jnp
from jax import lax
from jax.experimental import pallas as pl
from jax.experimental.pallas import tpu as pltpu

F32 = jnp.float32
BF16 = jnp.bfloat16

D_MODEL = 1024
GRID_W = 64
N_HEADS = 8
Q_LORA = 512
KV_LORA = 256
QK_NOPE = 64
QK_ROPE = 32
V_HEAD = 64
ROPE_FREQS = QK_ROPE // 4
ROPE_THETA = 10000.0
ATTN_SCALE = 1.0 / math.sqrt(QK_NOPE + QK_ROPE)
LOG2_E = math.log2(math.e)
POOL_GROUPS = 4
POOL_WINDOWS = (2, 4, 8, 16)
POOL_WIDTH = 512
POOL_GC = POOL_WIDTH // POOL_GROUPS
POOL_HALO = 8
N_EXPERTS = 64
N_EXPERT_GROUPS = 8
GROUP_SIZE = N_EXPERTS // N_EXPERT_GROUPS
TOPK_GROUPS = 4
TOP_K = 8
D_EXPERT = 256
ROUTED_SCALE = 2.5
LN_EPS = 1e-5
RMS_EPS = 1e-6
ALPHA = 2.0 ** 0.25

LANES = 128
SUBLANES = 8
HEAD_PAD = LANES
V_ROWS = 80
ONES_ROW = V_HEAD
IN_PAD = Q_LORA + KV_LORA + POOL_WIDTH + LANES

PROJ_TILE = 512
ATTN_TQ = 512
ATTN_TK = PROJ_TILE
ATTN_SUB = 256
ATTN_AHEAD = 2
ATTN_UNROLL = 16
TOK_TILE = 256
MIX_TILE = TOK_TILE
MOE_TM = 1024
SEG_ROWS = 16
SEG_WIN = 48
WIN_GROUP = 16
HX_WIDTH = D_MODEL + 2 * N_EXPERTS
VMEM_LIMIT = 48 * 1024 * 1024
NEG_BIG = -1e30


def _silu(v):
    return v * jax.nn.sigmoid(v)


def _layer_norm(z, g, b):
    mu = jnp.mean(z, axis=-1, keepdims=True)
    zc = z - mu
    var = jnp.mean(zc * zc, axis=-1, keepdims=True)
    return zc * lax.rsqrt(var + LN_EPS) * g + b


def _rms_norm(v, g):
    return v * lax.rsqrt(jnp.mean(v * v, axis=-1, keepdims=True) + RMS_EPS) * g


def _dot_nt(a, b):
    return lax.dot_general(a, b, (((1,), (1,)), ((), ())), preferred_element_type=F32)


def _ada_kernel(c_ref, w_ref, b_ref, o_ref):
    cv = _silu(c_ref[...])
    o_ref[...] = jnp.dot(cv, w_ref[...], preferred_element_type=F32,
                         precision=lax.Precision.HIGHEST) + b_ref[...]


def _ada(cvec, w_ada, b_ada):
    rows, d = cvec.shape
    n = w_ada.shape[1]
    tn = 1024
    return pl.pallas_call(
        _ada_kernel,
        grid=(n // tn,),
        in_specs=[pl.BlockSpec((rows, d), lambda j: (0, 0)),
                  pl.BlockSpec((d, tn), lambda j: (0, j)),
                  pl.BlockSpec((1, tn), lambda j: (0, j))],
        out_specs=pl.BlockSpec((rows, tn), lambda j: (0, j)),
        out_shape=jax.ShapeDtypeStruct((rows, n), F32),
        compiler_params=pltpu.CompilerParams(dimension_semantics=("arbitrary",),
                                             vmem_limit_bytes=VMEM_LIMIT),
        name="ada",
    )(cvec, w_ada, b_ada)


def _rope_lanes(v, cos, sin_lo, sin_hi):
    return v * cos + pltpu.roll(v, LANES - 8, axis=1) * sin_lo + pltpu.roll(v, 8, axis=1) * sin_hi


def _proj_kernel(*refs, with_q):
    if with_q:
        (x_ref, sc_ref, sh_ref, cos_ref, slo_ref, shi_ref, cos_t_ref, sin_t_ref, win_ref, qg_ref,
         wuq_ref, kvg_ref, wuk_ref, wuv_ref, q_ref, k_ref, v_ref, u_ref) = refs
    else:
        (x_ref, sc_ref, sh_ref, win_ref, kvg_ref, wuk_ref, wuv_ref, k_ref, v_ref) = refs
    h = (x_ref[0] * (1.0 + sc_ref[0]) + sh_ref[0]).astype(BF16)
    p = jnp.dot(h, win_ref[...], preferred_element_type=F32)
    tile = p.shape[0]
    kr = p[:, IN_PAD - LANES:]
    kvn = _rms_norm(p[:, Q_LORA:Q_LORA + KV_LORA], kvg_ref[...]).astype(BF16)
    kfull = jnp.dot(kvn, wuk_ref[...], preferred_element_type=F32)
    v_t = _dot_nt(wuv_ref[...], kvn)
    row = lax.broadcasted_iota(jnp.int32, (N_HEADS * V_ROWS, 1), 0)
    v_t = v_t + (row % V_ROWS == ONES_ROW).astype(F32)
    if with_q:
        kr = _rope_lanes(kr, cos_ref[...], slo_ref[...], shi_ref[...])
        u_ref[0] = p[:, Q_LORA + KV_LORA:Q_LORA + KV_LORA + POOL_WIDTH]
        qn = _rms_norm(p[:, :Q_LORA], qg_ref[...]).astype(BF16)
        q_t = _dot_nt(wuq_ref[...], qn) * (ATTN_SCALE * LOG2_E)
        cos_t, sin_t = cos_t_ref[...], sin_t_ref[...]
    for hd in range(N_HEADS):
        k_ref[0, hd] = (kfull[:, hd * HEAD_PAD:(hd + 1) * HEAD_PAD] + kr).astype(BF16)
        v_ref[0, hd, 0] = v_t[hd * V_ROWS:(hd + 1) * V_ROWS].astype(BF16)
        if with_q:
            base = hd * HEAD_PAD
            q_ref[0, hd, 0:QK_NOPE, :] = q_t[base:base + QK_NOPE].astype(BF16)
            rope = []
            for ax in range(2):
                lo = q_t[base + QK_NOPE + 16 * ax:base + QK_NOPE + 16 * ax + 8]
                hi = q_t[base + QK_NOPE + 16 * ax + 8:base + QK_NOPE + 16 * ax + 16]
                cs, sn = cos_t[8 * ax:8 * ax + 8], sin_t[8 * ax:8 * ax + 8]
                rope += [lo * cs - hi * sn, hi * cs + lo * sn]
            rope.append(jnp.zeros((HEAD_PAD - QK_NOPE - QK_ROPE, tile), F32))
            q_ref[0, hd, QK_NOPE:, :] = jnp.concatenate(rope, axis=0).astype(BF16)


def _proj(x, sc, sh, tables, w_in_r, q_g, w_uq_t, kv_g, w_uk_p, w_uv_t, tile):
    b, s, d = x.shape
    with_q = tables is not None
    grid = (b, s // tile)
    row = lambda bi, i: (bi, i, 0)
    vec = lambda bi, i: (bi, 0, 0)
    const2 = lambda bi, i: (0, 0)
    k_out = pl.BlockSpec((1, N_HEADS, tile, HEAD_PAD), lambda bi, i: (bi, 0, i, 0))
    k_shape = jax.ShapeDtypeStruct((b, N_HEADS, s, HEAD_PAD), BF16)
    v_out = pl.BlockSpec((1, N_HEADS, 1, V_ROWS, tile), lambda bi, i: (bi, 0, i, 0, 0))
    v_shape = jax.ShapeDtypeStruct((b, N_HEADS, s // tile, V_ROWS, tile), BF16)
    in_specs = [pl.BlockSpec((1, tile, d), row),
                pl.BlockSpec((1, 1, d), vec), pl.BlockSpec((1, 1, d), vec)]
    args = [x, sc, sh]
    if with_q:
        lane_tabs, row_tabs = tables
        in_specs += [pl.BlockSpec((tile, LANES), lambda bi, i: (i, 0))] * 3
        in_specs += [pl.BlockSpec((2 * ROPE_FREQS, tile), lambda bi, i: (0, i))] * 2
        args += list(lane_tabs) + list(row_tabs)
    in_specs.append(pl.BlockSpec(w_in_r.shape, const2)); args.append(w_in_r)
    if with_q:
        in_specs += [pl.BlockSpec(q_g.shape, const2), pl.BlockSpec(w_uq_t.shape, const2)]
        args += [q_g, w_uq_t]
    in_specs += [pl.BlockSpec(kv_g.shape, const2), pl.BlockSpec(w_uk_p.shape, const2),
                 pl.BlockSpec(w_uv_t.shape, const2)]
    args += [kv_g, w_uk_p, w_uv_t]
    if with_q:
        q_out = pl.BlockSpec((1, N_HEADS, HEAD_PAD, tile), lambda bi, i: (bi, 0, 0, i))
        q_shape = jax.ShapeDtypeStruct((b, N_HEADS, HEAD_PAD, s), BF16)
        out_specs = [q_out, k_out, v_out, pl.BlockSpec((1, tile, POOL_WIDTH), row)]
        out_shape = [q_shape, k_shape, v_shape, jax.ShapeDtypeStruct((b, s, POOL_WIDTH), F32)]
    else:
        out_specs = [k_out, v_out]
        out_shape = [k_shape, v_shape]
    return pl.pallas_call(
        functools.partial(_proj_kernel, with_q=with_q),
        grid=grid, in_specs=in_specs, out_specs=out_specs, out_shape=out_shape,
        compiler_params=pltpu.CompilerParams(dimension_semantics=("arbitrary", "arbitrary"),
                                             vmem_limit_bytes=VMEM_LIMIT),
        name="proj" if with_q else "proj_ctx",
    )(*args)


def _attn_kernel(q_ref, kc_ref, vc_ref, k_ref, v_ref, o_ref, *, n_kblk, tk):
    tq = q_ref.shape[3]
    qs = [q_ref[0, hh] for hh in range(2)]

    def scores(hh, kb):
        return jnp.dot(kb, qs[hh], preferred_element_type=F32)

    def update(s_t, vb_t, m, acc):
        m_new = jnp.maximum(m, jnp.max(s_t, axis=0, keepdims=True))
        p_t = jnp.exp2(s_t - m_new).astype(BF16)
        acc = jnp.exp2(m - m_new) * acc + jnp.dot(vb_t, p_t, preferred_element_type=F32)
        return m_new, acc

    def run_items(items, state):
        pending = [scores(hh, kb()) for hh, kb, _ in items[:ATTN_AHEAD]]
        for j, (hh, _, vb) in enumerate(items):
            if j + ATTN_AHEAD < len(items):
                nh, nkb, _ = items[j + ATTN_AHEAD]
                pending.append(scores(nh, nkb()))
            state[hh] = update(pending.pop(0), vb(), *state[hh])
        return state

    def block_items(blk, off):
        out = []
        for sub in range(tk // ATTN_SUB):
            for hh in range(2):
                lo = sub * ATTN_SUB
                out.append((hh,
                            lambda hh=hh, lo=lo: k_ref[0, hh, pl.ds(off + lo, ATTN_SUB), :],
                            lambda hh=hh, lo=lo: v_ref[0, hh, blk, :, lo:lo + ATTN_SUB]))
        return out

    ctx_items = [(hh, lambda hh=hh: kc_ref[0, hh], lambda hh=hh: vc_ref[0, hh, 0])
                 for hh in range(2)]
    state = [(jnp.full((1, tq), NEG_BIG, F32), jnp.zeros((V_ROWS, tq), F32)) for _ in range(2)]
    n_iter = n_kblk // ATTN_UNROLL
    if n_iter == 1:
        items = ctx_items
        for blk in range(n_kblk):
            items = items + block_items(blk, blk * tk)
        state = run_items(items, state)
    else:
        state = run_items(ctx_items, state)

        def body(i, carry):
            items = []
            for r in range(ATTN_UNROLL):
                blk = i * ATTN_UNROLL + r
                items += block_items(blk, pl.multiple_of(blk * tk, tk))
            st = run_items(items, [(carry[0], carry[1]), (carry[2], carry[3])])
            return st[0] + st[1]

        carry = lax.fori_loop(0, n_iter, body, state[0] + state[1])
        state = [(carry[0], carry[1]), (carry[2], carry[3])]
    carry = state[0] + state[1]
    outs = [carry[2 * hh + 1][:V_HEAD] / carry[2 * hh + 1][ONES_ROW:ONES_ROW + 1] for hh in range(2)]
    o_ref[0] = jnp.concatenate(outs, axis=0).T.astype(o_ref.dtype)


def _attention(q_t, kc, vc_t, k, v_t):
    b, nh, dp, s = q_t.shape
    c = kc.shape[2]
    tq, tk = ATTN_TQ, ATTN_TK
    n_kblk = s // tk
    kern = functools.partial(_attn_kernel, n_kblk=n_kblk, tk=tk)
    return pl.pallas_call(
        kern,
        grid=(b, nh // 2, s // tq),
        in_specs=[pl.BlockSpec((1, 2, dp, tq), lambda bi, hp, qi: (bi, hp, 0, qi)),
                  pl.BlockSpec((1, 2, c, dp), lambda bi, hp, qi: (bi, hp, 0, 0)),
                  pl.BlockSpec((1, 2, 1, V_ROWS, c), lambda bi, hp, qi: (bi, hp, 0, 0, 0)),
                  pl.BlockSpec((1, 2, s, dp), lambda bi, hp, qi: (bi, hp, 0, 0)),
                  pl.BlockSpec((1, 2, n_kblk, V_ROWS, tk), lambda bi, hp, qi: (bi, hp, 0, 0, 0))],
        out_specs=pl.BlockSpec((1, tq, 2 * V_HEAD), lambda bi, hp, qi: (bi, qi, hp)),
        out_shape=jax.ShapeDtypeStruct((b, s, nh * V_HEAD), BF16),
        compiler_params=pltpu.CompilerParams(
            dimension_semantics=("arbitrary", "arbitrary", "arbitrary"),
            vmem_limit_bytes=VMEM_LIMIT),
        name="attn",
    )(q_t, kc, vc_t, k, v_t)


def _route(logits_t, bias_t):
    e, t = logits_t.shape
    scores = jax.nn.sigmoid(logits_t)
    biased = scores + bias_t
    neg_inf = F32(-jnp.inf)
    gscore = []
    for g in range(N_EXPERT_GROUPS):
        v = biased[g * GROUP_SIZE:(g + 1) * GROUP_SIZE]
        m1 = jnp.max(v, axis=0, keepdims=True)
        at_max = v == m1
        n_max = jnp.sum(at_max.astype(F32), axis=0, keepdims=True)
        m2 = jnp.max(jnp.where(at_max, neg_inf, v), axis=0, keepdims=True)
        gscore.append(m1 + jnp.where(n_max >= 2.0, m1, m2))
    masked = []
    for g in range(N_EXPERT_GROUPS):
        rank = jnp.zeros((1, t), F32)
        for o in range(N_EXPERT_GROUPS):
            if o == g:
                continue
            beats = (gscore[o] >= gscore[g]) if o < g else (gscore[o] > gscore[g])
            rank = rank + beats.astype(F32)
        keep = rank < float(TOPK_GROUPS)
        masked.append(jnp.where(keep, biased[g * GROUP_SIZE:(g + 1) * GROUP_SIZE], neg_inf))
    work = jnp.concatenate(masked, axis=0)
    rows = lax.broadcasted_iota(jnp.int32, (e, t), 0)
    sel = jnp.zeros((e, t), F32)
    for _ in range(TOP_K):
        m = jnp.max(work, axis=0, keepdims=True)
        first = jnp.min(jnp.where(work == m, rows, e), axis=0, keepdims=True)
        pick = rows == first
        sel = jnp.where(pick, 1.0, sel)
        work = jnp.where(pick, neg_inf, work)
    w = sel * scores
    gates = w / jnp.sum(w, axis=0, keepdims=True) * ROUTED_SCALE
    return sel, gates


def _mix_kernel(attn_ref, u_ref, up_ref, un_ref, x_ref, g1_ref, sc2_ref, sh2_ref, wpool_ref,
                pscale_ref, wout_ref, ln_g_ref, ln_b_ref, wr_ref, rb_ref, tri_ref, ones_ref,
                x1_ref, hx_ref, pos_ref, seg_ref, cnt_ref, uext_ref, *, seq):
    i = pl.program_id(1)

    @pl.when((pl.program_id(0) == 0) & (i == 0))
    def _():
        cnt_ref[...] = jnp.zeros_like(cnt_ref)

    tile = u_ref.shape[1]
    u = u_ref[0]
    uext_ref[0:POOL_HALO] = jnp.where(i == 0, 0.0, up_ref[0])
    uext_ref[POOL_HALO:POOL_HALO + tile] = u
    uext_ref[POOL_HALO + tile:] = jnp.where(i == pl.num_programs(1) - 1, 0.0, un_ref[0])
    t = i * tile + lax.broadcasted_iota(jnp.int32, (tile, POOL_GC), 0)
    pooled = []
    for g, w in enumerate(POOL_WINDOWS):
        lanes = slice(g * POOL_GC, (g + 1) * POOL_GC)
        tot = uext_ref[POOL_HALO - w // 2:POOL_HALO - w // 2 + tile, lanes]
        for dlt in range(1, w):
            start = POOL_HALO - w // 2 + dlt
            tot = tot + uext_ref[start:start + tile, lanes]
        cnt = (jnp.minimum(t - w // 2 + w, seq) - jnp.maximum(t - w // 2, 0)).astype(F32)
        pg = (tot / cnt - u[:, lanes]).astype(BF16)
        po = jnp.dot(pg, wpool_ref[g], preferred_element_type=F32) * pscale_ref[:, lanes]
        pooled.append(po.astype(BF16))
    mixed = jnp.concatenate([attn_ref[0]] + pooled, axis=1)
    y = jnp.dot(mixed, wout_ref[...], preferred_element_type=F32)
    x1 = _layer_norm(ALPHA * x_ref[0] + g1_ref[0] * y, ln_g_ref[...], ln_b_ref[...])
    x1_ref[0] = x1
    h2 = x1 * (1.0 + sc2_ref[0]) + sh2_ref[0]
    logits_t = lax.dot_general(wr_ref[...], h2, (((1,), (1,)), ((), ())),
                               preferred_element_type=F32, precision=lax.Precision.HIGHEST)
    sel, gates_t = _route(logits_t, rb_ref[...])
    g_hi = gates_t.astype(BF16)
    g_lo = (gates_t - g_hi.astype(F32)).astype(BF16)
    g_tok = jnp.concatenate([g_hi.astype(F32), g_lo.astype(F32)], axis=0).T
    hx_ref[0] = jnp.concatenate([h2.astype(BF16), g_tok.astype(BF16)], axis=1)
    sel_b = sel.astype(BF16)
    start = cnt_ref[...]
    pos_t = jnp.dot(sel_b, tri_ref[...], preferred_element_type=F32) + start[:, 0:1]
    pos_ref[0] = jnp.where(sel > 0.0, pos_t, -1.0).astype(jnp.int32)
    n_tok = jnp.dot(sel_b, ones_ref[...], preferred_element_type=F32)
    n_chunk = jnp.floor((n_tok + (SEG_ROWS - 1)) * (1.0 / SEG_ROWS))
    seg_ref[0, 0] = start.astype(jnp.int32)
    seg_ref[0, 1] = n_chunk.astype(jnp.int32)
    cnt_ref[...] = start + n_chunk * SEG_ROWS


def _mix(attn, u, x, g1, sc2, sh2, w_pool, pool_scale, w_out, ln_g, ln_b, w_r_t, rb_t):
    b, s, d = x.shape
    tile = MIX_TILE
    hb = tile // POOL_HALO
    row = lambda bi, i: (bi, i, 0)
    vec = lambda bi, i: (bi, 0, 0)
    c2 = lambda bi, i: (0, 0)
    lane_row = lambda bi, i: (bi, 0, i)
    tri = (lax.broadcasted_iota(jnp.int32, (tile, tile), 0)
           < lax.broadcasted_iota(jnp.int32, (tile, tile), 1)).astype(BF16)
    ones = jnp.ones((tile, LANES), BF16)
    return pl.pallas_call(
        functools.partial(_mix_kernel, seq=s),
        grid=(b, s // tile),
        in_specs=[pl.BlockSpec((1, tile, POOL_WIDTH), row),
                  pl.BlockSpec((1, tile, POOL_WIDTH), row),
                  pl.BlockSpec((1, POOL_HALO, POOL_WIDTH),
                               lambda bi, i: (bi, jnp.maximum(i * hb - 1, 0), 0)),
                  pl.BlockSpec((1, POOL_HALO, POOL_WIDTH),
                               lambda bi, i: (bi, jnp.minimum((i + 1) * hb, s // POOL_HALO - 1), 0)),
                  pl.BlockSpec((1, tile, d), row),
                  pl.BlockSpec((1, 1, d), vec), pl.BlockSpec((1, 1, d), vec),
                  pl.BlockSpec((1, 1, d), vec),
                  pl.BlockSpec(w_pool.shape, lambda bi, i: (0, 0, 0)),
                  pl.BlockSpec(pool_scale.shape, c2),
                  pl.BlockSpec(w_out.shape, c2),
                  pl.BlockSpec(ln_g.shape, c2), pl.BlockSpec(ln_b.shape, c2),
                  pl.BlockSpec(w_r_t.shape, c2), pl.BlockSpec(rb_t.shape, c2),
                  pl.BlockSpec(tri.shape, c2), pl.BlockSpec(ones.shape, c2)],
        out_specs=[pl.BlockSpec((1, tile, d), row),
                   pl.BlockSpec((1, tile, HX_WIDTH), row),
                   pl.BlockSpec((1, N_EXPERTS, tile), lane_row),
                   pl.BlockSpec((1, 2, N_EXPERTS, LANES),
                                lambda bi, i: (bi * (s // tile) + i, 0, 0, 0)),
                   pl.BlockSpec((N_EXPERTS, LANES), c2)],
        out_shape=[jax.ShapeDtypeStruct((b, s, d), F32),
                   jax.ShapeDtypeStruct((b, s, HX_WIDTH), BF16),
                   jax.ShapeDtypeStruct((b, N_EXPERTS, s), jnp.int32),
                   jax.ShapeDtypeStruct((b * (s // tile), 2, N_EXPERTS, LANES), jnp.int32),
                   jax.ShapeDtypeStruct((N_EXPERTS, LANES), F32)],
        scratch_shapes=[pltpu.VMEM((tile + 2 * POOL_HALO, POOL_WIDTH), F32)],
        compiler_params=pltpu.CompilerParams(dimension_semantics=("arbitrary", "arbitrary"),
                                             vmem_limit_bytes=VMEM_LIMIT),
        name="mix",
    )(attn, u, u, u, x, g1, sc2, sh2, w_pool, pool_scale, w_out, ln_g, ln_b, w_r_t, rb_t,
      tri, ones)


def _one_hot(pos_row, start, window, n_tok):
    rows = lax.broadcasted_iota(jnp.int32, (SEG_WIN, n_tok), 0)
    hit = rows == (pos_row - (start + window * SEG_WIN))
    return jnp.where(hit, 1.0, 0.0).astype(BF16)


def _window_rows(base_ref, start_ref, step, e, window):
    row = base_ref[e] + start_ref[step * N_EXPERTS + e] + window * SEG_WIN
    return pl.ds(pl.multiple_of(row, SEG_ROWS), SEG_WIN)


def _extra_windows(nchunk_ref, step, e):
    rows = nchunk_ref[step * N_EXPERTS + e] * SEG_ROWS
    return jnp.maximum((rows + SEG_WIN - 1) // SEG_WIN, 1)


def _dispatch_kernel(base_ref, start_ref, nchunk_ref, over_ref, total_ref, pcnt_ref, nu_ref,
                     hx_ref, pos_ref, xs_hbm, stage, extra, zero_ref, sem):
    i = pl.program_id(0)
    n_tiles = xs_hbm.shape[0] // MOE_TM
    n_tok = hx_ref.shape[0]

    def zero_copy(row, n_rows):
        return pltpu.make_async_copy(zero_ref.at[pl.ds(0, n_rows)],
                                     xs_hbm.at[pl.ds(pl.multiple_of(row, SEG_ROWS), n_rows)],
                                     sem.at[1])

    @pl.when(i == 0)
    def _():
        zero_ref[...] = jnp.zeros_like(zero_ref)
        for wait in (False, True):
            def per_tail(j, c):
                cp = zero_copy(j * MOE_TM, MOE_TM)
                cp.wait() if wait else cp.start()
                return c
            lax.fori_loop(nu_ref[0], n_tiles, per_tail, 0)

            def per_expert(e, c):
                row0 = base_ref[e] + total_ref[e]

                def per_block(j, c2):
                    cp = zero_copy(row0 + j * SEG_ROWS, SEG_ROWS)
                    cp.wait() if wait else cp.start()
                    return c2
                return lax.fori_loop(0, (pcnt_ref[e] - total_ref[e]) // SEG_ROWS, per_block, c)
            lax.fori_loop(0, N_EXPERTS, per_expert, 0)

    slot = i % 2
    hx = hx_ref[...]
    for g in range(N_EXPERTS // WIN_GROUP):
        oh = jnp.concatenate(
            [_one_hot(pos_ref[0, e:e + 1, :], start_ref[i * N_EXPERTS + e], 0, n_tok)
             for e in range(g * WIN_GROUP, (g + 1) * WIN_GROUP)], axis=0)
        rows = pl.ds(g * WIN_GROUP * SEG_WIN, WIN_GROUP * SEG_WIN)
        stage[slot, rows, :] = jnp.dot(oh, hx, preferred_element_type=F32).astype(BF16)

    def wait_windows(which):
        pltpu.make_async_copy(stage.at[which], xs_hbm.at[pl.ds(0, N_EXPERTS * SEG_WIN)],
                              sem.at[0]).wait()

    @pl.when(i > 0)
    def _():
        wait_windows(1 - slot)

    for e in range(N_EXPERTS):
        pltpu.make_async_copy(stage.at[slot, pl.ds(e * SEG_WIN, SEG_WIN)],
                              xs_hbm.at[_window_rows(base_ref, start_ref, i, e, 0)],
                              sem.at[0]).start()

    @pl.when(i == pl.num_programs(0) - 1)
    def _():
        wait_windows(slot)

    @pl.when(over_ref[i] > 0)
    def _():
        def more_windows(e, c):
            def one(window, c2):
                oh = _one_hot(pos_ref[0, pl.ds(e, 1), :], start_ref[i * N_EXPERTS + e], window,
                              n_tok)
                extra[...] = jnp.dot(oh, hx, preferred_element_type=F32).astype(BF16)
                cp = pltpu.make_async_copy(
                    extra, xs_hbm.at[_window_rows(base_ref, start_ref, i, e, window)], sem.at[2])
                cp.start()
                cp.wait()
                return c2
            return lax.fori_loop(1, _extra_windows(nchunk_ref, i, e), one, c)
        lax.fori_loop(0, N_EXPERTS, more_windows, 0)


def _dispatch(base, seg_start, seg_chunks, seg_over, total, pcnt, n_used, hx, pos, n_slots):
    t, width = hx.shape
    tpb = pos.shape[2] // TOK_TILE
    return pl.pallas_call(
        _dispatch_kernel,
        grid_spec=pltpu.PrefetchScalarGridSpec(
            num_scalar_prefetch=7, grid=(t // TOK_TILE,),
            in_specs=[pl.BlockSpec((TOK_TILE, width), lambda i, *_: (i, 0)),
                      pl.BlockSpec((1, N_EXPERTS, TOK_TILE), lambda i, *_: (i // tpb, 0, i % tpb))],
            out_specs=pl.BlockSpec(memory_space=pl.ANY),
            scratch_shapes=[pltpu.VMEM((2, N_EXPERTS * SEG_WIN, width), BF16),
                            pltpu.VMEM((SEG_WIN, width), BF16),
                            pltpu.VMEM((MOE_TM, width), BF16),
                            pltpu.SemaphoreType.DMA((3,))]),
        out_shape=jax.ShapeDtypeStruct((n_slots, width), BF16),
        compiler_params=pltpu.CompilerParams(dimension_semantics=("arbitrary",),
                                             vmem_limit_bytes=VMEM_LIMIT),
        name="dispatch",
    )(base, seg_start, seg_chunks, seg_over, total, pcnt, n_used, hx, pos)


def _experts_kernel(te_ref, nu_ref, xs_ref, wg_ref, wu_ref, wd_ref, ys_ref, wg_b, wu_b, wd_b):
    i = pl.program_id(0)

    @pl.when(i < nu_ref[0])
    def _():
        @pl.when((i == 0) | (te_ref[i] != te_ref[jnp.maximum(i - 1, 0)]))
        def _():
            wg_b[...] = wg_ref[0].astype(BF16)
            wu_b[...] = wu_ref[0].astype(BF16)
            wd_b[...] = wd_ref[0].astype(BF16)

        half = MOE_TM // 2
        halves = [pl.ds(k * half, half) for k in range(2)]
        hidden = []
        for rows in halves:
            x = xs_ref[rows, :D_MODEL]
            hidden.append((jnp.dot(x, wg_b[...], preferred_element_type=F32),
                           jnp.dot(x, wu_b[...], preferred_element_type=F32)))
        for rows, (hg, hu) in zip(halves, hidden):
            g = xs_ref[rows, D_MODEL:].astype(F32)
            lane = lax.broadcasted_iota(jnp.int32, g.shape, 1)
            mine = (lane == te_ref[i]) | (lane == te_ref[i] + N_EXPERTS)
            gate = jnp.sum(jnp.where(mine, g, 0.0), axis=1, keepdims=True)
            a = (_silu(hg) * hu * gate).astype(BF16)
            ys_ref[rows, :] = jnp.dot(a, wd_b[...], preferred_element_type=F32).astype(ys_ref.dtype)

    @pl.when(i >= nu_ref[0])
    def _():
        ys_ref[...] = jnp.zeros_like(ys_ref)


def _experts(tile_expert, n_used, xs, w_e_gate, w_e_up, w_e_down):
    n_slots, width = xs.shape
    _, d, f = w_e_gate.shape
    slot_tile = lambda i, te, nu: (jnp.minimum(i, nu[0] - 1), 0)
    expert = lambda i, te, nu: (te[i], 0, 0)
    return pl.pallas_call(
        _experts_kernel,
        grid_spec=pltpu.PrefetchScalarGridSpec(
            num_scalar_prefetch=2, grid=(n_slots // MOE_TM,),
            in_specs=[pl.BlockSpec((MOE_TM, width), slot_tile),
                      pl.BlockSpec((1, d, f), expert), pl.BlockSpec((1, d, f), expert),
                      pl.BlockSpec((1, f, d), expert)],
            out_specs=pl.BlockSpec((MOE_TM, d), lambda i, te, nu: (i, 0)),
            scratch_shapes=[pltpu.VMEM((d, f), BF16), pltpu.VMEM((d, f), BF16),
                            pltpu.VMEM((f, d), BF16)]),
        out_shape=jax.ShapeDtypeStruct((n_slots, d), BF16),
        compiler_params=pltpu.CompilerParams(dimension_semantics=("arbitrary",),
                                             vmem_limit_bytes=VMEM_LIMIT),
        name="experts",
    )(tile_expert, n_used, xs, w_e_gate, w_e_up, w_e_down)


def _dot_tn(a, b):
    return lax.dot_general(a, b, (((0,), (0,)), ((), ())), preferred_element_type=F32)


def _combine_kernel(base_ref, start_ref, nchunk_ref, over_ref, hx_ref, pos_ref, x1_ref, g2_ref, wsg_ref,
                    wsu_ref, wsd_ref, ln_g_ref, ln_b_ref, ys_hbm, o_ref, win, extra, acc_ref, sem):
    i = pl.program_id(0)
    n = pl.num_programs(0)
    n_tok = hx_ref.shape[0]

    def fetch(step, slot):
        for e in range(N_EXPERTS):
            pltpu.make_async_copy(ys_hbm.at[_window_rows(base_ref, start_ref, step, e, 0)],
                                  win.at[slot, pl.ds(e * SEG_WIN, SEG_WIN)],
                                  sem.at[slot]).start()

    @pl.when(i == 0)
    def _():
        fetch(0, 0)

    @pl.when(i + 1 < n)
    def _():
        fetch(i + 1, (i + 1) % 2)

    slot = i % 2
    h = hx_ref[:, :D_MODEL]
    a = _silu(jnp.dot(h, wsg_ref[...], preferred_element_type=F32)) * jnp.dot(
        h, wsu_ref[...], preferred_element_type=F32)
    moe = jnp.dot(a.astype(BF16), wsd_ref[...], preferred_element_type=F32)
    pltpu.make_async_copy(ys_hbm.at[pl.ds(0, N_EXPERTS * SEG_WIN)], win.at[slot], sem.at[slot]).wait()
    for g in range(N_EXPERTS // WIN_GROUP):
        oh = jnp.concatenate(
            [_one_hot(pos_ref[0, e:e + 1, :], start_ref[i * N_EXPERTS + e], 0, n_tok)
             for e in range(g * WIN_GROUP, (g + 1) * WIN_GROUP)], axis=0)
        rows = pl.ds(g * WIN_GROUP * SEG_WIN, WIN_GROUP * SEG_WIN)
        moe = moe + _dot_tn(oh, win[slot, rows, :])
    acc_ref[...] = moe

    @pl.when(over_ref[i] > 0)
    def _():
        def more_windows(e, c):
            def one(window, c2):
                cp = pltpu.make_async_copy(
                    ys_hbm.at[_window_rows(base_ref, start_ref, i, e, window)], extra, sem.at[2])
                cp.start()
                cp.wait()
                oh = _one_hot(pos_ref[0, pl.ds(e, 1), :], start_ref[i * N_EXPERTS + e], window,
                              n_tok)
                acc_ref[...] += _dot_tn(oh, extra[...])
                return c2
            return lax.fori_loop(1, _extra_windows(nchunk_ref, i, e), one, c)
        lax.fori_loop(0, N_EXPERTS, more_windows, 0)

    z = ALPHA * x1_ref[...] + g2_ref[0] * acc_ref[...]
    o_ref[...] = _layer_norm(z, ln_g_ref[...], ln_b_ref[...])


def _combine(base, seg_start, seg_chunks, seg_over, hx, pos, x1, g2, w_s_gate, w_s_up, w_s_down,
             ln_g, ln_b, ys):
    t, d = x1.shape
    tpb = pos.shape[2] // TOK_TILE
    row = lambda i, *_: (i, 0)
    c2 = lambda i, *_: (0, 0)
    return pl.pallas_call(
        _combine_kernel,
        grid_spec=pltpu.PrefetchScalarGridSpec(
            num_scalar_prefetch=4, grid=(t // TOK_TILE,),
            in_specs=[pl.BlockSpec((TOK_TILE, hx.shape[1]), row),
                      pl.BlockSpec((1, N_EXPERTS, TOK_TILE), lambda i, *_: (i // tpb, 0, i % tpb)),
                      pl.BlockSpec((TOK_TILE, d), row),
                      pl.BlockSpec((1, 1, d), lambda i, *_: (i // tpb, 0, 0)),
                      pl.BlockSpec(w_s_gate.shape, c2), pl.BlockSpec(w_s_up.shape, c2),
                      pl.BlockSpec(w_s_down.shape, c2),
                      pl.BlockSpec(ln_g.shape, c2), pl.BlockSpec(ln_b.shape, c2),
                      pl.BlockSpec(memory_space=pl.ANY)],
            out_specs=pl.BlockSpec((TOK_TILE, d), row),
            scratch_shapes=[pltpu.VMEM((2, N_EXPERTS * SEG_WIN, d), BF16),
                            pltpu.VMEM((SEG_WIN, d), BF16),
                            pltpu.VMEM((TOK_TILE, d), F32),
                            pltpu.SemaphoreType.DMA((3,))]),
        out_shape=jax.ShapeDtypeStruct((t, d), F32),
        compiler_params=pltpu.CompilerParams(dimension_semantics=("arbitrary",),
                                             vmem_limit_bytes=VMEM_LIMIT),
        name="combine",
    )(base, seg_start, seg_chunks, seg_over, hx, pos, x1, g2, w_s_gate, w_s_up, w_s_down, ln_g,
      ln_b, ys)


def _moe(hx, pos, seg, counts, x1, g2, w_e_gate, w_e_up, w_e_down, w_s_gate, w_s_up, w_s_down,
         ln_g, ln_b):
    b, s, d = x1.shape
    t = b * s
    n_seg = (t // TOK_TILE) * N_EXPERTS
    max_rows = t * TOP_K + n_seg * (SEG_ROWS - 1) + N_EXPERTS * (SEG_WIN + MOE_TM)
    n_tiles = -(-max_rows // MOE_TM)
    total = jnp.round(counts[:, 0]).astype(jnp.int32)
    pcnt = (total + SEG_WIN + MOE_TM - 1) // MOE_TM * MOE_TM
    ends = jnp.cumsum(pcnt)
    base = ends - pcnt
    n_used = (ends[-1] // MOE_TM).reshape(1)
    tile_ids = jnp.arange(n_tiles, dtype=jnp.int32)
    tile_expert = jnp.sum((ends[None, :] <= tile_ids[:, None] * MOE_TM).astype(jnp.int32), axis=1)
    tile_expert = jnp.minimum(tile_expert, N_EXPERTS - 1)
    tile_expert = jnp.where(tile_ids < n_used, tile_expert, tile_expert[n_used[0] - 1])
    seg_start = seg[:, 0, :, 0].reshape(n_seg)
    seg_chunks = seg[:, 1, :, 0]
    seg_over = (jnp.max(seg_chunks, axis=1) * SEG_ROWS > SEG_WIN).astype(jnp.int32)
    seg_chunks = seg_chunks.reshape(n_seg)

    hxf = hx.reshape(t, hx.shape[2])
    xs = _dispatch(base, seg_start, seg_chunks, seg_over, total, pcnt, n_used, hxf, pos,
                   n_tiles * MOE_TM)
    ys = _experts(tile_expert, n_used, xs, w_e_gate, w_e_up, w_e_down)
    out = _combine(base, seg_start, seg_chunks, seg_over, hxf, pos, x1.reshape(t, d), g2, w_s_gate,
                   w_s_up, w_s_down, ln_g, ln_b, ys)
    return out.reshape(b, s, d)


def _rope_tables(seq):
    t = jnp.arange(seq)
    pos = jnp.stack([t // GRID_W, t % GRID_W], axis=-1).astype(F32)
    inv_freq = ROPE_THETA ** (-jnp.arange(ROPE_FREQS, dtype=F32) / ROPE_FREQS)
    ang = pos[:, :, None] * inv_freq
    cos, sin = jnp.cos(ang), jnp.sin(ang)
    zero = jnp.zeros_like(sin)
    cos_r = jnp.stack([cos, cos], axis=2).reshape(seq, QK_ROPE)
    sin_lo = jnp.stack([-sin, zero], axis=2).reshape(seq, QK_ROPE)
    sin_hi = jnp.stack([zero, sin], axis=2).reshape(seq, QK_ROPE)
    pads = ((0, 0), (QK_NOPE, HEAD_PAD - QK_NOPE - QK_ROPE))
    lane_tabs = (jnp.pad(cos_r, pads, constant_values=1.0), jnp.pad(sin_lo, pads),
                 jnp.pad(sin_hi, pads))
    row_tabs = (cos.reshape(seq, 2 * ROPE_FREQS).T, sin.reshape(seq, 2 * ROPE_FREQS).T)
    return lane_tabs, row_tabs


def _pad_heads(w, width, padded):
    k = w.shape[0]
    w = jnp.pad(w.reshape(k, N_HEADS, width), ((0, 0), (0, 0), (0, padded - width)))
    return w.reshape(k, N_HEADS * padded)


def kernel(x, c, ctx, c_ctx, w_ada, b_ada, w_in, q_norm_g, w_uq, kv_norm_g, w_ukv, w_pool, pool_scale, w_out, ln1_g, ln1_b, w_router, router_bias, w_e_gate, w_e_up, w_e_down, w_s_gate, w_s_up, w_s_down, ln2_g, ln2_b):
    assert w_ada.shape[0] == 1, "single-layer block"
    b, s, d = x.shape

    cvec = jnp.concatenate([c, c_ctx[None], jnp.zeros((SUBLANES - b - 1, d), F32)], axis=0)
    mod = _ada(cvec, w_ada[0], b_ada)
    sh1, sc1, g1, sh2, sc2, g2 = [mod[:b, k * d:(k + 1) * d][:, None, :] for k in range(6)]
    sh1c, sc1c = [jnp.broadcast_to(mod[b, k * d:(k + 1) * d], (b, 1, d)) for k in range(2)]

    wi = w_in[0]
    kr_cols = jnp.pad(wi[:, Q_LORA + KV_LORA:Q_LORA + KV_LORA + QK_ROPE],
                      ((0, 0), (QK_NOPE, HEAD_PAD - QK_NOPE - QK_ROPE)))
    w_in_r = jnp.concatenate([wi[:, :Q_LORA + KV_LORA], wi[:, Q_LORA + KV_LORA + QK_ROPE:], kr_cols],
                             axis=1).astype(BF16)
    w_uq_t = _pad_heads(w_uq[0], QK_NOPE + QK_ROPE, HEAD_PAD).T.astype(BF16)
    wkv = w_ukv[0].reshape(KV_LORA, N_HEADS, QK_NOPE + V_HEAD)
    w_uk_p = _pad_heads(wkv[:, :, :QK_NOPE].reshape(KV_LORA, -1), QK_NOPE, HEAD_PAD).astype(BF16)
    w_uv_t = _pad_heads(wkv[:, :, QK_NOPE:].reshape(KV_LORA, -1), V_HEAD, V_ROWS).T.astype(BF16)
    tables = _rope_tables(s)

    q_t, k, v_t, u = _proj(x, sc1, sh1, tables, w_in_r, q_norm_g, w_uq_t, kv_norm_g, w_uk_p,
                           w_uv_t, PROJ_TILE)
    kc, vc_t = _proj(ctx, sc1c, sh1c, None, w_in_r, None, None, kv_norm_g, w_uk_p, w_uv_t,
                     ctx.shape[1])
    attn = _attention(q_t, kc, vc_t, k, v_t)

    x1, hx, pos, seg, counts = _mix(attn, u, x, g1, sc2, sh2, w_pool[0].astype(BF16),
                                    pool_scale, w_out[0].astype(BF16), ln1_g, ln1_b,
                                    w_router[0].T, router_bias[0][:, None])
    return _moe(hx, pos, seg, counts, x1, g2, w_e_gate[0], w_e_up[0], w_e_down[0],
                w_s_gate[0].astype(BF16), w_s_up[0].astype(BF16), w_s_down[0].astype(BF16),
                ln2_g, ln2_b)
```

```python
import functools
import math

import jax
import jax.numpy as jnp
from jax import lax
from jax.experimental import pallas as pl
from jax.experimental.pallas import tpu as pltpu

F32 = jnp.float32
BF16 = jnp.bfloat16

D_MODEL = 1024
GRID_W = 64
N_HEADS = 8
Q_LORA = 512
KV_LORA = 256
QK_NOPE = 64
QK_ROPE = 32
V_HEAD = 64
ROPE_FREQS = QK_ROPE // 4
ROPE_THETA = 10000.0
ATTN_SCALE = 1.0 / math.sqrt(QK_NOPE + QK_ROPE)
LOG2_E = math.log2(math.e)
POOL_GROUPS = 4
POOL_WINDOWS = (2, 4, 8, 16)
POOL_WIDTH = 512
POOL_GC = POOL_WIDTH // POOL_GROUPS
POOL_HALO = 8
N_EXPERTS = 64
N_EXPERT_GROUPS = 8
GROUP_SIZE = N_EXPERTS // N_EXPERT_GROUPS
TOPK_GROUPS = 4
TOP_K = 8
D_EXPERT = 256
ROUTED_SCALE = 2.5
LN_EPS = 1e-5
RMS_EPS = 1e-6
ALPHA = 2.0 ** 0.25

LANES = 128
SUBLANES = 8
HEAD_PAD = LANES
V_ROWS = 80
ONES_ROW = V_HEAD
IN_PAD = Q_LORA + KV_LORA + POOL_WIDTH + LANES

PROJ_TILE = 512
ATTN_TQ = 512
ATTN_TK = PROJ_TILE
ATTN_SUB = 256
ATTN_AHEAD = 2
ATTN_UNROLL = 16
TOK_TILE = 256
MIX_TILE = TOK_TILE
MOE_TM = 1024
SEG_ROWS = 16
SEG_WIN = 64
WIN_GROUP = 16
HX_WIDTH = D_MODEL + 2 * N_EXPERTS
VMEM_LIMIT = 48 * 1024 * 1024
NEG_BIG = -1e30


def _silu(v):
    return v * jax.nn.sigmoid(v)


def _layer_norm(z, g, b):
    mu = jnp.mean(z, axis=-1, keepdims=True)
    zc = z - mu
    var = jnp.mean(zc * zc, axis=-1, keepdims=True)
    return zc * lax.rsqrt(var + LN_EPS) * g + b


def _rms_norm(v, g):
    return v * lax.rsqrt(jnp.mean(v * v, axis=-1, keepdims=True) + RMS_EPS) * g


def _dot_nt(a, b):
    return lax.dot_general(a, b, (((1,), (1,)), ((), ())), preferred_element_type=F32)


def _ada_kernel(c_ref, w_ref, b_ref, o_ref):
    cv = _silu(c_ref[...])
    o_ref[...] = jnp.dot(cv, w_ref[...], preferred_element_type=F32,
                         precision=lax.Precision.HIGHEST) + b_ref[...]


def _ada(cvec, w_ada, b_ada):
    rows, d = cvec.shape
    n = w_ada.shape[1]
    tn = 1024
    return pl.pallas_call(
        _ada_kernel,
        grid=(n // tn,),
        in_specs=[pl.BlockSpec((rows, d), lambda j: (0, 0)),
                  pl.BlockSpec((d, tn), lambda j: (0, j)),
                  pl.BlockSpec((1, tn), lambda j: (0, j))],
        out_specs=pl.BlockSpec((rows, tn), lambda j: (0, j)),
        out_shape=jax.ShapeDtypeStruct((rows, n), F32),
        compiler_params=pltpu.CompilerParams(dimension_semantics=("arbitrary",),
                                             vmem_limit_bytes=VMEM_LIMIT),
        name="ada",
    )(cvec, w_ada, b_ada)


def _rope_lanes(v, cos, sin_lo, sin_hi):
    return v * cos + pltpu.roll(v, LANES - 8, axis=1) * sin_lo + pltpu.roll(v, 8, axis=1) * sin_hi


def _proj_kernel(*refs, with_q):
    if with_q:
        (x_ref, sc_ref, sh_ref, cos_ref, slo_ref, shi_ref, cos_t_ref, sin_t_ref, win_ref, qg_ref,
         wuq_ref, kvg_ref, wuk_ref, wuv_ref, q_ref, k_ref, v_ref, u_ref) = refs
    else:
        (x_ref, sc_ref, sh_ref, win_ref, kvg_ref, wuk_ref, wuv_ref, k_ref, v_ref) = refs
    h = (x_ref[0] * (1.0 + sc_ref[0]) + sh_ref[0]).astype(BF16)
    p = jnp.dot(h, win_ref[...], preferred_element_type=F32)
    tile = p.shape[0]
    kr = p[:, IN_PAD - LANES:]
    kvn = _rms_norm(p[:, Q_LORA:Q_LORA + KV_LORA], kvg_ref[...]).astype(BF16)
    kfull = jnp.dot(kvn, wuk_ref[...], preferred_element_type=F32)
    v_t = _dot_nt(wuv_ref[...], kvn)
    row = lax.broadcasted_iota(jnp.int32, (N_HEADS * V_ROWS, 1), 0)
    v_t = v_t + (row % V_ROWS == ONES_ROW).astype(F32)
    if with_q:
        kr = _rope_lanes(kr, cos_ref[...], slo_ref[...], shi_ref[...])
        u_ref[0] = p[:, Q_LORA + KV_LORA:Q_LORA + KV_LORA + POOL_WIDTH]
        qn = _rms_norm(p[:, :Q_LORA], qg_ref[...]).astype(BF16)
        q_t = _dot_nt(wuq_ref[...], qn) * (ATTN_SCALE * LOG2_E)
        cos_t, sin_t = cos_t_ref[...], sin_t_ref[...]
    for hd in range(N_HEADS):
        k_ref[0, hd] = (kfull[:, hd * HEAD_PAD:(hd + 1) * HEAD_PAD] + kr).astype(BF16)
        v_ref[0, hd, 0] = v_t[hd * V_ROWS:(hd + 1) * V_ROWS].astype(BF16)
        if with_q:
            base = hd * HEAD_PAD
            q_ref[0, hd, 0:QK_NOPE, :] = q_t[base:base + QK_NOPE].astype(BF16)
            rope = []
            for ax in range(2):
                lo = q_t[base + QK_NOPE + 16 * ax:base + QK_NOPE + 16 * ax + 8]
                hi = q_t[base + QK_NOPE + 16 * ax + 8:base + QK_NOPE + 16 * ax + 16]
                cs, sn = cos_t[8 * ax:8 * ax + 8], sin_t[8 * ax:8 * ax + 8]
                rope += [lo * cs - hi * sn, hi * cs + lo * sn]
            rope.append(jnp.zeros((HEAD_PAD - QK_NOPE - QK_ROPE, tile), F32))
            q_ref[0, hd, QK_NOPE:, :] = jnp.concatenate(rope, axis=0).astype(BF16)


def _proj(x, sc, sh, tables, w_in_r, q_g, w_uq_t, kv_g, w_uk_p, w_uv_t, tile):
    b, s, d = x.shape
    with_q = tables is not None
    grid = (b, s // tile)
    row = lambda bi, i: (bi, i, 0)
    vec = lambda bi, i: (bi, 0, 0)
    const2 = lambda bi, i: (0, 0)
    k_out = pl.BlockSpec((1, N_HEADS, tile, HEAD_PAD), lambda bi, i: (bi, 0, i, 0))
    k_shape = jax.ShapeDtypeStruct((b, N_HEADS, s, HEAD_PAD), BF16)
    v_out = pl.BlockSpec((1, N_HEADS, 1, V_ROWS, tile), lambda bi, i: (bi, 0, i, 0, 0))
    v_shape = jax.ShapeDtypeStruct((b, N_HEADS, s // tile, V_ROWS, tile), BF16)
    in_specs = [pl.BlockSpec((1, tile, d), row),
                pl.BlockSpec((1, 1, d), vec), pl.BlockSpec((1, 1, d), vec)]
    args = [x, sc, sh]
    if with_q:
        lane_tabs, row_tabs = tables
        in_specs += [pl.BlockSpec((tile, LANES), lambda bi, i: (i, 0))] * 3
        in_specs += [pl.BlockSpec((2 * ROPE_FREQS, tile), lambda bi, i: (0, i))] * 2
        args += list(lane_tabs) + list(row_tabs)
    in_specs.append(pl.BlockSpec(w_in_r.shape, const2)); args.append(w_in_r)
    if with_q:
        in_specs += [pl.BlockSpec(q_g.shape, const2), pl.BlockSpec(w_uq_t.shape, const2)]
        args += [q_g, w_uq_t]
    in_specs += [pl.BlockSpec(kv_g.shape, const2), pl.BlockSpec(w_uk_p.shape, const2),
                 pl.BlockSpec(w_uv_t.shape, const2)]
    args += [kv_g, w_uk_p, w_uv_t]
    if with_q:
        q_out = pl.BlockSpec((1, N_HEADS, HEAD_PAD, tile), lambda bi, i: (bi, 0, 0, i))
        q_shape = jax.ShapeDtypeStruct((b, N_HEADS, HEAD_PAD, s), BF16)
        out_specs = [q_out, k_out, v_out, pl.BlockSpec((1, tile, POOL_WIDTH), row)]
        out_shape = [q_shape, k_shape, v_shape, jax.ShapeDtypeStruct((b, s, POOL_WIDTH), F32)]
    else:
        out_specs = [k_out, v_out]
        out_shape = [k_shape, v_shape]
    return pl.pallas_call(
        functools.partial(_proj_kernel, with_q=with_q),
        grid=grid, in_specs=in_specs, out_specs=out_specs, out_shape=out_shape,
        compiler_params=pltpu.CompilerParams(dimension_semantics=("arbitrary", "arbitrary"),
                                             vmem_limit_bytes=VMEM_LIMIT),
        name="proj" if with_q else "proj_ctx",
    )(*args)


def _attn_kernel(q_ref, kc_ref, vc_ref, k_ref, v_ref, o_ref, *, n_kblk, tk):
    tq = q_ref.shape[3]
    qs = [q_ref[0, hh] for hh in range(2)]

    def scores(hh, kb):
        return jnp.dot(kb, qs[hh], preferred_element_type=F32)

    def update(s_t, vb_t, m, acc):
        m_new = jnp.maximum(m, jnp.max(s_t, axis=0, keepdims=True))
        p_t = jnp.exp2(s_t - m_new).astype(BF16)
        acc = jnp.exp2(m - m_new) * acc + jnp.dot(vb_t, p_t, preferred_element_type=F32)
        return m_new, acc

    def run_items(items, state):
        pending = [scores(hh, kb()) for hh, kb, _ in items[:ATTN_AHEAD]]
        for j, (hh, _, vb) in enumerate(items):
            if j + ATTN_AHEAD < len(items):
                nh, nkb, _ = items[j + ATTN_AHEAD]
                pending.append(scores(nh, nkb()))
            state[hh] = update(pending.pop(0), vb(), *state[hh])
        return state

    def block_items(blk, off):
        out = []
        for sub in range(tk // ATTN_SUB):
            for hh in range(2):
                lo = sub * ATTN_SUB
                out.append((hh,
                            lambda hh=hh, lo=lo: k_ref[0, hh, pl.ds(off + lo, ATTN_SUB), :],
                            lambda hh=hh, lo=lo: v_ref[0, hh, blk, :, lo:lo + ATTN_SUB]))
        return out

    ctx_items = [(hh, lambda hh=hh: kc_ref[0, hh], lambda hh=hh: vc_ref[0, hh, 0])
                 for hh in range(2)]
    state = [(jnp.full((1, tq), NEG_BIG, F32), jnp.zeros((V_ROWS, tq), F32)) for _ in range(2)]
    n_iter = n_kblk // ATTN_UNROLL
    if n_iter == 1:
        items = ctx_items
        for blk in range(n_kblk):
            items = items + block_items(blk, blk * tk)
        state = run_items(items, state)
    else:
        state = run_items(ctx_items, state)

        def body(i, carry):
            items = []
            for r in range(ATTN_UNROLL):
                blk = i * ATTN_UNROLL + r
                items += block_items(blk, pl.multiple_of(blk * tk, tk))
            st = run_items(items, [(carry[0], carry[1]), (carry[2], carry[3])])
            return st[0] + st[1]

        carry = lax.fori_loop(0, n_iter, body, state[0] + state[1])
        state = [(carry[0], carry[1]), (carry[2], carry[3])]
    carry = state[0] + state[1]
    outs = [carry[2 * hh + 1][:V_HEAD] / carry[2 * hh + 1][ONES_ROW:ONES_ROW + 1] for hh in range(2)]
    o_ref[0] = jnp.concatenate(outs, axis=0).T.astype(o_ref.dtype)


def _attention(q_t, kc, vc_t, k, v_t):
    b, nh, dp, s = q_t.shape
    c = kc.shape[2]
    tq, tk = ATTN_TQ, ATTN_TK
    n_kblk = s // tk
    kern = functools.partial(_attn_kernel, n_kblk=n_kblk, tk=tk)
    return pl.pallas_call(
        kern,
        grid=(b, nh // 2, s // tq),
        in_specs=[pl.BlockSpec((1, 2, dp, tq), lambda bi, hp, qi: (bi, hp, 0, qi)),
                  pl.BlockSpec((1, 2, c, dp), lambda bi, hp, qi: (bi, hp, 0, 0)),
                  pl.BlockSpec((1, 2, 1, V_ROWS, c), lambda bi, hp, qi: (bi, hp, 0, 0, 0)),
                  pl.BlockSpec((1, 2, s, dp), lambda bi, hp, qi: (bi, hp, 0, 0)),
                  pl.BlockSpec((1, 2, n_kblk, V_ROWS, tk), lambda bi, hp, qi: (bi, hp, 0, 0, 0))],
        out_specs=pl.BlockSpec((1, tq, 2 * V_HEAD), lambda bi, hp, qi: (bi, qi, hp)),
        out_shape=jax.ShapeDtypeStruct((b, s, nh * V_HEAD), BF16),
        compiler_params=pltpu.CompilerParams(
            dimension_semantics=("arbitrary", "arbitrary", "arbitrary"),
            vmem_limit_bytes=VMEM_LIMIT),
        name="attn",
    )(q_t, kc, vc_t, k, v_t)


def _route(logits_t, bias_t):
    e, t = logits_t.shape
    scores = jax.nn.sigmoid(logits_t)
    biased = scores + bias_t
    neg_inf = F32(-jnp.inf)
    gscore = []
    for g in range(N_EXPERT_GROUPS):
        v = biased[g * GROUP_SIZE:(g + 1) * GROUP_SIZE]
        m1 = jnp.max(v, axis=0, keepdims=True)
        at_max = v == m1
        n_max = jnp.sum(at_max.astype(F32), axis=0, keepdims=True)
        m2 = jnp.max(jnp.where(at_max, neg_inf, v), axis=0, keepdims=True)
        gscore.append(m1 + jnp.where(n_max >= 2.0, m1, m2))
    masked = []
    for g in range(N_EXPERT_GROUPS):
        rank = jnp.zeros((1, t), F32)
        for o in range(N_EXPERT_GROUPS):
            if o == g:
                continue
            beats = (gscore[o] >= gscore[g]) if o < g else (gscore[o] > gscore[g])
            rank = rank + beats.astype(F32)
        keep = rank < float(TOPK_GROUPS)
        masked.append(jnp.where(keep, biased[g * GROUP_SIZE:(g + 1) * GROUP_SIZE], neg_inf))
    work = jnp.concatenate(masked, axis=0)
    rows = lax.broadcasted_iota(jnp.int32, (e, t), 0)
    sel = jnp.zeros((e, t), F32)
    for _ in range(TOP_K):
        m = jnp.max(work, axis=0, keepdims=True)
        first = jnp.min(jnp.where(work == m, rows, e), axis=0, keepdims=True)
        pick = rows == first
        sel = jnp.where(pick, 1.0, sel)
        work = jnp.where(pick, neg_inf, work)
    w = sel * scores
    gates = w / jnp.sum(w, axis=0, keepdims=True) * ROUTED_SCALE
    return sel, gates


def _mix_kernel(attn_ref, u_ref, up_ref, un_ref, x_ref, g1_ref, sc2_ref, sh2_ref, wpool_ref,
                pscale_ref, wout_ref, ln_g_ref, ln_b_ref, wr_ref, rb_ref, tri_ref, ones_ref,
                x1_ref, hx_ref, pos_ref, seg_ref, cnt_ref, uext_ref, *, seq):
    i = pl.program_id(1)

    @pl.when((pl.program_id(0) == 0) & (i == 0))
    def _():
        cnt_ref[...] = jnp.zeros_like(cnt_ref)

    tile = u_ref.shape[1]
    u = u_ref[0]
    uext_ref[0:POOL_HALO] = jnp.where(i == 0, 0.0, up_ref[0])
    uext_ref[POOL_HALO:POOL_HALO + tile] = u
    uext_ref[POOL_HALO + tile:] = jnp.where(i == pl.num_programs(1) - 1, 0.0, un_ref[0])
    t = i * tile + lax.broadcasted_iota(jnp.int32, (tile, POOL_GC), 0)
    pooled = []
    for g, w in enumerate(POOL_WINDOWS):
        lanes = slice(g * POOL_GC, (g + 1) * POOL_GC)
        tot = uext_ref[POOL_HALO - w // 2:POOL_HALO - w // 2 + tile, lanes]
        for dlt in range(1, w):
            start = POOL_HALO - w // 2 + dlt
            tot = tot + uext_ref[start:start + tile, lanes]
        cnt = (jnp.minimum(t - w // 2 + w, seq) - jnp.maximum(t - w // 2, 0)).astype(F32)
        pg = (tot / cnt - u[:, lanes]).astype(BF16)
        po = jnp.dot(pg, wpool_ref[g], preferred_element_type=F32) * pscale_ref[:, lanes]
        pooled.append(po.astype(BF16))
    mixed = jnp.concatenate([attn_ref[0]] + pooled, axis=1)
    y = jnp.dot(mixed, wout_ref[...], preferred_element_type=F32)
    x1 = _layer_norm(ALPHA * x_ref[0] + g1_ref[0] * y, ln_g_ref[...], ln_b_ref[...])
    x1_ref[0] = x1
    h2 = x1 * (1.0 + sc2_ref[0]) + sh2_ref[0]
    logits_t = lax.dot_general(wr_ref[...], h2, (((1,), (1,)), ((), ())),
                               preferred_element_type=F32, precision=lax.Precision.HIGHEST)
    sel, gates_t = _route(logits_t, rb_ref[...])
    g_hi = gates_t.astype(BF16)
    g_lo = (gates_t - g_hi.astype(F32)).astype(BF16)
    g_tok = jnp.concatenate([g_hi.astype(F32), g_lo.astype(F32)], axis=0).T
    hx_ref[0] = jnp.concatenate([h2.astype(BF16), g_tok.astype(BF16)], axis=1)
    sel_b = sel.astype(BF16)
    start = cnt_ref[...]
    pos_t = jnp.dot(sel_b, tri_ref[...], preferred_element_type=F32) + start[:, 0:1]
    pos_ref[0] = jnp.where(sel > 0.0, pos_t, -1.0).astype(jnp.int32)
    n_tok = jnp.dot(sel_b, ones_ref[...], preferred_element_type=F32)
    n_chunk = jnp.floor((n_tok + (SEG_ROWS - 1)) * (1.0 / SEG_ROWS))
    seg_ref[0, 0] = start.astype(jnp.int32)
    seg_ref[0, 1] = n_chunk.astype(jnp.int32)
    cnt_ref[...] = start + n_chunk * SEG_ROWS


def _mix(attn, u, x, g1, sc2, sh2, w_pool, pool_scale, w_out, ln_g, ln_b, w_r_t, rb_t):
    b, s, d = x.shape
    tile = MIX_TILE
    hb = tile // POOL_HALO
    row = lambda bi, i: (bi, i, 0)
    vec = lambda bi, i: (bi, 0, 0)
    c2 = lambda bi, i: (0, 0)
    lane_row = lambda bi, i: (bi, 0, i)
    tri = (lax.broadcasted_iota(jnp.int32, (tile, tile), 0)
           < lax.broadcasted_iota(jnp.int32, (tile, tile), 1)).astype(BF16)
    ones = jnp.ones((tile, LANES), BF16)
    return pl.pallas_call(
        functools.partial(_mix_kernel, seq=s),
        grid=(b, s // tile),
        in_specs=[pl.BlockSpec((1, tile, POOL_WIDTH), row),
                  pl.BlockSpec((1, tile, POOL_WIDTH), row),
                  pl.BlockSpec((1, POOL_HALO, POOL_WIDTH),
                               lambda bi, i: (bi, jnp.maximum(i * hb - 1, 0), 0)),
                  pl.BlockSpec((1, POOL_HALO, POOL_WIDTH),
                               lambda bi, i: (bi, jnp.minimum((i + 1) * hb, s // POOL_HALO - 1), 0)),
                  pl.BlockSpec((1, tile, d), row),
                  pl.BlockSpec((1, 1, d), vec), pl.BlockSpec((1, 1, d), vec),
                  pl.BlockSpec((1, 1, d), vec),
                  pl.BlockSpec(w_pool.shape, lambda bi, i: (0, 0, 0)),
                  pl.BlockSpec(pool_scale.shape, c2),
                  pl.BlockSpec(w_out.shape, c2),
                  pl.BlockSpec(ln_g.shape, c2), pl.BlockSpec(ln_b.shape, c2),
                  pl.BlockSpec(w_r_t.shape, c2), pl.BlockSpec(rb_t.shape, c2),
                  pl.BlockSpec(tri.shape, c2), pl.BlockSpec(ones.shape, c2)],
        out_specs=[pl.BlockSpec((1, tile, d), row),
                   pl.BlockSpec((1, tile, HX_WIDTH), row),
                   pl.BlockSpec((1, N_EXPERTS, tile), lane_row),
                   pl.BlockSpec((1, 2, N_EXPERTS, LANES),
                                lambda bi, i: (bi * (s // tile) + i, 0, 0, 0)),
                   pl.BlockSpec((N_EXPERTS, LANES), c2)],
        out_shape=[jax.ShapeDtypeStruct((b, s, d), F32),
                   jax.ShapeDtypeStruct((b, s, HX_WIDTH), BF16),
                   jax.ShapeDtypeStruct((b, N_EXPERTS, s), jnp.int32),
                   jax.ShapeDtypeStruct((b * (s // tile), 2, N_EXPERTS, LANES), jnp.int32),
                   jax.ShapeDtypeStruct((N_EXPERTS, LANES), F32)],
        scratch_shapes=[pltpu.VMEM((tile + 2 * POOL_HALO, POOL_WIDTH), F32)],
        compiler_params=pltpu.CompilerParams(dimension_semantics=("arbitrary", "arbitrary"),
                                             vmem_limit_bytes=VMEM_LIMIT),
        name="mix",
    )(attn, u, u, u, x, g1, sc2, sh2, w_pool, pool_scale, w_out, ln_g, ln_b, w_r_t, rb_t,
      tri, ones)


def _one_hot(pos_row, start, window, n_tok):
    rows = lax.broadcasted_iota(jnp.int32, (SEG_WIN, n_tok), 0)
    hit = rows == (pos_row - (start + window * SEG_WIN))
    return jnp.where(hit, 1.0, 0.0).astype(BF16)


def _window_rows(base_ref, start_ref, step, e, window):
    row = base_ref[e] + start_ref[step * N_EXPERTS + e] + window * SEG_WIN
    return pl.ds(pl.multiple_of(row, SEG_ROWS), SEG_WIN)


def _extra_windows(nchunk_ref, step, e):
    rows = nchunk_ref[step * N_EXPERTS + e] * SEG_ROWS
    return jnp.maximum((rows + SEG_WIN - 1) // SEG_WIN, 1)


def _dispatch_kernel(base_ref, start_ref, nchunk_ref, over_ref, total_ref, pcnt_ref, nu_ref,
                     hx_ref, pos_ref, xs_hbm, stage, extra, zero_ref, sem):
    i = pl.program_id(0)
    n_tiles = xs_hbm.shape[0] // MOE_TM
    n_tok = hx_ref.shape[0]

    def zero_copy(row, n_rows):
        return pltpu.make_async_copy(zero_ref.at[pl.ds(0, n_rows)],
                                     xs_hbm.at[pl.ds(pl.multiple_of(row, SEG_ROWS), n_rows)],
                                     sem.at[1])

    @pl.when(i == 0)
    def _():
        zero_ref[...] = jnp.zeros_like(zero_ref)
        for wait in (False, True):
            def per_tail(j, c):
                cp = zero_copy(j * MOE_TM, MOE_TM)
                cp.wait() if wait else cp.start()
                return c
            lax.fori_loop(nu_ref[0], n_tiles, per_tail, 0)

            def per_expert(e, c):
                row0 = base_ref[e] + total_ref[e]

                def per_block(j, c2):
                    cp = zero_copy(row0 + j * SEG_ROWS, SEG_ROWS)
                    cp.wait() if wait else cp.start()
                    return c2
                return lax.fori_loop(0, (pcnt_ref[e] - total_ref[e]) // SEG_ROWS, per_block, c)
            lax.fori_loop(0, N_EXPERTS, per_expert, 0)

    slot = i % 2
    hx = hx_ref[...]
    for g in range(N_EXPERTS // WIN_GROUP):
        oh = jnp.concatenate(
            [_one_hot(pos_ref[0, e:e + 1, :], start_ref[i * N_EXPERTS + e], 0, n_tok)
             for e in range(g * WIN_GROUP, (g + 1) * WIN_GROUP)], axis=0)
        rows = pl.ds(g * WIN_GROUP * SEG_WIN, WIN_GROUP * SEG_WIN)
        stage[slot, rows, :] = jnp.dot(oh, hx, preferred_element_type=F32).astype(BF16)

    def wait_windows(which):
        pltpu.make_async_copy(stage.at[which], xs_hbm.at[pl.ds(0, N_EXPERTS * SEG_WIN)],
                              sem.at[0]).wait()

    @pl.when(i > 0)
    def _():
        wait_windows(1 - slot)

    for e in range(N_EXPERTS):
        pltpu.make_async_copy(stage.at[slot, pl.ds(e * SEG_WIN, SEG_WIN)],
                              xs_hbm.at[_window_rows(base_ref, start_ref, i, e, 0)],
                              sem.at[0]).start()

    @pl.when(i == pl.num_programs(0) - 1)
    def _():
        wait_windows(slot)

    @pl.when(over_ref[i] > 0)
    def _():
        def more_windows(e, c):
            def one(window, c2):
                oh = _one_hot(pos_ref[0, pl.ds(e, 1), :], start_ref[i * N_EXPERTS + e], window,
                              n_tok)
                extra[...] = jnp.dot(oh, hx, preferred_element_type=F32).astype(BF16)
                cp = pltpu.make_async_copy(
                    extra, xs_hbm.at[_window_rows(base_ref, start_ref, i, e, window)], sem.at[2])
                cp.start()
                cp.wait()
                return c2
            return lax.fori_loop(1, _extra_windows(nchunk_ref, i, e), one, c)
        lax.fori_loop(0, N_EXPERTS, more_windows, 0)


def _dispatch(base, seg_start, seg_chunks, seg_over, total, pcnt, n_used, hx, pos, n_slots):
    t, width = hx.shape
    tpb = pos.shape[2] // TOK_TILE
    return pl.pallas_call(
        _dispatch_kernel,
        grid_spec=pltpu.PrefetchScalarGridSpec(
            num_scalar_prefetch=7, grid=(t // TOK_TILE,),
            in_specs=[pl.BlockSpec((TOK_TILE, width), lambda i, *_: (i, 0)),
                      pl.BlockSpec((1, N_EXPERTS, TOK_TILE), lambda i, *_: (i // tpb, 0, i % tpb))],
            out_specs=pl.BlockSpec(memory_space=pl.ANY),
            scratch_shapes=[pltpu.VMEM((2, N_EXPERTS * SEG_WIN, width), BF16),
                            pltpu.VMEM((SEG_WIN, width), BF16),
                            pltpu.VMEM((MOE_TM, width), BF16),
                            pltpu.SemaphoreType.DMA((3,))]),
        out_shape=jax.ShapeDtypeStruct((n_slots, width), BF16),
        compiler_params=pltpu.CompilerParams(dimension_semantics=("arbitrary",),
                                             vmem_limit_bytes=VMEM_LIMIT),
        name="dispatch",
    )(base, seg_start, seg_chunks, seg_over, total, pcnt, n_used, hx, pos)


def _experts_kernel(te_ref, nu_ref, xs_ref, wg_ref, wu_ref, wd_ref, ys_ref, wg_b, wu_b, wd_b):
    i = pl.program_id(0)

    @pl.when(i < nu_ref[0])
    def _():
        @pl.when((i == 0) | (te_ref[i] != te_ref[jnp.maximum(i - 1, 0)]))
        def _():
            wg_b[...] = wg_ref[0].astype(BF16)
            wu_b[...] = wu_ref[0].astype(BF16)
            wd_b[...] = wd_ref[0].astype(BF16)

        half = MOE_TM // 2
        halves = [pl.ds(k * half, half) for k in range(2)]
        hidden = []
        for rows in halves:
            x = xs_ref[rows, :D_MODEL]
            hidden.append((jnp.dot(x, wg_b[...], preferred_element_type=F32),
                           jnp.dot(x, wu_b[...], preferred_element_type=F32)))
        for rows, (hg, hu) in zip(halves, hidden):
            g = xs_ref[rows, D_MODEL:].astype(F32)
            lane = lax.broadcasted_iota(jnp.int32, g.shape, 1)
            mine = (lane == te_ref[i]) | (lane == te_ref[i] + N_EXPERTS)
            gate = jnp.sum(jnp.where(mine, g, 0.0), axis=1, keepdims=True)
            a = (_silu(hg) * hu * gate).astype(BF16)
            ys_ref[rows, :] = jnp.dot(a, wd_b[...], preferred_element_type=F32).astype(ys_ref.dtype)

    @pl.when(i >= nu_ref[0])
    def _():
        ys_ref[...] = jnp.zeros_like(ys_ref)


def _experts(tile_expert, n_used, xs, w_e_gate, w_e_up, w_e_down):
    n_slots, width = xs.shape
    _, d, f = w_e_gate.shape
    slot_tile = lambda i, te, nu: (jnp.minimum(i, nu[0] - 1), 0)
    expert = lambda i, te, nu: (te[i], 0, 0)
    return pl.pallas_call(
        _experts_kernel,
        grid_spec=pltpu.PrefetchScalarGridSpec(
            num_scalar_prefetch=2, grid=(n_slots // MOE_TM,),
            in_specs=[pl.BlockSpec((MOE_TM, width), slot_tile),
                      pl.BlockSpec((1, d, f), expert), pl.BlockSpec((1, d, f), expert),
                      pl.BlockSpec((1, f, d), expert)],
            out_specs=pl.BlockSpec((MOE_TM, d), lambda i, te, nu: (i, 0)),
            scratch_shapes=[pltpu.VMEM((d, f), BF16), pltpu.VMEM((d, f), BF16),
                            pltpu.VMEM((f, d), BF16)]),
        out_shape=jax.ShapeDtypeStruct((n_slots, d), BF16),
        compiler_params=pltpu.CompilerParams(dimension_semantics=("arbitrary",),
                                             vmem_limit_bytes=VMEM_LIMIT),
        name="experts",
    )(tile_expert, n_used, xs, w_e_gate, w_e_up, w_e_down)


def _dot_tn(a, b):
    return lax.dot_general(a, b, (((0,), (0,)), ((), ())), preferred_element_type=F32)


def _combine_kernel(base_ref, start_ref, nchunk_ref, over_ref, hx_ref, pos_ref, x1_ref, g2_ref, wsg_ref,
                    wsu_ref, wsd_ref, ln_g_ref, ln_b_ref, ys_hbm, o_ref, win, extra, acc_ref, sem):
    i = pl.program_id(0)
    n = pl.num_programs(0)
    n_tok = hx_ref.shape[0]

    def fetch(step, slot):
        for e in range(N_EXPERTS):
            pltpu.make_async_copy(ys_hbm.at[_window_rows(base_ref, start_ref, step, e, 0)],
                                  win.at[slot, pl.ds(e * SEG_WIN, SEG_WIN)],
                                  sem.at[slot]).start()

    @pl.when(i == 0)
    def _():
        fetch(0, 0)

    @pl.when(i + 1 < n)
    def _():
        fetch(i + 1, (i + 1) % 2)

    slot = i % 2
    h = hx_ref[:, :D_MODEL]
    a = _silu(jnp.dot(h, wsg_ref[...], preferred_element_type=F32)) * jnp.dot(
        h, wsu_ref[...], preferred_element_type=F32)
    moe = jnp.dot(a.astype(BF16), wsd_ref[...], preferred_element_type=F32)
    pltpu.make_async_copy(ys_hbm.at[pl.ds(0, N_EXPERTS * SEG_WIN)], win.at[slot], sem.at[slot]).wait()
    for g in range(N_EXPERTS // WIN_GROUP):
        oh = jnp.concatenate(
            [_one_hot(pos_ref[0, e:e + 1, :], start_ref[i * N_EXPERTS + e], 0, n_tok)
             for e in range(g * WIN_GROUP, (g + 1) * WIN_GROUP)], axis=0)
        rows = pl.ds(g * WIN_GROUP * SEG_WIN, WIN_GROUP * SEG_WIN)
        moe = moe + _dot_tn(oh, win[slot, rows, :])
    acc_ref[...] = moe

    @pl.when(over_ref[i] > 0)
    def _():
        def more_windows(e, c):
            def one(window, c2):
                cp = pltpu.make_async_copy(
                    ys_hbm.at[_window_rows(base_ref, start_ref, i, e, window)], extra, sem.at[2])
                cp.start()
                cp.wait()
                oh = _one_hot(pos_ref[0, pl.ds(e, 1), :], start_ref[i * N_EXPERTS + e], window,
                              n_tok)
                acc_ref[...] += _dot_tn(oh, extra[...])
                return c2
            return lax.fori_loop(1, _extra_windows(nchunk_ref, i, e), one, c)
        lax.fori_loop(0, N_EXPERTS, more_windows, 0)

    z = ALPHA * x1_ref[...] + g2_ref[0] * acc_ref[...]
    o_ref[...] = _layer_norm(z, ln_g_ref[...], ln_b_ref[...])


def _combine(base, seg_start, seg_chunks, seg_over, hx, pos, x1, g2, w_s_gate, w_s_up, w_s_down,
             ln_g, ln_b, ys):
    t, d = x1.shape
    tpb = pos.shape[2] // TOK_TILE
    row = lambda i, *_: (i, 0)
    c2 = lambda i, *_: (0, 0)
    return pl.pallas_call(
        _combine_kernel,
        grid_spec=pltpu.PrefetchScalarGridSpec(
            num_scalar_prefetch=4, grid=(t // TOK_TILE,),
            in_specs=[pl.BlockSpec((TOK_TILE, hx.shape[1]), row),
                      pl.BlockSpec((1, N_EXPERTS, TOK_TILE), lambda i, *_: (i // tpb, 0, i % tpb)),
                      pl.BlockSpec((TOK_TILE, d), row),
                      pl.BlockSpec((1, 1, d), lambda i, *_: (i // tpb, 0, 0)),
                      pl.BlockSpec(w_s_gate.shape, c2), pl.BlockSpec(w_s_up.shape, c2),
                      pl.BlockSpec(w_s_down.shape, c2),
                      pl.BlockSpec(ln_g.shape, c2), pl.BlockSpec(ln_b.shape, c2),
                      pl.BlockSpec(memory_space=pl.ANY)],
            out_specs=pl.BlockSpec((TOK_TILE, d), row),
            scratch_shapes=[pltpu.VMEM((2, N_EXPERTS * SEG_WIN, d), BF16),
                            pltpu.VMEM((SEG_WIN, d), BF16),
                            pltpu.VMEM((TOK_TILE, d), F32),
                            pltpu.SemaphoreType.DMA((3,))]),
        out_shape=jax.ShapeDtypeStruct((t, d), F32),
        compiler_params=pltpu.CompilerParams(dimension_semantics=("arbitrary",),
                                             vmem_limit_bytes=VMEM_LIMIT),
        name="combine",
    )(base, seg_start, seg_chunks, seg_over, hx, pos, x1, g2, w_s_gate, w_s_up, w_s_down, ln_g,
      ln_b, ys)


def _moe(hx, pos, seg, counts, x1, g2, w_e_gate, w_e_up, w_e_down, w_s_gate, w_s_up, w_s_down,
         ln_g, ln_b):
    b, s, d = x1.shape
    t = b * s
    n_seg = (t // TOK_TILE) * N_EXPERTS
    max_rows = t * TOP_K + n_seg * (SEG_ROWS - 1) + N_EXPERTS * (SEG_WIN + MOE_TM)
    n_tiles = -(-max_rows // MOE_TM)
    total = jnp.round(counts[:, 0]).astype(jnp.int32)
    pcnt = (total + SEG_WIN + MOE_TM - 1) // MOE_TM * MOE_TM
    ends = jnp.cumsum(pcnt)
    base = ends - pcnt
    n_used = (ends[-1] // MOE_TM).reshape(1)
    tile_ids = jnp.arange(n_tiles, dtype=jnp.int32)
    tile_expert = jnp.sum((ends[None, :] <= tile_ids[:, None] * MOE_TM).astype(jnp.int32), axis=1)
    tile_expert = jnp.minimum(tile_expert, N_EXPERTS - 1)
    tile_expert = jnp.where(tile_ids < n_used, tile_expert, tile_expert[n_used[0] - 1])
    seg_start = seg[:, 0, :, 0].reshape(n_seg)
    seg_chunks = seg[:, 1, :, 0]
    seg_over = (jnp.max(seg_chunks, axis=1) * SEG_ROWS > SEG_WIN).astype(jnp.int32)
    seg_chunks = seg_chunks.reshape(n_seg)

    hxf = hx.reshape(t, hx.shape[2])
    xs = _dispatch(base, seg_start, seg_chunks, seg_over, total, pcnt, n_used, hxf, pos,
                   n_tiles * MOE_TM)
    ys = _experts(tile_expert, n_used, xs, w_e_gate, w_e_up, w_e_down)
    out = _combine(base, seg_start, seg_chunks, seg_over, hxf, pos, x1.reshape(t, d), g2, w_s_gate,
                   w_s_up, w_s_down, ln_g, ln_b, ys)
    return out.reshape(b, s, d)


def _rope_tables(seq):
    t = jnp.arange(seq)
    pos = jnp.stack([t // GRID_W, t % GRID_W], axis=-1).astype(F32)
    inv_freq = ROPE_THETA ** (-jnp.arange(ROPE_FREQS, dtype=F32) / ROPE_FREQS)
    ang = pos[:, :, None] * inv_freq
    cos, sin = jnp.cos(ang), jnp.sin(ang)
    zero = jnp.zeros_like(sin)
    cos_r = jnp.stack([cos, cos], axis=2).reshape(seq, QK_ROPE)
    sin_lo = jnp.stack([-sin, zero], axis=2).reshape(seq, QK_ROPE)
    sin_hi = jnp.stack([zero, sin], axis=2).reshape(seq, QK_ROPE)
    pads = ((0, 0), (QK_NOPE, HEAD_PAD - QK_NOPE - QK_ROPE))
    lane_tabs = (jnp.pad(cos_r, pads, constant_values=1.0), jnp.pad(sin_lo, pads),
                 jnp.pad(sin_hi, pads))
    row_tabs = (cos.reshape(seq, 2 * ROPE_FREQS).T, sin.reshape(seq, 2 * ROPE_FREQS).T)
    return lane_tabs, row_tabs


def _pad_heads(w, width, padded):
    k = w.shape[0]
    w = jnp.pad(w.reshape(k, N_HEADS, width), ((0, 0), (0, 0), (0, padded - width)))
    return w.reshape(k, N_HEADS * padded)


def kernel(x, c, ctx, c_ctx, w_ada, b_ada, w_in, q_norm_g, w_uq, kv_norm_g, w_ukv, w_pool, pool_scale, w_out, ln1_g, ln1_b, w_router, router_bias, w_e_gate, w_e_up, w_e_down, w_s_gate, w_s_up, w_s_down, ln2_g, ln2_b):
    assert w_ada.shape[0] == 1, "single-layer block"
    b, s, d = x.shape

    cvec = jnp.concatenate([c, c_ctx[None], jnp.zeros((SUBLANES - b - 1, d), F32)], axis=0)
    mod = _ada(cvec, w_ada[0], b_ada)
    sh1, sc1, g1, sh2, sc2, g2 = [mod[:b, k * d:(k + 1) * d][:, None, :] for k in range(6)]
    sh1c, sc1c = [jnp.broadcast_to(mod[b, k * d:(k + 1) * d], (b, 1, d)) for k in range(2)]

    wi = w_in[0]
    kr_cols = jnp.pad(wi[:, Q_LORA + KV_LORA:Q_LORA + KV_LORA + QK_ROPE],
                      ((0, 0), (QK_NOPE, HEAD_PAD - QK_NOPE - QK_ROPE)))
    w_in_r = jnp.concatenate([wi[:, :Q_LORA + KV_LORA], wi[:, Q_LORA + KV_LORA + QK_ROPE:], kr_cols],
                             axis=1).astype(BF16)
    w_uq_t = _pad_heads(w_uq[0], QK_NOPE + QK_ROPE, HEAD_PAD).T.astype(BF16)
    wkv = w_ukv[0].reshape(KV_LORA, N_HEADS, QK_NOPE + V_HEAD)
    w_uk_p = _pad_heads(wkv[:, :, :QK_NOPE].reshape(KV_LORA, -1), QK_NOPE, HEAD_PAD).astype(BF16)
    w_uv_t = _pad_heads(wkv[:, :, QK_NOPE:].reshape(KV_LORA, -1), V_HEAD, V_ROWS).T.astype(BF16)
    tables = _rope_tables(s)

    q_t, k, v_t, u = _proj(x, sc1, sh1, tables, w_in_r, q_norm_g, w_uq_t, kv_norm_g, w_uk_p,
                           w_uv_t, PROJ_TILE)
    kc, vc_t = _proj(ctx, sc1c, sh1c, None, w_in_r, None, None, kv_norm_g, w_uk_p, w_uv_t,
                     ctx.shape[1])
    attn = _attention(q_t, kc, vc_t, k, v_t)

    x1, hx, pos, seg, counts = _mix(attn, u, x, g1, sc2, sh2, w_pool[0].astype(BF16),
                                    pool_scale, w_out[0].astype(BF16), ln1_g, ln1_b,
                                    w_router[0].T, router_bias[0][:, None])
    return _moe(hx, pos, seg, counts, x1, g2, w_e_gate[0], w_e_up[0], w_e_down[0],
                w_s_gate[0].astype(BF16), w_s_up[0].astype(BF16), w_s_down[0].astype(BF16),
                ln2_g, ln2_b)
```

```python
import functools
import math

import jax
import jax.numpy as jnp
from jax import lax
from jax.experimental import pallas as pl
from jax.experimental.pallas import tpu as pltpu

F32 = jnp.float32
BF16 = jnp.bfloat16

D_MODEL = 1024
GRID_W = 64
N_HEADS = 8
Q_LORA = 512
KV_LORA = 256
QK_NOPE = 64
QK_ROPE = 32
V_HEAD = 64
ROPE_FREQS = QK_ROPE // 4
ROPE_THETA = 10000.0
ATTN_SCALE = 1.0 / math.sqrt(QK_NOPE + QK_ROPE)
LOG2_E = math.log2(math.e)
POOL_GROUPS = 4
POOL_WINDOWS = (2, 4, 8, 16)
POOL_WIDTH = 512
POOL_GC = POOL_WIDTH // POOL_GROUPS
POOL_HALO = 8
N_EXPERTS = 64
N_EXPERT_GROUPS = 8
GROUP_SIZE = N_EXPERTS // N_EXPERT_GROUPS
TOPK_GROUPS = 4
TOP_K = 8
D_EXPERT = 256
ROUTED_SCALE = 2.5
LN_EPS = 1e-5
RMS_EPS = 1e-6
ALPHA = 2.0 ** 0.25

LANES = 128
SUBLANES = 8
HEAD_PAD = LANES
V_ROWS = 80
ONES_ROW = V_HEAD
IN_PAD = Q_LORA + KV_LORA + POOL_WIDTH + LANES

PROJ_TILE = 512
ATTN_TQ = 512
ATTN_TK = PROJ_TILE
ATTN_SUB = 256
ATTN_AHEAD = 2
ATTN_UNROLL = 16
TOK_TILE = 256
MIX_TILE = TOK_TILE
MOE_TM = 1024
SEG_ROWS = 16
SEG_WIN = 64
WIN_CHUNKS = SEG_WIN // SEG_ROWS
WIN_GROUP = 16
HX_WIDTH = D_MODEL + 2 * N_EXPERTS
VMEM_LIMIT = 48 * 1024 * 1024
NEG_BIG = -1e30


def _silu(v):
    return v * jax.nn.sigmoid(v)


def _layer_norm(z, g, b):
    mu = jnp.mean(z, axis=-1, keepdims=True)
    zc = z - mu
    var = jnp.mean(zc * zc, axis=-1, keepdims=True)
    return zc * lax.rsqrt(var + LN_EPS) * g + b


def _rms_norm(v, g):
    return v * lax.rsqrt(jnp.mean(v * v, axis=-1, keepdims=True) + RMS_EPS) * g


def _dot_nt(a, b):
    return lax.dot_general(a, b, (((1,), (1,)), ((), ())), preferred_element_type=F32)


def _ada_kernel(c_ref, w_ref, b_ref, o_ref):
    cv = _silu(c_ref[...])
    o_ref[...] = jnp.dot(cv, w_ref[...], preferred_element_type=F32,
                         precision=lax.Precision.HIGHEST) + b_ref[...]


def _ada(cvec, w_ada, b_ada):
    rows, d = cvec.shape
    n = w_ada.shape[1]
    tn = 1024
    return pl.pallas_call(
        _ada_kernel,
        grid=(n // tn,),
        in_specs=[pl.BlockSpec((rows, d), lambda j: (0, 0)),
                  pl.BlockSpec((d, tn), lambda j: (0, j)),
                  pl.BlockSpec((1, tn), lambda j: (0, j))],
        out_specs=pl.BlockSpec((rows, tn), lambda j: (0, j)),
        out_shape=jax.ShapeDtypeStruct((rows, n), F32),
        compiler_params=pltpu.CompilerParams(dimension_semantics=("arbitrary",),
                                             vmem_limit_bytes=VMEM_LIMIT),
        name="ada",
    )(cvec, w_ada, b_ada)


def _rope_lanes(v, cos, sin_lo, sin_hi):
    return v * cos + pltpu.roll(v, LANES - 8, axis=1) * sin_lo + pltpu.roll(v, 8, axis=1) * sin_hi


def _proj_kernel(*refs, with_q):
    if with_q:
        (x_ref, sc_ref, sh_ref, cos_ref, slo_ref, shi_ref, cos_t_ref, sin_t_ref, win_ref, qg_ref,
         wuq_ref, kvg_ref, wuk_ref, wuv_ref, q_ref, k_ref, v_ref, u_ref) = refs
    else:
        (x_ref, sc_ref, sh_ref, win_ref, kvg_ref, wuk_ref, wuv_ref, k_ref, v_ref) = refs
    h = (x_ref[0] * (1.0 + sc_ref[0]) + sh_ref[0]).astype(BF16)
    p = jnp.dot(h, win_ref[...], preferred_element_type=F32)
    tile = p.shape[0]
    kr = p[:, IN_PAD - LANES:]
    kvn = _rms_norm(p[:, Q_LORA:Q_LORA + KV_LORA], kvg_ref[...]).astype(BF16)
    kfull = jnp.dot(kvn, wuk_ref[...], preferred_element_type=F32)
    v_t = _dot_nt(wuv_ref[...], kvn)
    row = lax.broadcasted_iota(jnp.int32, (N_HEADS * V_ROWS, 1), 0)
    v_t = v_t + (row % V_ROWS == ONES_ROW).astype(F32)
    if with_q:
        kr = _rope_lanes(kr, cos_ref[...], slo_ref[...], shi_ref[...])
        u_ref[0] = p[:, Q_LORA + KV_LORA:Q_LORA + KV_LORA + POOL_WIDTH]
        qn = _rms_norm(p[:, :Q_LORA], qg_ref[...]).astype(BF16)
        q_t = _dot_nt(wuq_ref[...], qn) * (ATTN_SCALE * LOG2_E)
        cos_t, sin_t = cos_t_ref[...], sin_t_ref[...]
    for hd in range(N_HEADS):
        k_ref[0, hd] = (kfull[:, hd * HEAD_PAD:(hd + 1) * HEAD_PAD] + kr).astype(BF16)
        v_ref[0, hd, 0] = v_t[hd * V_ROWS:(hd + 1) * V_ROWS].astype(BF16)
        if with_q:
            base = hd * HEAD_PAD
            q_ref[0, hd, 0:QK_NOPE, :] = q_t[base:base + QK_NOPE].astype(BF16)
            rope = []
            for ax in range(2):
                lo = q_t[base + QK_NOPE + 16 * ax:base + QK_NOPE + 16 * ax + 8]
                hi = q_t[base + QK_NOPE + 16 * ax + 8:base + QK_NOPE + 16 * ax + 16]
                cs, sn = cos_t[8 * ax:8 * ax + 8], sin_t[8 * ax:8 * ax + 8]
                rope += [lo * cs - hi * sn, hi * cs + lo * sn]
            rope.append(jnp.zeros((HEAD_PAD - QK_NOPE - QK_ROPE, tile), F32))
            q_ref[0, hd, QK_NOPE:, :] = jnp.concatenate(rope, axis=0).astype(BF16)


def _proj(x, sc, sh, tables, w_in_r, q_g, w_uq_t, kv_g, w_uk_p, w_uv_t, tile):
    b, s, d = x.shape
    with_q = tables is not None
    grid = (b, s // tile)
    row = lambda bi, i: (bi, i, 0)
    vec = lambda bi, i: (bi, 0, 0)
    const2 = lambda bi, i: (0, 0)
    k_out = pl.BlockSpec((1, N_HEADS, tile, HEAD_PAD), lambda bi, i: (bi, 0, i, 0))
    k_shape = jax.ShapeDtypeStruct((b, N_HEADS, s, HEAD_PAD), BF16)
    v_out = pl.BlockSpec((1, N_HEADS, 1, V_ROWS, tile), lambda bi, i: (bi, 0, i, 0, 0))
    v_shape = jax.ShapeDtypeStruct((b, N_HEADS, s // tile, V_ROWS, tile), BF16)
    in_specs = [pl.BlockSpec((1, tile, d), row),
                pl.BlockSpec((1, 1, d), vec), pl.BlockSpec((1, 1, d), vec)]
    args = [x, sc, sh]
    if with_q:
        lane_tabs, row_tabs = tables
        in_specs += [pl.BlockSpec((tile, LANES), lambda bi, i: (i, 0))] * 3
        in_specs += [pl.BlockSpec((2 * ROPE_FREQS, tile), lambda bi, i: (0, i))] * 2
        args += list(lane_tabs) + list(row_tabs)
    in_specs.append(pl.BlockSpec(w_in_r.shape, const2)); args.append(w_in_r)
    if with_q:
        in_specs += [pl.BlockSpec(q_g.shape, const2), pl.BlockSpec(w_uq_t.shape, const2)]
        args += [q_g, w_uq_t]
    in_specs += [pl.BlockSpec(kv_g.shape, const2), pl.BlockSpec(w_uk_p.shape, const2),
                 pl.BlockSpec(w_uv_t.shape, const2)]
    args += [kv_g, w_uk_p, w_uv_t]
    if with_q:
        q_out = pl.BlockSpec((1, N_HEADS, HEAD_PAD, tile), lambda bi, i: (bi, 0, 0, i))
        q_shape = jax.ShapeDtypeStruct((b, N_HEADS, HEAD_PAD, s), BF16)
        out_specs = [q_out, k_out, v_out, pl.BlockSpec((1, tile, POOL_WIDTH), row)]
        out_shape = [q_shape, k_shape, v_shape, jax.ShapeDtypeStruct((b, s, POOL_WIDTH), F32)]
    else:
        out_specs = [k_out, v_out]
        out_shape = [k_shape, v_shape]
    return pl.pallas_call(
        functools.partial(_proj_kernel, with_q=with_q),
        grid=grid, in_specs=in_specs, out_specs=out_specs, out_shape=out_shape,
        compiler_params=pltpu.CompilerParams(dimension_semantics=("arbitrary", "arbitrary"),
                                             vmem_limit_bytes=VMEM_LIMIT),
        name="proj" if with_q else "proj_ctx",
    )(*args)


def _attn_kernel(q_ref, kc_ref, vc_ref, k_ref, v_ref, o_ref, *, n_kblk, tk):
    tq = q_ref.shape[3]
    qs = [q_ref[0, hh] for hh in range(2)]

    def scores(hh, kb):
        return jnp.dot(kb, qs[hh], preferred_element_type=F32)

    def update(s_t, vb_t, m, acc):
        m_new = jnp.maximum(m, jnp.max(s_t, axis=0, keepdims=True))
        p_t = jnp.exp2(s_t - m_new).astype(BF16)
        acc = jnp.exp2(m - m_new) * acc + jnp.dot(vb_t, p_t, preferred_element_type=F32)
        return m_new, acc

    def run_items(items, state):
        pending = [scores(hh, kb()) for hh, kb, _ in items[:ATTN_AHEAD]]
        for j, (hh, _, vb) in enumerate(items):
            if j + ATTN_AHEAD < len(items):
                nh, nkb, _ = items[j + ATTN_AHEAD]
                pending.append(scores(nh, nkb()))
            state[hh] = update(pending.pop(0), vb(), *state[hh])
        return state

    def block_items(blk, off):
        out = []
        for sub in range(tk // ATTN_SUB):
            for hh in range(2):
                lo = sub * ATTN_SUB
                out.append((hh,
                            lambda hh=hh, lo=lo: k_ref[0, hh, pl.ds(off + lo, ATTN_SUB), :],
                            lambda hh=hh, lo=lo: v_ref[0, hh, blk, :, lo:lo + ATTN_SUB]))
        return out

    ctx_items = [(hh, lambda hh=hh: kc_ref[0, hh], lambda hh=hh: vc_ref[0, hh, 0])
                 for hh in range(2)]
    state = [(jnp.full((1, tq), NEG_BIG, F32), jnp.zeros((V_ROWS, tq), F32)) for _ in range(2)]
    n_iter = n_kblk // ATTN_UNROLL
    if n_iter == 1:
        items = ctx_items
        for blk in range(n_kblk):
            items = items + block_items(blk, blk * tk)
        state = run_items(items, state)
    else:
        state = run_items(ctx_items, state)

        def body(i, carry):
            items = []
            for r in range(ATTN_UNROLL):
                blk = i * ATTN_UNROLL + r
                items += block_items(blk, pl.multiple_of(blk * tk, tk))
            st = run_items(items, [(carry[0], carry[1]), (carry[2], carry[3])])
            return st[0] + st[1]

        carry = lax.fori_loop(0, n_iter, body, state[0] + state[1])
        state = [(carry[0], carry[1]), (carry[2], carry[3])]
    carry = state[0] + state[1]
    outs = [carry[2 * hh + 1][:V_HEAD] / carry[2 * hh + 1][ONES_ROW:ONES_ROW + 1] for hh in range(2)]
    o_ref[0] = jnp.concatenate(outs, axis=0).T.astype(o_ref.dtype)


def _attention(q_t, kc, vc_t, k, v_t):
    b, nh, dp, s = q_t.shape
    c = kc.shape[2]
    tq, tk = ATTN_TQ, ATTN_TK
    n_kblk = s // tk
    kern = functools.partial(_attn_kernel, n_kblk=n_kblk, tk=tk)
    return pl.pallas_call(
        kern,
        grid=(b, nh // 2, s // tq),
        in_specs=[pl.BlockSpec((1, 2, dp, tq), lambda bi, hp, qi: (bi, hp, 0, qi)),
                  pl.BlockSpec((1, 2, c, dp), lambda bi, hp, qi: (bi, hp, 0, 0)),
                  pl.BlockSpec((1, 2, 1, V_ROWS, c), lambda bi, hp, qi: (bi, hp, 0, 0, 0)),
                  pl.BlockSpec((1, 2, s, dp), lambda bi, hp, qi: (bi, hp, 0, 0)),
                  pl.BlockSpec((1, 2, n_kblk, V_ROWS, tk), lambda bi, hp, qi: (bi, hp, 0, 0, 0))],
        out_specs=pl.BlockSpec((1, tq, 2 * V_HEAD), lambda bi, hp, qi: (bi, qi, hp)),
        out_shape=jax.ShapeDtypeStruct((b, s, nh * V_HEAD), BF16),
        compiler_params=pltpu.CompilerParams(
            dimension_semantics=("arbitrary", "arbitrary", "arbitrary"),
            vmem_limit_bytes=VMEM_LIMIT),
        name="attn",
    )(q_t, kc, vc_t, k, v_t)


def _route(logits_t, bias_t):
    e, t = logits_t.shape
    scores = jax.nn.sigmoid(logits_t)
    biased = scores + bias_t
    neg_inf = F32(-jnp.inf)
    gscore = []
    for g in range(N_EXPERT_GROUPS):
        v = biased[g * GROUP_SIZE:(g + 1) * GROUP_SIZE]
        m1 = jnp.max(v, axis=0, keepdims=True)
        at_max = v == m1
        n_max = jnp.sum(at_max.astype(F32), axis=0, keepdims=True)
        m2 = jnp.max(jnp.where(at_max, neg_inf, v), axis=0, keepdims=True)
        gscore.append(m1 + jnp.where(n_max >= 2.0, m1, m2))
    masked = []
    for g in range(N_EXPERT_GROUPS):
        rank = jnp.zeros((1, t), F32)
        for o in range(N_EXPERT_GROUPS):
            if o == g:
                continue
            beats = (gscore[o] >= gscore[g]) if o < g else (gscore[o] > gscore[g])
            rank = rank + beats.astype(F32)
        keep = rank < float(TOPK_GROUPS)
        masked.append(jnp.where(keep, biased[g * GROUP_SIZE:(g + 1) * GROUP_SIZE], neg_inf))
    work = jnp.concatenate(masked, axis=0)
    rows = lax.broadcasted_iota(jnp.int32, (e, t), 0)
    sel = jnp.zeros((e, t), F32)
    for _ in range(TOP_K):
        m = jnp.max(work, axis=0, keepdims=True)
        first = jnp.min(jnp.where(work == m, rows, e), axis=0, keepdims=True)
        pick = rows == first
        sel = jnp.where(pick, 1.0, sel)
        work = jnp.where(pick, neg_inf, work)
    w = sel * scores
    gates = w / jnp.sum(w, axis=0, keepdims=True) * ROUTED_SCALE
    return sel, gates


def _mix_kernel(attn_ref, u_ref, up_ref, un_ref, x_ref, g1_ref, sc2_ref, sh2_ref, wpool_ref,
                pscale_ref, wout_ref, ln_g_ref, ln_b_ref, wr_ref, rb_ref, tri_ref, ones_ref,
                x1_ref, hx_ref, pos_ref, seg_ref, cnt_ref, uext_ref, *, seq):
    i = pl.program_id(1)

    @pl.when((pl.program_id(0) == 0) & (i == 0))
    def _():
        cnt_ref[...] = jnp.zeros_like(cnt_ref)

    tile = u_ref.shape[1]
    u = u_ref[0]
    uext_ref[0:POOL_HALO] = jnp.where(i == 0, 0.0, up_ref[0])
    uext_ref[POOL_HALO:POOL_HALO + tile] = u
    uext_ref[POOL_HALO + tile:] = jnp.where(i == pl.num_programs(1) - 1, 0.0, un_ref[0])
    t = i * tile + lax.broadcasted_iota(jnp.int32, (tile, POOL_GC), 0)
    pooled = []
    for g, w in enumerate(POOL_WINDOWS):
        lanes = slice(g * POOL_GC, (g + 1) * POOL_GC)
        tot = uext_ref[POOL_HALO - w // 2:POOL_HALO - w // 2 + tile, lanes]
        for dlt in range(1, w):
            start = POOL_HALO - w // 2 + dlt
            tot = tot + uext_ref[start:start + tile, lanes]
        cnt = (jnp.minimum(t - w // 2 + w, seq) - jnp.maximum(t - w // 2, 0)).astype(F32)
        pg = (tot / cnt - u[:, lanes]).astype(BF16)
        po = jnp.dot(pg, wpool_ref[g], preferred_element_type=F32) * pscale_ref[:, lanes]
        pooled.append(po.astype(BF16))
    mixed = jnp.concatenate([attn_ref[0]] + pooled, axis=1)
    y = jnp.dot(mixed, wout_ref[...], preferred_element_type=F32)
    x1 = _layer_norm(ALPHA * x_ref[0] + g1_ref[0] * y, ln_g_ref[...], ln_b_ref[...])
    x1_ref[0] = x1
    h2 = x1 * (1.0 + sc2_ref[0]) + sh2_ref[0]
    logits_t = lax.dot_general(wr_ref[...], h2, (((1,), (1,)), ((), ())),
                               preferred_element_type=F32, precision=lax.Precision.HIGHEST)
    sel, gates_t = _route(logits_t, rb_ref[...])
    g_hi = gates_t.astype(BF16)
    g_lo = (gates_t - g_hi.astype(F32)).astype(BF16)
    g_tok = jnp.concatenate([g_hi.astype(F32), g_lo.astype(F32)], axis=0).T
    hx_ref[0] = jnp.concatenate([h2.astype(BF16), g_tok.astype(BF16)], axis=1)
    sel_b = sel.astype(BF16)
    start = cnt_ref[...]
    pos_t = jnp.dot(sel_b, tri_ref[...], preferred_element_type=F32) + start[:, 0:1]
    pos_ref[0] = jnp.where(sel > 0.0, pos_t, -1.0).astype(jnp.int32)
    n_tok = jnp.dot(sel_b, ones_ref[...], preferred_element_type=F32)
    n_chunk = jnp.floor((n_tok + (SEG_ROWS - 1)) * (1.0 / SEG_ROWS))
    seg_ref[0, 0] = start.astype(jnp.int32)
    seg_ref[0, 1] = n_chunk.astype(jnp.int32)
    cnt_ref[...] = start + n_chunk * SEG_ROWS


def _mix(attn, u, x, g1, sc2, sh2, w_pool, pool_scale, w_out, ln_g, ln_b, w_r_t, rb_t):
    b, s, d = x.shape
    tile = MIX_TILE
    hb = tile // POOL_HALO
    row = lambda bi, i: (bi, i, 0)
    vec = lambda bi, i: (bi, 0, 0)
    c2 = lambda bi, i: (0, 0)
    lane_row = lambda bi, i: (bi, 0, i)
    tri = (lax.broadcasted_iota(jnp.int32, (tile, tile), 0)
           < lax.broadcasted_iota(jnp.int32, (tile, tile), 1)).astype(BF16)
    ones = jnp.ones((tile, LANES), BF16)
    return pl.pallas_call(
        functools.partial(_mix_kernel, seq=s),
        grid=(b, s // tile),
        in_specs=[pl.BlockSpec((1, tile, POOL_WIDTH), row),
                  pl.BlockSpec((1, tile, POOL_WIDTH), row),
                  pl.BlockSpec((1, POOL_HALO, POOL_WIDTH),
                               lambda bi, i: (bi, jnp.maximum(i * hb - 1, 0), 0)),
                  pl.BlockSpec((1, POOL_HALO, POOL_WIDTH),
                               lambda bi, i: (bi, jnp.minimum((i + 1) * hb, s // POOL_HALO - 1), 0)),
                  pl.BlockSpec((1, tile, d), row),
                  pl.BlockSpec((1, 1, d), vec), pl.BlockSpec((1, 1, d), vec),
                  pl.BlockSpec((1, 1, d), vec),
                  pl.BlockSpec(w_pool.shape, lambda bi, i: (0, 0, 0)),
                  pl.BlockSpec(pool_scale.shape, c2),
                  pl.BlockSpec(w_out.shape, c2),
                  pl.BlockSpec(ln_g.shape, c2), pl.BlockSpec(ln_b.shape, c2),
                  pl.BlockSpec(w_r_t.shape, c2), pl.BlockSpec(rb_t.shape, c2),
                  pl.BlockSpec(tri.shape, c2), pl.BlockSpec(ones.shape, c2)],
        out_specs=[pl.BlockSpec((1, tile, d), row),
                   pl.BlockSpec((1, tile, HX_WIDTH), row),
                   pl.BlockSpec((1, N_EXPERTS, tile), lane_row),
                   pl.BlockSpec((1, 2, N_EXPERTS, LANES),
                                lambda bi, i: (bi * (s // tile) + i, 0, 0, 0)),
                   pl.BlockSpec((N_EXPERTS, LANES), c2)],
        out_shape=[jax.ShapeDtypeStruct((b, s, d), F32),
                   jax.ShapeDtypeStruct((b, s, HX_WIDTH), BF16),
                   jax.ShapeDtypeStruct((b, N_EXPERTS, s), jnp.int32),
                   jax.ShapeDtypeStruct((b * (s // tile), 2, N_EXPERTS, LANES), jnp.int32),
                   jax.ShapeDtypeStruct((N_EXPERTS, LANES), F32)],
        scratch_shapes=[pltpu.VMEM((tile + 2 * POOL_HALO, POOL_WIDTH), F32)],
        compiler_params=pltpu.CompilerParams(dimension_semantics=("arbitrary", "arbitrary"),
                                             vmem_limit_bytes=VMEM_LIMIT),
        name="mix",
    )(attn, u, u, u, x, g1, sc2, sh2, w_pool, pool_scale, w_out, ln_g, ln_b, w_r_t, rb_t,
      tri, ones)


def _one_hot(pos_row, start, window, n_tok):
    rows = lax.broadcasted_iota(jnp.int32, (SEG_WIN, n_tok), 0)
    hit = rows == (pos_row - (start * SEG_ROWS + window * SEG_WIN))
    return jnp.where(hit, 1.0, 0.0).astype(BF16)


def _window_chunks(base_ref, start_ref, step, e, window):
    return pl.ds(base_ref[e] + start_ref[step * N_EXPERTS + e] + window * WIN_CHUNKS, WIN_CHUNKS)


def _extra_windows(nchunk_ref, step, e):
    rows = nchunk_ref[step * N_EXPERTS + e] * SEG_ROWS
    return jnp.maximum((rows + SEG_WIN - 1) // SEG_WIN, 1)


def _dispatch_kernel(base_ref, start_ref, nchunk_ref, over_ref, total_ref, pcnt_ref, nu_ref,
                     hx_ref, pos_ref, xs_hbm, stage, extra, zero_ref, sem):
    i = pl.program_id(0)
    tile_chunks = MOE_TM // SEG_ROWS
    n_tiles = xs_hbm.shape[0] // tile_chunks
    n_tok, width = hx_ref.shape

    def zero_copy(chunk, n_chunks):
        return pltpu.make_async_copy(zero_ref.at[pl.ds(0, n_chunks)],
                                     xs_hbm.at[pl.ds(chunk, n_chunks)], sem.at[1])

    @pl.when(i == 0)
    def _():
        zero_ref[...] = jnp.zeros_like(zero_ref)
        for wait in (False, True):
            def per_tail(j, c):
                cp = zero_copy(j * tile_chunks, tile_chunks)
                cp.wait() if wait else cp.start()
                return c
            lax.fori_loop(nu_ref[0], n_tiles, per_tail, 0)

            def per_expert(e, c):
                def per_chunk(j, c2):
                    cp = zero_copy(base_ref[e] + j, 1)
                    cp.wait() if wait else cp.start()
                    return c2
                return lax.fori_loop(total_ref[e], pcnt_ref[e], per_chunk, c)
            lax.fori_loop(0, N_EXPERTS, per_expert, 0)

    slot = i % 2
    hx = hx_ref[...]
    group_chunks = WIN_GROUP * WIN_CHUNKS
    for g in range(N_EXPERTS // WIN_GROUP):
        oh = jnp.concatenate(
            [_one_hot(pos_ref[0, e:e + 1, :], start_ref[i * N_EXPERTS + e], 0, n_tok)
             for e in range(g * WIN_GROUP, (g + 1) * WIN_GROUP)], axis=0)
        rows = jnp.dot(oh, hx, preferred_element_type=F32).astype(BF16)
        stage[slot, pl.ds(g * group_chunks, group_chunks)] = rows.reshape(group_chunks, SEG_ROWS,
                                                                         width)

    def wait_windows(which):
        pltpu.make_async_copy(stage.at[which], xs_hbm.at[pl.ds(0, N_EXPERTS * WIN_CHUNKS)],
                              sem.at[0]).wait()

    @pl.when(i > 0)
    def _():
        wait_windows(1 - slot)

    for e in range(N_EXPERTS):
        pltpu.make_async_copy(stage.at[slot, pl.ds(e * WIN_CHUNKS, WIN_CHUNKS)],
                              xs_hbm.at[_window_chunks(base_ref, start_ref, i, e, 0)],
                              sem.at[0]).start()

    @pl.when(i == pl.num_programs(0) - 1)
    def _():
        wait_windows(slot)

    @pl.when(over_ref[i] > 0)
    def _():
        def more_windows(e, c):
            def one(window, c2):
                oh = _one_hot(pos_ref[0, pl.ds(e, 1), :], start_ref[i * N_EXPERTS + e], window,
                              n_tok)
                rows = jnp.dot(oh, hx, preferred_element_type=F32).astype(BF16)
                extra[...] = rows.reshape(WIN_CHUNKS, SEG_ROWS, width)
                cp = pltpu.make_async_copy(
                    extra, xs_hbm.at[_window_chunks(base_ref, start_ref, i, e, window)], sem.at[2])
                cp.start()
                cp.wait()
                return c2
            return lax.fori_loop(1, _extra_windows(nchunk_ref, i, e), one, c)
        lax.fori_loop(0, N_EXPERTS, more_windows, 0)


def _dispatch(base, seg_start, seg_chunks, seg_over, total, pcnt, n_used, hx, pos, n_slots):
    t, width = hx.shape
    tpb = pos.shape[2] // TOK_TILE
    return pl.pallas_call(
        _dispatch_kernel,
        grid_spec=pltpu.PrefetchScalarGridSpec(
            num_scalar_prefetch=7, grid=(t // TOK_TILE,),
            in_specs=[pl.BlockSpec((TOK_TILE, width), lambda i, *_: (i, 0)),
                      pl.BlockSpec((1, N_EXPERTS, TOK_TILE), lambda i, *_: (i // tpb, 0, i % tpb))],
            out_specs=pl.BlockSpec(memory_space=pl.ANY),
            scratch_shapes=[pltpu.VMEM((2, N_EXPERTS * WIN_CHUNKS, SEG_ROWS, width), BF16),
                            pltpu.VMEM((WIN_CHUNKS, SEG_ROWS, width), BF16),
                            pltpu.VMEM((MOE_TM // SEG_ROWS, SEG_ROWS, width), BF16),
                            pltpu.SemaphoreType.DMA((3,))]),
        out_shape=jax.ShapeDtypeStruct((n_slots // SEG_ROWS, SEG_ROWS, width), BF16),
        compiler_params=pltpu.CompilerParams(dimension_semantics=("arbitrary",),
                                             vmem_limit_bytes=VMEM_LIMIT),
        name="dispatch",
    )(base, seg_start, seg_chunks, seg_over, total, pcnt, n_used, hx, pos)


def _experts_kernel(te_ref, nu_ref, xs_ref, wg_ref, wu_ref, wd_ref, ys_ref, wg_b, wu_b, wd_b):
    i = pl.program_id(0)

    @pl.when(i < nu_ref[0])
    def _():
        @pl.when((i == 0) | (te_ref[i] != te_ref[jnp.maximum(i - 1, 0)]))
        def _():
            wg_b[...] = wg_ref[0].astype(BF16)
            wu_b[...] = wu_ref[0].astype(BF16)
            wd_b[...] = wd_ref[0].astype(BF16)

        half = MOE_TM // 2
        half_chunks = half // SEG_ROWS
        halves = [pl.ds(k * half_chunks, half_chunks) for k in range(2)]
        hidden = []
        for chunks in halves:
            xg = xs_ref[chunks].reshape(half, xs_ref.shape[2])
            x = xg[:, :D_MODEL]
            hidden.append((xg[:, D_MODEL:].astype(F32),
                           jnp.dot(x, wg_b[...], preferred_element_type=F32),
                           jnp.dot(x, wu_b[...], preferred_element_type=F32)))
        for chunks, (g, hg, hu) in zip(halves, hidden):
            lane = lax.broadcasted_iota(jnp.int32, g.shape, 1)
            mine = (lane == te_ref[i]) | (lane == te_ref[i] + N_EXPERTS)
            gate = jnp.sum(jnp.where(mine, g, 0.0), axis=1, keepdims=True)
            a = (_silu(hg) * hu * gate).astype(BF16)
            y = jnp.dot(a, wd_b[...], preferred_element_type=F32).astype(ys_ref.dtype)
            ys_ref[chunks] = y.reshape(half_chunks, SEG_ROWS, ys_ref.shape[2])

    @pl.when(i >= nu_ref[0])
    def _():
        ys_ref[...] = jnp.zeros_like(ys_ref)


def _experts(tile_expert, n_used, xs, w_e_gate, w_e_up, w_e_down):
    n_chunks, _, width = xs.shape
    _, d, f = w_e_gate.shape
    tile_chunks = MOE_TM // SEG_ROWS
    slot_tile = lambda i, te, nu: (jnp.minimum(i, nu[0] - 1), 0, 0)
    expert = lambda i, te, nu: (te[i], 0, 0)
    return pl.pallas_call(
        _experts_kernel,
        grid_spec=pltpu.PrefetchScalarGridSpec(
            num_scalar_prefetch=2, grid=(n_chunks // tile_chunks,),
            in_specs=[pl.BlockSpec((tile_chunks, SEG_ROWS, width), slot_tile),
                      pl.BlockSpec((1, d, f), expert), pl.BlockSpec((1, d, f), expert),
                      pl.BlockSpec((1, f, d), expert)],
            out_specs=pl.BlockSpec((tile_chunks, SEG_ROWS, d), lambda i, te, nu: (i, 0, 0)),
            scratch_shapes=[pltpu.VMEM((d, f), BF16), pltpu.VMEM((d, f), BF16),
                            pltpu.VMEM((f, d), BF16)]),
        out_shape=jax.ShapeDtypeStruct((n_chunks, SEG_ROWS, d), BF16),
        compiler_params=pltpu.CompilerParams(dimension_semantics=("arbitrary",),
                                             vmem_limit_bytes=VMEM_LIMIT),
        name="experts",
    )(tile_expert, n_used, xs, w_e_gate, w_e_up, w_e_down)


def _dot_tn(a, b):
    return lax.dot_general(a, b, (((0,), (0,)), ((), ())), preferred_element_type=F32)


def _combine_kernel(base_ref, start_ref, nchunk_ref, over_ref, hx_ref, pos_ref, x1_ref, g2_ref, wsg_ref,
                    wsu_ref, wsd_ref, ln_g_ref, ln_b_ref, ys_hbm, o_ref, win, extra, acc_ref, sem):
    i = pl.program_id(0)
    n = pl.num_programs(0)
    n_tok = hx_ref.shape[0]

    def fetch(step, slot):
        for e in range(N_EXPERTS):
            pltpu.make_async_copy(ys_hbm.at[_window_chunks(base_ref, start_ref, step, e, 0)],
                                  win.at[slot, pl.ds(e * WIN_CHUNKS, WIN_CHUNKS)],
                                  sem.at[slot]).start()

    @pl.when(i == 0)
    def _():
        fetch(0, 0)

    @pl.when(i + 1 < n)
    def _():
        fetch(i + 1, (i + 1) % 2)

    slot = i % 2
    h = hx_ref[:, :D_MODEL]
    a = _silu(jnp.dot(h, wsg_ref[...], preferred_element_type=F32)) * jnp.dot(
        h, wsu_ref[...], preferred_element_type=F32)
    moe = jnp.dot(a.astype(BF16), wsd_ref[...], preferred_element_type=F32)
    pltpu.make_async_copy(ys_hbm.at[pl.ds(0, N_EXPERTS * WIN_CHUNKS)], win.at[slot],
                          sem.at[slot]).wait()
    d = win.shape[3]
    group_chunks = WIN_GROUP * WIN_CHUNKS
    for g in range(N_EXPERTS // WIN_GROUP):
        oh = jnp.concatenate(
            [_one_hot(pos_ref[0, e:e + 1, :], start_ref[i * N_EXPERTS + e], 0, n_tok)
             for e in range(g * WIN_GROUP, (g + 1) * WIN_GROUP)], axis=0)
        rows = win[slot, pl.ds(g * group_chunks, group_chunks)].reshape(WIN_GROUP * SEG_WIN, d)
        moe = moe + _dot_tn(oh, rows)
    acc_ref[...] = moe

    @pl.when(over_ref[i] > 0)
    def _():
        def more_windows(e, c):
            def one(window, c2):
                cp = pltpu.make_async_copy(
                    ys_hbm.at[_window_chunks(base_ref, start_ref, i, e, window)], extra, sem.at[2])
                cp.start()
                cp.wait()
                oh = _one_hot(pos_ref[0, pl.ds(e, 1), :], start_ref[i * N_EXPERTS + e], window,
                              n_tok)
                acc_ref[...] += _dot_tn(oh, extra[...].reshape(SEG_WIN, d))
                return c2
            return lax.fori_loop(1, _extra_windows(nchunk_ref, i, e), one, c)
        lax.fori_loop(0, N_EXPERTS, more_windows, 0)

    z = ALPHA * x1_ref[...] + g2_ref[0] * acc_ref[...]
    o_ref[...] = _layer_norm(z, ln_g_ref[...], ln_b_ref[...])


def _combine(base, seg_start, seg_chunks, seg_over, hx, pos, x1, g2, w_s_gate, w_s_up, w_s_down,
             ln_g, ln_b, ys):
    t, d = x1.shape
    tpb = pos.shape[2] // TOK_TILE
    row = lambda i, *_: (i, 0)
    c2 = lambda i, *_: (0, 0)
    return pl.pallas_call(
        _combine_kernel,
        grid_spec=pltpu.PrefetchScalarGridSpec(
            num_scalar_prefetch=4, grid=(t // TOK_TILE,),
            in_specs=[pl.BlockSpec((TOK_TILE, hx.shape[1]), row),
                      pl.BlockSpec((1, N_EXPERTS, TOK_TILE), lambda i, *_: (i // tpb, 0, i % tpb)),
                      pl.BlockSpec((TOK_TILE, d), row),
                      pl.BlockSpec((1, 1, d), lambda i, *_: (i // tpb, 0, 0)),
                      pl.BlockSpec(w_s_gate.shape, c2), pl.BlockSpec(w_s_up.shape, c2),
                      pl.BlockSpec(w_s_down.shape, c2),
                      pl.BlockSpec(ln_g.shape, c2), pl.BlockSpec(ln_b.shape, c2),
                      pl.BlockSpec(memory_space=pl.ANY)],
            out_specs=pl.BlockSpec((TOK_TILE, d), row),
            scratch_shapes=[pltpu.VMEM((2, N_EXPERTS * WIN_CHUNKS, SEG_ROWS, d), BF16),
                            pltpu.VMEM((WIN_CHUNKS, SEG_ROWS, d), BF16),
                            pltpu.VMEM((TOK_TILE, d), F32),
                            pltpu.SemaphoreType.DMA((3,))]),
        out_shape=jax.ShapeDtypeStruct((t, d), F32),
        compiler_params=pltpu.CompilerParams(dimension_semantics=("arbitrary",),
                                             vmem_limit_bytes=VMEM_LIMIT),
        name="combine",
    )(base, seg_start, seg_chunks, seg_over, hx, pos, x1, g2, w_s_gate, w_s_up, w_s_down, ln_g,
      ln_b, ys)


def _moe(hx, pos, seg, counts, x1, g2, w_e_gate, w_e_up, w_e_down, w_s_gate, w_s_up, w_s_down,
         ln_g, ln_b):
    b, s, d = x1.shape
    t = b * s
    n_seg = (t // TOK_TILE) * N_EXPERTS
    max_rows = t * TOP_K + n_seg * (SEG_ROWS - 1) + N_EXPERTS * (SEG_WIN + MOE_TM)
    n_tiles = -(-max_rows // MOE_TM)
    total = jnp.round(counts[:, 0]).astype(jnp.int32)
    pcnt = (total + SEG_WIN + MOE_TM - 1) // MOE_TM * MOE_TM
    ends = jnp.cumsum(pcnt)
    base = ends - pcnt
    n_used = (ends[-1] // MOE_TM).reshape(1)
    tile_ids = jnp.arange(n_tiles, dtype=jnp.int32)
    tile_expert = jnp.sum((ends[None, :] <= tile_ids[:, None] * MOE_TM).astype(jnp.int32), axis=1)
    tile_expert = jnp.minimum(tile_expert, N_EXPERTS - 1)
    tile_expert = jnp.where(tile_ids < n_used, tile_expert, tile_expert[n_used[0] - 1])
    base, total, pcnt = base // SEG_ROWS, total // SEG_ROWS, pcnt // SEG_ROWS
    seg_start = seg[:, 0, :, 0].reshape(n_seg) // SEG_ROWS
    seg_chunks = seg[:, 1, :, 0]
    seg_over = (jnp.max(seg_chunks, axis=1) * SEG_ROWS > SEG_WIN).astype(jnp.int32)
    seg_chunks = seg_chunks.reshape(n_seg)

    hxf = hx.reshape(t, hx.shape[2])
    xs = _dispatch(base, seg_start, seg_chunks, seg_over, total, pcnt, n_used, hxf, pos,
                   n_tiles * MOE_TM)
    ys = _experts(tile_expert, n_used, xs, w_e_gate, w_e_up, w_e_down)
    out = _combine(base, seg_start, seg_chunks, seg_over, hxf, pos, x1.reshape(t, d), g2, w_s_gate,
                   w_s_up, w_s_down, ln_g, ln_b, ys)
    return out.reshape(b, s, d)


def _rope_tables(seq):
    t = jnp.arange(seq)
    pos = jnp.stack([t // GRID_W, t % GRID_W], axis=-1).astype(F32)
    inv_freq = ROPE_THETA ** (-jnp.arange(ROPE_FREQS, dtype=F32) / ROPE_FREQS)
    ang = pos[:, :, None] * inv_freq
    cos, sin = jnp.cos(ang), jnp.sin(ang)
    zero = jnp.zeros_like(sin)
    cos_r = jnp.stack([cos, cos], axis=2).reshape(seq, QK_ROPE)
    sin_lo = jnp.stack([-sin, zero], axis=2).reshape(seq, QK_ROPE)
    sin_hi = jnp.stack([zero, sin], axis=2).reshape(seq, QK_ROPE)
    pads = ((0, 0), (QK_NOPE, HEAD_PAD - QK_NOPE - QK_ROPE))
    lane_tabs = (jnp.pad(cos_r, pads, constant_values=1.0), jnp.pad(sin_lo, pads),
                 jnp.pad(sin_hi, pads))
    row_tabs = (cos.reshape(seq, 2 * ROPE_FREQS).T, sin.reshape(seq, 2 * ROPE_FREQS).T)
    return lane_tabs, row_tabs


def _pad_heads(w, width, padded):
    k = w.shape[0]
    w = jnp.pad(w.reshape(k, N_HEADS, width), ((0, 0), (0, 0), (0, padded - width)))
    return w.reshape(k, N_HEADS * padded)


def kernel(x, c, ctx, c_ctx, w_ada, b_ada, w_in, q_norm_g, w_uq, kv_norm_g, w_ukv, w_pool, pool_scale, w_out, ln1_g, ln1_b, w_router, router_bias, w_e_gate, w_e_up, w_e_down, w_s_gate, w_s_up, w_s_down, ln2_g, ln2_b):
    assert w_ada.shape[0] == 1, "single-layer block"
    b, s, d = x.shape

    cvec = jnp.concatenate([c, c_ctx[None], jnp.zeros((SUBLANES - b - 1, d), F32)], axis=0)
    mod = _ada(cvec, w_ada[0], b_ada)
    sh1, sc1, g1, sh2, sc2, g2 = [mod[:b, k * d:(k + 1) * d][:, None, :] for k in range(6)]
    sh1c, sc1c = [jnp.broadcast_to(mod[b, k * d:(k + 1) * d], (b, 1, d)) for k in range(2)]

    wi = w_in[0]
    kr_cols = jnp.pad(wi[:, Q_LORA + KV_LORA:Q_LORA + KV_LORA + QK_ROPE],
                      ((0, 0), (QK_NOPE, HEAD_PAD - QK_NOPE - QK_ROPE)))
    w_in_r = jnp.concatenate([wi[:, :Q_LORA + KV_LORA], wi[:, Q_LORA + KV_LORA + QK_ROPE:], kr_cols],
                             axis=1).astype(BF16)
    w_uq_t = _pad_heads(w_uq[0], QK_NOPE + QK_ROPE, HEAD_PAD).T.astype(BF16)
    wkv = w_ukv[0].reshape(KV_LORA, N_HEADS, QK_NOPE + V_HEAD)
    w_uk_p = _pad_heads(wkv[:, :, :QK_NOPE].reshape(KV_LORA, -1), QK_NOPE, HEAD_PAD).astype(BF16)
    w_uv_t = _pad_heads(wkv[:, :, QK_NOPE:].reshape(KV_LORA, -1), V_HEAD, V_ROWS).T.astype(BF16)
    tables = _rope_tables(s)

    q_t, k, v_t, u = _proj(x, sc1, sh1, tables, w_in_r, q_norm_g, w_uq_t, kv_norm_g, w_uk_p,
                           w_uv_t, PROJ_TILE)
    kc, vc_t = _proj(ctx, sc1c, sh1c, None, w_in_r, None, None, kv_norm_g, w_uk_p, w_uv_t,
                     ctx.shape[1])
    attn = _attention(q_t, kc, vc_t, k, v_t)

    x1, hx, pos, seg, counts = _mix(attn, u, x, g1, sc2, sh2, w_pool[0].astype(BF16),
                                    pool_scale, w_out[0].astype(BF16), ln1_g, ln1_b,
                                    w_router[0].T, router_bias[0][:, None])
    return _moe(hx, pos, seg, counts, x1, g2, w_e_gate[0], w_e_up[0], w_e_down[0],
                w_s_gate[0].astype(BF16), w_s_up[0].astype(BF16), w_s_down[0].astype(BF16),
                ln2_g, ln2_b)
```

```python
import functools
import math

import jax
import jax.numpy as jnp
from jax import lax
from jax.experimental import pallas as pl
from jax.experimental.pallas import tpu as pltpu

F32 = jnp.float32
BF16 = jnp.bfloat16

D_MODEL = 1024
GRID_W = 64
N_HEADS = 8
Q_LORA = 512
KV_LORA = 256
QK_NOPE = 64
QK_ROPE = 32
V_HEAD = 64
ROPE_FREQS = QK_ROPE // 4
ROPE_THETA = 10000.0
ATTN_SCALE = 1.0 / math.sqrt(QK_NOPE + QK_ROPE)
LOG2_E = math.log2(math.e)
POOL_GROUPS = 4
POOL_WINDOWS = (2, 4, 8, 16)
POOL_WIDTH = 512
POOL_GC = POOL_WIDTH // POOL_GROUPS
POOL_HALO = 8
N_EXPERTS = 64
N_EXPERT_GROUPS = 8
GROUP_SIZE = N_EXPERTS // N_EXPERT_GROUPS
TOPK_GROUPS = 4
TOP_K = 8
D_EXPERT = 256
ROUTED_SCALE = 2.5
LN_EPS = 1e-5
RMS_EPS = 1e-6
ALPHA = 2.0 ** 0.25

LANES = 128
SUBLANES = 8
HEAD_PAD = LANES
V_ROWS = 80
ONES_ROW = V_HEAD
IN_PAD = Q_LORA + KV_LORA + POOL_WIDTH + LANES

PROJ_TILE = 512
ATTN_TQ = 512
ATTN_TK = PROJ_TILE
ATTN_SUB = 256
ATTN_AHEAD = 2
ATTN_UNROLL = 16
TOK_TILE = 256
MIX_TILE = TOK_TILE
MOE_TM = 1024
SEG_ROWS = 16
SEG_WIN = 64
WIN_CHUNKS = SEG_WIN // SEG_ROWS
HALF_CHUNKS = WIN_CHUNKS // 2
WIN_GROUP = 16
HX_WIDTH = D_MODEL + 2 * N_EXPERTS
VMEM_LIMIT = 48 * 1024 * 1024
NEG_BIG = -1e30


def _silu(v):
    return v * jax.nn.sigmoid(v)


def _layer_norm(z, g, b):
    mu = jnp.mean(z, axis=-1, keepdims=True)
    zc = z - mu
    var = jnp.mean(zc * zc, axis=-1, keepdims=True)
    return zc * lax.rsqrt(var + LN_EPS) * g + b


def _rms_norm(v, g):
    return v * lax.rsqrt(jnp.mean(v * v, axis=-1, keepdims=True) + RMS_EPS) * g


def _dot_nt(a, b):
    return lax.dot_general(a, b, (((1,), (1,)), ((), ())), preferred_element_type=F32)


def _ada_kernel(c_ref, w_ref, b_ref, o_ref):
    cv = _silu(c_ref[...])
    o_ref[...] = jnp.dot(cv, w_ref[...], preferred_element_type=F32,
                         precision=lax.Precision.HIGHEST) + b_ref[...]


def _ada(cvec, w_ada, b_ada):
    rows, d = cvec.shape
    n = w_ada.shape[1]
    tn = 1024
    return pl.pallas_call(
        _ada_kernel,
        grid=(n // tn,),
        in_specs=[pl.BlockSpec((rows, d), lambda j: (0, 0)),
                  pl.BlockSpec((d, tn), lambda j: (0, j)),
                  pl.BlockSpec((1, tn), lambda j: (0, j))],
        out_specs=pl.BlockSpec((rows, tn), lambda j: (0, j)),
        out_shape=jax.ShapeDtypeStruct((rows, n), F32),
        compiler_params=pltpu.CompilerParams(dimension_semantics=("arbitrary",),
                                             vmem_limit_bytes=VMEM_LIMIT),
        name="ada",
    )(cvec, w_ada, b_ada)


def _rope_lanes(v, cos, sin_lo, sin_hi):
    return v * cos + pltpu.roll(v, LANES - 8, axis=1) * sin_lo + pltpu.roll(v, 8, axis=1) * sin_hi


def _proj_kernel(*refs, with_q):
    if with_q:
        (x_ref, sc_ref, sh_ref, cos_ref, slo_ref, shi_ref, cos_t_ref, sin_t_ref, win_ref, qg_ref,
         wuq_ref, kvg_ref, wuk_ref, wuv_ref, q_ref, k_ref, v_ref, u_ref) = refs
    else:
        (x_ref, sc_ref, sh_ref, win_ref, kvg_ref, wuk_ref, wuv_ref, k_ref, v_ref) = refs
    h = (x_ref[0] * (1.0 + sc_ref[0]) + sh_ref[0]).astype(BF16)
    p = jnp.dot(h, win_ref[...], preferred_element_type=F32)
    tile = p.shape[0]
    kr = p[:, IN_PAD - LANES:]
    kvn = _rms_norm(p[:, Q_LORA:Q_LORA + KV_LORA], kvg_ref[...]).astype(BF16)
    kfull = jnp.dot(kvn, wuk_ref[...], preferred_element_type=F32)
    v_t = _dot_nt(wuv_ref[...], kvn)
    row = lax.broadcasted_iota(jnp.int32, (N_HEADS * V_ROWS, 1), 0)
    v_t = v_t + (row % V_ROWS == ONES_ROW).astype(F32)
    if with_q:
        kr = _rope_lanes(kr, cos_ref[...], slo_ref[...], shi_ref[...])
        u_ref[0] = p[:, Q_LORA + KV_LORA:Q_LORA + KV_LORA + POOL_WIDTH]
        qn = _rms_norm(p[:, :Q_LORA], qg_ref[...]).astype(BF16)
        q_t = _dot_nt(wuq_ref[...], qn) * (ATTN_SCALE * LOG2_E)
        cos_t, sin_t = cos_t_ref[...], sin_t_ref[...]
    for hd in range(N_HEADS):
        k_ref[0, hd] = (kfull[:, hd * HEAD_PAD:(hd + 1) * HEAD_PAD] + kr).astype(BF16)
        v_ref[0, hd, 0] = v_t[hd * V_ROWS:(hd + 1) * V_ROWS].astype(BF16)
        if with_q:
            base = hd * HEAD_PAD
            q_ref[0, hd, 0:QK_NOPE, :] = q_t[base:base + QK_NOPE].astype(BF16)
            rope = []
            for ax in range(2):
                lo = q_t[base + QK_NOPE + 16 * ax:base + QK_NOPE + 16 * ax + 8]
                hi = q_t[base + QK_NOPE + 16 * ax + 8:base + QK_NOPE + 16 * ax + 16]
                cs, sn = cos_t[8 * ax:8 * ax + 8], sin_t[8 * ax:8 * ax + 8]
                rope += [lo * cs - hi * sn, hi * cs + lo * sn]
            rope.append(jnp.zeros((HEAD_PAD - QK_NOPE - QK_ROPE, tile), F32))
            q_ref[0, hd, QK_NOPE:, :] = jnp.concatenate(rope, axis=0).astype(BF16)


def _proj(x, sc, sh, tables, w_in_r, q_g, w_uq_t, kv_g, w_uk_p, w_uv_t, tile):
    b, s, d = x.shape
    with_q = tables is not None
    grid = (b, s // tile)
    row = lambda bi, i: (bi, i, 0)
    vec = lambda bi, i: (bi, 0, 0)
    const2 = lambda bi, i: (0, 0)
    k_out = pl.BlockSpec((1, N_HEADS, tile, HEAD_PAD), lambda bi, i: (bi, 0, i, 0))
    k_shape = jax.ShapeDtypeStruct((b, N_HEADS, s, HEAD_PAD), BF16)
    v_out = pl.BlockSpec((1, N_HEADS, 1, V_ROWS, tile), lambda bi, i: (bi, 0, i, 0, 0))
    v_shape = jax.ShapeDtypeStruct((b, N_HEADS, s // tile, V_ROWS, tile), BF16)
    in_specs = [pl.BlockSpec((1, tile, d), row),
                pl.BlockSpec((1, 1, d), vec), pl.BlockSpec((1, 1, d), vec)]
    args = [x, sc, sh]
    if with_q:
        lane_tabs, row_tabs = tables
        in_specs += [pl.BlockSpec((tile, LANES), lambda bi, i: (i, 0))] * 3
        in_specs += [pl.BlockSpec((2 * ROPE_FREQS, tile), lambda bi, i: (0, i))] * 2
        args += list(lane_tabs) + list(row_tabs)
    in_specs.append(pl.BlockSpec(w_in_r.shape, const2)); args.append(w_in_r)
    if with_q:
        in_specs += [pl.BlockSpec(q_g.shape, const2), pl.BlockSpec(w_uq_t.shape, const2)]
        args += [q_g, w_uq_t]
    in_specs += [pl.BlockSpec(kv_g.shape, const2), pl.BlockSpec(w_uk_p.shape, const2),
                 pl.BlockSpec(w_uv_t.shape, const2)]
    args += [kv_g, w_uk_p, w_uv_t]
    if with_q:
        q_out = pl.BlockSpec((1, N_HEADS, HEAD_PAD, tile), lambda bi, i: (bi, 0, 0, i))
        q_shape = jax.ShapeDtypeStruct((b, N_HEADS, HEAD_PAD, s), BF16)
        out_specs = [q_out, k_out, v_out, pl.BlockSpec((1, tile, POOL_WIDTH), row)]
        out_shape = [q_shape, k_shape, v_shape, jax.ShapeDtypeStruct((b, s, POOL_WIDTH), F32)]
    else:
        out_specs = [k_out, v_out]
        out_shape = [k_shape, v_shape]
    return pl.pallas_call(
        functools.partial(_proj_kernel, with_q=with_q),
        grid=grid, in_specs=in_specs, out_specs=out_specs, out_shape=out_shape,
        compiler_params=pltpu.CompilerParams(dimension_semantics=("arbitrary", "arbitrary"),
                                             vmem_limit_bytes=VMEM_LIMIT),
        name="proj" if with_q else "proj_ctx",
    )(*args)


def _attn_kernel(q_ref, kc_ref, vc_ref, k_ref, v_ref, o_ref, *, n_kblk, tk):
    tq = q_ref.shape[3]
    qs = [q_ref[0, hh] for hh in range(2)]

    def scores(hh, kb):
        return jnp.dot(kb, qs[hh], preferred_element_type=F32)

    def update(s_t, vb_t, m, acc):
        m_new = jnp.maximum(m, jnp.max(s_t, axis=0, keepdims=True))
        p_t = jnp.exp2(s_t - m_new).astype(BF16)
        acc = jnp.exp2(m - m_new) * acc + jnp.dot(vb_t, p_t, preferred_element_type=F32)
        return m_new, acc

    def run_items(items, state):
        pending = [scores(hh, kb()) for hh, kb, _ in items[:ATTN_AHEAD]]
        for j, (hh, _, vb) in enumerate(items):
            if j + ATTN_AHEAD < len(items):
                nh, nkb, _ = items[j + ATTN_AHEAD]
                pending.append(scores(nh, nkb()))
            state[hh] = update(pending.pop(0), vb(), *state[hh])
        return state

    def block_items(blk, off):
        out = []
        for sub in range(tk // ATTN_SUB):
            for hh in range(2):
                lo = sub * ATTN_SUB
                out.append((hh,
                            lambda hh=hh, lo=lo: k_ref[0, hh, pl.ds(off + lo, ATTN_SUB), :],
                            lambda hh=hh, lo=lo: v_ref[0, hh, blk, :, lo:lo + ATTN_SUB]))
        return out

    ctx_items = [(hh, lambda hh=hh: kc_ref[0, hh], lambda hh=hh: vc_ref[0, hh, 0])
                 for hh in range(2)]
    state = [(jnp.full((1, tq), NEG_BIG, F32), jnp.zeros((V_ROWS, tq), F32)) for _ in range(2)]
    n_iter = n_kblk // ATTN_UNROLL
    if n_iter == 1:
        items = ctx_items
        for blk in range(n_kblk):
            items = items + block_items(blk, blk * tk)
        state = run_items(items, state)
    else:
        state = run_items(ctx_items, state)

        def body(i, carry):
            items = []
            for r in range(ATTN_UNROLL):
                blk = i * ATTN_UNROLL + r
                items += block_items(blk, pl.multiple_of(blk * tk, tk))
            st = run_items(items, [(carry[0], carry[1]), (carry[2], carry[3])])
            return st[0] + st[1]

        carry = lax.fori_loop(0, n_iter, body, state[0] + state[1])
        state = [(carry[0], carry[1]), (carry[2], carry[3])]
    carry = state[0] + state[1]
    outs = [carry[2 * hh + 1][:V_HEAD] / carry[2 * hh + 1][ONES_ROW:ONES_ROW + 1] for hh in range(2)]
    o_ref[0] = jnp.concatenate(outs, axis=0).T.astype(o_ref.dtype)


def _attention(q_t, kc, vc_t, k, v_t):
    b, nh, dp, s = q_t.shape
    c = kc.shape[2]
    tq, tk = ATTN_TQ, ATTN_TK
    n_kblk = s // tk
    kern = functools.partial(_attn_kernel, n_kblk=n_kblk, tk=tk)
    return pl.pallas_call(
        kern,
        grid=(b, nh // 2, s // tq),
        in_specs=[pl.BlockSpec((1, 2, dp, tq), lambda bi, hp, qi: (bi, hp, 0, qi)),
                  pl.BlockSpec((1, 2, c, dp), lambda bi, hp, qi: (bi, hp, 0, 0)),
                  pl.BlockSpec((1, 2, 1, V_ROWS, c), lambda bi, hp, qi: (bi, hp, 0, 0, 0)),
                  pl.BlockSpec((1, 2, s, dp), lambda bi, hp, qi: (bi, hp, 0, 0)),
                  pl.BlockSpec((1, 2, n_kblk, V_ROWS, tk), lambda bi, hp, qi: (bi, hp, 0, 0, 0))],
        out_specs=pl.BlockSpec((1, tq, 2 * V_HEAD), lambda bi, hp, qi: (bi, qi, hp)),
        out_shape=jax.ShapeDtypeStruct((b, s, nh * V_HEAD), BF16),
        compiler_params=pltpu.CompilerParams(
            dimension_semantics=("arbitrary", "arbitrary", "arbitrary"),
            vmem_limit_bytes=VMEM_LIMIT),
        name="attn",
    )(q_t, kc, vc_t, k, v_t)


def _route(logits_t, bias_t):
    e, t = logits_t.shape
    scores = jax.nn.sigmoid(logits_t)
    biased = scores + bias_t
    neg_inf = F32(-jnp.inf)
    gscore = []
    for g in range(N_EXPERT_GROUPS):
        v = biased[g * GROUP_SIZE:(g + 1) * GROUP_SIZE]
        m1 = jnp.max(v, axis=0, keepdims=True)
        at_max = v == m1
        n_max = jnp.sum(at_max.astype(F32), axis=0, keepdims=True)
        m2 = jnp.max(jnp.where(at_max, neg_inf, v), axis=0, keepdims=True)
        gscore.append(m1 + jnp.where(n_max >= 2.0, m1, m2))
    masked = []
    for g in range(N_EXPERT_GROUPS):
        rank = jnp.zeros((1, t), F32)
        for o in range(N_EXPERT_GROUPS):
            if o == g:
                continue
            beats = (gscore[o] >= gscore[g]) if o < g else (gscore[o] > gscore[g])
            rank = rank + beats.astype(F32)
        keep = rank < float(TOPK_GROUPS)
        masked.append(jnp.where(keep, biased[g * GROUP_SIZE:(g + 1) * GROUP_SIZE], neg_inf))
    work = jnp.concatenate(masked, axis=0)
    rows = lax.broadcasted_iota(jnp.int32, (e, t), 0)
    sel = jnp.zeros((e, t), F32)
    for _ in range(TOP_K):
        m = jnp.max(work, axis=0, keepdims=True)
        first = jnp.min(jnp.where(work == m, rows, e), axis=0, keepdims=True)
        pick = rows == first
        sel = jnp.where(pick, 1.0, sel)
        work = jnp.where(pick, neg_inf, work)
    w = sel * scores
    gates = w / jnp.sum(w, axis=0, keepdims=True) * ROUTED_SCALE
    return sel, gates


def _mix_kernel(attn_ref, u_ref, up_ref, un_ref, x_ref, g1_ref, sc2_ref, sh2_ref, wpool_ref,
                pscale_ref, wout_ref, ln_g_ref, ln_b_ref, wr_ref, rb_ref, tri_ref, ones_ref,
                x1_ref, hx_ref, pos_ref, seg_ref, cnt_ref, uext_ref, *, seq):
    i = pl.program_id(1)

    @pl.when((pl.program_id(0) == 0) & (i == 0))
    def _():
        cnt_ref[...] = jnp.zeros_like(cnt_ref)

    tile = u_ref.shape[1]
    u = u_ref[0]
    uext_ref[0:POOL_HALO] = jnp.where(i == 0, 0.0, up_ref[0])
    uext_ref[POOL_HALO:POOL_HALO + tile] = u
    uext_ref[POOL_HALO + tile:] = jnp.where(i == pl.num_programs(1) - 1, 0.0, un_ref[0])
    t = i * tile + lax.broadcasted_iota(jnp.int32, (tile, POOL_GC), 0)
    pooled = []
    for g, w in enumerate(POOL_WINDOWS):
        lanes = slice(g * POOL_GC, (g + 1) * POOL_GC)
        tot = uext_ref[POOL_HALO - w // 2:POOL_HALO - w // 2 + tile, lanes]
        for dlt in range(1, w):
            start = POOL_HALO - w // 2 + dlt
            tot = tot + uext_ref[start:start + tile, lanes]
        cnt = (jnp.minimum(t - w // 2 + w, seq) - jnp.maximum(t - w // 2, 0)).astype(F32)
        pg = (tot / cnt - u[:, lanes]).astype(BF16)
        po = jnp.dot(pg, wpool_ref[g], preferred_element_type=F32) * pscale_ref[:, lanes]
        pooled.append(po.astype(BF16))
    mixed = jnp.concatenate([attn_ref[0]] + pooled, axis=1)
    y = jnp.dot(mixed, wout_ref[...], preferred_element_type=F32)
    x1 = _layer_norm(ALPHA * x_ref[0] + g1_ref[0] * y, ln_g_ref[...], ln_b_ref[...])
    x1_ref[0] = x1
    h2 = x1 * (1.0 + sc2_ref[0]) + sh2_ref[0]
    logits_t = lax.dot_general(wr_ref[...], h2, (((1,), (1,)), ((), ())),
                               preferred_element_type=F32, precision=lax.Precision.HIGHEST)
    sel, gates_t = _route(logits_t, rb_ref[...])
    g_hi = gates_t.astype(BF16)
    g_lo = (gates_t - g_hi.astype(F32)).astype(BF16)
    g_tok = jnp.concatenate([g_hi.astype(F32), g_lo.astype(F32)], axis=0).T
    hx_ref[0] = jnp.concatenate([h2.astype(BF16), g_tok.astype(BF16)], axis=1)
    sel_b = sel.astype(BF16)
    start = cnt_ref[...]
    pos_t = jnp.dot(sel_b, tri_ref[...], preferred_element_type=F32) + start[:, 0:1]
    pos_ref[0] = jnp.where(sel > 0.0, pos_t, -1.0).astype(jnp.int32)
    n_tok = jnp.dot(sel_b, ones_ref[...], preferred_element_type=F32)
    n_chunk = jnp.floor((n_tok + (SEG_ROWS - 1)) * (1.0 / SEG_ROWS))
    seg_ref[0, 0] = start.astype(jnp.int32)
    seg_ref[0, 1] = n_chunk.astype(jnp.int32)
    cnt_ref[...] = start + n_chunk * SEG_ROWS


def _mix(attn, u, x, g1, sc2, sh2, w_pool, pool_scale, w_out, ln_g, ln_b, w_r_t, rb_t):
    b, s, d = x.shape
    tile = MIX_TILE
    hb = tile // POOL_HALO
    row = lambda bi, i: (bi, i, 0)
    vec = lambda bi, i: (bi, 0, 0)
    c2 = lambda bi, i: (0, 0)
    lane_row = lambda bi, i: (bi, 0, i)
    tri = (lax.broadcasted_iota(jnp.int32, (tile, tile), 0)
           < lax.broadcasted_iota(jnp.int32, (tile, tile), 1)).astype(BF16)
    ones = jnp.ones((tile, LANES), BF16)
    return pl.pallas_call(
        functools.partial(_mix_kernel, seq=s),
        grid=(b, s // tile),
        in_specs=[pl.BlockSpec((1, tile, POOL_WIDTH), row),
                  pl.BlockSpec((1, tile, POOL_WIDTH), row),
                  pl.BlockSpec((1, POOL_HALO, POOL_WIDTH),
                               lambda bi, i: (bi, jnp.maximum(i * hb - 1, 0), 0)),
                  pl.BlockSpec((1, POOL_HALO, POOL_WIDTH),
                               lambda bi, i: (bi, jnp.minimum((i + 1) * hb, s // POOL_HALO - 1), 0)),
                  pl.BlockSpec((1, tile, d), row),
                  pl.BlockSpec((1, 1, d), vec), pl.BlockSpec((1, 1, d), vec),
                  pl.BlockSpec((1, 1, d), vec),
                  pl.BlockSpec(w_pool.shape, lambda bi, i: (0, 0, 0)),
                  pl.BlockSpec(pool_scale.shape, c2),
                  pl.BlockSpec(w_out.shape, c2),
                  pl.BlockSpec(ln_g.shape, c2), pl.BlockSpec(ln_b.shape, c2),
                  pl.BlockSpec(w_r_t.shape, c2), pl.BlockSpec(rb_t.shape, c2),
                  pl.BlockSpec(tri.shape, c2), pl.BlockSpec(ones.shape, c2)],
        out_specs=[pl.BlockSpec((1, tile, d), row),
                   pl.BlockSpec((1, tile, HX_WIDTH), row),
                   pl.BlockSpec((1, N_EXPERTS, tile), lane_row),
                   pl.BlockSpec((1, 2, N_EXPERTS, LANES),
                                lambda bi, i: (bi * (s // tile) + i, 0, 0, 0)),
                   pl.BlockSpec((N_EXPERTS, LANES), c2)],
        out_shape=[jax.ShapeDtypeStruct((b, s, d), F32),
                   jax.ShapeDtypeStruct((b, s, HX_WIDTH), BF16),
                   jax.ShapeDtypeStruct((b, N_EXPERTS, s), jnp.int32),
                   jax.ShapeDtypeStruct((b * (s // tile), 2, N_EXPERTS, LANES), jnp.int32),
                   jax.ShapeDtypeStruct((N_EXPERTS, LANES), F32)],
        scratch_shapes=[pltpu.VMEM((tile + 2 * POOL_HALO, POOL_WIDTH), F32)],
        compiler_params=pltpu.CompilerParams(dimension_semantics=("arbitrary", "arbitrary"),
                                             vmem_limit_bytes=VMEM_LIMIT),
        name="mix",
    )(attn, u, u, u, x, g1, sc2, sh2, w_pool, pool_scale, w_out, ln_g, ln_b, w_r_t, rb_t,
      tri, ones)


def _one_hot(pos_row, start, window, n_tok):
    rows = lax.broadcasted_iota(jnp.int32, (SEG_WIN, n_tok), 0)
    hit = rows == (pos_row - (start * SEG_ROWS + window * SEG_WIN))
    return jnp.where(hit, 1.0, 0.0).astype(BF16)


def _window_chunks(base_ref, start_ref, step, e, window, n_chunks=WIN_CHUNKS):
    return pl.ds(base_ref[e] + start_ref[step * N_EXPERTS + e] + window * WIN_CHUNKS, n_chunks)


def _start_window(seg_chunks, make_copy):
    @pl.when(seg_chunks <= HALF_CHUNKS)
    def _():
        make_copy(HALF_CHUNKS).start()

    @pl.when(seg_chunks > HALF_CHUNKS)
    def _():
        make_copy(WIN_CHUNKS).start()


def _wait_half_windows(make_copy, units):
    make_copy(N_EXPERTS).wait()
    rest = units - N_EXPERTS
    for bit in range(N_EXPERTS.bit_length()):
        @pl.when((rest >> bit) & 1 == 1)
        def _():
            make_copy(1 << bit).wait()


def _extra_windows(nchunk_ref, step, e):
    rows = nchunk_ref[step * N_EXPERTS + e] * SEG_ROWS
    return jnp.maximum((rows + SEG_WIN - 1) // SEG_WIN, 1)


def _dispatch_kernel(base_ref, start_ref, nchunk_ref, units_ref, over_ref, total_ref, pcnt_ref,
                     nu_ref, hx_ref, pos_ref, xs_hbm, stage, extra, zero_ref, sem):
    i = pl.program_id(0)
    tile_chunks = MOE_TM // SEG_ROWS
    n_tiles = xs_hbm.shape[0] // tile_chunks
    n_tok, width = hx_ref.shape

    def zero_copy(chunk, n_chunks):
        return pltpu.make_async_copy(zero_ref.at[pl.ds(0, n_chunks)],
                                     xs_hbm.at[pl.ds(chunk, n_chunks)], sem.at[1])

    @pl.when(i == 0)
    def _():
        zero_ref[...] = jnp.zeros_like(zero_ref)
        for wait in (False, True):
            def per_tail(j, c):
                cp = zero_copy(j * tile_chunks, tile_chunks)
                cp.wait() if wait else cp.start()
                return c
            lax.fori_loop(nu_ref[0], n_tiles, per_tail, 0)

            def per_expert(e, c):
                def per_chunk(j, c2):
                    cp = zero_copy(base_ref[e] + j, 1)
                    cp.wait() if wait else cp.start()
                    return c2
                return lax.fori_loop(total_ref[e], pcnt_ref[e], per_chunk, c)
            lax.fori_loop(0, N_EXPERTS, per_expert, 0)

    slot = i % 2
    hx = hx_ref[...]
    group_chunks = WIN_GROUP * WIN_CHUNKS
    for g in range(N_EXPERTS // WIN_GROUP):
        oh = jnp.concatenate(
            [_one_hot(pos_ref[0, e:e + 1, :], start_ref[i * N_EXPERTS + e], 0, n_tok)
             for e in range(g * WIN_GROUP, (g + 1) * WIN_GROUP)], axis=0)
        rows = jnp.dot(oh, hx, preferred_element_type=F32).astype(BF16)
        stage[slot, pl.ds(g * group_chunks, group_chunks)] = rows.reshape(group_chunks, SEG_ROWS,
                                                                         width)

    def wait_windows(step, which):
        _wait_half_windows(
            lambda k: pltpu.make_async_copy(stage.at[which, pl.ds(0, k * HALF_CHUNKS)],
                                            xs_hbm.at[pl.ds(0, k * HALF_CHUNKS)], sem.at[0]),
            units_ref[step])

    @pl.when(i > 0)
    def _():
        wait_windows(i - 1, 1 - slot)

    for e in range(N_EXPERTS):
        _start_window(nchunk_ref[i * N_EXPERTS + e],
                      lambda n: pltpu.make_async_copy(
                          stage.at[slot, pl.ds(e * WIN_CHUNKS, n)],
                          xs_hbm.at[_window_chunks(base_ref, start_ref, i, e, 0, n)], sem.at[0]))

    @pl.when(i == pl.num_programs(0) - 1)
    def _():
        wait_windows(i, slot)

    @pl.when(over_ref[i] > 0)
    def _():
        def more_windows(e, c):
            def one(window, c2):
                oh = _one_hot(pos_ref[0, pl.ds(e, 1), :], start_ref[i * N_EXPERTS + e], window,
                              n_tok)
                rows = jnp.dot(oh, hx, preferred_element_type=F32).astype(BF16)
                extra[...] = rows.reshape(WIN_CHUNKS, SEG_ROWS, width)
                cp = pltpu.make_async_copy(
                    extra, xs_hbm.at[_window_chunks(base_ref, start_ref, i, e, window)], sem.at[2])
                cp.start()
                cp.wait()
                return c2
            return lax.fori_loop(1, _extra_windows(nchunk_ref, i, e), one, c)
        lax.fori_loop(0, N_EXPERTS, more_windows, 0)


def _dispatch(base, seg_start, seg_chunks, seg_units, seg_over, total, pcnt, n_used, hx, pos,
              n_slots):
    t, width = hx.shape
    tpb = pos.shape[2] // TOK_TILE
    return pl.pallas_call(
        _dispatch_kernel,
        grid_spec=pltpu.PrefetchScalarGridSpec(
            num_scalar_prefetch=8, grid=(t // TOK_TILE,),
            in_specs=[pl.BlockSpec((TOK_TILE, width), lambda i, *_: (i, 0)),
                      pl.BlockSpec((1, N_EXPERTS, TOK_TILE), lambda i, *_: (i // tpb, 0, i % tpb))],
            out_specs=pl.BlockSpec(memory_space=pl.ANY),
            scratch_shapes=[pltpu.VMEM((2, N_EXPERTS * WIN_CHUNKS, SEG_ROWS, width), BF16),
                            pltpu.VMEM((WIN_CHUNKS, SEG_ROWS, width), BF16),
                            pltpu.VMEM((MOE_TM // SEG_ROWS, SEG_ROWS, width), BF16),
                            pltpu.SemaphoreType.DMA((3,))]),
        out_shape=jax.ShapeDtypeStruct((n_slots // SEG_ROWS, SEG_ROWS, width), BF16),
        compiler_params=pltpu.CompilerParams(dimension_semantics=("arbitrary",),
                                             vmem_limit_bytes=VMEM_LIMIT),
        name="dispatch",
    )(base, seg_start, seg_chunks, seg_units, seg_over, total, pcnt, n_used, hx, pos)


def _experts_kernel(te_ref, nu_ref, xs_ref, wg_ref, wu_ref, wd_ref, ys_ref, wg_b, wu_b, wd_b):
    i = pl.program_id(0)

    @pl.when(i < nu_ref[0])
    def _():
        @pl.when((i == 0) | (te_ref[i] != te_ref[jnp.maximum(i - 1, 0)]))
        def _():
            wg_b[...] = wg_ref[0].astype(BF16)
            wu_b[...] = wu_ref[0].astype(BF16)
            wd_b[...] = wd_ref[0].astype(BF16)

        half = MOE_TM // 2
        half_chunks = half // SEG_ROWS
        halves = [pl.ds(k * half_chunks, half_chunks) for k in range(2)]
        hidden = []
        for chunks in halves:
            xg = xs_ref[chunks].reshape(half, xs_ref.shape[2])
            x = xg[:, :D_MODEL]
            hidden.append((xg[:, D_MODEL:].astype(F32),
                           jnp.dot(x, wg_b[...], preferred_element_type=F32),
                           jnp.dot(x, wu_b[...], preferred_element_type=F32)))
        for chunks, (g, hg, hu) in zip(halves, hidden):
            lane = lax.broadcasted_iota(jnp.int32, g.shape, 1)
            mine = (lane == te_ref[i]) | (lane == te_ref[i] + N_EXPERTS)
            gate = jnp.sum(jnp.where(mine, g, 0.0), axis=1, keepdims=True)
            a = (_silu(hg) * hu * gate).astype(BF16)
            y = jnp.dot(a, wd_b[...], preferred_element_type=F32).astype(ys_ref.dtype)
            ys_ref[chunks] = y.reshape(half_chunks, SEG_ROWS, ys_ref.shape[2])

    @pl.when(i >= nu_ref[0])
    def _():
        ys_ref[...] = jnp.zeros_like(ys_ref)


def _experts(tile_expert, n_used, xs, w_e_gate, w_e_up, w_e_down):
    n_chunks, _, width = xs.shape
    _, d, f = w_e_gate.shape
    tile_chunks = MOE_TM // SEG_ROWS
    slot_tile = lambda i, te, nu: (jnp.minimum(i, nu[0] - 1), 0, 0)
    expert = lambda i, te, nu: (te[i], 0, 0)
    return pl.pallas_call(
        _experts_kernel,
        grid_spec=pltpu.PrefetchScalarGridSpec(
            num_scalar_prefetch=2, grid=(n_chunks // tile_chunks,),
            in_specs=[pl.BlockSpec((tile_chunks, SEG_ROWS, width), slot_tile),
                      pl.BlockSpec((1, d, f), expert), pl.BlockSpec((1, d, f), expert),
                      pl.BlockSpec((1, f, d), expert)],
            out_specs=pl.BlockSpec((tile_chunks, SEG_ROWS, d), lambda i, te, nu: (i, 0, 0)),
            scratch_shapes=[pltpu.VMEM((d, f), BF16), pltpu.VMEM((d, f), BF16),
                            pltpu.VMEM((f, d), BF16)]),
        out_shape=jax.ShapeDtypeStruct((n_chunks, SEG_ROWS, d), BF16),
        compiler_params=pltpu.CompilerParams(dimension_semantics=("arbitrary",),
                                             vmem_limit_bytes=VMEM_LIMIT),
        name="experts",
    )(tile_expert, n_used, xs, w_e_gate, w_e_up, w_e_down)


def _dot_tn(a, b):
    return lax.dot_general(a, b, (((0,), (0,)), ((), ())), preferred_element_type=F32)


def _combine_kernel(base_ref, start_ref, nchunk_ref, units_ref, over_ref, hx_ref, pos_ref, x1_ref,
                    g2_ref, wsg_ref, wsu_ref, wsd_ref, ln_g_ref, ln_b_ref, ys_hbm, o_ref, win, extra,
                    acc_ref, sem):
    i = pl.program_id(0)
    n = pl.num_programs(0)
    n_tok = hx_ref.shape[0]

    def fetch(step, slot):
        for e in range(N_EXPERTS):
            _start_window(nchunk_ref[step * N_EXPERTS + e],
                          lambda k: pltpu.make_async_copy(
                              ys_hbm.at[_window_chunks(base_ref, start_ref, step, e, 0, k)],
                              win.at[slot, pl.ds(e * WIN_CHUNKS, k)], sem.at[slot]))

    @pl.when(i == 0)
    def _():
        win[...] = jnp.zeros_like(win)
        fetch(0, 0)

    @pl.when(i + 1 < n)
    def _():
        fetch(i + 1, (i + 1) % 2)

    slot = i % 2
    h = hx_ref[:, :D_MODEL]
    a = _silu(jnp.dot(h, wsg_ref[...], preferred_element_type=F32)) * jnp.dot(
        h, wsu_ref[...], preferred_element_type=F32)
    moe = jnp.dot(a.astype(BF16), wsd_ref[...], preferred_element_type=F32)
    _wait_half_windows(
        lambda k: pltpu.make_async_copy(ys_hbm.at[pl.ds(0, k * HALF_CHUNKS)],
                                        win.at[slot, pl.ds(0, k * HALF_CHUNKS)], sem.at[slot]),
        units_ref[i])
    d = win.shape[3]
    group_chunks = WIN_GROUP * WIN_CHUNKS
    for g in range(N_EXPERTS // WIN_GROUP):
        oh = jnp.concatenate(
            [_one_hot(pos_ref[0, e:e + 1, :], start_ref[i * N_EXPERTS + e], 0, n_tok)
             for e in range(g * WIN_GROUP, (g + 1) * WIN_GROUP)], axis=0)
        rows = win[slot, pl.ds(g * group_chunks, group_chunks)].reshape(WIN_GROUP * SEG_WIN, d)
        moe = moe + _dot_tn(oh, rows)
    acc_ref[...] = moe

    @pl.when(over_ref[i] > 0)
    def _():
        def more_windows(e, c):
            def one(window, c2):
                cp = pltpu.make_async_copy(
                    ys_hbm.at[_window_chunks(base_ref, start_ref, i, e, window)], extra, sem.at[2])
                cp.start()
                cp.wait()
                oh = _one_hot(pos_ref[0, pl.ds(e, 1), :], start_ref[i * N_EXPERTS + e], window,
                              n_tok)
                acc_ref[...] += _dot_tn(oh, extra[...].reshape(SEG_WIN, d))
                return c2
            return lax.fori_loop(1, _extra_windows(nchunk_ref, i, e), one, c)
        lax.fori_loop(0, N_EXPERTS, more_windows, 0)

    z = ALPHA * x1_ref[...] + g2_ref[0] * acc_ref[...]
    o_ref[...] = _layer_norm(z, ln_g_ref[...], ln_b_ref[...])


def _combine(base, seg_start, seg_chunks, seg_units, seg_over, hx, pos, x1, g2, w_s_gate, w_s_up,
             w_s_down, ln_g, ln_b, ys):
    t, d = x1.shape
    tpb = pos.shape[2] // TOK_TILE
    row = lambda i, *_: (i, 0)
    c2 = lambda i, *_: (0, 0)
    return pl.pallas_call(
        _combine_kernel,
        grid_spec=pltpu.PrefetchScalarGridSpec(
            num_scalar_prefetch=5, grid=(t // TOK_TILE,),
            in_specs=[pl.BlockSpec((TOK_TILE, hx.shape[1]), row),
                      pl.BlockSpec((1, N_EXPERTS, TOK_TILE), lambda i, *_: (i // tpb, 0, i % tpb)),
                      pl.BlockSpec((TOK_TILE, d), row),
                      pl.BlockSpec((1, 1, d), lambda i, *_: (i // tpb, 0, 0)),
                      pl.BlockSpec(w_s_gate.shape, c2), pl.BlockSpec(w_s_up.shape, c2),
                      pl.BlockSpec(w_s_down.shape, c2),
                      pl.BlockSpec(ln_g.shape, c2), pl.BlockSpec(ln_b.shape, c2),
                      pl.BlockSpec(memory_space=pl.ANY)],
            out_specs=pl.BlockSpec((TOK_TILE, d), row),
            scratch_shapes=[pltpu.VMEM((2, N_EXPERTS * WIN_CHUNKS, SEG_ROWS, d), BF16),
                            pltpu.VMEM((WIN_CHUNKS, SEG_ROWS, d), BF16),
                            pltpu.VMEM((TOK_TILE, d), F32),
                            pltpu.SemaphoreType.DMA((3,))]),
        out_shape=jax.ShapeDtypeStruct((t, d), F32),
        compiler_params=pltpu.CompilerParams(dimension_semantics=("arbitrary",),
                                             vmem_limit_bytes=VMEM_LIMIT),
        name="combine",
    )(base, seg_start, seg_chunks, seg_units, seg_over, hx, pos, x1, g2, w_s_gate, w_s_up, w_s_down,
      ln_g, ln_b, ys)


def _moe(hx, pos, seg, counts, x1, g2, w_e_gate, w_e_up, w_e_down, w_s_gate, w_s_up, w_s_down,
         ln_g, ln_b):
    b, s, d = x1.shape
    t = b * s
    n_seg = (t // TOK_TILE) * N_EXPERTS
    max_rows = t * TOP_K + n_seg * (SEG_ROWS - 1) + N_EXPERTS * (SEG_WIN + MOE_TM)
    n_tiles = -(-max_rows // MOE_TM)
    total = jnp.round(counts[:, 0]).astype(jnp.int32)
    pcnt = (total + SEG_WIN + MOE_TM - 1) // MOE_TM * MOE_TM
    ends = jnp.cumsum(pcnt)
    base = ends - pcnt
    n_used = (ends[-1] // MOE_TM).reshape(1)
    tile_ids = jnp.arange(n_tiles, dtype=jnp.int32)
    tile_expert = jnp.sum((ends[None, :] <= tile_ids[:, None] * MOE_TM).astype(jnp.int32), axis=1)
    tile_expert = jnp.minimum(tile_expert, N_EXPERTS - 1)
    tile_expert = jnp.where(tile_ids < n_used, tile_expert, tile_expert[n_used[0] - 1])
    base, total, pcnt = base // SEG_ROWS, total // SEG_ROWS, pcnt // SEG_ROWS
    seg_start = seg[:, 0, :, 0].reshape(n_seg) // SEG_ROWS
    seg_chunks = seg[:, 1, :, 0]
    seg_over = (jnp.max(seg_chunks, axis=1) * SEG_ROWS > SEG_WIN).astype(jnp.int32)
    seg_units = jnp.sum(jnp.where(seg_chunks <= HALF_CHUNKS, 1, 2), axis=1).astype(jnp.int32)
    seg_chunks = seg_chunks.reshape(n_seg)

    hxf = hx.reshape(t, hx.shape[2])
    xs = _dispatch(base, seg_start, seg_chunks, seg_units, seg_over, total, pcnt, n_used, hxf, pos,
                   n_tiles * MOE_TM)
    ys = _experts(tile_expert, n_used, xs, w_e_gate, w_e_up, w_e_down)
    out = _combine(base, seg_start, seg_chunks, seg_units, seg_over, hxf, pos, x1.reshape(t, d), g2,
                   w_s_gate, w_s_up, w_s_down, ln_g, ln_b, ys)
    return out.reshape(b, s, d)


def _rope_tables(seq):
    t = jnp.arange(seq)
    pos = jnp.stack([t // GRID_W, t % GRID_W], axis=-1).astype(F32)
    inv_freq = ROPE_THETA ** (-jnp.arange(ROPE_FREQS, dtype=F32) / ROPE_FREQS)
    ang = pos[:, :, None] * inv_freq
    cos, sin = jnp.cos(ang), jnp.sin(ang)
    zero = jnp.zeros_like(sin)
    cos_r = jnp.stack([cos, cos], axis=2).reshape(seq, QK_ROPE)
    sin_lo = jnp.stack([-sin, zero], axis=2).reshape(seq, QK_ROPE)
    sin_hi = jnp.stack([zero, sin], axis=2).reshape(seq, QK_ROPE)
    pads = ((0, 0), (QK_NOPE, HEAD_PAD - QK_NOPE - QK_ROPE))
    lane_tabs = (jnp.pad(cos_r, pads, constant_values=1.0), jnp.pad(sin_lo, pads),
                 jnp.pad(sin_hi, pads))
    row_tabs = (cos.reshape(seq, 2 * ROPE_FREQS).T, sin.reshape(seq, 2 * ROPE_FREQS).T)
    return lane_tabs, row_tabs


def _pad_heads(w, width, padded):
    k = w.shape[0]
    w = jnp.pad(w.reshape(k, N_HEADS, width), ((0, 0), (0, 0), (0, padded - width)))
    return w.reshape(k, N_HEADS * padded)


def kernel(x, c, ctx, c_ctx, w_ada, b_ada, w_in, q_norm_g, w_uq, kv_norm_g, w_ukv, w_pool, pool_scale, w_out, ln1_g, ln1_b, w_router, router_bias, w_e_gate, w_e_up, w_e_down, w_s_gate, w_s_up, w_s_down, ln2_g, ln2_b):
    assert w_ada.shape[0] == 1, "single-layer block"
    b, s, d = x.shape

    cvec = jnp.concatenate([c, c_ctx[None], jnp.zeros((SUBLANES - b - 1, d), F32)], axis=0)
    mod = _ada(cvec, w_ada[0], b_ada)
    sh1, sc1, g1, sh2, sc2, g2 = [mod[:b, k * d:(k + 1) * d][:, None, :] for k in range(6)]
    sh1c, sc1c = [jnp.broadcast_to(mod[b, k * d:(k + 1) * d], (b, 1, d)) for k in range(2)]

    wi = w_in[0]
    kr_cols = jnp.pad(wi[:, Q_LORA + KV_LORA:Q_LORA + KV_LORA + QK_ROPE],
                      ((0, 0), (QK_NOPE, HEAD_PAD - QK_NOPE - QK_ROPE)))
    w_in_r = jnp.concatenate([wi[:, :Q_LORA + KV_LORA], wi[:, Q_LORA + KV_LORA + QK_ROPE:], kr_cols],
                             axis=1).astype(BF16)
    w_uq_t = _pad_heads(w_uq[0], QK_NOPE + QK_ROPE, HEAD_PAD).T.astype(BF16)
    wkv = w_ukv[0].reshape(KV_LORA, N_HEADS, QK_NOPE + V_HEAD)
    w_uk_p = _pad_heads(wkv[:, :, :QK_NOPE].reshape(KV_LORA, -1), QK_NOPE, HEAD_PAD).astype(BF16)
    w_uv_t = _pad_heads(wkv[:, :, QK_NOPE:].reshape(KV_LORA, -1), V_HEAD, V_ROWS).T.astype(BF16)
    tables = _rope_tables(s)

    q_t, k, v_t, u = _proj(x, sc1, sh1, tables, w_in_r, q_norm_g, w_uq_t, kv_norm_g, w_uk_p,
                           w_uv_t, PROJ_TILE)
    kc, vc_t = _proj(ctx, sc1c, sh1c, None, w_in_r, None, None, kv_norm_g, w_uk_p, w_uv_t,
                     ctx.shape[1])
    attn = _attention(q_t, kc, vc_t, k, v_t)

    x1, hx, pos, seg, counts = _mix(attn, u, x, g1, sc2, sh2, w_pool[0].astype(BF16),
                                    pool_scale, w_out[0].astype(BF16), ln1_g, ln1_b,
                                    w_router[0].T, router_bias[0][:, None])
    return _moe(hx, pos, seg, counts, x1, g2, w_e_gate[0], w_e_up[0], w_e_down[0],
                w_s_gate[0].astype(BF16), w_s_up[0].astype(BF16), w_s_down[0].astype(BF16),
                ln2_g, ln2_b)
```

```python
import functools
import math

import jax
import jax.numpy as jnp
from jax import lax
from jax.experimental import pallas as pl
from jax.experimental.pallas import tpu as pltpu

F32 = jnp.float32
BF16 = jnp.bfloat16

D_MODEL = 1024
GRID_W = 64
N_HEADS = 8
Q_LORA = 512
KV_LORA = 256
QK_NOPE = 64
QK_ROPE = 32
V_HEAD = 64
ROPE_FREQS = QK_ROPE // 4
ROPE_THETA = 10000.0
ATTN_SCALE = 1.0 / math.sqrt(QK_NOPE + QK_ROPE)
LOG2_E = math.log2(math.e)
POOL_GROUPS = 4
POOL_WINDOWS = (2, 4, 8, 16)
POOL_WIDTH = 512
POOL_GC = POOL_WIDTH // POOL_GROUPS
POOL_HALO = 8
N_EXPERTS = 64
N_EXPERT_GROUPS = 8
GROUP_SIZE = N_EXPERTS // N_EXPERT_GROUPS
TOPK_GROUPS = 4
TOP_K = 8
D_EXPERT = 256
ROUTED_SCALE = 2.5
LN_EPS = 1e-5
RMS_EPS = 1e-6
ALPHA = 2.0 ** 0.25

LANES = 128
SUBLANES = 8
HEAD_PAD = LANES
V_ROWS = 80
ONES_ROW = V_HEAD
IN_PAD = Q_LORA + KV_LORA + POOL_WIDTH + LANES

PROJ_TILE = 512
ATTN_TQ = 512
ATTN_TK = PROJ_TILE
ATTN_SUB = 256
ATTN_AHEAD = 2
ATTN_UNROLL = 16
TOK_TILE = 256
MIX_TILE = 2 * TOK_TILE
MOE_TM = 1024
SEG_ROWS = 16
SEG_WIN = 64
WIN_CHUNKS = SEG_WIN // SEG_ROWS
HALF_CHUNKS = WIN_CHUNKS // 2
WIN_GROUP = 16
HX_WIDTH = D_MODEL + 2 * N_EXPERTS
VMEM_LIMIT = 48 * 1024 * 1024
NEG_BIG = -1e30


def _silu(v):
    return v * jax.nn.sigmoid(v)


def _layer_norm(z, g, b):
    mu = jnp.mean(z, axis=-1, keepdims=True)
    zc = z - mu
    var = jnp.mean(zc * zc, axis=-1, keepdims=True)
    return zc * lax.rsqrt(var + LN_EPS) * g + b


def _rms_norm(v, g):
    return v * lax.rsqrt(jnp.mean(v * v, axis=-1, keepdims=True) + RMS_EPS) * g


def _dot_nt(a, b):
    return lax.dot_general(a, b, (((1,), (1,)), ((), ())), preferred_element_type=F32)


def _ada_kernel(c_ref, w_ref, b_ref, o_ref):
    cv = _silu(c_ref[...])
    o_ref[...] = jnp.dot(cv, w_ref[...], preferred_element_type=F32,
                         precision=lax.Precision.HIGHEST) + b_ref[...]


def _ada(cvec, w_ada, b_ada):
    rows, d = cvec.shape
    n = w_ada.shape[1]
    tn = 1024
    return pl.pallas_call(
        _ada_kernel,
        grid=(n // tn,),
        in_specs=[pl.BlockSpec((rows, d), lambda j: (0, 0)),
                  pl.BlockSpec((d, tn), lambda j: (0, j)),
                  pl.BlockSpec((1, tn), lambda j: (0, j))],
        out_specs=pl.BlockSpec((rows, tn), lambda j: (0, j)),
        out_shape=jax.ShapeDtypeStruct((rows, n), F32),
        compiler_params=pltpu.CompilerParams(dimension_semantics=("arbitrary",),
                                             vmem_limit_bytes=VMEM_LIMIT),
        name="ada",
    )(cvec, w_ada, b_ada)


def _rope_lanes(v, cos, sin_lo, sin_hi):
    return v * cos + pltpu.roll(v, LANES - 8, axis=1) * sin_lo + pltpu.roll(v, 8, axis=1) * sin_hi


def _proj_kernel(*refs, with_q):
    if with_q:
        (x_ref, sc_ref, sh_ref, cos_ref, slo_ref, shi_ref, cos_t_ref, sin_t_ref, win_ref, qg_ref,
         wuq_ref, kvg_ref, wuk_ref, wuv_ref, q_ref, k_ref, v_ref, u_ref) = refs
    else:
        (x_ref, sc_ref, sh_ref, win_ref, kvg_ref, wuk_ref, wuv_ref, k_ref, v_ref) = refs
    h = (x_ref[0] * (1.0 + sc_ref[0]) + sh_ref[0]).astype(BF16)
    p = jnp.dot(h, win_ref[...], preferred_element_type=F32)
    tile = p.shape[0]
    kr = p[:, IN_PAD - LANES:]
    kvn = _rms_norm(p[:, Q_LORA:Q_LORA + KV_LORA], kvg_ref[...]).astype(BF16)
    kfull = jnp.dot(kvn, wuk_ref[...], preferred_element_type=F32)
    v_t = _dot_nt(wuv_ref[...], kvn)
    row = lax.broadcasted_iota(jnp.int32, (N_HEADS * V_ROWS, 1), 0)
    v_t = v_t + (row % V_ROWS == ONES_ROW).astype(F32)
    if with_q:
        kr = _rope_lanes(kr, cos_ref[...], slo_ref[...], shi_ref[...])
        u_ref[0] = p[:, Q_LORA + KV_LORA:Q_LORA + KV_LORA + POOL_WIDTH]
        qn = _rms_norm(p[:, :Q_LORA], qg_ref[...]).astype(BF16)
        q_t = _dot_nt(wuq_ref[...], qn) * (ATTN_SCALE * LOG2_E)
        cos_t, sin_t = cos_t_ref[...], sin_t_ref[...]
    for hd in range(N_HEADS):
        k_ref[0, hd] = (kfull[:, hd * HEAD_PAD:(hd + 1) * HEAD_PAD] + kr).astype(BF16)
        v_ref[0, hd, 0] = v_t[hd * V_ROWS:(hd + 1) * V_ROWS].astype(BF16)
        if with_q:
            base = hd * HEAD_PAD
            q_ref[0, hd, 0:QK_NOPE, :] = q_t[base:base + QK_NOPE].astype(BF16)
            rope = []
            for ax in range(2):
                lo = q_t[base + QK_NOPE + 16 * ax:base + QK_NOPE + 16 * ax + 8]
                hi = q_t[base + QK_NOPE + 16 * ax + 8:base + QK_NOPE + 16 * ax + 16]
                cs, sn = cos_t[8 * ax:8 * ax + 8], sin_t[8 * ax:8 * ax + 8]
                rope += [lo * cs - hi * sn, hi * cs + lo * sn]
            rope.append(jnp.zeros((HEAD_PAD - QK_NOPE - QK_ROPE, tile), F32))
            q_ref[0, hd, QK_NOPE:, :] = jnp.concatenate(rope, axis=0).astype(BF16)


def _proj(x, sc, sh, tables, w_in_r, q_g, w_uq_t, kv_g, w_uk_p, w_uv_t, tile):
    b, s, d = x.shape
    with_q = tables is not None
    grid = (b, s // tile)
    row = lambda bi, i: (bi, i, 0)
    vec = lambda bi, i: (bi, 0, 0)
    const2 = lambda bi, i: (0, 0)
    k_out = pl.BlockSpec((1, N_HEADS, tile, HEAD_PAD), lambda bi, i: (bi, 0, i, 0))
    k_shape = jax.ShapeDtypeStruct((b, N_HEADS, s, HEAD_PAD), BF16)
    v_out = pl.BlockSpec((1, N_HEADS, 1, V_ROWS, tile), lambda bi, i: (bi, 0, i, 0, 0))
    v_shape = jax.ShapeDtypeStruct((b, N_HEADS, s // tile, V_ROWS, tile), BF16)
    in_specs = [pl.BlockSpec((1, tile, d), row),
                pl.BlockSpec((1, 1, d), vec), pl.BlockSpec((1, 1, d), vec)]
    args = [x, sc, sh]
    if with_q:
        lane_tabs, row_tabs = tables
        in_specs += [pl.BlockSpec((tile, LANES), lambda bi, i: (i, 0))] * 3
        in_specs += [pl.BlockSpec((2 * ROPE_FREQS, tile), lambda bi, i: (0, i))] * 2
        args += list(lane_tabs) + list(row_tabs)
    in_specs.append(pl.BlockSpec(w_in_r.shape, const2)); args.append(w_in_r)
    if with_q:
        in_specs += [pl.BlockSpec(q_g.shape, const2), pl.BlockSpec(w_uq_t.shape, const2)]
        args += [q_g, w_uq_t]
    in_specs += [pl.BlockSpec(kv_g.shape, const2), pl.BlockSpec(w_uk_p.shape, const2),
                 pl.BlockSpec(w_uv_t.shape, const2)]
    args += [kv_g, w_uk_p, w_uv_t]
    if with_q:
        q_out = pl.BlockSpec((1, N_HEADS, HEAD_PAD, tile), lambda bi, i: (bi, 0, 0, i))
        q_shape = jax.ShapeDtypeStruct((b, N_HEADS, HEAD_PAD, s), BF16)
        out_specs = [q_out, k_out, v_out, pl.BlockSpec((1, tile, POOL_WIDTH), row)]
        out_shape = [q_shape, k_shape, v_shape, jax.ShapeDtypeStruct((b, s, POOL_WIDTH), F32)]
    else:
        out_specs = [k_out, v_out]
        out_shape = [k_shape, v_shape]
    return pl.pallas_call(
        functools.partial(_proj_kernel, with_q=with_q),
        grid=grid, in_specs=in_specs, out_specs=out_specs, out_shape=out_shape,
        compiler_params=pltpu.CompilerParams(dimension_semantics=("arbitrary", "arbitrary"),
                                             vmem_limit_bytes=VMEM_LIMIT),
        name="proj" if with_q else "proj_ctx",
    )(*args)


def _attn_kernel(q_ref, kc_ref, vc_ref, k_ref, v_ref, o_ref, *, n_kblk, tk):
    tq = q_ref.shape[3]
    qs = [q_ref[0, hh] for hh in range(2)]

    def scores(hh, kb):
        return jnp.dot(kb, qs[hh], preferred_element_type=F32)

    def update(s_t, vb_t, m, acc):
        m_new = jnp.maximum(m, jnp.max(s_t, axis=0, keepdims=True))
        p_t = jnp.exp2(s_t - m_new).astype(BF16)
        acc = jnp.exp2(m - m_new) * acc + jnp.dot(vb_t, p_t, preferred_element_type=F32)
        return m_new, acc

    def run_items(items, state):
        pending = [scores(hh, kb()) for hh, kb, _ in items[:ATTN_AHEAD]]
        for j, (hh, _, vb) in enumerate(items):
            if j + ATTN_AHEAD < len(items):
                nh, nkb, _ = items[j + ATTN_AHEAD]
                pending.append(scores(nh, nkb()))
            state[hh] = update(pending.pop(0), vb(), *state[hh])
        return state

    def block_items(blk, off):
        out = []
        for sub in range(tk // ATTN_SUB):
            for hh in range(2):
                lo = sub * ATTN_SUB
                out.append((hh,
                            lambda hh=hh, lo=lo: k_ref[0, hh, pl.ds(off + lo, ATTN_SUB), :],
                            lambda hh=hh, lo=lo: v_ref[0, hh, blk, :, lo:lo + ATTN_SUB]))
        return out

    ctx_items = [(hh, lambda hh=hh: kc_ref[0, hh], lambda hh=hh: vc_ref[0, hh, 0])
                 for hh in range(2)]
    state = [(jnp.full((1, tq), NEG_BIG, F32), jnp.zeros((V_ROWS, tq), F32)) for _ in range(2)]
    n_iter = n_kblk // ATTN_UNROLL
    if n_iter == 1:
        items = ctx_items
        for blk in range(n_kblk):
            items = items + block_items(blk, blk * tk)
        state = run_items(items, state)
    else:
        state = run_items(ctx_items, state)

        def body(i, carry):
            items = []
            for r in range(ATTN_UNROLL):
                blk = i * ATTN_UNROLL + r
                items += block_items(blk, pl.multiple_of(blk * tk, tk))
            st = run_items(items, [(carry[0], carry[1]), (carry[2], carry[3])])
            return st[0] + st[1]

        carry = lax.fori_loop(0, n_iter, body, state[0] + state[1])
        state = [(carry[0], carry[1]), (carry[2], carry[3])]
    carry = state[0] + state[1]
    outs = [carry[2 * hh + 1][:V_HEAD] / carry[2 * hh + 1][ONES_ROW:ONES_ROW + 1] for hh in range(2)]
    o_ref[0] = jnp.concatenate(outs, axis=0).T.astype(o_ref.dtype)


def _attention(q_t, kc, vc_t, k, v_t):
    b, nh, dp, s = q_t.shape
    c = kc.shape[2]
    tq, tk = ATTN_TQ, ATTN_TK
    n_kblk = s // tk
    kern = functools.partial(_attn_kernel, n_kblk=n_kblk, tk=tk)
    return pl.pallas_call(
        kern,
        grid=(b, nh // 2, s // tq),
        in_specs=[pl.BlockSpec((1, 2, dp, tq), lambda bi, hp, qi: (bi, hp, 0, qi)),
                  pl.BlockSpec((1, 2, c, dp), lambda bi, hp, qi: (bi, hp, 0, 0)),
                  pl.BlockSpec((1, 2, 1, V_ROWS, c), lambda bi, hp, qi: (bi, hp, 0, 0, 0)),
                  pl.BlockSpec((1, 2, s, dp), lambda bi, hp, qi: (bi, hp, 0, 0)),
                  pl.BlockSpec((1, 2, n_kblk, V_ROWS, tk), lambda bi, hp, qi: (bi, hp, 0, 0, 0))],
        out_specs=pl.BlockSpec((1, tq, 2 * V_HEAD), lambda bi, hp, qi: (bi, qi, hp)),
        out_shape=jax.ShapeDtypeStruct((b, s, nh * V_HEAD), BF16),
        compiler_params=pltpu.CompilerParams(
            dimension_semantics=("arbitrary", "arbitrary", "arbitrary"),
            vmem_limit_bytes=VMEM_LIMIT),
        name="attn",
    )(q_t, kc, vc_t, k, v_t)


def _route(logits_t, bias_t):
    e, t = logits_t.shape
    scores = jax.nn.sigmoid(logits_t)
    biased = scores + bias_t
    neg_inf = F32(-jnp.inf)
    gscore = []
    for g in range(N_EXPERT_GROUPS):
        v = biased[g * GROUP_SIZE:(g + 1) * GROUP_SIZE]
        m1 = jnp.max(v, axis=0, keepdims=True)
        at_max = v == m1
        n_max = jnp.sum(at_max.astype(F32), axis=0, keepdims=True)
        m2 = jnp.max(jnp.where(at_max, neg_inf, v), axis=0, keepdims=True)
        gscore.append(m1 + jnp.where(n_max >= 2.0, m1, m2))
    masked = []
    for g in range(N_EXPERT_GROUPS):
        rank = jnp.zeros((1, t), F32)
        for o in range(N_EXPERT_GROUPS):
            if o == g:
                continue
            beats = (gscore[o] >= gscore[g]) if o < g else (gscore[o] > gscore[g])
            rank = rank + beats.astype(F32)
        keep = rank < float(TOPK_GROUPS)
        masked.append(jnp.where(keep, biased[g * GROUP_SIZE:(g + 1) * GROUP_SIZE], neg_inf))
    work = jnp.concatenate(masked, axis=0)
    rows = lax.broadcasted_iota(jnp.int32, (e, t), 0)
    sel = jnp.zeros((e, t), F32)
    for _ in range(TOP_K):
        m = jnp.max(work, axis=0, keepdims=True)
        first = jnp.min(jnp.where(work == m, rows, e), axis=0, keepdims=True)
        pick = rows == first
        sel = jnp.where(pick, 1.0, sel)
        work = jnp.where(pick, neg_inf, work)
    w = sel * scores
    gates = w / jnp.sum(w, axis=0, keepdims=True) * ROUTED_SCALE
    return sel, gates


def _mix_kernel(attn_ref, u_ref, up_ref, un_ref, x_ref, g1_ref, sc2_ref, sh2_ref, wpool_ref,
                pscale_ref, wout_ref, ln_g_ref, ln_b_ref, wr_ref, rb_ref, tri_ref, ones_ref,
                x1_ref, hx_ref, pos_ref, seg_ref, cnt_ref, uext_ref, *, seq):
    i = pl.program_id(1)

    @pl.when((pl.program_id(0) == 0) & (i == 0))
    def _():
        cnt_ref[...] = jnp.zeros_like(cnt_ref)

    tile = u_ref.shape[1]
    u = u_ref[0]
    uext_ref[0:POOL_HALO] = jnp.where(i == 0, 0.0, up_ref[0])
    uext_ref[POOL_HALO:POOL_HALO + tile] = u
    uext_ref[POOL_HALO + tile:] = jnp.where(i == pl.num_programs(1) - 1, 0.0, un_ref[0])
    t = i * tile + lax.broadcasted_iota(jnp.int32, (tile, POOL_GC), 0)
    pooled = []
    for g, w in enumerate(POOL_WINDOWS):
        lanes = slice(g * POOL_GC, (g + 1) * POOL_GC)
        tot = uext_ref[POOL_HALO - w // 2:POOL_HALO - w // 2 + tile, lanes]
        for dlt in range(1, w):
            start = POOL_HALO - w // 2 + dlt
            tot = tot + uext_ref[start:start + tile, lanes]
        cnt = (jnp.minimum(t - w // 2 + w, seq) - jnp.maximum(t - w // 2, 0)).astype(F32)
        pg = (tot / cnt - u[:, lanes]).astype(BF16)
        po = jnp.dot(pg, wpool_ref[g], preferred_element_type=F32) * pscale_ref[:, lanes]
        pooled.append(po.astype(BF16))
    mixed = jnp.concatenate([attn_ref[0]] + pooled, axis=1)
    y = jnp.dot(mixed, wout_ref[...], preferred_element_type=F32)
    x1 = _layer_norm(ALPHA * x_ref[0] + g1_ref[0] * y, ln_g_ref[...], ln_b_ref[...])
    x1_ref[0] = x1
    h2 = x1 * (1.0 + sc2_ref[0]) + sh2_ref[0]
    h_hi = h2.astype(BF16)
    h_lo = (h2 - h_hi.astype(F32)).astype(BF16)
    w_r = wr_ref[...]
    w_hi = w_r.astype(BF16)
    w_lo = (w_r - w_hi.astype(F32)).astype(BF16)
    logits_t = _dot_nt(w_hi, h_hi) + (_dot_nt(w_hi, h_lo) + _dot_nt(w_lo, h_hi))
    sel, gates_t = _route(logits_t, rb_ref[...])
    g_hi = gates_t.astype(BF16)
    g_lo = (gates_t - g_hi.astype(F32)).astype(BF16)
    g_tok = jnp.concatenate([g_hi.astype(F32), g_lo.astype(F32)], axis=0).T
    hx_ref[0] = jnp.concatenate([h2.astype(BF16), g_tok.astype(BF16)], axis=1)
    start = cnt_ref[...]
    for k in range(tile // TOK_TILE):
        cols = slice(k * TOK_TILE, (k + 1) * TOK_TILE)
        sel_k = sel[:, cols]
        sel_b = sel_k.astype(BF16)
        pos_t = jnp.dot(sel_b, tri_ref[...], preferred_element_type=F32) + start[:, 0:1]
        pos_ref[0, :, cols] = jnp.where(sel_k > 0.0, pos_t, -1.0).astype(jnp.int32)
        n_tok = jnp.dot(sel_b, ones_ref[...], preferred_element_type=F32)
        n_chunk = jnp.floor((n_tok + (SEG_ROWS - 1)) * (1.0 / SEG_ROWS))
        seg_ref[k, 0] = start.astype(jnp.int32)
        seg_ref[k, 1] = n_chunk.astype(jnp.int32)
        start = start + n_chunk * SEG_ROWS
    cnt_ref[...] = start


def _mix(attn, u, x, g1, sc2, sh2, w_pool, pool_scale, w_out, ln_g, ln_b, w_r_t, rb_t):
    b, s, d = x.shape
    tile = MIX_TILE
    hb = tile // POOL_HALO
    row = lambda bi, i: (bi, i, 0)
    vec = lambda bi, i: (bi, 0, 0)
    c2 = lambda bi, i: (0, 0)
    lane_row = lambda bi, i: (bi, 0, i)
    tri = (lax.broadcasted_iota(jnp.int32, (TOK_TILE, TOK_TILE), 0)
           < lax.broadcasted_iota(jnp.int32, (TOK_TILE, TOK_TILE), 1)).astype(BF16)
    ones = jnp.ones((TOK_TILE, LANES), BF16)
    seg_per_step = tile // TOK_TILE
    return pl.pallas_call(
        functools.partial(_mix_kernel, seq=s),
        grid=(b, s // tile),
        in_specs=[pl.BlockSpec((1, tile, POOL_WIDTH), row),
                  pl.BlockSpec((1, tile, POOL_WIDTH), row),
                  pl.BlockSpec((1, POOL_HALO, POOL_WIDTH),
                               lambda bi, i: (bi, jnp.maximum(i * hb - 1, 0), 0)),
                  pl.BlockSpec((1, POOL_HALO, POOL_WIDTH),
                               lambda bi, i: (bi, jnp.minimum((i + 1) * hb, s // POOL_HALO - 1), 0)),
                  pl.BlockSpec((1, tile, d), row),
                  pl.BlockSpec((1, 1, d), vec), pl.BlockSpec((1, 1, d), vec),
                  pl.BlockSpec((1, 1, d), vec),
                  pl.BlockSpec(w_pool.shape, lambda bi, i: (0, 0, 0)),
                  pl.BlockSpec(pool_scale.shape, c2),
                  pl.BlockSpec(w_out.shape, c2),
                  pl.BlockSpec(ln_g.shape, c2), pl.BlockSpec(ln_b.shape, c2),
                  pl.BlockSpec(w_r_t.shape, c2), pl.BlockSpec(rb_t.shape, c2),
                  pl.BlockSpec(tri.shape, c2), pl.BlockSpec(ones.shape, c2)],
        out_specs=[pl.BlockSpec((1, tile, d), row),
                   pl.BlockSpec((1, tile, HX_WIDTH), row),
                   pl.BlockSpec((1, N_EXPERTS, tile), lane_row),
                   pl.BlockSpec((seg_per_step, 2, N_EXPERTS, LANES),
                                lambda bi, i: (bi * (s // tile) + i, 0, 0, 0)),
                   pl.BlockSpec((N_EXPERTS, LANES), c2)],
        out_shape=[jax.ShapeDtypeStruct((b, s, d), F32),
                   jax.ShapeDtypeStruct((b, s, HX_WIDTH), BF16),
                   jax.ShapeDtypeStruct((b, N_EXPERTS, s), jnp.int32),
                   jax.ShapeDtypeStruct((b * (s // TOK_TILE), 2, N_EXPERTS, LANES), jnp.int32),
                   jax.ShapeDtypeStruct((N_EXPERTS, LANES), F32)],
        scratch_shapes=[pltpu.VMEM((tile + 2 * POOL_HALO, POOL_WIDTH), F32)],
        compiler_params=pltpu.CompilerParams(dimension_semantics=("arbitrary", "arbitrary"),
                                             vmem_limit_bytes=VMEM_LIMIT),
        name="mix",
    )(attn, u, u, u, x, g1, sc2, sh2, w_pool, pool_scale, w_out, ln_g, ln_b, w_r_t, rb_t,
      tri, ones)


def _one_hot(pos_row, start, window, n_tok):
    rows = lax.broadcasted_iota(jnp.int32, (SEG_WIN, n_tok), 0)
    hit = rows == (pos_row - (start * SEG_ROWS + window * SEG_WIN))
    return jnp.where(hit, 1.0, 0.0).astype(BF16)


def _window_chunks(base_ref, start_ref, step, e, window, n_chunks=WIN_CHUNKS):
    return pl.ds(base_ref[e] + start_ref[step * N_EXPERTS + e] + window * WIN_CHUNKS, n_chunks)


def _start_window(seg_chunks, make_copy):
    @pl.when(seg_chunks <= HALF_CHUNKS)
    def _():
        make_copy(HALF_CHUNKS).start()

    @pl.when(seg_chunks > HALF_CHUNKS)
    def _():
        make_copy(WIN_CHUNKS).start()


def _wait_half_windows(make_copy, units):
    make_copy(N_EXPERTS).wait()
    rest = units - N_EXPERTS
    for bit in range(N_EXPERTS.bit_length()):
        @pl.when((rest >> bit) & 1 == 1)
        def _():
            make_copy(1 << bit).wait()


def _extra_windows(nchunk_ref, step, e):
    rows = nchunk_ref[step * N_EXPERTS + e] * SEG_ROWS
    return jnp.maximum((rows + SEG_WIN - 1) // SEG_WIN, 1)


def _dispatch_kernel(base_ref, start_ref, nchunk_ref, units_ref, over_ref, total_ref, pcnt_ref,
                     nu_ref, hx_ref, pos_ref, xs_hbm, stage, extra, zero_ref, sem):
    i = pl.program_id(0)
    tile_chunks = MOE_TM // SEG_ROWS
    n_tiles = xs_hbm.shape[0] // tile_chunks
    n_tok, width = hx_ref.shape

    def zero_copy(chunk, n_chunks):
        return pltpu.make_async_copy(zero_ref.at[pl.ds(0, n_chunks)],
                                     xs_hbm.at[pl.ds(chunk, n_chunks)], sem.at[1])

    @pl.when(i == 0)
    def _():
        zero_ref[...] = jnp.zeros_like(zero_ref)
        for wait in (False, True):
            def per_tail(j, c):
                cp = zero_copy(j * tile_chunks, tile_chunks)
                cp.wait() if wait else cp.start()
                return c
            lax.fori_loop(nu_ref[0], n_tiles, per_tail, 0)

            def per_expert(e, c):
                def per_chunk(j, c2):
                    cp = zero_copy(base_ref[e] + j, 1)
                    cp.wait() if wait else cp.start()
                    return c2
                return lax.fori_loop(total_ref[e], pcnt_ref[e], per_chunk, c)
            lax.fori_loop(0, N_EXPERTS, per_expert, 0)

    slot = i % 2
    hx = hx_ref[...]
    group_chunks = WIN_GROUP * WIN_CHUNKS
    for g in range(N_EXPERTS // WIN_GROUP):
        oh = jnp.concatenate(
            [_one_hot(pos_ref[0, e:e + 1, :], start_ref[i * N_EXPERTS + e], 0, n_tok)
             for e in range(g * WIN_GROUP, (g + 1) * WIN_GROUP)], axis=0)
        rows = jnp.dot(oh, hx, preferred_element_type=F32).astype(BF16)
        stage[slot, pl.ds(g * group_chunks, group_chunks)] = rows.reshape(group_chunks, SEG_ROWS,
                                                                         width)

    def wait_windows(step, which):
        _wait_half_windows(
            lambda k: pltpu.make_async_copy(stage.at[which, pl.ds(0, k * HALF_CHUNKS)],
                                            xs_hbm.at[pl.ds(0, k * HALF_CHUNKS)], sem.at[0]),
            units_ref[step])

    @pl.when(i > 0)
    def _():
        wait_windows(i - 1, 1 - slot)

    for e in range(N_EXPERTS):
        _start_window(nchunk_ref[i * N_EXPERTS + e],
                      lambda n: pltpu.make_async_copy(
                          stage.at[slot, pl.ds(e * WIN_CHUNKS, n)],
                          xs_hbm.at[_window_chunks(base_ref, start_ref, i, e, 0, n)], sem.at[0]))

    @pl.when(i == pl.num_programs(0) - 1)
    def _():
        wait_windows(i, slot)

    @pl.when(over_ref[i] > 0)
    def _():
        def more_windows(e, c):
            def one(window, c2):
                oh = _one_hot(pos_ref[0, pl.ds(e, 1), :], start_ref[i * N_EXPERTS + e], window,
                              n_tok)
                rows = jnp.dot(oh, hx, preferred_element_type=F32).astype(BF16)
                extra[...] = rows.reshape(WIN_CHUNKS, SEG_ROWS, width)
                cp = pltpu.make_async_copy(
                    extra, xs_hbm.at[_window_chunks(base_ref, start_ref, i, e, window)], sem.at[2])
                cp.start()
                cp.wait()
                return c2
            return lax.fori_loop(1, _extra_windows(nchunk_ref, i, e), one, c)
        lax.fori_loop(0, N_EXPERTS, more_windows, 0)


def _dispatch(base, seg_start, seg_chunks, seg_units, seg_over, total, pcnt, n_used, hx, pos,
              n_slots):
    t, width = hx.shape
    tpb = pos.shape[2] // TOK_TILE
    return pl.pallas_call(
        _dispatch_kernel,
        grid_spec=pltpu.PrefetchScalarGridSpec(
            num_scalar_prefetch=8, grid=(t // TOK_TILE,),
            in_specs=[pl.BlockSpec((TOK_TILE, width), lambda i, *_: (i, 0)),
                      pl.BlockSpec((1, N_EXPERTS, TOK_TILE), lambda i, *_: (i // tpb, 0, i % tpb))],
            out_specs=pl.BlockSpec(memory_space=pl.ANY),
            scratch_shapes=[pltpu.VMEM((2, N_EXPERTS * WIN_CHUNKS, SEG_ROWS, width), BF16),
                            pltpu.VMEM((WIN_CHUNKS, SEG_ROWS, width), BF16),
                            pltpu.VMEM((MOE_TM // SEG_ROWS, SEG_ROWS, width), BF16),
                            pltpu.SemaphoreType.DMA((3,))]),
        out_shape=jax.ShapeDtypeStruct((n_slots // SEG_ROWS, SEG_ROWS, width), BF16),
        compiler_params=pltpu.CompilerParams(dimension_semantics=("arbitrary",),
                                             vmem_limit_bytes=VMEM_LIMIT),
        name="dispatch",
    )(base, seg_start, seg_chunks, seg_units, seg_over, total, pcnt, n_used, hx, pos)


def _experts_kernel(te_ref, nu_ref, xs_ref, wg_ref, wu_ref, wd_ref, ys_ref, wg_b, wu_b, wd_b):
    i = pl.program_id(0)

    @pl.when(i < nu_ref[0])
    def _():
        @pl.when((i == 0) | (te_ref[i] != te_ref[jnp.maximum(i - 1, 0)]))
        def _():
            wg_b[...] = wg_ref[0].astype(BF16)
            wu_b[...] = wu_ref[0].astype(BF16)
            wd_b[...] = wd_ref[0].astype(BF16)

        half = MOE_TM // 2
        half_chunks = half // SEG_ROWS
        halves = [pl.ds(k * half_chunks, half_chunks) for k in range(2)]
        hidden = []
        for chunks in halves:
            xg = xs_ref[chunks].reshape(half, xs_ref.shape[2])
            x = xg[:, :D_MODEL]
            hidden.append((xg[:, D_MODEL:].astype(F32),
                           jnp.dot(x, wg_b[...], preferred_element_type=F32),
                           jnp.dot(x, wu_b[...], preferred_element_type=F32)))
        for chunks, (g, hg, hu) in zip(halves, hidden):
            lane = lax.broadcasted_iota(jnp.int32, g.shape, 1)
            mine = (lane == te_ref[i]) | (lane == te_ref[i] + N_EXPERTS)
            gate = jnp.sum(jnp.where(mine, g, 0.0), axis=1, keepdims=True)
            a = (_silu(hg) * hu * gate).astype(BF16)
            y = jnp.dot(a, wd_b[...], preferred_element_type=F32).astype(ys_ref.dtype)
            ys_ref[chunks] = y.reshape(half_chunks, SEG_ROWS, ys_ref.shape[2])

    @pl.when(i >= nu_ref[0])
    def _():
        ys_ref[...] = jnp.zeros_like(ys_ref)


def _experts(tile_expert, n_used, xs, w_e_gate, w_e_up, w_e_down):
    n_chunks, _, width = xs.shape
    _, d, f = w_e_gate.shape
    tile_chunks = MOE_TM // SEG_ROWS
    slot_tile = lambda i, te, nu: (jnp.minimum(i, nu[0] - 1), 0, 0)
    expert = lambda i, te, nu: (te[i], 0, 0)
    return pl.pallas_call(
        _experts_kernel,
        grid_spec=pltpu.PrefetchScalarGridSpec(
            num_scalar_prefetch=2, grid=(n_chunks // tile_chunks,),
            in_specs=[pl.BlockSpec((tile_chunks, SEG_ROWS, width), slot_tile),
                      pl.BlockSpec((1, d, f), expert), pl.BlockSpec((1, d, f), expert),
                      pl.BlockSpec((1, f, d), expert)],
            out_specs=pl.BlockSpec((tile_chunks, SEG_ROWS, d), lambda i, te, nu: (i, 0, 0)),
            scratch_shapes=[pltpu.VMEM((d, f), BF16), pltpu.VMEM((d, f), BF16),
                            pltpu.VMEM((f, d), BF16)]),
        out_shape=jax.ShapeDtypeStruct((n_chunks, SEG_ROWS, d), BF16),
        compiler_params=pltpu.CompilerParams(dimension_semantics=("arbitrary",),
                                             vmem_limit_bytes=VMEM_LIMIT),
        name="experts",
    )(tile_expert, n_used, xs, w_e_gate, w_e_up, w_e_down)


def _dot_tn(a, b):
    return lax.dot_general(a, b, (((0,), (0,)), ((), ())), preferred_element_type=F32)


def _combine_kernel(base_ref, start_ref, nchunk_ref, units_ref, over_ref, hx_ref, pos_ref, x1_ref,
                    g2_ref, wsg_ref, wsu_ref, wsd_ref, ln_g_ref, ln_b_ref, ys_hbm, o_ref, win, extra,
                    acc_ref, sem):
    i = pl.program_id(0)
    n = pl.num_programs(0)
    n_tok = hx_ref.shape[0]

    def fetch(step, slot):
        for e in range(N_EXPERTS):
            _start_window(nchunk_ref[step * N_EXPERTS + e],
                          lambda k: pltpu.make_async_copy(
                              ys_hbm.at[_window_chunks(base_ref, start_ref, step, e, 0, k)],
                              win.at[slot, pl.ds(e * WIN_CHUNKS, k)], sem.at[slot]))

    @pl.when(i == 0)
    def _():
        win[...] = jnp.zeros_like(win)
        fetch(0, 0)

    @pl.when(i + 1 < n)
    def _():
        fetch(i + 1, (i + 1) % 2)

    slot = i % 2
    h = hx_ref[:, :D_MODEL]
    a = _silu(jnp.dot(h, wsg_ref[...], preferred_element_type=F32)) * jnp.dot(
        h, wsu_ref[...], preferred_element_type=F32)
    moe = jnp.dot(a.astype(BF16), wsd_ref[...], preferred_element_type=F32)
    _wait_half_windows(
        lambda k: pltpu.make_async_copy(ys_hbm.at[pl.ds(0, k * HALF_CHUNKS)],
                                        win.at[slot, pl.ds(0, k * HALF_CHUNKS)], sem.at[slot]),
        units_ref[i])
    d = win.shape[3]
    group_chunks = WIN_GROUP * WIN_CHUNKS
    for g in range(N_EXPERTS // WIN_GROUP):
        oh = jnp.concatenate(
            [_one_hot(pos_ref[0, e:e + 1, :], start_ref[i * N_EXPERTS + e], 0, n_tok)
             for e in range(g * WIN_GROUP, (g + 1) * WIN_GROUP)], axis=0)
        rows = win[slot, pl.ds(g * group_chunks, group_chunks)].reshape(WIN_GROUP * SEG_WIN, d)
        moe = moe + _dot_tn(oh, rows)
    acc_ref[...] = moe

    @pl.when(over_ref[i] > 0)
    def _():
        def more_windows(e, c):
            def one(window, c2):
                cp = pltpu.make_async_copy(
                    ys_hbm.at[_window_chunks(base_ref, start_ref, i, e, window)], extra, sem.at[2])
                cp.start()
                cp.wait()
                oh = _one_hot(pos_ref[0, pl.ds(e, 1), :], start_ref[i * N_EXPERTS + e], window,
                              n_tok)
                acc_ref[...] += _dot_tn(oh, extra[...].reshape(SEG_WIN, d))
                return c2
            return lax.fori_loop(1, _extra_windows(nchunk_ref, i, e), one, c)
        lax.fori_loop(0, N_EXPERTS, more_windows, 0)

    z = ALPHA * x1_ref[...] + g2_ref[0] * acc_ref[...]
    o_ref[...] = _layer_norm(z, ln_g_ref[...], ln_b_ref[...])


def _combine(base, seg_start, seg_chunks, seg_units, seg_over, hx, pos, x1, g2, w_s_gate, w_s_up,
             w_s_down, ln_g, ln_b, ys):
    t, d = x1.shape
    tpb = pos.shape[2] // TOK_TILE
    row = lambda i, *_: (i, 0)
    c2 = lambda i, *_: (0, 0)
    return pl.pallas_call(
        _combine_kernel,
        grid_spec=pltpu.PrefetchScalarGridSpec(
            num_scalar_prefetch=5, grid=(t // TOK_TILE,),
            in_specs=[pl.BlockSpec((TOK_TILE, hx.shape[1]), row),
                      pl.BlockSpec((1, N_EXPERTS, TOK_TILE), lambda i, *_: (i // tpb, 0, i % tpb)),
                      pl.BlockSpec((TOK_TILE, d), row),
                      pl.BlockSpec((1, 1, d), lambda i, *_: (i // tpb, 0, 0)),
                      pl.BlockSpec(w_s_gate.shape, c2), pl.BlockSpec(w_s_up.shape, c2),
                      pl.BlockSpec(w_s_down.shape, c2),
                      pl.BlockSpec(ln_g.shape, c2), pl.BlockSpec(ln_b.shape, c2),
                      pl.BlockSpec(memory_space=pl.ANY)],
            out_specs=pl.BlockSpec((TOK_TILE, d), row),
            scratch_shapes=[pltpu.VMEM((2, N_EXPERTS * WIN_CHUNKS, SEG_ROWS, d), BF16),
                            pltpu.VMEM((WIN_CHUNKS, SEG_ROWS, d), BF16),
                            pltpu.VMEM((TOK_TILE, d), F32),
                            pltpu.SemaphoreType.DMA((3,))]),
        out_shape=jax.ShapeDtypeStruct((t, d), F32),
        compiler_params=pltpu.CompilerParams(dimension_semantics=("arbitrary",),
                                             vmem_limit_bytes=VMEM_LIMIT),
        name="combine",
    )(base, seg_start, seg_chunks, seg_units, seg_over, hx, pos, x1, g2, w_s_gate, w_s_up, w_s_down,
      ln_g, ln_b, ys)


def _moe(hx, pos, seg, counts, x1, g2, w_e_gate, w_e_up, w_e_down, w_s_gate, w_s_up, w_s_down,
         ln_g, ln_b):
    b, s, d = x1.shape
    t = b * s
    n_seg = (t // TOK_TILE) * N_EXPERTS
    max_rows = t * TOP_K + n_seg * (SEG_ROWS - 1) + N_EXPERTS * (SEG_WIN + MOE_TM)
    n_tiles = -(-max_rows // MOE_TM)
    total = jnp.round(counts[:, 0]).astype(jnp.int32)
    pcnt = (total + SEG_WIN + MOE_TM - 1) // MOE_TM * MOE_TM
    ends = jnp.cumsum(pcnt)
    base = ends - pcnt
    n_used = (ends[-1] // MOE_TM).reshape(1)
    tile_ids = jnp.arange(n_tiles, dtype=jnp.int32)
    tile_expert = jnp.sum((ends[None, :] <= tile_ids[:, None] * MOE_TM).astype(jnp.int32), axis=1)
    tile_expert = jnp.minimum(tile_expert, N_EXPERTS - 1)
    tile_expert = jnp.where(tile_ids < n_used, tile_expert, tile_expert[n_used[0] - 1])
    base, total, pcnt = base // SEG_ROWS, total // SEG_ROWS, pcnt // SEG_ROWS
    seg_start = seg[:, 0, :, 0].reshape(n_seg) // SEG_ROWS
    seg_chunks = seg[:, 1, :, 0]
    seg_over = (jnp.max(seg_chunks, axis=1) * SEG_ROWS > SEG_WIN).astype(jnp.int32)
    seg_units = jnp.sum(jnp.where(seg_chunks <= HALF_CHUNKS, 1, 2), axis=1).astype(jnp.int32)
    seg_chunks = seg_chunks.reshape(n_seg)

    hxf = hx.reshape(t, hx.shape[2])
    xs = _dispatch(base, seg_start, seg_chunks, seg_units, seg_over, total, pcnt, n_used, hxf, pos,
                   n_tiles * MOE_TM)
    ys = _experts(tile_expert, n_used, xs, w_e_gate, w_e_up, w_e_down)
    out = _combine(base, seg_start, seg_chunks, seg_units, seg_over, hxf, pos, x1.reshape(t, d), g2,
                   w_s_gate, w_s_up, w_s_down, ln_g, ln_b, ys)
    return out.reshape(b, s, d)


def _rope_tables(seq):
    t = jnp.arange(seq)
    pos = jnp.stack([t // GRID_W, t % GRID_W], axis=-1).astype(F32)
    inv_freq = ROPE_THETA ** (-jnp.arange(ROPE_FREQS, dtype=F32) / ROPE_FREQS)
    ang = pos[:, :, None] * inv_freq
    cos, sin = jnp.cos(ang), jnp.sin(ang)
    zero = jnp.zeros_like(sin)
    cos_r = jnp.stack([cos, cos], axis=2).reshape(seq, QK_ROPE)
    sin_lo = jnp.stack([-sin, zero], axis=2).reshape(seq, QK_ROPE)
    sin_hi = jnp.stack([zero, sin], axis=2).reshape(seq, QK_ROPE)
    pads = ((0, 0), (QK_NOPE, HEAD_PAD - QK_NOPE - QK_ROPE))
    lane_tabs = (jnp.pad(cos_r, pads, constant_values=1.0), jnp.pad(sin_lo, pads),
                 jnp.pad(sin_hi, pads))
    row_tabs = (cos.reshape(seq, 2 * ROPE_FREQS).T, sin.reshape(seq, 2 * ROPE_FREQS).T)
    return lane_tabs, row_tabs


def _pad_heads(w, width, padded):
    k = w.shape[0]
    w = jnp.pad(w.reshape(k, N_HEADS, width), ((0, 0), (0, 0), (0, padded - width)))
    return w.reshape(k, N_HEADS * padded)


def kernel(x, c, ctx, c_ctx, w_ada, b_ada, w_in, q_norm_g, w_uq, kv_norm_g, w_ukv, w_pool, pool_scale, w_out, ln1_g, ln1_b, w_router, router_bias, w_e_gate, w_e_up, w_e_down, w_s_gate, w_s_up, w_s_down, ln2_g, ln2_b):
    assert w_ada.shape[0] == 1, "single-layer block"
    b, s, d = x.shape

    cvec = jnp.concatenate([c, c_ctx[None], jnp.zeros((SUBLANES - b - 1, d), F32)], axis=0)
    mod = _ada(cvec, w_ada[0], b_ada)
    sh1, sc1, g1, sh2, sc2, g2 = [mod[:b, k * d:(k + 1) * d][:, None, :] for k in range(6)]
    sh1c, sc1c = [jnp.broadcast_to(mod[b, k * d:(k + 1) * d], (b, 1, d)) for k in range(2)]

    wi = w_in[0]
    kr_cols = jnp.pad(wi[:, Q_LORA + KV_LORA:Q_LORA + KV_LORA + QK_ROPE],
                      ((0, 0), (QK_NOPE, HEAD_PAD - QK_NOPE - QK_ROPE)))
    w_in_r = jnp.concatenate([wi[:, :Q_LORA + KV_LORA], wi[:, Q_LORA + KV_LORA + QK_ROPE:], kr_cols],
                             axis=1).astype(BF16)
    w_uq_t = _pad_heads(w_uq[0], QK_NOPE + QK_ROPE, HEAD_PAD).T.astype(BF16)
    wkv = w_ukv[0].reshape(KV_LORA, N_HEADS, QK_NOPE + V_HEAD)
    w_uk_p = _pad_heads(wkv[:, :, :QK_NOPE].reshape(KV_LORA, -1), QK_NOPE, HEAD_PAD).astype(BF16)
    w_uv_t = _pad_heads(wkv[:, :, QK_NOPE:].reshape(KV_LORA, -1), V_HEAD, V_ROWS).T.astype(BF16)
    tables = _rope_tables(s)

    q_t, k, v_t, u = _proj(x, sc1, sh1, tables, w_in_r, q_norm_g, w_uq_t, kv_norm_g, w_uk_p,
                           w_uv_t, PROJ_TILE)
    kc, vc_t = _proj(ctx, sc1c, sh1c, None, w_in_r, None, None, kv_norm_g, w_uk_p, w_uv_t,
                     ctx.shape[1])
    attn = _attention(q_t, kc, vc_t, k, v_t)

    x1, hx, pos, seg, counts = _mix(attn, u, x, g1, sc2, sh2, w_pool[0].astype(BF16),
                                    pool_scale, w_out[0].astype(BF16), ln1_g, ln1_b,
                                    w_router[0].T, router_bias[0][:, None])
    return _moe(hx, pos, seg, counts, x1, g2, w_e_gate[0], w_e_up[0], w_e_down[0],
                w_s_gate[0].astype(BF16), w_s_up[0].astype(BF16), w_s_down[0].astype(BF16),
                ln2_g, ln2_b)
```

```python
import functools
import math

import jax
import jax.numpy as jnp
from jax import lax
from jax.experimental import pallas as pl
from jax.experimental.pallas import tpu as pltpu

F32 = jnp.float32
BF16 = jnp.bfloat16

D_MODEL = 1024
GRID_W = 64
N_HEADS = 8
Q_LORA = 512
KV_LORA = 256
QK_NOPE = 64
QK_ROPE = 32
V_HEAD = 64
ROPE_FREQS = QK_ROPE // 4
ROPE_THETA = 10000.0
ATTN_SCALE = 1.0 / math.sqrt(QK_NOPE + QK_ROPE)
LOG2_E = math.log2(math.e)
POOL_GROUPS = 4
POOL_WINDOWS = (2, 4, 8, 16)
POOL_WIDTH = 512
POOL_GC = POOL_WIDTH // POOL_GROUPS
POOL_HALO = 8
N_EXPERTS = 64
N_EXPERT_GROUPS = 8
GROUP_SIZE = N_EXPERTS // N_EXPERT_GROUPS
TOPK_GROUPS = 4
TOP_K = 8
D_EXPERT = 256
ROUTED_SCALE = 2.5
LN_EPS = 1e-5
RMS_EPS = 1e-6
ALPHA = 2.0 ** 0.25

LANES = 128
SUBLANES = 8
HEAD_PAD = LANES
V_ROWS = 80
ONES_ROW = V_HEAD
IN_PAD = Q_LORA + KV_LORA + POOL_WIDTH + LANES

PROJ_TILE = 512
ATTN_TQ = 512
ATTN_TK = PROJ_TILE
ATTN_SUB = 256
ATTN_AHEAD = 2
ATTN_UNROLL = 16
TOK_TILE = 512
MIX_TILE = TOK_TILE
MOE_TM = 1024
SEG_ROWS = 16
SEG_WIN = 128
WIN_CHUNKS = SEG_WIN // SEG_ROWS
HALF_CHUNKS = WIN_CHUNKS // 2
WIN_GROUP = 4
HX_WIDTH = D_MODEL + 2 * N_EXPERTS
VMEM_LIMIT = 48 * 1024 * 1024
MOE_VMEM_LIMIT = 60 * 1024 * 1024
NEG_BIG = -1e30


def _silu(v):
    return v * jax.nn.sigmoid(v)


def _layer_norm(z, g, b):
    mu = jnp.mean(z, axis=-1, keepdims=True)
    zc = z - mu
    var = jnp.mean(zc * zc, axis=-1, keepdims=True)
    return zc * lax.rsqrt(var + LN_EPS) * g + b


def _rms_norm(v, g):
    return v * lax.rsqrt(jnp.mean(v * v, axis=-1, keepdims=True) + RMS_EPS) * g


def _dot_nt(a, b):
    return lax.dot_general(a, b, (((1,), (1,)), ((), ())), preferred_element_type=F32)


def _ada_kernel(c_ref, w_ref, b_ref, o_ref):
    cv = _silu(c_ref[...])
    o_ref[...] = jnp.dot(cv, w_ref[...], preferred_element_type=F32,
                         precision=lax.Precision.HIGHEST) + b_ref[...]


def _ada(cvec, w_ada, b_ada):
    rows, d = cvec.shape
    n = w_ada.shape[1]
    tn = 1024
    return pl.pallas_call(
        _ada_kernel,
        grid=(n // tn,),
        in_specs=[pl.BlockSpec((rows, d), lambda j: (0, 0)),
                  pl.BlockSpec((d, tn), lambda j: (0, j)),
                  pl.BlockSpec((1, tn), lambda j: (0, j))],
        out_specs=pl.BlockSpec((rows, tn), lambda j: (0, j)),
        out_shape=jax.ShapeDtypeStruct((rows, n), F32),
        compiler_params=pltpu.CompilerParams(dimension_semantics=("arbitrary",),
                                             vmem_limit_bytes=VMEM_LIMIT),
        name="ada",
    )(cvec, w_ada, b_ada)


def _rope_lanes(v, cos, sin_lo, sin_hi):
    return v * cos + pltpu.roll(v, LANES - 8, axis=1) * sin_lo + pltpu.roll(v, 8, axis=1) * sin_hi


def _proj_kernel(*refs, with_q):
    if with_q:
        (x_ref, sc_ref, sh_ref, cos_ref, slo_ref, shi_ref, cos_t_ref, sin_t_ref, win_ref, qg_ref,
         wuq_ref, kvg_ref, wuk_ref, wuv_ref, q_ref, k_ref, v_ref, u_ref) = refs
    else:
        (x_ref, sc_ref, sh_ref, win_ref, kvg_ref, wuk_ref, wuv_ref, k_ref, v_ref) = refs
    h = (x_ref[0] * (1.0 + sc_ref[0]) + sh_ref[0]).astype(BF16)
    p = jnp.dot(h, win_ref[...], preferred_element_type=F32)
    tile = p.shape[0]
    kr = p[:, IN_PAD - LANES:]
    kvn = _rms_norm(p[:, Q_LORA:Q_LORA + KV_LORA], kvg_ref[...]).astype(BF16)
    kfull = jnp.dot(kvn, wuk_ref[...], preferred_element_type=F32)
    v_t = _dot_nt(wuv_ref[...], kvn)
    row = lax.broadcasted_iota(jnp.int32, (N_HEADS * V_ROWS, 1), 0)
    v_t = v_t + (row % V_ROWS == ONES_ROW).astype(F32)
    if with_q:
        kr = _rope_lanes(kr, cos_ref[...], slo_ref[...], shi_ref[...])
        u_ref[0] = p[:, Q_LORA + KV_LORA:Q_LORA + KV_LORA + POOL_WIDTH]
        qn = _rms_norm(p[:, :Q_LORA], qg_ref[...]).astype(BF16)
        q_t = _dot_nt(wuq_ref[...], qn) * (ATTN_SCALE * LOG2_E)
        cos_t, sin_t = cos_t_ref[...], sin_t_ref[...]
    for hd in range(N_HEADS):
        k_ref[0, hd] = (kfull[:, hd * HEAD_PAD:(hd + 1) * HEAD_PAD] + kr).astype(BF16)
        v_ref[0, hd, 0] = v_t[hd * V_ROWS:(hd + 1) * V_ROWS].astype(BF16)
        if with_q:
            base = hd * HEAD_PAD
            q_ref[0, hd, 0:QK_NOPE, :] = q_t[base:base + QK_NOPE].astype(BF16)
            rope = []
            for ax in range(2):
                lo = q_t[base + QK_NOPE + 16 * ax:base + QK_NOPE + 16 * ax + 8]
                hi = q_t[base + QK_NOPE + 16 * ax + 8:base + QK_NOPE + 16 * ax + 16]
                cs, sn = cos_t[8 * ax:8 * ax + 8], sin_t[8 * ax:8 * ax + 8]
                rope += [lo * cs - hi * sn, hi * cs + lo * sn]
            rope.append(jnp.zeros((HEAD_PAD - QK_NOPE - QK_ROPE, tile), F32))
            q_ref[0, hd, QK_NOPE:, :] = jnp.concatenate(rope, axis=0).astype(BF16)


def _proj(x, sc, sh, tables, w_in_r, q_g, w_uq_t, kv_g, w_uk_p, w_uv_t, tile):
    b, s, d = x.shape
    with_q = tables is not None
    grid = (b, s // tile)
    row = lambda bi, i: (bi, i, 0)
    vec = lambda bi, i: (bi, 0, 0)
    const2 = lambda bi, i: (0, 0)
    k_out = pl.BlockSpec((1, N_HEADS, tile, HEAD_PAD), lambda bi, i: (bi, 0, i, 0))
    k_shape = jax.ShapeDtypeStruct((b, N_HEADS, s, HEAD_PAD), BF16)
    v_out = pl.BlockSpec((1, N_HEADS, 1, V_ROWS, tile), lambda bi, i: (bi, 0, i, 0, 0))
    v_shape = jax.ShapeDtypeStruct((b, N_HEADS, s // tile, V_ROWS, tile), BF16)
    in_specs = [pl.BlockSpec((1, tile, d), row),
                pl.BlockSpec((1, 1, d), vec), pl.BlockSpec((1, 1, d), vec)]
    args = [x, sc, sh]
    if with_q:
        lane_tabs, row_tabs = tables
        in_specs += [pl.BlockSpec((tile, LANES), lambda bi, i: (i, 0))] * 3
        in_specs += [pl.BlockSpec((2 * ROPE_FREQS, tile), lambda bi, i: (0, i))] * 2
        args += list(lane_tabs) + list(row_tabs)
    in_specs.append(pl.BlockSpec(w_in_r.shape, const2)); args.append(w_in_r)
    if with_q:
        in_specs += [pl.BlockSpec(q_g.shape, const2), pl.BlockSpec(w_uq_t.shape, const2)]
        args += [q_g, w_uq_t]
    in_specs += [pl.BlockSpec(kv_g.shape, const2), pl.BlockSpec(w_uk_p.shape, const2),
                 pl.BlockSpec(w_uv_t.shape, const2)]
    args += [kv_g, w_uk_p, w_uv_t]
    if with_q:
        q_out = pl.BlockSpec((1, N_HEADS, HEAD_PAD, tile), lambda bi, i: (bi, 0, 0, i))
        q_shape = jax.ShapeDtypeStruct((b, N_HEADS, HEAD_PAD, s), BF16)
        out_specs = [q_out, k_out, v_out, pl.BlockSpec((1, tile, POOL_WIDTH), row)]
        out_shape = [q_shape, k_shape, v_shape, jax.ShapeDtypeStruct((b, s, POOL_WIDTH), F32)]
    else:
        out_specs = [k_out, v_out]
        out_shape = [k_shape, v_shape]
    return pl.pallas_call(
        functools.partial(_proj_kernel, with_q=with_q),
        grid=grid, in_specs=in_specs, out_specs=out_specs, out_shape=out_shape,
        compiler_params=pltpu.CompilerParams(dimension_semantics=("arbitrary", "arbitrary"),
                                             vmem_limit_bytes=VMEM_LIMIT),
        name="proj" if with_q else "proj_ctx",
    )(*args)


def _attn_kernel(q_ref, kc_ref, vc_ref, k_ref, v_ref, o_ref, *, n_kblk, tk):
    tq = q_ref.shape[3]
    qs = [q_ref[0, hh] for hh in range(2)]

    def scores(hh, kb):
        return jnp.dot(kb, qs[hh], preferred_element_type=F32)

    def update(s_t, vb_t, m, acc):
        m_new = jnp.maximum(m, jnp.max(s_t, axis=0, keepdims=True))
        p_t = jnp.exp2(s_t - m_new).astype(BF16)
        acc = jnp.exp2(m - m_new) * acc + jnp.dot(vb_t, p_t, preferred_element_type=F32)
        return m_new, acc

    def run_items(items, state):
        pending = [scores(hh, kb()) for hh, kb, _ in items[:ATTN_AHEAD]]
        for j, (hh, _, vb) in enumerate(items):
            if j + ATTN_AHEAD < len(items):
                nh, nkb, _ = items[j + ATTN_AHEAD]
                pending.append(scores(nh, nkb()))
            state[hh] = update(pending.pop(0), vb(), *state[hh])
        return state

    def block_items(blk, off):
        out = []
        for sub in range(tk // ATTN_SUB):
            for hh in range(2):
                lo = sub * ATTN_SUB
                out.append((hh,
                            lambda hh=hh, lo=lo: k_ref[0, hh, pl.ds(off + lo, ATTN_SUB), :],
                            lambda hh=hh, lo=lo: v_ref[0, hh, blk, :, lo:lo + ATTN_SUB]))
        return out

    ctx_items = [(hh, lambda hh=hh: kc_ref[0, hh], lambda hh=hh: vc_ref[0, hh, 0])
                 for hh in range(2)]
    state = [(jnp.full((1, tq), NEG_BIG, F32), jnp.zeros((V_ROWS, tq), F32)) for _ in range(2)]
    n_iter = n_kblk // ATTN_UNROLL
    if n_iter == 1:
        items = ctx_items
        for blk in range(n_kblk):
            items = items + block_items(blk, blk * tk)
        state = run_items(items, state)
    else:
        state = run_items(ctx_items, state)

        def body(i, carry):
            items = []
            for r in range(ATTN_UNROLL):
                blk = i * ATTN_UNROLL + r
                items += block_items(blk, pl.multiple_of(blk * tk, tk))
            st = run_items(items, [(carry[0], carry[1]), (carry[2], carry[3])])
            return st[0] + st[1]

        carry = lax.fori_loop(0, n_iter, body, state[0] + state[1])
        state = [(carry[0], carry[1]), (carry[2], carry[3])]
    carry = state[0] + state[1]
    outs = [carry[2 * hh + 1][:V_HEAD] / carry[2 * hh + 1][ONES_ROW:ONES_ROW + 1] for hh in range(2)]
    o_ref[0] = jnp.concatenate(outs, axis=0).T.astype(o_ref.dtype)


def _attention(q_t, kc, vc_t, k, v_t):
    b, nh, dp, s = q_t.shape
    c = kc.shape[2]
    tq, tk = ATTN_TQ, ATTN_TK
    n_kblk = s // tk
    kern = functools.partial(_attn_kernel, n_kblk=n_kblk, tk=tk)
    return pl.pallas_call(
        kern,
        grid=(b, nh // 2, s // tq),
        in_specs=[pl.BlockSpec((1, 2, dp, tq), lambda bi, hp, qi: (bi, hp, 0, qi)),
                  pl.BlockSpec((1, 2, c, dp), lambda bi, hp, qi: (bi, hp, 0, 0)),
                  pl.BlockSpec((1, 2, 1, V_ROWS, c), lambda bi, hp, qi: (bi, hp, 0, 0, 0)),
                  pl.BlockSpec((1, 2, s, dp), lambda bi, hp, qi: (bi, hp, 0, 0)),
                  pl.BlockSpec((1, 2, n_kblk, V_ROWS, tk), lambda bi, hp, qi: (bi, hp, 0, 0, 0))],
        out_specs=pl.BlockSpec((1, tq, 2 * V_HEAD), lambda bi, hp, qi: (bi, qi, hp)),
        out_shape=jax.ShapeDtypeStruct((b, s, nh * V_HEAD), BF16),
        compiler_params=pltpu.CompilerParams(
            dimension_semantics=("arbitrary", "arbitrary", "arbitrary"),
            vmem_limit_bytes=VMEM_LIMIT),
        name="attn",
    )(q_t, kc, vc_t, k, v_t)


def _route(logits_t, bias_t):
    e, t = logits_t.shape
    scores = jax.nn.sigmoid(logits_t)
    biased = scores + bias_t
    neg_inf = F32(-jnp.inf)
    gscore = []
    for g in range(N_EXPERT_GROUPS):
        v = biased[g * GROUP_SIZE:(g + 1) * GROUP_SIZE]
        m1 = jnp.max(v, axis=0, keepdims=True)
        at_max = v == m1
        n_max = jnp.sum(at_max.astype(F32), axis=0, keepdims=True)
        m2 = jnp.max(jnp.where(at_max, neg_inf, v), axis=0, keepdims=True)
        gscore.append(m1 + jnp.where(n_max >= 2.0, m1, m2))
    masked = []
    for g in range(N_EXPERT_GROUPS):
        rank = jnp.zeros((1, t), F32)
        for o in range(N_EXPERT_GROUPS):
            if o == g:
                continue
            beats = (gscore[o] >= gscore[g]) if o < g else (gscore[o] > gscore[g])
            rank = rank + beats.astype(F32)
        keep = rank < float(TOPK_GROUPS)
        masked.append(jnp.where(keep, biased[g * GROUP_SIZE:(g + 1) * GROUP_SIZE], neg_inf))
    work = jnp.concatenate(masked, axis=0)
    rows = lax.broadcasted_iota(jnp.int32, (e, t), 0)
    sel = jnp.zeros((e, t), F32)
    for _ in range(TOP_K):
        m = jnp.max(work, axis=0, keepdims=True)
        first = jnp.min(jnp.where(work == m, rows, e), axis=0, keepdims=True)
        pick = rows == first
        sel = jnp.where(pick, 1.0, sel)
        work = jnp.where(pick, neg_inf, work)
    w = sel * scores
    gates = w / jnp.sum(w, axis=0, keepdims=True) * ROUTED_SCALE
    return sel, gates


def _mix_kernel(attn_ref, u_ref, up_ref, un_ref, x_ref, g1_ref, sc2_ref, sh2_ref, wpool_ref,
                pscale_ref, wout_ref, ln_g_ref, ln_b_ref, wr_ref, rb_ref, tri_ref, ones_ref,
                x1_ref, hx_ref, pos_ref, seg_ref, cnt_ref, uext_ref, *, seq):
    i = pl.program_id(1)

    @pl.when((pl.program_id(0) == 0) & (i == 0))
    def _():
        cnt_ref[...] = jnp.zeros_like(cnt_ref)

    tile = u_ref.shape[1]
    u = u_ref[0]
    uext_ref[0:POOL_HALO] = jnp.where(i == 0, 0.0, up_ref[0])
    uext_ref[POOL_HALO:POOL_HALO + tile] = u
    uext_ref[POOL_HALO + tile:] = jnp.where(i == pl.num_programs(1) - 1, 0.0, un_ref[0])
    t = i * tile + lax.broadcasted_iota(jnp.int32, (tile, POOL_GC), 0)
    pooled = []
    for g, w in enumerate(POOL_WINDOWS):
        lanes = slice(g * POOL_GC, (g + 1) * POOL_GC)
        tot = uext_ref[POOL_HALO - w // 2:POOL_HALO - w // 2 + tile, lanes]
        for dlt in range(1, w):
            start = POOL_HALO - w // 2 + dlt
            tot = tot + uext_ref[start:start + tile, lanes]
        cnt = (jnp.minimum(t - w // 2 + w, seq) - jnp.maximum(t - w // 2, 0)).astype(F32)
        pg = (tot / cnt - u[:, lanes]).astype(BF16)
        po = jnp.dot(pg, wpool_ref[g], preferred_element_type=F32) * pscale_ref[:, lanes]
        pooled.append(po.astype(BF16))
    mixed = jnp.concatenate([attn_ref[0]] + pooled, axis=1)
    y = jnp.dot(mixed, wout_ref[...], preferred_element_type=F32)
    x1 = _layer_norm(ALPHA * x_ref[0] + g1_ref[0] * y, ln_g_ref[...], ln_b_ref[...])
    x1_ref[0] = x1
    h2 = x1 * (1.0 + sc2_ref[0]) + sh2_ref[0]
    h_hi = h2.astype(BF16)
    h_lo = (h2 - h_hi.astype(F32)).astype(BF16)
    w_r = wr_ref[...]
    w_hi = w_r.astype(BF16)
    w_lo = (w_r - w_hi.astype(F32)).astype(BF16)
    logits_t = _dot_nt(w_hi, h_hi) + (_dot_nt(w_hi, h_lo) + _dot_nt(w_lo, h_hi))
    sel, gates_t = _route(logits_t, rb_ref[...])
    g_hi = gates_t.astype(BF16)
    g_lo = (gates_t - g_hi.astype(F32)).astype(BF16)
    g_tok = jnp.concatenate([g_hi.astype(F32), g_lo.astype(F32)], axis=0).T
    hx_ref[0] = jnp.concatenate([h2.astype(BF16), g_tok.astype(BF16)], axis=1)
    start = cnt_ref[...]
    for k in range(tile // TOK_TILE):
        cols = slice(k * TOK_TILE, (k + 1) * TOK_TILE)
        sel_k = sel[:, cols]
        sel_b = sel_k.astype(BF16)
        pos_t = jnp.dot(sel_b, tri_ref[...], preferred_element_type=F32) + start[:, 0:1]
        pos_ref[0, :, cols] = jnp.where(sel_k > 0.0, pos_t, -1.0).astype(jnp.int32)
        n_tok = jnp.dot(sel_b, ones_ref[...], preferred_element_type=F32)
        n_chunk = jnp.floor((n_tok + (SEG_ROWS - 1)) * (1.0 / SEG_ROWS))
        seg_ref[k, 0] = start.astype(jnp.int32)
        seg_ref[k, 1] = n_chunk.astype(jnp.int32)
        start = start + n_chunk * SEG_ROWS
    cnt_ref[...] = start


def _mix(attn, u, x, g1, sc2, sh2, w_pool, pool_scale, w_out, ln_g, ln_b, w_r_t, rb_t):
    b, s, d = x.shape
    tile = MIX_TILE
    hb = tile // POOL_HALO
    row = lambda bi, i: (bi, i, 0)
    vec = lambda bi, i: (bi, 0, 0)
    c2 = lambda bi, i: (0, 0)
    lane_row = lambda bi, i: (bi, 0, i)
    tri = (lax.broadcasted_iota(jnp.int32, (TOK_TILE, TOK_TILE), 0)
           < lax.broadcasted_iota(jnp.int32, (TOK_TILE, TOK_TILE), 1)).astype(BF16)
    ones = jnp.ones((TOK_TILE, LANES), BF16)
    seg_per_step = tile // TOK_TILE
    return pl.pallas_call(
        functools.partial(_mix_kernel, seq=s),
        grid=(b, s // tile),
        in_specs=[pl.BlockSpec((1, tile, POOL_WIDTH), row),
                  pl.BlockSpec((1, tile, POOL_WIDTH), row),
                  pl.BlockSpec((1, POOL_HALO, POOL_WIDTH),
                               lambda bi, i: (bi, jnp.maximum(i * hb - 1, 0), 0)),
                  pl.BlockSpec((1, POOL_HALO, POOL_WIDTH),
                               lambda bi, i: (bi, jnp.minimum((i + 1) * hb, s // POOL_HALO - 1), 0)),
                  pl.BlockSpec((1, tile, d), row),
                  pl.BlockSpec((1, 1, d), vec), pl.BlockSpec((1, 1, d), vec),
                  pl.BlockSpec((1, 1, d), vec),
                  pl.BlockSpec(w_pool.shape, lambda bi, i: (0, 0, 0)),
                  pl.BlockSpec(pool_scale.shape, c2),
                  pl.BlockSpec(w_out.shape, c2),
                  pl.BlockSpec(ln_g.shape, c2), pl.BlockSpec(ln_b.shape, c2),
                  pl.BlockSpec(w_r_t.shape, c2), pl.BlockSpec(rb_t.shape, c2),
                  pl.BlockSpec(tri.shape, c2), pl.BlockSpec(ones.shape, c2)],
        out_specs=[pl.BlockSpec((1, tile, d), row),
                   pl.BlockSpec((1, tile, HX_WIDTH), row),
                   pl.BlockSpec((1, N_EXPERTS, tile), lane_row),
                   pl.BlockSpec((seg_per_step, 2, N_EXPERTS, LANES),
                                lambda bi, i: (bi * (s // tile) + i, 0, 0, 0)),
                   pl.BlockSpec((N_EXPERTS, LANES), c2)],
        out_shape=[jax.ShapeDtypeStruct((b, s, d), F32),
                   jax.ShapeDtypeStruct((b, s, HX_WIDTH), BF16),
                   jax.ShapeDtypeStruct((b, N_EXPERTS, s), jnp.int32),
                   jax.ShapeDtypeStruct((b * (s // TOK_TILE), 2, N_EXPERTS, LANES), jnp.int32),
                   jax.ShapeDtypeStruct((N_EXPERTS, LANES), F32)],
        scratch_shapes=[pltpu.VMEM((tile + 2 * POOL_HALO, POOL_WIDTH), F32)],
        compiler_params=pltpu.CompilerParams(dimension_semantics=("arbitrary", "arbitrary"),
                                             vmem_limit_bytes=VMEM_LIMIT),
        name="mix",
    )(attn, u, u, u, x, g1, sc2, sh2, w_pool, pool_scale, w_out, ln_g, ln_b, w_r_t, rb_t,
      tri, ones)


def _one_hot(pos_row, start, window, n_tok):
    rows = lax.broadcasted_iota(jnp.int32, (SEG_WIN, n_tok), 0)
    hit = rows == (pos_row - (start * SEG_ROWS + window * SEG_WIN))
    return jnp.where(hit, 1.0, 0.0).astype(BF16)


def _window_chunks(base_ref, start_ref, step, e, window, n_chunks=WIN_CHUNKS):
    return pl.ds(base_ref[e] + start_ref[step * N_EXPERTS + e] + window * WIN_CHUNKS, n_chunks)


def _start_window(seg_chunks, make_copy):
    @pl.when(seg_chunks <= HALF_CHUNKS)
    def _():
        make_copy(HALF_CHUNKS).start()

    @pl.when(seg_chunks > HALF_CHUNKS)
    def _():
        make_copy(WIN_CHUNKS).start()


def _wait_half_windows(make_copy, units):
    make_copy(N_EXPERTS).wait()
    rest = units - N_EXPERTS
    for bit in range(N_EXPERTS.bit_length()):
        @pl.when((rest >> bit) & 1 == 1)
        def _():
            make_copy(1 << bit).wait()


def _extra_windows(nchunk_ref, step, e):
    rows = nchunk_ref[step * N_EXPERTS + e] * SEG_ROWS
    return jnp.maximum((rows + SEG_WIN - 1) // SEG_WIN, 1)


def _dispatch_kernel(base_ref, start_ref, nchunk_ref, units_ref, over_ref, total_ref, pcnt_ref,
                     nu_ref, hx_ref, pos_ref, xs_hbm, stage, extra, zero_ref, sem):
    i = pl.program_id(0)
    tile_chunks = MOE_TM // SEG_ROWS
    n_tiles = xs_hbm.shape[0] // tile_chunks
    n_tok, width = hx_ref.shape

    def zero_copy(chunk, n_chunks):
        return pltpu.make_async_copy(zero_ref.at[pl.ds(0, n_chunks)],
                                     xs_hbm.at[pl.ds(chunk, n_chunks)], sem.at[1])

    @pl.when(i == 0)
    def _():
        zero_ref[...] = jnp.zeros_like(zero_ref)
        for wait in (False, True):
            def per_tail(j, c):
                cp = zero_copy(j * tile_chunks, tile_chunks)
                cp.wait() if wait else cp.start()
                return c
            lax.fori_loop(nu_ref[0], n_tiles, per_tail, 0)

            def per_expert(e, c):
                def per_chunk(j, c2):
                    cp = zero_copy(base_ref[e] + j, 1)
                    cp.wait() if wait else cp.start()
                    return c2
                return lax.fori_loop(total_ref[e], pcnt_ref[e], per_chunk, c)
            lax.fori_loop(0, N_EXPERTS, per_expert, 0)

    slot = i % 2
    hx = hx_ref[...]
    group_chunks = WIN_GROUP * WIN_CHUNKS
    for g in range(N_EXPERTS // WIN_GROUP):
        oh = jnp.concatenate(
            [_one_hot(pos_ref[0, e:e + 1, :], start_ref[i * N_EXPERTS + e], 0, n_tok)
             for e in range(g * WIN_GROUP, (g + 1) * WIN_GROUP)], axis=0)
        rows = jnp.dot(oh, hx, preferred_element_type=F32).astype(BF16)
        stage[slot, pl.ds(g * group_chunks, group_chunks)] = rows.reshape(group_chunks, SEG_ROWS,
                                                                         width)

    def wait_windows(step, which):
        _wait_half_windows(
            lambda k: pltpu.make_async_copy(stage.at[which, pl.ds(0, k * HALF_CHUNKS)],
                                            xs_hbm.at[pl.ds(0, k * HALF_CHUNKS)], sem.at[0]),
            units_ref[step])

    @pl.when(i > 0)
    def _():
        wait_windows(i - 1, 1 - slot)

    for e in range(N_EXPERTS):
        _start_window(nchunk_ref[i * N_EXPERTS + e],
                      lambda n: pltpu.make_async_copy(
                          stage.at[slot, pl.ds(e * WIN_CHUNKS, n)],
                          xs_hbm.at[_window_chunks(base_ref, start_ref, i, e, 0, n)], sem.at[0]))

    @pl.when(i == pl.num_programs(0) - 1)
    def _():
        wait_windows(i, slot)

    @pl.when(over_ref[i] > 0)
    def _():
        def more_windows(e, c):
            def one(window, c2):
                oh = _one_hot(pos_ref[0, pl.ds(e, 1), :], start_ref[i * N_EXPERTS + e], window,
                              n_tok)
                rows = jnp.dot(oh, hx, preferred_element_type=F32).astype(BF16)
                extra[...] = rows.reshape(WIN_CHUNKS, SEG_ROWS, width)
                cp = pltpu.make_async_copy(
                    extra, xs_hbm.at[_window_chunks(base_ref, start_ref, i, e, window)], sem.at[2])
                cp.start()
                cp.wait()
                return c2
            return lax.fori_loop(1, _extra_windows(nchunk_ref, i, e), one, c)
        lax.fori_loop(0, N_EXPERTS, more_windows, 0)


def _dispatch(base, seg_start, seg_chunks, seg_units, seg_over, total, pcnt, n_used, hx, pos,
              n_slots):
    t, width = hx.shape
    tpb = pos.shape[2] // TOK_TILE
    return pl.pallas_call(
        _dispatch_kernel,
        grid_spec=pltpu.PrefetchScalarGridSpec(
            num_scalar_prefetch=8, grid=(t // TOK_TILE,),
            in_specs=[pl.BlockSpec((TOK_TILE, width), lambda i, *_: (i, 0)),
                      pl.BlockSpec((1, N_EXPERTS, TOK_TILE), lambda i, *_: (i // tpb, 0, i % tpb))],
            out_specs=pl.BlockSpec(memory_space=pl.ANY),
            scratch_shapes=[pltpu.VMEM((2, N_EXPERTS * WIN_CHUNKS, SEG_ROWS, width), BF16),
                            pltpu.VMEM((WIN_CHUNKS, SEG_ROWS, width), BF16),
                            pltpu.VMEM((MOE_TM // SEG_ROWS, SEG_ROWS, width), BF16),
                            pltpu.SemaphoreType.DMA((3,))]),
        out_shape=jax.ShapeDtypeStruct((n_slots // SEG_ROWS, SEG_ROWS, width), BF16),
        compiler_params=pltpu.CompilerParams(dimension_semantics=("arbitrary",),
                                             vmem_limit_bytes=MOE_VMEM_LIMIT),
        name="dispatch",
    )(base, seg_start, seg_chunks, seg_units, seg_over, total, pcnt, n_used, hx, pos)


def _experts_kernel(te_ref, nu_ref, xs_ref, wg_ref, wu_ref, wd_ref, ys_ref, wg_b, wu_b, wd_b):
    i = pl.program_id(0)

    @pl.when(i < nu_ref[0])
    def _():
        @pl.when((i == 0) | (te_ref[i] != te_ref[jnp.maximum(i - 1, 0)]))
        def _():
            wg_b[...] = wg_ref[0].astype(BF16)
            wu_b[...] = wu_ref[0].astype(BF16)
            wd_b[...] = wd_ref[0].astype(BF16)

        half = MOE_TM // 2
        half_chunks = half // SEG_ROWS
        halves = [pl.ds(k * half_chunks, half_chunks) for k in range(2)]
        hidden = []
        for chunks in halves:
            xg = xs_ref[chunks].reshape(half, xs_ref.shape[2])
            x = xg[:, :D_MODEL]
            hidden.append((xg[:, D_MODEL:].astype(F32),
                           jnp.dot(x, wg_b[...], preferred_element_type=F32),
                           jnp.dot(x, wu_b[...], preferred_element_type=F32)))
        for chunks, (g, hg, hu) in zip(halves, hidden):
            lane = lax.broadcasted_iota(jnp.int32, g.shape, 1)
            mine = (lane == te_ref[i]) | (lane == te_ref[i] + N_EXPERTS)
            gate = jnp.sum(jnp.where(mine, g, 0.0), axis=1, keepdims=True)
            a = (_silu(hg) * hu * gate).astype(BF16)
            y = jnp.dot(a, wd_b[...], preferred_element_type=F32).astype(ys_ref.dtype)
            ys_ref[chunks] = y.reshape(half_chunks, SEG_ROWS, ys_ref.shape[2])

    @pl.when(i >= nu_ref[0])
    def _():
        ys_ref[...] = jnp.zeros_like(ys_ref)


def _experts(tile_expert, n_used, xs, w_e_gate, w_e_up, w_e_down):
    n_chunks, _, width = xs.shape
    _, d, f = w_e_gate.shape
    tile_chunks = MOE_TM // SEG_ROWS
    slot_tile = lambda i, te, nu: (jnp.minimum(i, nu[0] - 1), 0, 0)
    expert = lambda i, te, nu: (te[i], 0, 0)
    return pl.pallas_call(
        _experts_kernel,
        grid_spec=pltpu.PrefetchScalarGridSpec(
            num_scalar_prefetch=2, grid=(n_chunks // tile_chunks,),
            in_specs=[pl.BlockSpec((tile_chunks, SEG_ROWS, width), slot_tile),
                      pl.BlockSpec((1, d, f), expert), pl.BlockSpec((1, d, f), expert),
                      pl.BlockSpec((1, f, d), expert)],
            out_specs=pl.BlockSpec((tile_chunks, SEG_ROWS, d), lambda i, te, nu: (i, 0, 0)),
            scratch_shapes=[pltpu.VMEM((d, f), BF16), pltpu.VMEM((d, f), BF16),
                            pltpu.VMEM((f, d), BF16)]),
        out_shape=jax.ShapeDtypeStruct((n_chunks, SEG_ROWS, d), BF16),
        compiler_params=pltpu.CompilerParams(dimension_semantics=("arbitrary",),
                                             vmem_limit_bytes=VMEM_LIMIT),
        name="experts",
    )(tile_expert, n_used, xs, w_e_gate, w_e_up, w_e_down)


def _dot_tn(a, b):
    return lax.dot_general(a, b, (((0,), (0,)), ((), ())), preferred_element_type=F32)


def _combine_kernel(base_ref, start_ref, nchunk_ref, units_ref, over_ref, hx_ref, pos_ref, x1_ref,
                    g2_ref, wsg_ref, wsu_ref, wsd_ref, ln_g_ref, ln_b_ref, ys_hbm, o_ref, win, extra,
                    acc_ref, sem):
    i = pl.program_id(0)
    n = pl.num_programs(0)
    n_tok = hx_ref.shape[0]

    def fetch(step, slot):
        for e in range(N_EXPERTS):
            _start_window(nchunk_ref[step * N_EXPERTS + e],
                          lambda k: pltpu.make_async_copy(
                              ys_hbm.at[_window_chunks(base_ref, start_ref, step, e, 0, k)],
                              win.at[slot, pl.ds(e * WIN_CHUNKS, k)], sem.at[slot]))

    @pl.when(i == 0)
    def _():
        win[...] = jnp.zeros_like(win)
        fetch(0, 0)

    @pl.when(i + 1 < n)
    def _():
        fetch(i + 1, (i + 1) % 2)

    slot = i % 2
    h = hx_ref[:, :D_MODEL]
    a = _silu(jnp.dot(h, wsg_ref[...], preferred_element_type=F32)) * jnp.dot(
        h, wsu_ref[...], preferred_element_type=F32)
    moe = jnp.dot(a.astype(BF16), wsd_ref[...], preferred_element_type=F32)
    _wait_half_windows(
        lambda k: pltpu.make_async_copy(ys_hbm.at[pl.ds(0, k * HALF_CHUNKS)],
                                        win.at[slot, pl.ds(0, k * HALF_CHUNKS)], sem.at[slot]),
        units_ref[i])
    d = win.shape[3]
    group_chunks = WIN_GROUP * WIN_CHUNKS
    for g in range(N_EXPERTS // WIN_GROUP):
        oh = jnp.concatenate(
            [_one_hot(pos_ref[0, e:e + 1, :], start_ref[i * N_EXPERTS + e], 0, n_tok)
             for e in range(g * WIN_GROUP, (g + 1) * WIN_GROUP)], axis=0)
        rows = win[slot, pl.ds(g * group_chunks, group_chunks)].reshape(WIN_GROUP * SEG_WIN, d)
        moe = moe + _dot_tn(oh, rows)
    acc_ref[...] = moe

    @pl.when(over_ref[i] > 0)
    def _():
        def more_windows(e, c):
            def one(window, c2):
                cp = pltpu.make_async_copy(
                    ys_hbm.at[_window_chunks(base_ref, start_ref, i, e, window)], extra, sem.at[2])
                cp.start()
                cp.wait()
                oh = _one_hot(pos_ref[0, pl.ds(e, 1), :], start_ref[i * N_EXPERTS + e], window,
                              n_tok)
                acc_ref[...] += _dot_tn(oh, extra[...].reshape(SEG_WIN, d))
                return c2
            return lax.fori_loop(1, _extra_windows(nchunk_ref, i, e), one, c)
        lax.fori_loop(0, N_EXPERTS, more_windows, 0)

    z = ALPHA * x1_ref[...] + g2_ref[0] * acc_ref[...]
    o_ref[...] = _layer_norm(z, ln_g_ref[...], ln_b_ref[...])


def _combine(base, seg_start, seg_chunks, seg_units, seg_over, hx, pos, x1, g2, w_s_gate, w_s_up,
             w_s_down, ln_g, ln_b, ys):
    t, d = x1.shape
    tpb = pos.shape[2] // TOK_TILE
    row = lambda i, *_: (i, 0)
    c2 = lambda i, *_: (0, 0)
    return pl.pallas_call(
        _combine_kernel,
        grid_spec=pltpu.PrefetchScalarGridSpec(
            num_scalar_prefetch=5, grid=(t // TOK_TILE,),
            in_specs=[pl.BlockSpec((TOK_TILE, hx.shape[1]), row),
                      pl.BlockSpec((1, N_EXPERTS, TOK_TILE), lambda i, *_: (i // tpb, 0, i % tpb)),
                      pl.BlockSpec((TOK_TILE, d), row),
                      pl.BlockSpec((1, 1, d), lambda i, *_: (i // tpb, 0, 0)),
                      pl.BlockSpec(w_s_gate.shape, c2), pl.BlockSpec(w_s_up.shape, c2),
                      pl.BlockSpec(w_s_down.shape, c2),
                      pl.BlockSpec(ln_g.shape, c2), pl.BlockSpec(ln_b.shape, c2),
                      pl.BlockSpec(memory_space=pl.ANY)],
            out_specs=pl.BlockSpec((TOK_TILE, d), row),
            scratch_shapes=[pltpu.VMEM((2, N_EXPERTS * WIN_CHUNKS, SEG_ROWS, d), BF16),
                            pltpu.VMEM((WIN_CHUNKS, SEG_ROWS, d), BF16),
                            pltpu.VMEM((TOK_TILE, d), F32),
                            pltpu.SemaphoreType.DMA((3,))]),
        out_shape=jax.ShapeDtypeStruct((t, d), F32),
        compiler_params=pltpu.CompilerParams(dimension_semantics=("arbitrary",),
                                             vmem_limit_bytes=MOE_VMEM_LIMIT),
        name="combine",
    )(base, seg_start, seg_chunks, seg_units, seg_over, hx, pos, x1, g2, w_s_gate, w_s_up, w_s_down,
      ln_g, ln_b, ys)


def _moe(hx, pos, seg, counts, x1, g2, w_e_gate, w_e_up, w_e_down, w_s_gate, w_s_up, w_s_down,
         ln_g, ln_b):
    b, s, d = x1.shape
    t = b * s
    n_seg = (t // TOK_TILE) * N_EXPERTS
    max_rows = t * TOP_K + n_seg * (SEG_ROWS - 1) + N_EXPERTS * (SEG_WIN + MOE_TM)
    n_tiles = -(-max_rows // MOE_TM)
    total = jnp.round(counts[:, 0]).astype(jnp.int32)
    pcnt = (total + SEG_WIN + MOE_TM - 1) // MOE_TM * MOE_TM
    ends = jnp.cumsum(pcnt)
    base = ends - pcnt
    n_used = (ends[-1] // MOE_TM).reshape(1)
    tile_ids = jnp.arange(n_tiles, dtype=jnp.int32)
    tile_expert = jnp.sum((ends[None, :] <= tile_ids[:, None] * MOE_TM).astype(jnp.int32), axis=1)
    tile_expert = jnp.minimum(tile_expert, N_EXPERTS - 1)
    tile_expert = jnp.where(tile_ids < n_used, tile_expert, tile_expert[n_used[0] - 1])
    base, total, pcnt = base // SEG_ROWS, total // SEG_ROWS, pcnt // SEG_ROWS
    seg_start = seg[:, 0, :, 0].reshape(n_seg) // SEG_ROWS
    seg_chunks = seg[:, 1, :, 0]
    seg_over = (jnp.max(seg_chunks, axis=1) * SEG_ROWS > SEG_WIN).astype(jnp.int32)
    seg_units = jnp.sum(jnp.where(seg_chunks <= HALF_CHUNKS, 1, 2), axis=1).astype(jnp.int32)
    seg_chunks = seg_chunks.reshape(n_seg)

    hxf = hx.reshape(t, hx.shape[2])
    xs = _dispatch(base, seg_start, seg_chunks, seg_units, seg_over, total, pcnt, n_used, hxf, pos,
                   n_tiles * MOE_TM)
    ys = _experts(tile_expert, n_used, xs, w_e_gate, w_e_up, w_e_down)
    out = _combine(base, seg_start, seg_chunks, seg_units, seg_over, hxf, pos, x1.reshape(t, d), g2,
                   w_s_gate, w_s_up, w_s_down, ln_g, ln_b, ys)
    return out.reshape(b, s, d)


def _rope_tables(seq):
    t = jnp.arange(seq)
    pos = jnp.stack([t // GRID_W, t % GRID_W], axis=-1).astype(F32)
    inv_freq = ROPE_THETA ** (-jnp.arange(ROPE_FREQS, dtype=F32) / ROPE_FREQS)
    ang = pos[:, :, None] * inv_freq
    cos, sin = jnp.cos(ang), jnp.sin(ang)
    zero = jnp.zeros_like(sin)
    cos_r = jnp.stack([cos, cos], axis=2).reshape(seq, QK_ROPE)
    sin_lo = jnp.stack([-sin, zero], axis=2).reshape(seq, QK_ROPE)
    sin_hi = jnp.stack([zero, sin], axis=2).reshape(seq, QK_ROPE)
    pads = ((0, 0), (QK_NOPE, HEAD_PAD - QK_NOPE - QK_ROPE))
    lane_tabs = (jnp.pad(cos_r, pads, constant_values=1.0), jnp.pad(sin_lo, pads),
                 jnp.pad(sin_hi, pads))
    row_tabs = (cos.reshape(seq, 2 * ROPE_FREQS).T, sin.reshape(seq, 2 * ROPE_FREQS).T)
    return lane_tabs, row_tabs


def _pad_heads(w, width, padded):
    k = w.shape[0]
    w = jnp.pad(w.reshape(k, N_HEADS, width), ((0, 0), (0, 0), (0, padded - width)))
    return w.reshape(k, N_HEADS * padded)


def kernel(x, c, ctx, c_ctx, w_ada, b_ada, w_in, q_norm_g, w_uq, kv_norm_g, w_ukv, w_pool, pool_scale, w_out, ln1_g, ln1_b, w_router, router_bias, w_e_gate, w_e_up, w_e_down, w_s_gate, w_s_up, w_s_down, ln2_g, ln2_b):
    assert w_ada.shape[0] == 1, "single-layer block"
    b, s, d = x.shape

    cvec = jnp.concatenate([c, c_ctx[None], jnp.zeros((SUBLANES - b - 1, d), F32)], axis=0)
    mod = _ada(cvec, w_ada[0], b_ada)
    sh1, sc1, g1, sh2, sc2, g2 = [mod[:b, k * d:(k + 1) * d][:, None, :] for k in range(6)]
    sh1c, sc1c = [jnp.broadcast_to(mod[b, k * d:(k + 1) * d], (b, 1, d)) for k in range(2)]

    wi = w_in[0]
    kr_cols = jnp.pad(wi[:, Q_LORA + KV_LORA:Q_LORA + KV_LORA + QK_ROPE],
                      ((0, 0), (QK_NOPE, HEAD_PAD - QK_NOPE - QK_ROPE)))
    w_in_r = jnp.concatenate([wi[:, :Q_LORA + KV_LORA], wi[:, Q_LORA + KV_LORA + QK_ROPE:], kr_cols],
                             axis=1).astype(BF16)
    w_uq_t = _pad_heads(w_uq[0], QK_NOPE + QK_ROPE, HEAD_PAD).T.astype(BF16)
    wkv = w_ukv[0].reshape(KV_LORA, N_HEADS, QK_NOPE + V_HEAD)
    w_uk_p = _pad_heads(wkv[:, :, :QK_NOPE].reshape(KV_LORA, -1), QK_NOPE, HEAD_PAD).astype(BF16)
    w_uv_t = _pad_heads(wkv[:, :, QK_NOPE:].reshape(KV_LORA, -1), V_HEAD, V_ROWS).T.astype(BF16)
    tables = _rope_tables(s)

    q_t, k, v_t, u = _proj(x, sc1, sh1, tables, w_in_r, q_norm_g, w_uq_t, kv_norm_g, w_uk_p,
                           w_uv_t, PROJ_TILE)
    kc, vc_t = _proj(ctx, sc1c, sh1c, None, w_in_r, None, None, kv_norm_g, w_uk_p, w_uv_t,
                     ctx.shape[1])
    attn = _attention(q_t, kc, vc_t, k, v_t)

    x1, hx, pos, seg, counts = _mix(attn, u, x, g1, sc2, sh2, w_pool[0].astype(BF16),
                                    pool_scale, w_out[0].astype(BF16), ln1_g, ln1_b,
                                    w_router[0].T, router_bias[0][:, None])
    return _moe(hx, pos, seg, counts, x1, g2, w_e_gate[0], w_e_up[0], w_e_down[0],
                w_s_gate[0].astype(BF16), w_s_up[0].astype(BF16), w_s_down[0].astype(BF16),
                ln2_g, ln2_b)
```

```python
import functools
import math

import jax
import jax.numpy as jnp
from jax import lax
from jax.experimental import pallas as pl
from jax.experimental.pallas import tpu as pltpu

F32 = jnp.float32
BF16 = jnp.bfloat16

D_MODEL = 1024
GRID_W = 64
N_HEADS = 8
Q_LORA = 512
KV_LORA = 256
QK_NOPE = 64
QK_ROPE = 32
V_HEAD = 64
ROPE_FREQS = QK_ROPE // 4
ROPE_THETA = 10000.0
ATTN_SCALE = 1.0 / math.sqrt(QK_NOPE + QK_ROPE)
LOG2_E = math.log2(math.e)
POOL_GROUPS = 4
POOL_WINDOWS = (2, 4, 8, 16)
POOL_WIDTH = 512
POOL_GC = POOL_WIDTH // POOL_GROUPS
POOL_HALO = 8
N_EXPERTS = 64
N_EXPERT_GROUPS = 8
GROUP_SIZE = N_EXPERTS // N_EXPERT_GROUPS
TOPK_GROUPS = 4
TOP_K = 8
D_EXPERT = 256
ROUTED_SCALE = 2.5
LN_EPS = 1e-5
RMS_EPS = 1e-6
ALPHA = 2.0 ** 0.25

LANES = 128
SUBLANES = 8
HEAD_PAD = LANES
V_ROWS = 80
ONES_ROW = V_HEAD
IN_PAD = Q_LORA + KV_LORA + POOL_WIDTH + LANES

PROJ_TILE = 512
ATTN_TQ = 512
ATTN_TK = PROJ_TILE
ATTN_SUB = 256
ATTN_AHEAD = 2
ATTN_UNROLL = 16
TOK_TILE = 256
MIX_TILE = 2 * TOK_TILE
MOE_TM = 1024
SEG_ROWS = 16
SEG_WIN = 64
WIN_CHUNKS = SEG_WIN // SEG_ROWS
HALF_CHUNKS = WIN_CHUNKS // 2
PAIR_CHUNKS = 2 * WIN_CHUNKS
WIN_GROUP = 16
HX_WIDTH = D_MODEL + 2 * N_EXPERTS
VMEM_LIMIT = 48 * 1024 * 1024
MOE_VMEM_LIMIT = 60 * 1024 * 1024
NEG_BIG = -1e30


def _silu(v):
    return v * jax.nn.sigmoid(v)


def _layer_norm(z, g, b):
    mu = jnp.mean(z, axis=-1, keepdims=True)
    zc = z - mu
    var = jnp.mean(zc * zc, axis=-1, keepdims=True)
    return zc * lax.rsqrt(var + LN_EPS) * g + b


def _rms_norm(v, g):
    return v * lax.rsqrt(jnp.mean(v * v, axis=-1, keepdims=True) + RMS_EPS) * g


def _dot_nt(a, b):
    return lax.dot_general(a, b, (((1,), (1,)), ((), ())), preferred_element_type=F32)


def _ada_kernel(c_ref, w_ref, b_ref, o_ref):
    cv = _silu(c_ref[...])
    o_ref[...] = jnp.dot(cv, w_ref[...], preferred_element_type=F32,
                         precision=lax.Precision.HIGHEST) + b_ref[...]


def _ada(cvec, w_ada, b_ada):
    rows, d = cvec.shape
    n = w_ada.shape[1]
    tn = 1024
    return pl.pallas_call(
        _ada_kernel,
        grid=(n // tn,),
        in_specs=[pl.BlockSpec((rows, d), lambda j: (0, 0)),
                  pl.BlockSpec((d, tn), lambda j: (0, j)),
                  pl.BlockSpec((1, tn), lambda j: (0, j))],
        out_specs=pl.BlockSpec((rows, tn), lambda j: (0, j)),
        out_shape=jax.ShapeDtypeStruct((rows, n), F32),
        compiler_params=pltpu.CompilerParams(dimension_semantics=("arbitrary",),
                                             vmem_limit_bytes=VMEM_LIMIT),
        name="ada",
    )(cvec, w_ada, b_ada)


def _rope_lanes(v, cos, sin_lo, sin_hi):
    return v * cos + pltpu.roll(v, LANES - 8, axis=1) * sin_lo + pltpu.roll(v, 8, axis=1) * sin_hi


def _proj_kernel(*refs, with_q):
    if with_q:
        (x_ref, sc_ref, sh_ref, cos_ref, slo_ref, shi_ref, cos_t_ref, sin_t_ref, win_ref, qg_ref,
         wuq_ref, kvg_ref, wuk_ref, wuv_ref, q_ref, k_ref, v_ref, u_ref) = refs
    else:
        (x_ref, sc_ref, sh_ref, win_ref, kvg_ref, wuk_ref, wuv_ref, k_ref, v_ref) = refs
    h = (x_ref[0] * (1.0 + sc_ref[0]) + sh_ref[0]).astype(BF16)
    p = jnp.dot(h, win_ref[...], preferred_element_type=F32)
    tile = p.shape[0]
    kr = p[:, IN_PAD - LANES:]
    kvn = _rms_norm(p[:, Q_LORA:Q_LORA + KV_LORA], kvg_ref[...]).astype(BF16)
    kfull = jnp.dot(kvn, wuk_ref[...], preferred_element_type=F32)
    v_t = _dot_nt(wuv_ref[...], kvn)
    row = lax.broadcasted_iota(jnp.int32, (N_HEADS * V_ROWS, 1), 0)
    v_t = v_t + (row % V_ROWS == ONES_ROW).astype(F32)
    if with_q:
        kr = _rope_lanes(kr, cos_ref[...], slo_ref[...], shi_ref[...])
        u_ref[0] = p[:, Q_LORA + KV_LORA:Q_LORA + KV_LORA + POOL_WIDTH]
        qn = _rms_norm(p[:, :Q_LORA], qg_ref[...]).astype(BF16)
        q_t = _dot_nt(wuq_ref[...], qn) * (ATTN_SCALE * LOG2_E)
        cos_t, sin_t = cos_t_ref[...], sin_t_ref[...]
    for hd in range(N_HEADS):
        k_ref[0, hd] = (kfull[:, hd * HEAD_PAD:(hd + 1) * HEAD_PAD] + kr).astype(BF16)
        v_ref[0, hd, 0] = v_t[hd * V_ROWS:(hd + 1) * V_ROWS].astype(BF16)
        if with_q:
            base = hd * HEAD_PAD
            q_ref[0, hd, 0:QK_NOPE, :] = q_t[base:base + QK_NOPE].astype(BF16)
            rope = []
            for ax in range(2):
                lo = q_t[base + QK_NOPE + 16 * ax:base + QK_NOPE + 16 * ax + 8]
                hi = q_t[base + QK_NOPE + 16 * ax + 8:base + QK_NOPE + 16 * ax + 16]
                cs, sn = cos_t[8 * ax:8 * ax + 8], sin_t[8 * ax:8 * ax + 8]
                rope += [lo * cs - hi * sn, hi * cs + lo * sn]
            rope.append(jnp.zeros((HEAD_PAD - QK_NOPE - QK_ROPE, tile), F32))
            q_ref[0, hd, QK_NOPE:, :] = jnp.concatenate(rope, axis=0).astype(BF16)


def _proj(x, sc, sh, tables, w_in_r, q_g, w_uq_t, kv_g, w_uk_p, w_uv_t, tile):
    b, s, d = x.shape
    with_q = tables is not None
    grid = (b, s // tile)
    row = lambda bi, i: (bi, i, 0)
    vec = lambda bi, i: (bi, 0, 0)
    const2 = lambda bi, i: (0, 0)
    k_out = pl.BlockSpec((1, N_HEADS, tile, HEAD_PAD), lambda bi, i: (bi, 0, i, 0))
    k_shape = jax.ShapeDtypeStruct((b, N_HEADS, s, HEAD_PAD), BF16)
    v_out = pl.BlockSpec((1, N_HEADS, 1, V_ROWS, tile), lambda bi, i: (bi, 0, i, 0, 0))
    v_shape = jax.ShapeDtypeStruct((b, N_HEADS, s // tile, V_ROWS, tile), BF16)
    in_specs = [pl.BlockSpec((1, tile, d), row),
                pl.BlockSpec((1, 1, d), vec), pl.BlockSpec((1, 1, d), vec)]
    args = [x, sc, sh]
    if with_q:
        lane_tabs, row_tabs = tables
        in_specs += [pl.BlockSpec((tile, LANES), lambda bi, i: (i, 0))] * 3
        in_specs += [pl.BlockSpec((2 * ROPE_FREQS, tile), lambda bi, i: (0, i))] * 2
        args += list(lane_tabs) + list(row_tabs)
    in_specs.append(pl.BlockSpec(w_in_r.shape, const2)); args.append(w_in_r)
    if with_q:
        in_specs += [pl.BlockSpec(q_g.shape, const2), pl.BlockSpec(w_uq_t.shape, const2)]
        args += [q_g, w_uq_t]
    in_specs += [pl.BlockSpec(kv_g.shape, const2), pl.BlockSpec(w_uk_p.shape, const2),
                 pl.BlockSpec(w_uv_t.shape, const2)]
    args += [kv_g, w_uk_p, w_uv_t]
    if with_q:
        q_out = pl.BlockSpec((1, N_HEADS, HEAD_PAD, tile), lambda bi, i: (bi, 0, 0, i))
        q_shape = jax.ShapeDtypeStruct((b, N_HEADS, HEAD_PAD, s), BF16)
        out_specs = [q_out, k_out, v_out, pl.BlockSpec((1, tile, POOL_WIDTH), row)]
        out_shape = [q_shape, k_shape, v_shape, jax.ShapeDtypeStruct((b, s, POOL_WIDTH), F32)]
    else:
        out_specs = [k_out, v_out]
        out_shape = [k_shape, v_shape]
    return pl.pallas_call(
        functools.partial(_proj_kernel, with_q=with_q),
        grid=grid, in_specs=in_specs, out_specs=out_specs, out_shape=out_shape,
        compiler_params=pltpu.CompilerParams(dimension_semantics=("arbitrary", "arbitrary"),
                                             vmem_limit_bytes=VMEM_LIMIT),
        name="proj" if with_q else "proj_ctx",
    )(*args)


def _attn_kernel(q_ref, kc_ref, vc_ref, k_ref, v_ref, o_ref, *, n_kblk, tk):
    tq = q_ref.shape[3]
    qs = [q_ref[0, hh] for hh in range(2)]

    def scores(hh, kb):
        return jnp.dot(kb, qs[hh], preferred_element_type=F32)

    def update(s_t, vb_t, m, acc):
        m_new = jnp.maximum(m, jnp.max(s_t, axis=0, keepdims=True))
        p_t = jnp.exp2(s_t - m_new).astype(BF16)
        acc = jnp.exp2(m - m_new) * acc + jnp.dot(vb_t, p_t, preferred_element_type=F32)
        return m_new, acc

    def run_items(items, state):
        pending = [scores(hh, kb()) for hh, kb, _ in items[:ATTN_AHEAD]]
        for j, (hh, _, vb) in enumerate(items):
            if j + ATTN_AHEAD < len(items):
                nh, nkb, _ = items[j + ATTN_AHEAD]
                pending.append(scores(nh, nkb()))
            state[hh] = update(pending.pop(0), vb(), *state[hh])
        return state

    def block_items(blk, off):
        out = []
        for sub in range(tk // ATTN_SUB):
            for hh in range(2):
                lo = sub * ATTN_SUB
                out.append((hh,
                            lambda hh=hh, lo=lo: k_ref[0, hh, pl.ds(off + lo, ATTN_SUB), :],
                            lambda hh=hh, lo=lo: v_ref[0, hh, blk, :, lo:lo + ATTN_SUB]))
        return out

    ctx_items = [(hh, lambda hh=hh: kc_ref[0, hh], lambda hh=hh: vc_ref[0, hh, 0])
                 for hh in range(2)]
    state = [(jnp.full((1, tq), NEG_BIG, F32), jnp.zeros((V_ROWS, tq), F32)) for _ in range(2)]
    n_iter = n_kblk // ATTN_UNROLL
    if n_iter == 1:
        items = ctx_items
        for blk in range(n_kblk):
            items = items + block_items(blk, blk * tk)
        state = run_items(items, state)
    else:
        state = run_items(ctx_items, state)

        def body(i, carry):
            items = []
            for r in range(ATTN_UNROLL):
                blk = i * ATTN_UNROLL + r
                items += block_items(blk, pl.multiple_of(blk * tk, tk))
            st = run_items(items, [tuple(carry[:2]), tuple(carry[2:])])
            return st[0] + st[1]

        carry = lax.fori_loop(0, n_iter, body, state[0] + state[1])
        state = [tuple(carry[:2]), tuple(carry[2:])]
    outs = [acc[:V_HEAD] / acc[ONES_ROW:ONES_ROW + 1] for _, acc in state]
    o_ref[0] = jnp.concatenate(outs, axis=0).T.astype(o_ref.dtype)


def _attention(q_t, kc, vc_t, k, v_t):
    b, nh, dp, s = q_t.shape
    c = kc.shape[2]
    tq, tk = ATTN_TQ, ATTN_TK
    n_kblk = s // tk
    kern = functools.partial(_attn_kernel, n_kblk=n_kblk, tk=tk)
    return pl.pallas_call(
        kern,
        grid=(b, nh // 2, s // tq),
        in_specs=[pl.BlockSpec((1, 2, dp, tq), lambda bi, hp, qi: (bi, hp, 0, qi)),
                  pl.BlockSpec((1, 2, c, dp), lambda bi, hp, qi: (bi, hp, 0, 0)),
                  pl.BlockSpec((1, 2, 1, V_ROWS, c), lambda bi, hp, qi: (bi, hp, 0, 0, 0)),
                  pl.BlockSpec((1, 2, s, dp), lambda bi, hp, qi: (bi, hp, 0, 0)),
                  pl.BlockSpec((1, 2, n_kblk, V_ROWS, tk), lambda bi, hp, qi: (bi, hp, 0, 0, 0))],
        out_specs=pl.BlockSpec((1, tq, 2 * V_HEAD), lambda bi, hp, qi: (bi, qi, hp)),
        out_shape=jax.ShapeDtypeStruct((b, s, nh * V_HEAD), BF16),
        compiler_params=pltpu.CompilerParams(
            dimension_semantics=("arbitrary", "arbitrary", "arbitrary"),
            vmem_limit_bytes=VMEM_LIMIT),
        name="attn",
    )(q_t, kc, vc_t, k, v_t)


def _route(logits_t, bias_t):
    e, t = logits_t.shape
    scores = jax.nn.sigmoid(logits_t)
    biased = scores + bias_t
    neg_inf = F32(-jnp.inf)
    gscore = []
    for g in range(N_EXPERT_GROUPS):
        v = biased[g * GROUP_SIZE:(g + 1) * GROUP_SIZE]
        m1 = jnp.max(v, axis=0, keepdims=True)
        at_max = v == m1
        n_max = jnp.sum(at_max.astype(F32), axis=0, keepdims=True)
        m2 = jnp.max(jnp.where(at_max, neg_inf, v), axis=0, keepdims=True)
        gscore.append(m1 + jnp.where(n_max >= 2.0, m1, m2))
    masked = []
    for g in range(N_EXPERT_GROUPS):
        rank = jnp.zeros((1, t), F32)
        for o in range(N_EXPERT_GROUPS):
            if o == g:
                continue
            beats = (gscore[o] >= gscore[g]) if o < g else (gscore[o] > gscore[g])
            rank = rank + beats.astype(F32)
        keep = rank < float(TOPK_GROUPS)
        masked.append(jnp.where(keep, biased[g * GROUP_SIZE:(g + 1) * GROUP_SIZE], neg_inf))
    work = jnp.concatenate(masked, axis=0)
    rows = lax.broadcasted_iota(jnp.int32, (e, t), 0)
    sel = jnp.zeros((e, t), F32)
    for _ in range(TOP_K):
        m = jnp.max(work, axis=0, keepdims=True)
        first = jnp.min(jnp.where(work == m, rows, e), axis=0, keepdims=True)
        pick = rows == first
        sel = jnp.where(pick, 1.0, sel)
        work = jnp.where(pick, neg_inf, work)
    w = sel * scores
    gates = w / jnp.sum(w, axis=0, keepdims=True) * ROUTED_SCALE
    return sel, gates


def _mix_kernel(attn_ref, u_ref, up_ref, un_ref, x_ref, g1_ref, sc2_ref, sh2_ref, wpool_ref,
                pscale_ref, wout_ref, ln_g_ref, ln_b_ref, wr_ref, rb_ref, tri_ref, ones_ref,
                x1_ref, hx_ref, pos_ref, seg_ref, cnt_ref, uext_ref, *, seq):
    i = pl.program_id(1)

    @pl.when((pl.program_id(0) == 0) & (i == 0))
    def _():
        cnt_ref[...] = jnp.zeros_like(cnt_ref)

    tile = u_ref.shape[1]
    u = u_ref[0]
    uext_ref[0:POOL_HALO] = jnp.where(i == 0, 0.0, up_ref[0])
    uext_ref[POOL_HALO:POOL_HALO + tile] = u
    uext_ref[POOL_HALO + tile:] = jnp.where(i == pl.num_programs(1) - 1, 0.0, un_ref[0])
    t = i * tile + lax.broadcasted_iota(jnp.int32, (tile, POOL_GC), 0)
    pooled = []
    for g, w in enumerate(POOL_WINDOWS):
        lanes = slice(g * POOL_GC, (g + 1) * POOL_GC)
        tot = uext_ref[POOL_HALO - w // 2:POOL_HALO - w // 2 + tile, lanes]
        for dlt in range(1, w):
            start = POOL_HALO - w // 2 + dlt
            tot = tot + uext_ref[start:start + tile, lanes]
        cnt = (jnp.minimum(t - w // 2 + w, seq) - jnp.maximum(t - w // 2, 0)).astype(F32)
        pg = (tot / cnt - u[:, lanes]).astype(BF16)
        po = jnp.dot(pg, wpool_ref[g], preferred_element_type=F32) * pscale_ref[:, lanes]
        pooled.append(po.astype(BF16))
    mixed = jnp.concatenate([attn_ref[0]] + pooled, axis=1)
    y = jnp.dot(mixed, wout_ref[...], preferred_element_type=F32)
    x1 = _layer_norm(ALPHA * x_ref[0] + g1_ref[0] * y, ln_g_ref[...], ln_b_ref[...])
    x1_ref[0] = x1
    h2 = x1 * (1.0 + sc2_ref[0]) + sh2_ref[0]
    h_hi = h2.astype(BF16)
    h_lo = (h2 - h_hi.astype(F32)).astype(BF16)
    w_r = wr_ref[...]
    w_hi = w_r.astype(BF16)
    w_lo = (w_r - w_hi.astype(F32)).astype(BF16)
    logits_t = _dot_nt(w_hi, h_hi) + (_dot_nt(w_hi, h_lo) + _dot_nt(w_lo, h_hi))
    sel, gates_t = _route(logits_t, rb_ref[...])
    g_hi = gates_t.astype(BF16)
    g_lo = (gates_t - g_hi.astype(F32)).astype(BF16)
    g_tok = jnp.concatenate([g_hi.astype(F32), g_lo.astype(F32)], axis=0).T
    hx_ref[0] = jnp.concatenate([h2.astype(BF16), g_tok.astype(BF16)], axis=1)
    start = cnt_ref[...]
    for k in range(tile // TOK_TILE):
        cols = slice(k * TOK_TILE, (k + 1) * TOK_TILE)
        sel_k = sel[:, cols]
        sel_b = sel_k.astype(BF16)
        pos_t = jnp.dot(sel_b, tri_ref[...], preferred_element_type=F32) + start[:, 0:1]
        pos_ref[0, :, cols] = jnp.where(sel_k > 0.0, pos_t, -1.0).astype(jnp.int32)
        n_tok = jnp.dot(sel_b, ones_ref[...], preferred_element_type=F32)
        n_chunk = jnp.floor((n_tok + (SEG_ROWS - 1)) * (1.0 / SEG_ROWS))
        seg_ref[k, 0] = start.astype(jnp.int32)
        seg_ref[k, 1] = n_chunk.astype(jnp.int32)
        start = start + n_chunk * SEG_ROWS
    cnt_ref[...] = start


def _mix(attn, u, x, g1, sc2, sh2, w_pool, pool_scale, w_out, ln_g, ln_b, w_r_t, rb_t):
    b, s, d = x.shape
    tile = MIX_TILE
    hb = tile // POOL_HALO
    row = lambda bi, i: (bi, i, 0)
    vec = lambda bi, i: (bi, 0, 0)
    c2 = lambda bi, i: (0, 0)
    lane_row = lambda bi, i: (bi, 0, i)
    tri = (lax.broadcasted_iota(jnp.int32, (TOK_TILE, TOK_TILE), 0)
           < lax.broadcasted_iota(jnp.int32, (TOK_TILE, TOK_TILE), 1)).astype(BF16)
    ones = jnp.ones((TOK_TILE, LANES), BF16)
    seg_per_step = tile // TOK_TILE
    return pl.pallas_call(
        functools.partial(_mix_kernel, seq=s),
        grid=(b, s // tile),
        in_specs=[pl.BlockSpec((1, tile, POOL_WIDTH), row),
                  pl.BlockSpec((1, tile, POOL_WIDTH), row),
                  pl.BlockSpec((1, POOL_HALO, POOL_WIDTH),
                               lambda bi, i: (bi, jnp.maximum(i * hb - 1, 0), 0)),
                  pl.BlockSpec((1, POOL_HALO, POOL_WIDTH),
                               lambda bi, i: (bi, jnp.minimum((i + 1) * hb, s // POOL_HALO - 1), 0)),
                  pl.BlockSpec((1, tile, d), row),
                  pl.BlockSpec((1, 1, d), vec), pl.BlockSpec((1, 1, d), vec),
                  pl.BlockSpec((1, 1, d), vec),
                  pl.BlockSpec(w_pool.shape, lambda bi, i: (0, 0, 0)),
                  pl.BlockSpec(pool_scale.shape, c2),
                  pl.BlockSpec(w_out.shape, c2),
                  pl.BlockSpec(ln_g.shape, c2), pl.BlockSpec(ln_b.shape, c2),
                  pl.BlockSpec(w_r_t.shape, c2), pl.BlockSpec(rb_t.shape, c2),
                  pl.BlockSpec(tri.shape, c2), pl.BlockSpec(ones.shape, c2)],
        out_specs=[pl.BlockSpec((1, tile, d), row),
                   pl.BlockSpec((1, tile, HX_WIDTH), row),
                   pl.BlockSpec((1, N_EXPERTS, tile), lane_row),
                   pl.BlockSpec((seg_per_step, 2, N_EXPERTS, LANES),
                                lambda bi, i: (bi * (s // tile) + i, 0, 0, 0)),
                   pl.BlockSpec((N_EXPERTS, LANES), c2)],
        out_shape=[jax.ShapeDtypeStruct((b, s, d), F32),
                   jax.ShapeDtypeStruct((b, s, HX_WIDTH), BF16),
                   jax.ShapeDtypeStruct((b, N_EXPERTS, s), jnp.int32),
                   jax.ShapeDtypeStruct((b * (s // TOK_TILE), 2, N_EXPERTS, LANES), jnp.int32),
                   jax.ShapeDtypeStruct((N_EXPERTS, LANES), F32)],
        scratch_shapes=[pltpu.VMEM((tile + 2 * POOL_HALO, POOL_WIDTH), F32)],
        compiler_params=pltpu.CompilerParams(dimension_semantics=("arbitrary", "arbitrary"),
                                             vmem_limit_bytes=VMEM_LIMIT),
        name="mix",
    )(attn, u, u, u, x, g1, sc2, sh2, w_pool, pool_scale, w_out, ln_g, ln_b, w_r_t, rb_t,
      tri, ones)


def _one_hot(pos_row, start, window, n_tok):
    rows = lax.broadcasted_iota(jnp.int32, (SEG_WIN, n_tok), 0)
    hit = rows == (pos_row - (start * SEG_ROWS + window * SEG_WIN))
    return jnp.where(hit, 1.0, 0.0).astype(BF16)


def _window_chunks(base_ref, start_ref, step, e, window, n_chunks=WIN_CHUNKS):
    return pl.ds(base_ref[e] + start_ref[step * N_EXPERTS + e] + window * WIN_CHUNKS, n_chunks)


def _start_window(seg_chunks, make_copy):
    @pl.when(seg_chunks <= HALF_CHUNKS)
    def _():
        make_copy(HALF_CHUNKS).start()

    @pl.when(seg_chunks > HALF_CHUNKS)
    def _():
        make_copy(WIN_CHUNKS).start()


def _wait_half_windows(make_copy, units):
    make_copy(N_EXPERTS).wait()
    rest = units - N_EXPERTS
    for bit in range(N_EXPERTS.bit_length()):
        @pl.when((rest >> bit) & 1 == 1)
        def _():
            make_copy(1 << bit).wait()


def _extra_windows(nchunk_ref, step, e):
    rows = nchunk_ref[step * N_EXPERTS + e] * SEG_ROWS
    return jnp.maximum((rows + SEG_WIN - 1) // SEG_WIN, 1)


def _dispatch_kernel(base_ref, start_ref, nchunk_ref, over_ref, total_ref, pcnt_ref,
                     nu_ref, hx_ref, pos_ref, xs_hbm, stage, extra, zero_ref, sem):
    i = pl.program_id(0)
    tile_chunks = MOE_TM // SEG_ROWS
    n_tiles = xs_hbm.shape[0] // tile_chunks
    width = hx_ref.shape[1]

    def zero_copy(chunk, n_chunks):
        return pltpu.make_async_copy(zero_ref.at[pl.ds(0, n_chunks)],
                                     xs_hbm.at[pl.ds(chunk, n_chunks)], sem.at[1])

    @pl.when(i == 0)
    def _():
        zero_ref[...] = jnp.zeros_like(zero_ref)
        for wait in (False, True):
            def per_tail(j, c):
                cp = zero_copy(j * tile_chunks, tile_chunks)
                cp.wait() if wait else cp.start()
                return c
            lax.fori_loop(nu_ref[0], n_tiles, per_tail, 0)

            def per_expert(e, c):
                def per_chunk(j, c2):
                    cp = zero_copy(base_ref[e] + j, 1)
                    cp.wait() if wait else cp.start()
                    return c2
                return lax.fori_loop(total_ref[e], pcnt_ref[e], per_chunk, c)
            lax.fori_loop(0, N_EXPERTS, per_expert, 0)

    slot = i % 2
    tiles = (2 * i, 2 * i + 1)

    def pair_is_plain(step):
        return (over_ref[2 * step] + over_ref[2 * step + 1]) == 0

    def wait_pair(which):
        pltpu.make_async_copy(stage.at[which], xs_hbm.at[pl.ds(0, N_EXPERTS * PAIR_CHUNKS)],
                              sem.at[0]).wait()

    def wait_previous():
        @pl.when((i > 0) & pair_is_plain(jnp.maximum(i - 1, 0)))
        def _():
            wait_pair(1 - slot)

    def tile_rows(half):
        return hx_ref[pl.ds(half * TOK_TILE, TOK_TILE), :]

    def tile_pos(e, half):
        return pos_ref[0, e, pl.ds(half * TOK_TILE, TOK_TILE)]

    @pl.when(pair_is_plain(i))
    def _():
        zeros = jnp.zeros((WIN_CHUNKS, SEG_ROWS, width), BF16)
        for e in range(N_EXPERTS):
            stage[slot, pl.ds(e * PAIR_CHUNKS + WIN_CHUNKS, WIN_CHUNKS)] = zeros
        for half in range(2):
            hx = tile_rows(half)
            for g in range(N_EXPERTS // WIN_GROUP):
                experts = range(g * WIN_GROUP, (g + 1) * WIN_GROUP)
                oh = jnp.concatenate(
                    [_one_hot(tile_pos(slice(e, e + 1), half),
                              start_ref[tiles[half] * N_EXPERTS + e], 0, TOK_TILE)
                     for e in experts], axis=0)
                rows = jnp.dot(oh, hx, preferred_element_type=F32).astype(BF16)
                rows = rows.reshape(WIN_GROUP, WIN_CHUNKS, SEG_ROWS, width)
                for k, e in enumerate(experts):
                    first = 0 if half == 0 else nchunk_ref[tiles[0] * N_EXPERTS + e]
                    stage[slot, pl.ds(e * PAIR_CHUNKS + first, WIN_CHUNKS)] = rows[k]
        wait_previous()
        for e in range(N_EXPERTS):
            chunk = base_ref[e] + start_ref[tiles[0] * N_EXPERTS + e]
            pltpu.make_async_copy(stage.at[slot, pl.ds(e * PAIR_CHUNKS, PAIR_CHUNKS)],
                                  xs_hbm.at[pl.ds(chunk, PAIR_CHUNKS)], sem.at[0]).start()

        @pl.when(i == pl.num_programs(0) - 1)
        def _():
            wait_pair(slot)

    @pl.when(jnp.logical_not(pair_is_plain(i)))
    def _():
        wait_previous()
        for half in range(2):
            hx = tile_rows(half)

            def per_expert(e, c):
                def one(window, c2):
                    oh = _one_hot(tile_pos(pl.ds(e, 1), half),
                                  start_ref[tiles[half] * N_EXPERTS + e], window, TOK_TILE)
                    rows = jnp.dot(oh, hx, preferred_element_type=F32).astype(BF16)
                    extra[...] = rows.reshape(WIN_CHUNKS, SEG_ROWS, width)
                    cp = pltpu.make_async_copy(
                        extra,
                        xs_hbm.at[_window_chunks(base_ref, start_ref, tiles[half], e, window)],
                        sem.at[2])
                    cp.start()
                    cp.wait()
                    return c2
                return lax.fori_loop(0, _extra_windows(nchunk_ref, tiles[half], e), one, c)
            lax.fori_loop(0, N_EXPERTS, per_expert, 0)


def _dispatch(base, seg_start, seg_chunks, seg_over, total, pcnt, n_used, hx, pos, n_slots):
    t, width = hx.shape
    pair = 2 * TOK_TILE
    ppb = pos.shape[2] // pair
    return pl.pallas_call(
        _dispatch_kernel,
        grid_spec=pltpu.PrefetchScalarGridSpec(
            num_scalar_prefetch=7, grid=(t // pair,),
            in_specs=[pl.BlockSpec((pair, width), lambda i, *_: (i, 0)),
                      pl.BlockSpec((1, N_EXPERTS, pair), lambda i, *_: (i // ppb, 0, i % ppb))],
            out_specs=pl.BlockSpec(memory_space=pl.ANY),
            scratch_shapes=[pltpu.VMEM((2, N_EXPERTS * PAIR_CHUNKS, SEG_ROWS, width), BF16),
                            pltpu.VMEM((WIN_CHUNKS, SEG_ROWS, width), BF16),
                            pltpu.VMEM((MOE_TM // SEG_ROWS, SEG_ROWS, width), BF16),
                            pltpu.SemaphoreType.DMA((3,))]),
        out_shape=jax.ShapeDtypeStruct((n_slots // SEG_ROWS, SEG_ROWS, width), BF16),
        compiler_params=pltpu.CompilerParams(dimension_semantics=("arbitrary",),
                                             vmem_limit_bytes=MOE_VMEM_LIMIT),
        name="dispatch",
    )(base, seg_start, seg_chunks, seg_over, total, pcnt, n_used, hx, pos)


def _experts_kernel(te_ref, nu_ref, xs_ref, wg_ref, wu_ref, wd_ref, ys_ref, wg_b, wu_b, wd_b):
    i = pl.program_id(0)

    @pl.when(i < nu_ref[0])
    def _():
        @pl.when((i == 0) | (te_ref[i] != te_ref[jnp.maximum(i - 1, 0)]))
        def _():
            wg_b[...] = wg_ref[0].astype(BF16)
            wu_b[...] = wu_ref[0].astype(BF16)
            wd_b[...] = wd_ref[0].astype(BF16)

        half = MOE_TM // 2
        half_chunks = half // SEG_ROWS
        halves = [pl.ds(k * half_chunks, half_chunks) for k in range(2)]
        hidden = []
        for chunks in halves:
            xg = xs_ref[chunks].reshape(half, xs_ref.shape[2])
            x = xg[:, :D_MODEL]
            hidden.append((xg[:, D_MODEL:].astype(F32),
                           jnp.dot(x, wg_b[...], preferred_element_type=F32),
                           jnp.dot(x, wu_b[...], preferred_element_type=F32)))
        for chunks, (g, hg, hu) in zip(halves, hidden):
            lane = lax.broadcasted_iota(jnp.int32, g.shape, 1)
            mine = (lane == te_ref[i]) | (lane == te_ref[i] + N_EXPERTS)
            gate = jnp.sum(jnp.where(mine, g, 0.0), axis=1, keepdims=True)
            a = (_silu(hg) * hu * gate).astype(BF16)
            y = jnp.dot(a, wd_b[...], preferred_element_type=F32).astype(ys_ref.dtype)
            ys_ref[chunks] = y.reshape(half_chunks, SEG_ROWS, ys_ref.shape[2])

    @pl.when(i >= nu_ref[0])
    def _():
        ys_ref[...] = jnp.zeros_like(ys_ref)


def _experts(tile_expert, n_used, xs, w_e_gate, w_e_up, w_e_down):
    n_chunks, _, width = xs.shape
    _, d, f = w_e_gate.shape
    tile_chunks = MOE_TM // SEG_ROWS
    slot_tile = lambda i, te, nu: (jnp.minimum(i, nu[0] - 1), 0, 0)
    expert = lambda i, te, nu: (te[i], 0, 0)
    return pl.pallas_call(
        _experts_kernel,
        grid_spec=pltpu.PrefetchScalarGridSpec(
            num_scalar_prefetch=2, grid=(n_chunks // tile_chunks,),
            in_specs=[pl.BlockSpec((tile_chunks, SEG_ROWS, width), slot_tile),
                      pl.BlockSpec((1, d, f), expert), pl.BlockSpec((1, d, f), expert),
                      pl.BlockSpec((1, f, d), expert)],
            out_specs=pl.BlockSpec((tile_chunks, SEG_ROWS, d), lambda i, te, nu: (i, 0, 0)),
            scratch_shapes=[pltpu.VMEM((d, f), BF16), pltpu.VMEM((d, f), BF16),
                            pltpu.VMEM((f, d), BF16)]),
        out_shape=jax.ShapeDtypeStruct((n_chunks, SEG_ROWS, d), BF16),
        compiler_params=pltpu.CompilerParams(dimension_semantics=("arbitrary",),
                                             vmem_limit_bytes=VMEM_LIMIT),
        name="experts",
    )(tile_expert, n_used, xs, w_e_gate, w_e_up, w_e_down)


def _dot_tn(a, b):
    return lax.dot_general(a, b, (((0,), (0,)), ((), ())), preferred_element_type=F32)


def _combine_kernel(base_ref, start_ref, nchunk_ref, units_ref, over_ref, hx_ref, pos_ref, x1_ref,
                    g2_ref, wsg_ref, wsu_ref, wsd_ref, ln_g_ref, ln_b_ref, ys_hbm, o_ref, win, extra,
                    acc_ref, sem):
    i = pl.program_id(0)
    n = pl.num_programs(0)
    n_tok = hx_ref.shape[0]

    def fetch(step, slot):
        for e in range(N_EXPERTS):
            _start_window(nchunk_ref[step * N_EXPERTS + e],
                          lambda k: pltpu.make_async_copy(
                              ys_hbm.at[_window_chunks(base_ref, start_ref, step, e, 0, k)],
                              win.at[slot, pl.ds(e * WIN_CHUNKS, k)], sem.at[slot]))

    @pl.when(i == 0)
    def _():
        win[...] = jnp.zeros_like(win)
        fetch(0, 0)

    @pl.when(i + 1 < n)
    def _():
        fetch(i + 1, (i + 1) % 2)

    slot = i % 2
    h = hx_ref[:, :D_MODEL]
    a = _silu(jnp.dot(h, wsg_ref[...], preferred_element_type=F32)) * jnp.dot(
        h, wsu_ref[...], preferred_element_type=F32)
    moe = jnp.dot(a.astype(BF16), wsd_ref[...], preferred_element_type=F32)
    _wait_half_windows(
        lambda k: pltpu.make_async_copy(ys_hbm.at[pl.ds(0, k * HALF_CHUNKS)],
                                        win.at[slot, pl.ds(0, k * HALF_CHUNKS)], sem.at[slot]),
        units_ref[i])
    d = win.shape[3]
    group_chunks = WIN_GROUP * WIN_CHUNKS
    for g in range(N_EXPERTS // WIN_GROUP):
        oh = jnp.concatenate(
            [_one_hot(pos_ref[0, e:e + 1, :], start_ref[i * N_EXPERTS + e], 0, n_tok)
             for e in range(g * WIN_GROUP, (g + 1) * WIN_GROUP)], axis=0)
        rows = win[slot, pl.ds(g * group_chunks, group_chunks)].reshape(WIN_GROUP * SEG_WIN, d)
        moe = moe + _dot_tn(oh, rows)
    acc_ref[...] = moe

    @pl.when(over_ref[i] > 0)
    def _():
        def more_windows(e, c):
            def one(window, c2):
                cp = pltpu.make_async_copy(
                    ys_hbm.at[_window_chunks(base_ref, start_ref, i, e, window)], extra, sem.at[2])
                cp.start()
                cp.wait()
                oh = _one_hot(pos_ref[0, pl.ds(e, 1), :], start_ref[i * N_EXPERTS + e], window,
                              n_tok)
                acc_ref[...] += _dot_tn(oh, extra[...].reshape(SEG_WIN, d))
                return c2
            return lax.fori_loop(1, _extra_windows(nchunk_ref, i, e), one, c)
        lax.fori_loop(0, N_EXPERTS, more_windows, 0)

    z = ALPHA * x1_ref[...] + g2_ref[0] * acc_ref[...]
    o_ref[...] = _layer_norm(z, ln_g_ref[...], ln_b_ref[...])


def _combine(base, seg_start, seg_chunks, seg_units, seg_over, hx, pos, x1, g2, w_s_gate, w_s_up,
             w_s_down, ln_g, ln_b, ys):
    t, d = x1.shape
    tpb = pos.shape[2] // TOK_TILE
    row = lambda i, *_: (i, 0)
    c2 = lambda i, *_: (0, 0)
    return pl.pallas_call(
        _combine_kernel,
        grid_spec=pltpu.PrefetchScalarGridSpec(
            num_scalar_prefetch=5, grid=(t // TOK_TILE,),
            in_specs=[pl.BlockSpec((TOK_TILE, hx.shape[1]), row),
                      pl.BlockSpec((1, N_EXPERTS, TOK_TILE), lambda i, *_: (i // tpb, 0, i % tpb)),
                      pl.BlockSpec((TOK_TILE, d), row),
                      pl.BlockSpec((1, 1, d), lambda i, *_: (i // tpb, 0, 0)),
                      pl.BlockSpec(w_s_gate.shape, c2), pl.BlockSpec(w_s_up.shape, c2),
                      pl.BlockSpec(w_s_down.shape, c2),
                      pl.BlockSpec(ln_g.shape, c2), pl.BlockSpec(ln_b.shape, c2),
                      pl.BlockSpec(memory_space=pl.ANY)],
            out_specs=pl.BlockSpec((TOK_TILE, d), row),
            scratch_shapes=[pltpu.VMEM((2, N_EXPERTS * WIN_CHUNKS, SEG_ROWS, d), BF16),
                            pltpu.VMEM((WIN_CHUNKS, SEG_ROWS, d), BF16),
                            pltpu.VMEM((TOK_TILE, d), F32),
                            pltpu.SemaphoreType.DMA((3,))]),
        out_shape=jax.ShapeDtypeStruct((t, d), F32),
        compiler_params=pltpu.CompilerParams(dimension_semantics=("arbitrary",),
                                             vmem_limit_bytes=MOE_VMEM_LIMIT),
        name="combine",
    )(base, seg_start, seg_chunks, seg_units, seg_over, hx, pos, x1, g2, w_s_gate, w_s_up, w_s_down,
      ln_g, ln_b, ys)


def _moe(hx, pos, seg, counts, x1, g2, w_e_gate, w_e_up, w_e_down, w_s_gate, w_s_up, w_s_down,
         ln_g, ln_b):
    b, s, d = x1.shape
    t = b * s
    n_seg = (t // TOK_TILE) * N_EXPERTS
    slack = PAIR_CHUNKS * SEG_ROWS
    max_rows = t * TOP_K + n_seg * (SEG_ROWS - 1) + N_EXPERTS * (slack + MOE_TM)
    n_tiles = -(-max_rows // MOE_TM)
    total = jnp.round(counts[:, 0]).astype(jnp.int32)
    pcnt = (total + slack + MOE_TM - 1) // MOE_TM * MOE_TM
    ends = jnp.cumsum(pcnt)
    base = ends - pcnt
    n_used = (ends[-1] // MOE_TM).reshape(1)
    tile_ids = jnp.arange(n_tiles, dtype=jnp.int32)
    tile_expert = jnp.sum((ends[None, :] <= tile_ids[:, None] * MOE_TM).astype(jnp.int32), axis=1)
    tile_expert = jnp.minimum(tile_expert, N_EXPERTS - 1)
    tile_expert = jnp.where(tile_ids < n_used, tile_expert, tile_expert[n_used[0] - 1])
    base, total, pcnt = base // SEG_ROWS, total // SEG_ROWS, pcnt // SEG_ROWS
    seg_start = seg[:, 0, :, 0].reshape(n_seg) // SEG_ROWS
    seg_chunks = seg[:, 1, :, 0]
    seg_over = (jnp.max(seg_chunks, axis=1) * SEG_ROWS > SEG_WIN).astype(jnp.int32)
    seg_units = jnp.sum(jnp.where(seg_chunks <= HALF_CHUNKS, 1, 2), axis=1).astype(jnp.int32)
    seg_chunks = seg_chunks.reshape(n_seg)

    hxf = hx.reshape(t, hx.shape[2])
    xs = _dispatch(base, seg_start, seg_chunks, seg_over, total, pcnt, n_used, hxf, pos,
                   n_tiles * MOE_TM)
    ys = _experts(tile_expert, n_used, xs, w_e_gate, w_e_up, w_e_down)
    out = _combine(base, seg_start, seg_chunks, seg_units, seg_over, hxf, pos, x1.reshape(t, d), g2,
                   w_s_gate, w_s_up, w_s_down, ln_g, ln_b, ys)
    return out.reshape(b, s, d)


def _rope_tables(seq):
    t = jnp.arange(seq)
    pos = jnp.stack([t // GRID_W, t % GRID_W], axis=-1).astype(F32)
    inv_freq = ROPE_THETA ** (-jnp.arange(ROPE_FREQS, dtype=F32) / ROPE_FREQS)
    ang = pos[:, :, None] * inv_freq
    cos, sin = jnp.cos(ang), jnp.sin(ang)
    zero = jnp.zeros_like(sin)
    cos_r = jnp.stack([cos, cos], axis=2).reshape(seq, QK_ROPE)
    sin_lo = jnp.stack([-sin, zero], axis=2).reshape(seq, QK_ROPE)
    sin_hi = jnp.stack([zero, sin], axis=2).reshape(seq, QK_ROPE)
    pads = ((0, 0), (QK_NOPE, HEAD_PAD - QK_NOPE - QK_ROPE))
    lane_tabs = (jnp.pad(cos_r, pads, constant_values=1.0), jnp.pad(sin_lo, pads),
                 jnp.pad(sin_hi, pads))
    row_tabs = (cos.reshape(seq, 2 * ROPE_FREQS).T, sin.reshape(seq, 2 * ROPE_FREQS).T)
    return lane_tabs, row_tabs


def _pad_heads(w, width, padded):
    k = w.shape[0]
    w = jnp.pad(w.reshape(k, N_HEADS, width), ((0, 0), (0, 0), (0, padded - width)))
    return w.reshape(k, N_HEADS * padded)


def kernel(x, c, ctx, c_ctx, w_ada, b_ada, w_in, q_norm_g, w_uq, kv_norm_g, w_ukv, w_pool, pool_scale, w_out, ln1_g, ln1_b, w_router, router_bias, w_e_gate, w_e_up, w_e_down, w_s_gate, w_s_up, w_s_down, ln2_g, ln2_b):
    assert w_ada.shape[0] == 1, "single-layer block"
    b, s, d = x.shape

    cvec = jnp.concatenate([c, c_ctx[None], jnp.zeros((SUBLANES - b - 1, d), F32)], axis=0)
    mod = _ada(cvec, w_ada[0], b_ada)
    sh1, sc1, g1, sh2, sc2, g2 = [mod[:b, k * d:(k + 1) * d][:, None, :] for k in range(6)]
    sh1c, sc1c = [jnp.broadcast_to(mod[b, k * d:(k + 1) * d], (b, 1, d)) for k in range(2)]

    wi = w_in[0]
    kr_cols = jnp.pad(wi[:, Q_LORA + KV_LORA:Q_LORA + KV_LORA + QK_ROPE],
                      ((0, 0), (QK_NOPE, HEAD_PAD - QK_NOPE - QK_ROPE)))
    w_in_r = jnp.concatenate([wi[:, :Q_LORA + KV_LORA], wi[:, Q_LORA + KV_LORA + QK_ROPE:], kr_cols],
                             axis=1).astype(BF16)
    w_uq_t = _pad_heads(w_uq[0], QK_NOPE + QK_ROPE, HEAD_PAD).T.astype(BF16)
    wkv = w_ukv[0].reshape(KV_LORA, N_HEADS, QK_NOPE + V_HEAD)
    w_uk_p = _pad_heads(wkv[:, :, :QK_NOPE].reshape(KV_LORA, -1), QK_NOPE, HEAD_PAD).astype(BF16)
    w_uv_t = _pad_heads(wkv[:, :, QK_NOPE:].reshape(KV_LORA, -1), V_HEAD, V_ROWS).T.astype(BF16)
    tables = _rope_tables(s)

    q_t, k, v_t, u = _proj(x, sc1, sh1, tables, w_in_r, q_norm_g, w_uq_t, kv_norm_g, w_uk_p,
                           w_uv_t, PROJ_TILE)
    kc, vc_t = _proj(ctx, sc1c, sh1c, None, w_in_r, None, None, kv_norm_g, w_uk_p, w_uv_t,
                     ctx.shape[1])
    attn = _attention(q_t, kc, vc_t, k, v_t)

    x1, hx, pos, seg, counts = _mix(attn, u, x, g1, sc2, sh2, w_pool[0].astype(BF16),
                                    pool_scale, w_out[0].astype(BF16), ln1_g, ln1_b,
                                    w_router[0].T, router_bias[0][:, None])
    return _moe(hx, pos, seg, counts, x1, g2, w_e_gate[0], w_e_up[0], w_e_down[0],
                w_s_gate[0].astype(BF16), w_s_up[0].astype(BF16), w_s_down[0].astype(BF16),
                ln2_g, ln2_b)
```

```python
import functools
import math

import jax
import jax.numpy as jnp
from jax import lax
from jax.experimental import pallas as pl
from jax.experimental.pallas import tpu as pltpu

F32 = jnp.float32
BF16 = jnp.bfloat16

D_MODEL = 1024
GRID_W = 64
N_HEADS = 8
Q_LORA = 512
KV_LORA = 256
QK_NOPE = 64
QK_ROPE = 32
V_HEAD = 64
ROPE_FREQS = QK_ROPE // 4
ROPE_THETA = 10000.0
ATTN_SCALE = 1.0 / math.sqrt(QK_NOPE + QK_ROPE)
LOG2_E = math.log2(math.e)
POOL_GROUPS = 4
POOL_WINDOWS = (2, 4, 8, 16)
POOL_WIDTH = 512
POOL_GC = POOL_WIDTH // POOL_GROUPS
POOL_HALO = 8
N_EXPERTS = 64
N_EXPERT_GROUPS = 8
GROUP_SIZE = N_EXPERTS // N_EXPERT_GROUPS
TOPK_GROUPS = 4
TOP_K = 8
D_EXPERT = 256
ROUTED_SCALE = 2.5
LN_EPS = 1e-5
RMS_EPS = 1e-6
ALPHA = 2.0 ** 0.25

LANES = 128
SUBLANES = 8
HEAD_PAD = LANES
V_ROWS = 80
ONES_ROW = V_HEAD
IN_PAD = Q_LORA + KV_LORA + POOL_WIDTH + LANES

PROJ_TILE = 512
ATTN_TQ = 512
ATTN_TK = PROJ_TILE
ATTN_SUB = 256
ATTN_AHEAD = 2
ATTN_UNROLL = 16
TOK_TILE = 256
MIX_TILE = 2 * TOK_TILE
MOE_TM = 1024
SEG_ROWS = 16
SEG_WIN = 64
WIN_CHUNKS = SEG_WIN // SEG_ROWS
HALF_CHUNKS = WIN_CHUNKS // 2
WIN_GROUP = 16
HX_WIDTH = D_MODEL + 2 * N_EXPERTS
VMEM_LIMIT = 48 * 1024 * 1024
NEG_BIG = -1e30


def _silu(v):
    return v * jax.nn.sigmoid(v)


def _layer_norm(z, g, b):
    mu = jnp.mean(z, axis=-1, keepdims=True)
    zc = z - mu
    var = jnp.mean(zc * zc, axis=-1, keepdims=True)
    return zc * lax.rsqrt(var + LN_EPS) * g + b


def _rms_norm(v, g):
    return v * lax.rsqrt(jnp.mean(v * v, axis=-1, keepdims=True) + RMS_EPS) * g


def _dot_nt(a, b):
    return lax.dot_general(a, b, (((1,), (1,)), ((), ())), preferred_element_type=F32)


def _ada_kernel(c_ref, w_ref, b_ref, o_ref):
    cv = _silu(c_ref[...])
    o_ref[...] = jnp.dot(cv, w_ref[...], preferred_element_type=F32,
                         precision=lax.Precision.HIGHEST) + b_ref[...]


def _ada(cvec, w_ada, b_ada):
    rows, d = cvec.shape
    n = w_ada.shape[1]
    tn = 1024
    return pl.pallas_call(
        _ada_kernel,
        grid=(n // tn,),
        in_specs=[pl.BlockSpec((rows, d), lambda j: (0, 0)),
                  pl.BlockSpec((d, tn), lambda j: (0, j)),
                  pl.BlockSpec((1, tn), lambda j: (0, j))],
        out_specs=pl.BlockSpec((rows, tn), lambda j: (0, j)),
        out_shape=jax.ShapeDtypeStruct((rows, n), F32),
        compiler_params=pltpu.CompilerParams(dimension_semantics=("arbitrary",),
                                             vmem_limit_bytes=VMEM_LIMIT),
        name="ada",
    )(cvec, w_ada, b_ada)


def _rope_lanes(v, cos, sin_lo, sin_hi):
    return v * cos + pltpu.roll(v, LANES - 8, axis=1) * sin_lo + pltpu.roll(v, 8, axis=1) * sin_hi


def _proj_kernel(*refs, with_q):
    if with_q:
        (x_ref, sc_ref, sh_ref, cos_ref, slo_ref, shi_ref, cos_t_ref, sin_t_ref, win_ref, qg_ref,
         wuq_ref, kvg_ref, wuk_ref, wuv_ref, q_ref, k_ref, v_ref, u_ref) = refs
    else:
        (x_ref, sc_ref, sh_ref, win_ref, kvg_ref, wuk_ref, wuv_ref, k_ref, v_ref) = refs
    h = (x_ref[0] * (1.0 + sc_ref[0]) + sh_ref[0]).astype(BF16)
    p = jnp.dot(h, win_ref[...], preferred_element_type=F32)
    tile = p.shape[0]
    kr = p[:, IN_PAD - LANES:]
    kvn = _rms_norm(p[:, Q_LORA:Q_LORA + KV_LORA], kvg_ref[...]).astype(BF16)
    kfull = jnp.dot(kvn, wuk_ref[...], preferred_element_type=F32)
    v_t = _dot_nt(wuv_ref[...], kvn)
    row = lax.broadcasted_iota(jnp.int32, (N_HEADS * V_ROWS, 1), 0)
    v_t = v_t + (row % V_ROWS == ONES_ROW).astype(F32)
    if with_q:
        kr = _rope_lanes(kr, cos_ref[...], slo_ref[...], shi_ref[...])
        u_ref[0] = p[:, Q_LORA + KV_LORA:Q_LORA + KV_LORA + POOL_WIDTH]
        qn = _rms_norm(p[:, :Q_LORA], qg_ref[...]).astype(BF16)
        q_t = _dot_nt(wuq_ref[...], qn) * (ATTN_SCALE * LOG2_E)
        cos_t, sin_t = cos_t_ref[...], sin_t_ref[...]
    for hd in range(N_HEADS):
        k_ref[0, hd] = (kfull[:, hd * HEAD_PAD:(hd + 1) * HEAD_PAD] + kr).astype(BF16)
        v_ref[0, hd, 0] = v_t[hd * V_ROWS:(hd + 1) * V_ROWS].astype(BF16)
        if with_q:
            base = hd * HEAD_PAD
            q_ref[0, hd, 0:QK_NOPE, :] = q_t[base:base + QK_NOPE].astype(BF16)
            rope = []
            for ax in range(2):
                lo = q_t[base + QK_NOPE + 16 * ax:base + QK_NOPE + 16 * ax + 8]
                hi = q_t[base + QK_NOPE + 16 * ax + 8:base + QK_NOPE + 16 * ax + 16]
                cs, sn = cos_t[8 * ax:8 * ax + 8], sin_t[8 * ax:8 * ax + 8]
                rope += [lo * cs - hi * sn, hi * cs + lo * sn]
            rope.append(jnp.zeros((HEAD_PAD - QK_NOPE - QK_ROPE, tile), F32))
            q_ref[0, hd, QK_NOPE:, :] = jnp.concatenate(rope, axis=0).astype(BF16)


def _proj(x, sc, sh, tables, w_in_r, q_g, w_uq_t, kv_g, w_uk_p, w_uv_t, tile):
    b, s, d = x.shape
    with_q = tables is not None
    grid = (b, s // tile)
    row = lambda bi, i: (bi, i, 0)
    vec = lambda bi, i: (bi, 0, 0)
    const2 = lambda bi, i: (0, 0)
    k_out = pl.BlockSpec((1, N_HEADS, tile, HEAD_PAD), lambda bi, i: (bi, 0, i, 0))
    k_shape = jax.ShapeDtypeStruct((b, N_HEADS, s, HEAD_PAD), BF16)
    v_out = pl.BlockSpec((1, N_HEADS, 1, V_ROWS, tile), lambda bi, i: (bi, 0, i, 0, 0))
    v_shape = jax.ShapeDtypeStruct((b, N_HEADS, s // tile, V_ROWS, tile), BF16)
    in_specs = [pl.BlockSpec((1, tile, d), row),
                pl.BlockSpec((1, 1, d), vec), pl.BlockSpec((1, 1, d), vec)]
    args = [x, sc, sh]
    if with_q:
        lane_tabs, row_tabs = tables
        in_specs += [pl.BlockSpec((tile, LANES), lambda bi, i: (i, 0))] * 3
        in_specs += [pl.BlockSpec((2 * ROPE_FREQS, tile), lambda bi, i: (0, i))] * 2
        args += list(lane_tabs) + list(row_tabs)
    in_specs.append(pl.BlockSpec(w_in_r.shape, const2)); args.append(w_in_r)
    if with_q:
        in_specs += [pl.BlockSpec(q_g.shape, const2), pl.BlockSpec(w_uq_t.shape, const2)]
        args += [q_g, w_uq_t]
    in_specs += [pl.BlockSpec(kv_g.shape, const2), pl.BlockSpec(w_uk_p.shape, const2),
                 pl.BlockSpec(w_uv_t.shape, const2)]
    args += [kv_g, w_uk_p, w_uv_t]
    if with_q:
        q_out = pl.BlockSpec((1, N_HEADS, HEAD_PAD, tile), lambda bi, i: (bi, 0, 0, i))
        q_shape = jax.ShapeDtypeStruct((b, N_HEADS, HEAD_PAD, s), BF16)
        out_specs = [q_out, k_out, v_out, pl.BlockSpec((1, tile, POOL_WIDTH), row)]
        out_shape = [q_shape, k_shape, v_shape, jax.ShapeDtypeStruct((b, s, POOL_WIDTH), F32)]
    else:
        out_specs = [k_out, v_out]
        out_shape = [k_shape, v_shape]
    return pl.pallas_call(
        functools.partial(_proj_kernel, with_q=with_q),
        grid=grid, in_specs=in_specs, out_specs=out_specs, out_shape=out_shape,
        compiler_params=pltpu.CompilerParams(dimension_semantics=("arbitrary", "arbitrary"),
                                             vmem_limit_bytes=VMEM_LIMIT),
        name="proj" if with_q else "proj_ctx",
    )(*args)


def _attn_kernel(q_ref, kc_ref, vc_ref, k_ref, v_ref, o_ref, *, n_kblk, tk):
    tq = q_ref.shape[3]
    qs = [q_ref[0, hh] for hh in range(2)]

    def scores(hh, kb):
        return jnp.dot(kb, qs[hh], preferred_element_type=F32)

    def update(s_t, vb_t, m, acc):
        m_new = jnp.maximum(m, jnp.max(s_t, axis=0, keepdims=True))
        p_t = jnp.exp2(s_t - m_new).astype(BF16)
        acc = jnp.exp2(m - m_new) * acc + jnp.dot(vb_t, p_t, preferred_element_type=F32)
        return m_new, acc

    def run_items(items, state):
        pending = [scores(hh, kb()) for hh, kb, _ in items[:ATTN_AHEAD]]
        for j, (hh, _, vb) in enumerate(items):
            if j + ATTN_AHEAD < len(items):
                nh, nkb, _ = items[j + ATTN_AHEAD]
                pending.append(scores(nh, nkb()))
            state[hh] = update(pending.pop(0), vb(), *state[hh])
        return state

    def block_items(blk, off):
        out = []
        for sub in range(tk // ATTN_SUB):
            for hh in range(2):
                lo = sub * ATTN_SUB
                out.append((hh,
                            lambda hh=hh, lo=lo: k_ref[0, hh, pl.ds(off + lo, ATTN_SUB), :],
                            lambda hh=hh, lo=lo: v_ref[0, hh, blk, :, lo:lo + ATTN_SUB]))
        return out

    ctx_items = [(hh, lambda hh=hh: kc_ref[0, hh], lambda hh=hh: vc_ref[0, hh, 0])
                 for hh in range(2)]
    state = [(jnp.full((1, tq), NEG_BIG, F32), jnp.zeros((V_ROWS, tq), F32)) for _ in range(2)]
    n_iter = n_kblk // ATTN_UNROLL
    if n_iter == 1:
        items = ctx_items
        for blk in range(n_kblk):
            items = items + block_items(blk, blk * tk)
        state = run_items(items, state)
    else:
        state = run_items(ctx_items, state)

        def body(i, carry):
            items = []
            for r in range(ATTN_UNROLL):
                blk = i * ATTN_UNROLL + r
                items += block_items(blk, pl.multiple_of(blk * tk, tk))
            st = run_items(items, [(carry[0], carry[1]), (carry[2], carry[3])])
            return st[0] + st[1]

        carry = lax.fori_loop(0, n_iter, body, state[0] + state[1])
        state = [(carry[0], carry[1]), (carry[2], carry[3])]
    carry = state[0] + state[1]
    outs = [carry[2 * hh + 1][:V_HEAD] / carry[2 * hh + 1][ONES_ROW:ONES_ROW + 1] for hh in range(2)]
    o_ref[0] = jnp.concatenate(outs, axis=0).T.astype(o_ref.dtype)


def _attention(q_t, kc, vc_t, k, v_t):
    b, nh, dp, s = q_t.shape
    c = kc.shape[2]
    tq, tk = ATTN_TQ, ATTN_TK
    n_kblk = s // tk
    kern = functools.partial(_attn_kernel, n_kblk=n_kblk, tk=tk)
    return pl.pallas_call(
        kern,
        grid=(b, nh // 2, s // tq),
        in_specs=[pl.BlockSpec((1, 2, dp, tq), lambda bi, hp, qi: (bi, hp, 0, qi)),
                  pl.BlockSpec((1, 2, c, dp), lambda bi, hp, qi: (bi, hp, 0, 0)),
                  pl.BlockSpec((1, 2, 1, V_ROWS, c), lambda bi, hp, qi: (bi, hp, 0, 0, 0)),
                  pl.BlockSpec((1, 2, s, dp), lambda bi, hp, qi: (bi, hp, 0, 0)),
                  pl.BlockSpec((1, 2, n_kblk, V_ROWS, tk), lambda bi, hp, qi: (bi, hp, 0, 0, 0))],
        out_specs=pl.BlockSpec((1, tq, 2 * V_HEAD), lambda bi, hp, qi: (bi, qi, hp)),
        out_shape=jax.ShapeDtypeStruct((b, s, nh * V_HEAD), BF16),
        compiler_params=pltpu.CompilerParams(
            dimension_semantics=("arbitrary", "arbitrary", "arbitrary"),
            vmem_limit_bytes=VMEM_LIMIT),
        name="attn",
    )(q_t, kc, vc_t, k, v_t)


def _route(logits_t, bias_t):
    e, t = logits_t.shape
    scores = jax.nn.sigmoid(logits_t)
    biased = scores + bias_t
    neg_inf = F32(-jnp.inf)
    gscore = []
    for g in range(N_EXPERT_GROUPS):
        v = biased[g * GROUP_SIZE:(g + 1) * GROUP_SIZE]
        m1 = jnp.max(v, axis=0, keepdims=True)
        at_max = v == m1
        n_max = jnp.sum(at_max.astype(F32), axis=0, keepdims=True)
        m2 = jnp.max(jnp.where(at_max, neg_inf, v), axis=0, keepdims=True)
        gscore.append(m1 + jnp.where(n_max >= 2.0, m1, m2))
    masked = []
    for g in range(N_EXPERT_GROUPS):
        rank = jnp.zeros((1, t), F32)
        for o in range(N_EXPERT_GROUPS):
            if o == g:
                continue
            beats = (gscore[o] >= gscore[g]) if o < g else (gscore[o] > gscore[g])
            rank = rank + beats.astype(F32)
        keep = rank < float(TOPK_GROUPS)
        masked.append(jnp.where(keep, biased[g * GROUP_SIZE:(g + 1) * GROUP_SIZE], neg_inf))
    work = jnp.concatenate(masked, axis=0)
    rows = lax.broadcasted_iota(jnp.int32, (e, t), 0)
    sel = jnp.zeros((e, t), F32)
    for _ in range(TOP_K):
        m = jnp.max(work, axis=0, keepdims=True)
        first = jnp.min(jnp.where(work == m, rows, e), axis=0, keepdims=True)
        pick = rows == first
        sel = jnp.where(pick, 1.0, sel)
        work = jnp.where(pick, neg_inf, work)
    w = sel * scores
    gates = w / jnp.sum(w, axis=0, keepdims=True) * ROUTED_SCALE
    return sel, gates


def _mix_kernel(attn_ref, u_ref, up_ref, un_ref, x_ref, g1_ref, sc2_ref, sh2_ref, wpool_ref,
                pscale_ref, wout_ref, ln_g_ref, ln_b_ref, wr_ref, rb_ref, tri_ref, ones_ref,
                x1_ref, hx_ref, pos_ref, seg_ref, cnt_ref, uext_ref, *, seq):
    i = pl.program_id(1)

    @pl.when((pl.program_id(0) == 0) & (i == 0))
    def _():
        cnt_ref[...] = jnp.zeros_like(cnt_ref)

    tile = u_ref.shape[1]
    u = u_ref[0]
    uext_ref[0:POOL_HALO] = jnp.where(i == 0, 0.0, up_ref[0])
    uext_ref[POOL_HALO:POOL_HALO + tile] = u
    uext_ref[POOL_HALO + tile:] = jnp.where(i == pl.num_programs(1) - 1, 0.0, un_ref[0])
    t = i * tile + lax.broadcasted_iota(jnp.int32, (tile, POOL_GC), 0)
    pooled = []
    for g, w in enumerate(POOL_WINDOWS):
        lanes = slice(g * POOL_GC, (g + 1) * POOL_GC)
        tot = uext_ref[POOL_HALO - w // 2:POOL_HALO - w // 2 + tile, lanes]
        for dlt in range(1, w):
            start = POOL_HALO - w // 2 + dlt
            tot = tot + uext_ref[start:start + tile, lanes]
        cnt = (jnp.minimum(t - w // 2 + w, seq) - jnp.maximum(t - w // 2, 0)).astype(F32)
        pg = (tot / cnt - u[:, lanes]).astype(BF16)
        po = jnp.dot(pg, wpool_ref[g], preferred_element_type=F32) * pscale_ref[:, lanes]
        pooled.append(po.astype(BF16))
    mixed = jnp.concatenate([attn_ref[0]] + pooled, axis=1)
    y = jnp.dot(mixed, wout_ref[...], preferred_element_type=F32)
    x1 = _layer_norm(ALPHA * x_ref[0] + g1_ref[0] * y, ln_g_ref[...], ln_b_ref[...])
    x1_ref[0] = x1
    h2 = x1 * (1.0 + sc2_ref[0]) + sh2_ref[0]
    h_hi = h2.astype(BF16)
    h_lo = (h2 - h_hi.astype(F32)).astype(BF16)
    w_r = wr_ref[...]
    w_hi = w_r.astype(BF16)
    w_lo = (w_r - w_hi.astype(F32)).astype(BF16)
    logits_t = _dot_nt(w_hi, h_hi) + (_dot_nt(w_hi, h_lo) + _dot_nt(w_lo, h_hi))
    sel, gates_t = _route(logits_t, rb_ref[...])
    g_hi = gates_t.astype(BF16)
    g_lo = (gates_t - g_hi.astype(F32)).astype(BF16)
    g_tok = jnp.concatenate([g_hi.astype(F32), g_lo.astype(F32)], axis=0).T
    hx_ref[0] = jnp.concatenate([h2.astype(BF16), g_tok.astype(BF16)], axis=1)
    start = cnt_ref[...]
    for k in range(tile // TOK_TILE):
        cols = slice(k * TOK_TILE, (k + 1) * TOK_TILE)
        sel_k = sel[:, cols]
        sel_b = sel_k.astype(BF16)
        pos_t = jnp.dot(sel_b, tri_ref[...], preferred_element_type=F32) + start[:, 0:1]
        pos_ref[0, :, cols] = jnp.where(sel_k > 0.0, pos_t, -1.0).astype(jnp.int32)
        n_tok = jnp.dot(sel_b, ones_ref[...], preferred_element_type=F32)
        n_chunk = jnp.floor((n_tok + (SEG_ROWS - 1)) * (1.0 / SEG_ROWS))
        seg_ref[k, 0] = start.astype(jnp.int32)
        seg_ref[k, 1] = n_chunk.astype(jnp.int32)
        start = start + n_chunk * SEG_ROWS
    cnt_ref[...] = start


def _mix(attn, u, x, g1, sc2, sh2, w_pool, pool_scale, w_out, ln_g, ln_b, w_r_t, rb_t):
    b, s, d = x.shape
    tile = MIX_TILE
    hb = tile // POOL_HALO
    row = lambda bi, i: (bi, i, 0)
    vec = lambda bi, i: (bi, 0, 0)
    c2 = lambda bi, i: (0, 0)
    lane_row = lambda bi, i: (bi, 0, i)
    tri = (lax.broadcasted_iota(jnp.int32, (TOK_TILE, TOK_TILE), 0)
           < lax.broadcasted_iota(jnp.int32, (TOK_TILE, TOK_TILE), 1)).astype(BF16)
    ones = jnp.ones((TOK_TILE, LANES), BF16)
    seg_per_step = tile // TOK_TILE
    return pl.pallas_call(
        functools.partial(_mix_kernel, seq=s),
        grid=(b, s // tile),
        in_specs=[pl.BlockSpec((1, tile, POOL_WIDTH), row),
                  pl.BlockSpec((1, tile, POOL_WIDTH), row),
                  pl.BlockSpec((1, POOL_HALO, POOL_WIDTH),
                               lambda bi, i: (bi, jnp.maximum(i * hb - 1, 0), 0)),
                  pl.BlockSpec((1, POOL_HALO, POOL_WIDTH),
                               lambda bi, i: (bi, jnp.minimum((i + 1) * hb, s // POOL_HALO - 1), 0)),
                  pl.BlockSpec((1, tile, d), row),
                  pl.BlockSpec((1, 1, d), vec), pl.BlockSpec((1, 1, d), vec),
                  pl.BlockSpec((1, 1, d), vec),
                  pl.BlockSpec(w_pool.shape, lambda bi, i: (0, 0, 0)),
                  pl.BlockSpec(pool_scale.shape, c2),
                  pl.BlockSpec(w_out.shape, c2),
                  pl.BlockSpec(ln_g.shape, c2), pl.BlockSpec(ln_b.shape, c2),
                  pl.BlockSpec(w_r_t.shape, c2), pl.BlockSpec(rb_t.shape, c2),
                  pl.BlockSpec(tri.shape, c2), pl.BlockSpec(ones.shape, c2)],
        out_specs=[pl.BlockSpec((1, tile, d), row),
                   pl.BlockSpec((1, tile, HX_WIDTH), row),
                   pl.BlockSpec((1, N_EXPERTS, tile), lane_row),
                   pl.BlockSpec((seg_per_step, 2, N_EXPERTS, LANES),
                                lambda bi, i: (bi * (s // tile) + i, 0, 0, 0)),
                   pl.BlockSpec((N_EXPERTS, LANES), c2)],
        out_shape=[jax.ShapeDtypeStruct((b, s, d), F32),
                   jax.ShapeDtypeStruct((b, s, HX_WIDTH), BF16),
                   jax.ShapeDtypeStruct((b, N_EXPERTS, s), jnp.int32),
                   jax.ShapeDtypeStruct((b * (s // TOK_TILE), 2, N_EXPERTS, LANES), jnp.int32),
                   jax.ShapeDtypeStruct((N_EXPERTS, LANES), F32)],
        scratch_shapes=[pltpu.VMEM((tile + 2 * POOL_HALO, POOL_WIDTH), F32)],
        compiler_params=pltpu.CompilerParams(dimension_semantics=("arbitrary", "arbitrary"),
                                             vmem_limit_bytes=VMEM_LIMIT),
        name="mix",
    )(attn, u, u, u, x, g1, sc2, sh2, w_pool, pool_scale, w_out, ln_g, ln_b, w_r_t, rb_t,
      tri, ones)


def _one_hot(pos_row, start, window, n_tok):
    rows = lax.broadcasted_iota(jnp.int32, (SEG_WIN, n_tok), 0)
    hit = rows == (pos_row - (start * SEG_ROWS + window * SEG_WIN))
    return jnp.where(hit, 1.0, 0.0).astype(BF16)


def _window_chunks(base_ref, start_ref, step, e, window, n_chunks=WIN_CHUNKS):
    return pl.ds(base_ref[e] + start_ref[step * N_EXPERTS + e] + window * WIN_CHUNKS, n_chunks)


def _start_window(seg_chunks, make_copy):
    @pl.when(seg_chunks <= HALF_CHUNKS)
    def _():
        make_copy(HALF_CHUNKS).start()

    @pl.when(seg_chunks > HALF_CHUNKS)
    def _():
        make_copy(WIN_CHUNKS).start()


def _wait_half_windows(make_copy, units):
    make_copy(N_EXPERTS).wait()
    rest = units - N_EXPERTS
    for bit in range(N_EXPERTS.bit_length()):
        @pl.when((rest >> bit) & 1 == 1)
        def _():
            make_copy(1 << bit).wait()


def _extra_windows(nchunk_ref, step, e):
    rows = nchunk_ref[step * N_EXPERTS + e] * SEG_ROWS
    return jnp.maximum((rows + SEG_WIN - 1) // SEG_WIN, 1)


def _dispatch_kernel(base_ref, start_ref, nchunk_ref, units_ref, over_ref, total_ref, pcnt_ref,
                     nu_ref, hx_ref, pos_ref, xs_hbm, stage, extra, zero_ref, sem):
    i = pl.program_id(0)
    tile_chunks = MOE_TM // SEG_ROWS
    n_tiles = xs_hbm.shape[0] // tile_chunks
    n_tok, width = hx_ref.shape

    def zero_copy(chunk, n_chunks):
        return pltpu.make_async_copy(zero_ref.at[pl.ds(0, n_chunks)],
                                     xs_hbm.at[pl.ds(chunk, n_chunks)], sem.at[1])

    @pl.when(i == 0)
    def _():
        zero_ref[...] = jnp.zeros_like(zero_ref)
        for wait in (False, True):
            def per_tail(j, c):
                cp = zero_copy(j * tile_chunks, tile_chunks)
                cp.wait() if wait else cp.start()
                return c
            lax.fori_loop(nu_ref[0], n_tiles, per_tail, 0)

            def per_expert(e, c):
                def per_chunk(j, c2):
                    cp = zero_copy(base_ref[e] + j, 1)
                    cp.wait() if wait else cp.start()
                    return c2
                return lax.fori_loop(total_ref[e], pcnt_ref[e], per_chunk, c)
            lax.fori_loop(0, N_EXPERTS, per_expert, 0)

    slot = i % 2
    hx = hx_ref[...]
    group_chunks = WIN_GROUP * WIN_CHUNKS
    for g in range(N_EXPERTS // WIN_GROUP):
        oh = jnp.concatenate(
            [_one_hot(pos_ref[0, e:e + 1, :], start_ref[i * N_EXPERTS + e], 0, n_tok)
             for e in range(g * WIN_GROUP, (g + 1) * WIN_GROUP)], axis=0)
        rows = jnp.dot(oh, hx, preferred_element_type=F32).astype(BF16)
        stage[slot, pl.ds(g * group_chunks, group_chunks)] = rows.reshape(group_chunks, SEG_ROWS,
                                                                         width)

    def wait_windows(step, which):
        _wait_half_windows(
            lambda k: pltpu.make_async_copy(stage.at[which, pl.ds(0, k * HALF_CHUNKS)],
                                            xs_hbm.at[pl.ds(0, k * HALF_CHUNKS)], sem.at[0]),
            units_ref[step])

    @pl.when(i > 0)
    def _():
        wait_windows(i - 1, 1 - slot)

    for e in range(N_EXPERTS):
        _start_window(nchunk_ref[i * N_EXPERTS + e],
                      lambda n: pltpu.make_async_copy(
                          stage.at[slot, pl.ds(e * WIN_CHUNKS, n)],
                          xs_hbm.at[_window_chunks(base_ref, start_ref, i, e, 0, n)], sem.at[0]))

    @pl.when(i == pl.num_programs(0) - 1)
    def _():
        wait_windows(i, slot)

    @pl.when(over_ref[i] > 0)
    def _():
        def more_windows(e, c):
            def one(window, c2):
                oh = _one_hot(pos_ref[0, pl.ds(e, 1), :], start_ref[i * N_EXPERTS + e], window,
                              n_tok)
                rows = jnp.dot(oh, hx, preferred_element_type=F32).astype(BF16)
                extra[...] = rows.reshape(WIN_CHUNKS, SEG_ROWS, width)
                cp = pltpu.make_async_copy(
                    extra, xs_hbm.at[_window_chunks(base_ref, start_ref, i, e, window)], sem.at[2])
                cp.start()
                cp.wait()
                return c2
            return lax.fori_loop(1, _extra_windows(nchunk_ref, i, e), one, c)
        lax.fori_loop(0, N_EXPERTS, more_windows, 0)


def _dispatch(base, seg_start, seg_chunks, seg_units, seg_over, total, pcnt, n_used, hx, pos,
              n_slots):
    t, width = hx.shape
    tpb = pos.shape[2] // TOK_TILE
    return pl.pallas_call(
        _dispatch_kernel,
        grid_spec=pltpu.PrefetchScalarGridSpec(
            num_scalar_prefetch=8, grid=(t // TOK_TILE,),
            in_specs=[pl.BlockSpec((TOK_TILE, width), lambda i, *_: (i, 0)),
                      pl.BlockSpec((1, N_EXPERTS, TOK_TILE), lambda i, *_: (i // tpb, 0, i % tpb))],
            out_specs=pl.BlockSpec(memory_space=pl.ANY),
            scratch_shapes=[pltpu.VMEM((2, N_EXPERTS * WIN_CHUNKS, SEG_ROWS, width), BF16),
                            pltpu.VMEM((WIN_CHUNKS, SEG_ROWS, width), BF16),
                            pltpu.VMEM((MOE_TM // SEG_ROWS, SEG_ROWS, width), BF16),
                            pltpu.SemaphoreType.DMA((3,))]),
        out_shape=jax.ShapeDtypeStruct((n_slots // SEG_ROWS, SEG_ROWS, width), BF16),
        compiler_params=pltpu.CompilerParams(dimension_semantics=("arbitrary",),
                                             vmem_limit_bytes=VMEM_LIMIT),
        name="dispatch",
    )(base, seg_start, seg_chunks, seg_units, seg_over, total, pcnt, n_used, hx, pos)


def _experts_kernel(te_ref, nu_ref, xs_ref, wg_ref, wu_ref, wd_ref, ys_ref, wg_b, wu_b, wd_b):
    i = pl.program_id(0)

    @pl.when(i < nu_ref[0])
    def _():
        @pl.when((i == 0) | (te_ref[i] != te_ref[jnp.maximum(i - 1, 0)]))
        def _():
            wg_b[...] = wg_ref[0].astype(BF16)
            wu_b[...] = wu_ref[0].astype(BF16)
            wd_b[...] = wd_ref[0].astype(BF16)

        half = MOE_TM // 2
        half_chunks = half // SEG_ROWS
        halves = [pl.ds(k * half_chunks, half_chunks) for k in range(2)]
        hidden = []
        for chunks in halves:
            xg = xs_ref[chunks].reshape(half, xs_ref.shape[2])
            x = xg[:, :D_MODEL]
            hidden.append((xg[:, D_MODEL:].astype(F32),
                           jnp.dot(x, wg_b[...], preferred_element_type=F32),
                           jnp.dot(x, wu_b[...], preferred_element_type=F32)))
        for chunks, (g, hg, hu) in zip(halves, hidden):
            lane = lax.broadcasted_iota(jnp.int32, g.shape, 1)
            mine = (lane == te_ref[i]) | (lane == te_ref[i] + N_EXPERTS)
            gate = jnp.sum(jnp.where(mine, g, 0.0), axis=1, keepdims=True)
            a = (_silu(hg) * hu * gate).astype(BF16)
            y = jnp.dot(a, wd_b[...], preferred_element_type=F32).astype(ys_ref.dtype)
            ys_ref[chunks] = y.reshape(half_chunks, SEG_ROWS, ys_ref.shape[2])

    @pl.when(i >= nu_ref[0])
    def _():
        ys_ref[...] = jnp.zeros_like(ys_ref)


def _experts(tile_expert, n_used, xs, w_e_gate, w_e_up, w_e_down):
    n_chunks, _, width = xs.shape
    _, d, f = w_e_gate.shape
    tile_chunks = MOE_TM // SEG_ROWS
    slot_tile = lambda i, te, nu: (jnp.minimum(i, nu[0] - 1), 0, 0)
    expert = lambda i, te, nu: (te[i], 0, 0)
    return pl.pallas_call(
        _experts_kernel,
        grid_spec=pltpu.PrefetchScalarGridSpec(
            num_scalar_prefetch=2, grid=(n_chunks // tile_chunks,),
            in_specs=[pl.BlockSpec((tile_chunks, SEG_ROWS, width), slot_tile),
                      pl.BlockSpec((1, d, f), expert), pl.BlockSpec((1, d, f), expert),
                      pl.BlockSpec((1, f, d), expert)],
            out_specs=pl.BlockSpec((tile_chunks, SEG_ROWS, d), lambda i, te, nu: (i, 0, 0)),
            scratch_shapes=[pltpu.VMEM((d, f), BF16), pltpu.VMEM((d, f), BF16),
                            pltpu.VMEM((f, d), BF16)]),
        out_shape=jax.ShapeDtypeStruct((n_chunks, SEG_ROWS, d), BF16),
        compiler_params=pltpu.CompilerParams(dimension_semantics=("arbitrary",),
                                             vmem_limit_bytes=VMEM_LIMIT),
        name="experts",
    )(tile_expert, n_used, xs, w_e_gate, w_e_up, w_e_down)


def _dot_tn(a, b):
    return lax.dot_general(a, b, (((0,), (0,)), ((), ())), preferred_element_type=F32)


def _combine_kernel(base_ref, start_ref, nchunk_ref, units_ref, over_ref, hx_ref, pos_ref, x1_ref,
                    g2_ref, wsg_ref, wsu_ref, wsd_ref, ln_g_ref, ln_b_ref, ys_hbm, o_ref, win, extra,
                    acc_ref, sem):
    i = pl.program_id(0)
    n = pl.num_programs(0)
    n_tok = hx_ref.shape[0]

    def fetch(step, slot):
        for e in range(N_EXPERTS):
            _start_window(nchunk_ref[step * N_EXPERTS + e],
                          lambda k: pltpu.make_async_copy(
                              ys_hbm.at[_window_chunks(base_ref, start_ref, step, e, 0, k)],
                              win.at[slot, pl.ds(e * WIN_CHUNKS, k)], sem.at[slot]))

    @pl.when(i == 0)
    def _():
        win[...] = jnp.zeros_like(win)
        fetch(0, 0)

    @pl.when(i + 1 < n)
    def _():
        fetch(i + 1, (i + 1) % 2)

    slot = i % 2
    h = hx_ref[:, :D_MODEL]
    a = _silu(jnp.dot(h, wsg_ref[...], preferred_element_type=F32)) * jnp.dot(
        h, wsu_ref[...], preferred_element_type=F32)
    moe = jnp.dot(a.astype(BF16), wsd_ref[...], preferred_element_type=F32)
    _wait_half_windows(
        lambda k: pltpu.make_async_copy(ys_hbm.at[pl.ds(0, k * HALF_CHUNKS)],
                                        win.at[slot, pl.ds(0, k * HALF_CHUNKS)], sem.at[slot]),
        units_ref[i])
    d = win.shape[3]
    group_chunks = WIN_GROUP * WIN_CHUNKS
    for g in range(N_EXPERTS // WIN_GROUP):
        oh = jnp.concatenate(
            [_one_hot(pos_ref[0, e:e + 1, :], start_ref[i * N_EXPERTS + e], 0, n_tok)
             for e in range(g * WIN_GROUP, (g + 1) * WIN_GROUP)], axis=0)
        rows = win[slot, pl.ds(g * group_chunks, group_chunks)].reshape(WIN_GROUP * SEG_WIN, d)
        moe = moe + _dot_tn(oh, rows)
    acc_ref[...] = moe

    @pl.when(over_ref[i] > 0)
    def _():
        def more_windows(e, c):
            def one(window, c2):
                cp = pltpu.make_async_copy(
                    ys_hbm.at[_window_chunks(base_ref, start_ref, i, e, window)], extra, sem.at[2])
                cp.start()
                cp.wait()
                oh = _one_hot(pos_ref[0, pl.ds(e, 1), :], start_ref[i * N_EXPERTS + e], window,
                              n_tok)
                acc_ref[...] += _dot_tn(oh, extra[...].reshape(SEG_WIN, d))
                return c2
            return lax.fori_loop(1, _extra_windows(nchunk_ref, i, e), one, c)
        lax.fori_loop(0, N_EXPERTS, more_windows, 0)

    z = ALPHA * x1_ref[...] + g2_ref[0] * acc_ref[...]
    o_ref[...] = _layer_norm(z, ln_g_ref[...], ln_b_ref[...])


def _combine(base, seg_start, seg_chunks, seg_units, seg_over, hx, pos, x1, g2, w_s_gate, w_s_up,
             w_s_down, ln_g, ln_b, ys):
    t, d = x1.shape
    tpb = pos.shape[2] // TOK_TILE
    row = lambda i, *_: (i, 0)
    c2 = lambda i, *_: (0, 0)
    return pl.pallas_call(
        _combine_kernel,
        grid_spec=pltpu.PrefetchScalarGridSpec(
            num_scalar_prefetch=5, grid=(t // TOK_TILE,),
            in_specs=[pl.BlockSpec((TOK_TILE, hx.shape[1]), row),
                      pl.BlockSpec((1, N_EXPERTS, TOK_TILE), lambda i, *_: (i // tpb, 0, i % tpb)),
                      pl.BlockSpec((TOK_TILE, d), row),
                      pl.BlockSpec((1, 1, d), lambda i, *_: (i // tpb, 0, 0)),
                      pl.BlockSpec(w_s_gate.shape, c2), pl.BlockSpec(w_s_up.shape, c2),
                      pl.BlockSpec(w_s_down.shape, c2),
                      pl.BlockSpec(ln_g.shape, c2), pl.BlockSpec(ln_b.shape, c2),
                      pl.BlockSpec(memory_space=pl.ANY)],
            out_specs=pl.BlockSpec((TOK_TILE, d), row),
            scratch_shapes=[pltpu.VMEM((2, N_EXPERTS * WIN_CHUNKS, SEG_ROWS, d), BF16),
                            pltpu.VMEM((WIN_CHUNKS, SEG_ROWS, d), BF16),
                            pltpu.VMEM((TOK_TILE, d), F32),
                            pltpu.SemaphoreType.DMA((3,))]),
        out_shape=jax.ShapeDtypeStruct((t, d), F32),
        compiler_params=pltpu.CompilerParams(dimension_semantics=("arbitrary",),
                                             vmem_limit_bytes=VMEM_LIMIT),
        name="combine",
    )(base, seg_start, seg_chunks, seg_units, seg_over, hx, pos, x1, g2, w_s_gate, w_s_up, w_s_down,
      ln_g, ln_b, ys)


def _moe(hx, pos, seg, counts, x1, g2, w_e_gate, w_e_up, w_e_down, w_s_gate, w_s_up, w_s_down,
         ln_g, ln_b):
    b, s, d = x1.shape
    t = b * s
    n_seg = (t // TOK_TILE) * N_EXPERTS
    max_rows = t * TOP_K + n_seg * (SEG_ROWS - 1) + N_EXPERTS * (SEG_WIN + MOE_TM)
    n_tiles = -(-max_rows // MOE_TM)
    total = jnp.round(counts[:, 0]).astype(jnp.int32)
    pcnt = (total + SEG_WIN + MOE_TM - 1) // MOE_TM * MOE_TM
    ends = jnp.cumsum(pcnt)
    base = ends - pcnt
    n_used = (ends[-1] // MOE_TM).reshape(1)
    tile_ids = jnp.arange(n_tiles, dtype=jnp.int32)
    tile_expert = jnp.sum((ends[None, :] <= tile_ids[:, None] * MOE_TM).astype(jnp.int32), axis=1)
    tile_expert = jnp.minimum(tile_expert, N_EXPERTS - 1)
    tile_expert = jnp.where(tile_ids < n_used, tile_expert, tile_expert[n_used[0] - 1])
    base, total, pcnt = base // SEG_ROWS, total // SEG_ROWS, pcnt // SEG_ROWS
    seg_start = seg[:, 0, :, 0].reshape(n_seg) // SEG_ROWS
    seg_chunks = seg[:, 1, :, 0]
    seg_over = (jnp.max(seg_chunks, axis=1) * SEG_ROWS > SEG_WIN).astype(jnp.int32)
    seg_units = jnp.sum(jnp.where(seg_chunks <= HALF_CHUNKS, 1, 2), axis=1).astype(jnp.int32)
    seg_chunks = seg_chunks.reshape(n_seg)

    hxf = hx.reshape(t, hx.shape[2])
    xs = _dispatch(base, seg_start, seg_chunks, seg_units, seg_over, total, pcnt, n_used, hxf, pos,
                   n_tiles * MOE_TM)
    ys = _experts(tile_expert, n_used, xs, w_e_gate, w_e_up, w_e_down)
    out = _combine(base, seg_start, seg_chunks, seg_units, seg_over, hxf, pos, x1.reshape(t, d), g2,
                   w_s_gate, w_s_up, w_s_down, ln_g, ln_b, ys)
    return out.reshape(b, s, d)


def _rope_tables(seq):
    t = jnp.arange(seq)
    pos = jnp.stack([t // GRID_W, t % GRID_W], axis=-1).astype(F32)
    inv_freq = ROPE_THETA ** (-jnp.arange(ROPE_FREQS, dtype=F32) / ROPE_FREQS)
    ang = pos[:, :, None] * inv_freq
    cos, sin = jnp.cos(ang), jnp.sin(ang)
    zero = jnp.zeros_like(sin)
    cos_r = jnp.stack([cos, cos], axis=2).reshape(seq, QK_ROPE)
    sin_lo = jnp.stack([-sin, zero], axis=2).reshape(seq, QK_ROPE)
    sin_hi = jnp.stack([zero, sin], axis=2).reshape(seq, QK_ROPE)
    pads = ((0, 0), (QK_NOPE, HEAD_PAD - QK_NOPE - QK_ROPE))
    lane_tabs = (jnp.pad(cos_r, pads, constant_values=1.0), jnp.pad(sin_lo, pads),
                 jnp.pad(sin_hi, pads))
    row_tabs = (cos.reshape(seq, 2 * ROPE_FREQS).T, sin.reshape(seq, 2 * ROPE_FREQS).T)
    return lane_tabs, row_tabs


def _pad_heads(w, width, padded):
    k = w.shape[0]
    w = jnp.pad(w.reshape(k, N_HEADS, width), ((0, 0), (0, 0), (0, padded - width)))
    return w.reshape(k, N_HEADS * padded)


def kernel(x, c, ctx, c_ctx, w_ada, b_ada, w_in, q_norm_g, w_uq, kv_norm_g, w_ukv, w_pool, pool_scale, w_out, ln1_g, ln1_b, w_router, router_bias, w_e_gate, w_e_up, w_e_down, w_s_gate, w_s_up, w_s_down, ln2_g, ln2_b):
    assert w_ada.shape[0] == 1, "single-layer block"
    b, s, d = x.shape

    cvec = jnp.concatenate([c, c_ctx[None], jnp.zeros((SUBLANES - b - 1, d), F32)], axis=0)
    mod = _ada(cvec, w_ada[0], b_ada)
    sh1, sc1, g1, sh2, sc2, g2 = [mod[:b, k * d:(k + 1) * d][:, None, :] for k in range(6)]
    sh1c, sc1c = [jnp.broadcast_to(mod[b, k * d:(k + 1) * d], (b, 1, d)) for k in range(2)]

    wi = w_in[0]
    kr_cols = jnp.pad(wi[:, Q_LORA + KV_LORA:Q_LORA + KV_LORA + QK_ROPE],
                      ((0, 0), (QK_NOPE, HEAD_PAD - QK_NOPE - QK_ROPE)))
    w_in_r = jnp.concatenate([wi[:, :Q_LORA + KV_LORA], wi[:, Q_LORA + KV_LORA + QK_ROPE:], kr_cols],
                             axis=1).astype(BF16)
    w_uq_t = _pad_heads(w_uq[0], QK_NOPE + QK_ROPE, HEAD_PAD).T.astype(BF16)
    wkv = w_ukv[0].reshape(KV_LORA, N_HEADS, QK_NOPE + V_HEAD)
    w_uk_p = _pad_heads(wkv[:, :, :QK_NOPE].reshape(KV_LORA, -1), QK_NOPE, HEAD_PAD).astype(BF16)
    w_uv_t = _pad_heads(wkv[:, :, QK_NOPE:].reshape(KV_LORA, -1), V_HEAD, V_ROWS).T.astype(BF16)
    tables = _rope_tables(s)

    q_t, k, v_t, u = _proj(x, sc1, sh1, tables, w_in_r, q_norm_g, w_uq_t, kv_norm_g, w_uk_p,
                           w_uv_t, PROJ_TILE)
    kc, vc_t = _proj(ctx, sc1c, sh1c, None, w_in_r, None, None, kv_norm_g, w_uk_p, w_uv_t,
                     ctx.shape[1])
    attn = _attention(q_t, kc, vc_t, k, v_t)

    x1, hx, pos, seg, counts = _mix(attn, u, x, g1, sc2, sh2, w_pool[0].astype(BF16),
                                    pool_scale, w_out[0].astype(BF16), ln1_g, ln1_b,
                                    w_router[0].T, router_bias[0][:, None])
    return _moe(hx, pos, seg, counts, x1, g2, w_e_gate[0], w_e_up[0], w_e_down[0],
                w_s_gate[0].astype(BF16), w_s_up[0].astype(BF16), w_s_down[0].astype(BF16),
                ln2_g, ln2_b)
```

```python
import functools
import math

import jax
import jax.numpy as jnp
from jax import lax
from jax.experimental import pallas as pl
from jax.experimental.pallas import tpu as pltpu

F32 = jnp.float32
BF16 = jnp.bfloat16

D_MODEL = 1024
GRID_W = 64
N_HEADS = 8
Q_LORA = 512
KV_LORA = 256
QK_NOPE = 64
QK_ROPE = 32
V_HEAD = 64
ROPE_FREQS = QK_ROPE // 4
ROPE_THETA = 10000.0
ATTN_SCALE = 1.0 / math.sqrt(QK_NOPE + QK_ROPE)
LOG2_E = math.log2(math.e)
POOL_GROUPS = 4
POOL_WINDOWS = (2, 4, 8, 16)
POOL_WIDTH = 512
POOL_GC = POOL_WIDTH // POOL_GROUPS
POOL_HALO = 8
N_EXPERTS = 64
N_EXPERT_GROUPS = 8
GROUP_SIZE = N_EXPERTS // N_EXPERT_GROUPS
TOPK_GROUPS = 4
TOP_K = 8
D_EXPERT = 256
ROUTED_SCALE = 2.5
LN_EPS = 1e-5
RMS_EPS = 1e-6
ALPHA = 2.0 ** 0.25

LANES = 128
SUBLANES = 8
HEAD_PAD = LANES
V_ROWS = 80
ONES_ROW = V_HEAD
IN_PAD = Q_LORA + KV_LORA + POOL_WIDTH + LANES

PROJ_TILE = 512
ATTN_TQ = 512
ATTN_TK = PROJ_TILE
ATTN_SUB = 256
ATTN_AHEAD = 2
ATTN_UNROLL = 16
TOK_TILE = 256
MIX_TILE = 2 * TOK_TILE
MOE_TM = 1024
SEG_ROWS = 16
SEG_WIN = 64
WIN_CHUNKS = SEG_WIN // SEG_ROWS
HALF_CHUNKS = WIN_CHUNKS // 2
WIN_GROUP = 16
HX_WIDTH = D_MODEL + 2 * N_EXPERTS
VMEM_LIMIT = 48 * 1024 * 1024
NEG_BIG = -1e30


def _silu(v):
    return v * jax.nn.sigmoid(v)


def _layer_norm(z, g, b):
    mu = jnp.mean(z, axis=-1, keepdims=True)
    zc = z - mu
    var = jnp.mean(zc * zc, axis=-1, keepdims=True)
    return zc * lax.rsqrt(var + LN_EPS) * g + b


def _rms_norm(v, g):
    return v * lax.rsqrt(jnp.mean(v * v, axis=-1, keepdims=True) + RMS_EPS) * g


def _dot_nt(a, b):
    return lax.dot_general(a, b, (((1,), (1,)), ((), ())), preferred_element_type=F32)


def _ada_kernel(c_ref, w_ref, b_ref, o_ref):
    cv = _silu(c_ref[...])
    o_ref[...] = jnp.dot(cv, w_ref[...], preferred_element_type=F32,
                         precision=lax.Precision.HIGHEST) + b_ref[...]


def _ada(cvec, w_ada, b_ada):
    rows, d = cvec.shape
    n = w_ada.shape[1]
    tn = 1024
    return pl.pallas_call(
        _ada_kernel,
        grid=(n // tn,),
        in_specs=[pl.BlockSpec((rows, d), lambda j: (0, 0)),
                  pl.BlockSpec((d, tn), lambda j: (0, j)),
                  pl.BlockSpec((1, tn), lambda j: (0, j))],
        out_specs=pl.BlockSpec((rows, tn), lambda j: (0, j)),
        out_shape=jax.ShapeDtypeStruct((rows, n), F32),
        compiler_params=pltpu.CompilerParams(dimension_semantics=("arbitrary",),
                                             vmem_limit_bytes=VMEM_LIMIT),
        name="ada",
    )(cvec, w_ada, b_ada)


def _rope_lanes(v, cos, sin_lo, sin_hi):
    return v * cos + pltpu.roll(v, LANES - 8, axis=1) * sin_lo + pltpu.roll(v, 8, axis=1) * sin_hi


def _proj_kernel(*refs, with_q):
    if with_q:
        (x_ref, sc_ref, sh_ref, cos_ref, slo_ref, shi_ref, cos_t_ref, sin_t_ref, win_ref, qg_ref,
         wuq_ref, kvg_ref, wuk_ref, wuv_ref, q_ref, k_ref, v_ref, u_ref) = refs
    else:
        (x_ref, sc_ref, sh_ref, win_ref, kvg_ref, wuk_ref, wuv_ref, k_ref, v_ref) = refs
    h = (x_ref[0] * (1.0 + sc_ref[0]) + sh_ref[0]).astype(BF16)
    p = jnp.dot(h, win_ref[...], preferred_element_type=F32)
    tile = p.shape[0]
    kr = p[:, IN_PAD - LANES:]
    kvn = _rms_norm(p[:, Q_LORA:Q_LORA + KV_LORA], kvg_ref[...]).astype(BF16)
    kfull = jnp.dot(kvn, wuk_ref[...], preferred_element_type=F32)
    v_t = _dot_nt(wuv_ref[...], kvn)
    row = lax.broadcasted_iota(jnp.int32, (N_HEADS * V_ROWS, 1), 0)
    v_t = v_t + (row % V_ROWS == ONES_ROW).astype(F32)
    if with_q:
        kr = _rope_lanes(kr, cos_ref[...], slo_ref[...], shi_ref[...])
        u_ref[0] = p[:, Q_LORA + KV_LORA:Q_LORA + KV_LORA + POOL_WIDTH]
        qn = _rms_norm(p[:, :Q_LORA], qg_ref[...]).astype(BF16)
        q_t = _dot_nt(wuq_ref[...], qn) * (ATTN_SCALE * LOG2_E)
        cos_t, sin_t = cos_t_ref[...], sin_t_ref[...]
    for hd in range(N_HEADS):
        k_ref[0, hd] = (kfull[:, hd * HEAD_PAD:(hd + 1) * HEAD_PAD] + kr).astype(BF16)
        v_ref[0, hd, 0] = v_t[hd * V_ROWS:(hd + 1) * V_ROWS].astype(BF16)
        if with_q:
            base = hd * HEAD_PAD
            q_ref[0, hd, 0:QK_NOPE, :] = q_t[base:base + QK_NOPE].astype(BF16)
            rope = []
            for ax in range(2):
                lo = q_t[base + QK_NOPE + 16 * ax:base + QK_NOPE + 16 * ax + 8]
                hi = q_t[base + QK_NOPE + 16 * ax + 8:base + QK_NOPE + 16 * ax + 16]
                cs, sn = cos_t[8 * ax:8 * ax + 8], sin_t[8 * ax:8 * ax + 8]
                rope += [lo * cs - hi * sn, hi * cs + lo * sn]
            rope.append(jnp.zeros((HEAD_PAD - QK_NOPE - QK_ROPE, tile), F32))
            q_ref[0, hd, QK_NOPE:, :] = jnp.concatenate(rope, axis=0).astype(BF16)


def _proj(x, sc, sh, tables, w_in_r, q_g, w_uq_t, kv_g, w_uk_p, w_uv_t, tile):
    b, s, d = x.shape
    with_q = tables is not None
    grid = (b, s // tile)
    row = lambda bi, i: (bi, i, 0)
    vec = lambda bi, i: (bi, 0, 0)
    const2 = lambda bi, i: (0, 0)
    k_out = pl.BlockSpec((1, N_HEADS, tile, HEAD_PAD), lambda bi, i: (bi, 0, i, 0))
    k_shape = jax.ShapeDtypeStruct((b, N_HEADS, s, HEAD_PAD), BF16)
    v_out = pl.BlockSpec((1, N_HEADS, 1, V_ROWS, tile), lambda bi, i: (bi, 0, i, 0, 0))
    v_shape = jax.ShapeDtypeStruct((b, N_HEADS, s // tile, V_ROWS, tile), BF16)
    in_specs = [pl.BlockSpec((1, tile, d), row),
                pl.BlockSpec((1, 1, d), vec), pl.BlockSpec((1, 1, d), vec)]
    args = [x, sc, sh]
    if with_q:
        lane_tabs, row_tabs = tables
        in_specs += [pl.BlockSpec((tile, LANES), lambda bi, i: (i, 0))] * 3
        in_specs += [pl.BlockSpec((2 * ROPE_FREQS, tile), lambda bi, i: (0, i))] * 2
        args += list(lane_tabs) + list(row_tabs)
    in_specs.append(pl.BlockSpec(w_in_r.shape, const2)); args.append(w_in_r)
    if with_q:
        in_specs += [pl.BlockSpec(q_g.shape, const2), pl.BlockSpec(w_uq_t.shape, const2)]
        args += [q_g, w_uq_t]
    in_specs += [pl.BlockSpec(kv_g.shape, const2), pl.BlockSpec(w_uk_p.shape, const2),
                 pl.BlockSpec(w_uv_t.shape, const2)]
    args += [kv_g, w_uk_p, w_uv_t]
    if with_q:
        q_out = pl.BlockSpec((1, N_HEADS, HEAD_PAD, tile), lambda bi, i: (bi, 0, 0, i))
        q_shape = jax.ShapeDtypeStruct((b, N_HEADS, HEAD_PAD, s), BF16)
        out_specs = [q_out, k_out, v_out, pl.BlockSpec((1, tile, POOL_WIDTH), row)]
        out_shape = [q_shape, k_shape, v_shape, jax.ShapeDtypeStruct((b, s, POOL_WIDTH), F32)]
    else:
        out_specs = [k_out, v_out]
        out_shape = [k_shape, v_shape]
    return pl.pallas_call(
        functools.partial(_proj_kernel, with_q=with_q),
        grid=grid, in_specs=in_specs, out_specs=out_specs, out_shape=out_shape,
        compiler_params=pltpu.CompilerParams(dimension_semantics=("arbitrary", "arbitrary"),
                                             vmem_limit_bytes=VMEM_LIMIT),
        name="proj" if with_q else "proj_ctx",
    )(*args)


def _attn_kernel(q_ref, kc_ref, vc_ref, k_ref, v_ref, o_ref, *, n_kblk, tk):
    tq = q_ref.shape[3]
    qs = [q_ref[0, hh] for hh in range(2)]

    def scores(hh, kb):
        return jnp.dot(kb, qs[hh], preferred_element_type=F32)

    def update(s_t, vb_t, m, acc):
        m_new = jnp.maximum(m, jnp.max(s_t, axis=0, keepdims=True))
        p_t = jnp.exp2(s_t - m_new).astype(BF16)
        acc = jnp.exp2(m - m_new) * acc + jnp.dot(vb_t, p_t, preferred_element_type=F32)
        return m_new, acc

    def run_items(items, state):
        pending = [scores(hh, kb()) for hh, kb, _ in items[:ATTN_AHEAD]]
        for j, (hh, _, vb) in enumerate(items):
            if j + ATTN_AHEAD < len(items):
                nh, nkb, _ = items[j + ATTN_AHEAD]
                pending.append(scores(nh, nkb()))
            state[hh] = update(pending.pop(0), vb(), *state[hh])
        return state

    def block_items(blk, off):
        out = []
        for sub in range(tk // ATTN_SUB):
            for hh in range(2):
                lo = sub * ATTN_SUB
                out.append((hh,
                            lambda hh=hh, lo=lo: k_ref[0, hh, pl.ds(off + lo, ATTN_SUB), :],
                            lambda hh=hh, lo=lo: v_ref[0, hh, blk, :, lo:lo + ATTN_SUB]))
        return out

    ctx_items = [(hh, lambda hh=hh: kc_ref[0, hh], lambda hh=hh: vc_ref[0, hh, 0])
                 for hh in range(2)]
    state = [(jnp.full((1, tq), NEG_BIG, F32), jnp.zeros((V_ROWS, tq), F32)) for _ in range(2)]
    n_iter = n_kblk // ATTN_UNROLL
    if n_iter == 1:
        items = ctx_items
        for blk in range(n_kblk):
            items = items + block_items(blk, blk * tk)
        state = run_items(items, state)
    else:
        state = run_items(ctx_items, state)

        def body(i, carry):
            items = []
            for r in range(ATTN_UNROLL):
                blk = i * ATTN_UNROLL + r
                items += block_items(blk, pl.multiple_of(blk * tk, tk))
            st = run_items(items, [(carry[0], carry[1]), (carry[2], carry[3])])
            return st[0] + st[1]

        carry = lax.fori_loop(0, n_iter, body, state[0] + state[1])
        state = [(carry[0], carry[1]), (carry[2], carry[3])]
    carry = state[0] + state[1]
    outs = [carry[2 * hh + 1][:V_HEAD] / carry[2 * hh + 1][ONES_ROW:ONES_ROW + 1] for hh in range(2)]
    o_ref[0] = jnp.concatenate(outs, axis=0).T.astype(o_ref.dtype)


def _attention(q_t, kc, vc_t, k, v_t):
    b, nh, dp, s = q_t.shape
    c = kc.shape[2]
    tq, tk = ATTN_TQ, ATTN_TK
    n_kblk = s // tk
    kern = functools.partial(_attn_kernel, n_kblk=n_kblk, tk=tk)
    return pl.pallas_call(
        kern,
        grid=(b, nh // 2, s // tq),
        in_specs=[pl.BlockSpec((1, 2, dp, tq), lambda bi, hp, qi: (bi, hp, 0, qi)),
                  pl.BlockSpec((1, 2, c, dp), lambda bi, hp, qi: (bi, hp, 0, 0)),
                  pl.BlockSpec((1, 2, 1, V_ROWS, c), lambda bi, hp, qi: (bi, hp, 0, 0, 0)),
                  pl.BlockSpec((1, 2, s, dp), lambda bi, hp, qi: (bi, hp, 0, 0)),
                  pl.BlockSpec((1, 2, n_kblk, V_ROWS, tk), lambda bi, hp, qi: (bi, hp, 0, 0, 0))],
        out_specs=pl.BlockSpec((1, tq, 2 * V_HEAD), lambda bi, hp, qi: (bi, qi, hp)),
        out_shape=jax.ShapeDtypeStruct((b, s, nh * V_HEAD), BF16),
        compiler_params=pltpu.CompilerParams(
            dimension_semantics=("arbitrary", "arbitrary", "arbitrary"),
            vmem_limit_bytes=VMEM_LIMIT),
        name="attn",
    )(q_t, kc, vc_t, k, v_t)


def _route(logits_t, bias_t):
    e, t = logits_t.shape
    scores = jax.nn.sigmoid(logits_t)
    biased = scores + bias_t
    neg_inf = F32(-jnp.inf)
    gscore = []
    for g in range(N_EXPERT_GROUPS):
        v = biased[g * GROUP_SIZE:(g + 1) * GROUP_SIZE]
        m1 = jnp.max(v, axis=0, keepdims=True)
        at_max = v == m1
        n_max = jnp.sum(at_max.astype(F32), axis=0, keepdims=True)
        m2 = jnp.max(jnp.where(at_max, neg_inf, v), axis=0, keepdims=True)
        gscore.append(m1 + jnp.where(n_max >= 2.0, m1, m2))
    masked = []
    for g in range(N_EXPERT_GROUPS):
        rank = jnp.zeros((1, t), F32)
        for o in range(N_EXPERT_GROUPS):
            if o == g:
                continue
            beats = (gscore[o] >= gscore[g]) if o < g else (gscore[o] > gscore[g])
            rank = rank + beats.astype(F32)
        keep = rank < float(TOPK_GROUPS)
        masked.append(jnp.where(keep, biased[g * GROUP_SIZE:(g + 1) * GROUP_SIZE], neg_inf))
    work = jnp.concatenate(masked, axis=0)
    rows = lax.broadcasted_iota(jnp.int32, (e, t), 0)
    sel = jnp.zeros((e, t), F32)
    for _ in range(TOP_K):
        m = jnp.max(work, axis=0, keepdims=True)
        first = jnp.min(jnp.where(work == m, rows, e), axis=0, keepdims=True)
        pick = rows == first
        sel = jnp.where(pick, 1.0, sel)
        work = jnp.where(pick, neg_inf, work)
    w = sel * scores
    gates = w / jnp.sum(w, axis=0, keepdims=True) * ROUTED_SCALE
    return sel, gates


def _mix_kernel(attn_ref, u_ref, up_ref, un_ref, x_ref, g1_ref, sc2_ref, sh2_ref, wpool_ref,
                pscale_ref, wout_ref, ln_g_ref, ln_b_ref, wr_ref, rb_ref, tri_ref, ones_ref,
                x1_ref, hx_ref, pos_ref, seg_ref, cnt_ref, uext_ref, *, seq):
    i = pl.program_id(1)

    @pl.when((pl.program_id(0) == 0) & (i == 0))
    def _():
        cnt_ref[...] = jnp.zeros_like(cnt_ref)

    tile = u_ref.shape[1]
    u = u_ref[0]
    uext_ref[0:POOL_HALO] = jnp.where(i == 0, 0.0, up_ref[0])
    uext_ref[POOL_HALO:POOL_HALO + tile] = u
    uext_ref[POOL_HALO + tile:] = jnp.where(i == pl.num_programs(1) - 1, 0.0, un_ref[0])
    t = i * tile + lax.broadcasted_iota(jnp.int32, (tile, POOL_GC), 0)
    pooled = []
    for g, w in enumerate(POOL_WINDOWS):
        lanes = slice(g * POOL_GC, (g + 1) * POOL_GC)
        tot = uext_ref[POOL_HALO - w // 2:POOL_HALO - w // 2 + tile, lanes]
        for dlt in range(1, w):
            start = POOL_HALO - w // 2 + dlt
            tot = tot + uext_ref[start:start + tile, lanes]
        cnt = (jnp.minimum(t - w // 2 + w, seq) - jnp.maximum(t - w // 2, 0)).astype(F32)
        pg = (tot / cnt - u[:, lanes]).astype(BF16)
        po = jnp.dot(pg, wpool_ref[g], preferred_element_type=F32) * pscale_ref[:, lanes]
        pooled.append(po.astype(BF16))
    mixed = jnp.concatenate([attn_ref[0]] + pooled, axis=1)
    y = jnp.dot(mixed, wout_ref[...], preferred_element_type=F32)
    x1 = _layer_norm(ALPHA * x_ref[0] + g1_ref[0] * y, ln_g_ref[...], ln_b_ref[...])
    x1_ref[0] = x1
    h2 = x1 * (1.0 + sc2_ref[0]) + sh2_ref[0]
    h_hi = h2.astype(BF16)
    h_lo = (h2 - h_hi.astype(F32)).astype(BF16)
    w_r = wr_ref[...]
    w_hi = w_r.astype(BF16)
    w_lo = (w_r - w_hi.astype(F32)).astype(BF16)
    logits_t = _dot_nt(w_hi, h_hi) + (_dot_nt(w_hi, h_lo) + _dot_nt(w_lo, h_hi))
    sel, gates_t = _route(logits_t, rb_ref[...])
    g_hi = gates_t.astype(BF16)
    g_lo = (gates_t - g_hi.astype(F32)).astype(BF16)
    g_tok = jnp.concatenate([g_hi.astype(F32), g_lo.astype(F32)], axis=0).T
    hx_ref[0] = jnp.concatenate([h2.astype(BF16), g_tok.astype(BF16)], axis=1)
    start = cnt_ref[...]
    for k in range(tile // TOK_TILE):
        cols = slice(k * TOK_TILE, (k + 1) * TOK_TILE)
        sel_k = sel[:, cols]
        sel_b = sel_k.astype(BF16)
        pos_t = jnp.dot(sel_b, tri_ref[...], preferred_element_type=F32) + start[:, 0:1]
        pos_ref[0, :, cols] = jnp.where(sel_k > 0.0, pos_t, -1.0).astype(jnp.int32)
        n_tok = jnp.dot(sel_b, ones_ref[...], preferred_element_type=F32)
        n_chunk = jnp.floor((n_tok + (SEG_ROWS - 1)) * (1.0 / SEG_ROWS))
        seg_ref[k, 0] = start.astype(jnp.int32)
        seg_ref[k, 1] = n_chunk.astype(jnp.int32)
        start = start + n_chunk * SEG_ROWS
    cnt_ref[...] = start


def _mix(attn, u, x, g1, sc2, sh2, w_pool, pool_scale, w_out, ln_g, ln_b, w_r_t, rb_t):
    b, s, d = x.shape
    tile = MIX_TILE
    hb = tile // POOL_HALO
    row = lambda bi, i: (bi, i, 0)
    vec = lambda bi, i: (bi, 0, 0)
    c2 = lambda bi, i: (0, 0)
    lane_row = lambda bi, i: (bi, 0, i)
    tri = (lax.broadcasted_iota(jnp.int32, (TOK_TILE, TOK_TILE), 0)
           < lax.broadcasted_iota(jnp.int32, (TOK_TILE, TOK_TILE), 1)).astype(BF16)
    ones = jnp.ones((TOK_TILE, LANES), BF16)
    seg_per_step = tile // TOK_TILE
    return pl.pallas_call(
        functools.partial(_mix_kernel, seq=s),
        grid=(b, s // tile),
        in_specs=[pl.BlockSpec((1, tile, POOL_WIDTH), row),
                  pl.BlockSpec((1, tile, POOL_WIDTH), row),
                  pl.BlockSpec((1, POOL_HALO, POOL_WIDTH),
                               lambda bi, i: (bi, jnp.maximum(i * hb - 1, 0), 0)),
                  pl.BlockSpec((1, POOL_HALO, POOL_WIDTH),
                               lambda bi, i: (bi, jnp.minimum((i + 1) * hb, s // POOL_HALO - 1), 0)),
                  pl.BlockSpec((1, tile, d), row),
                  pl.BlockSpec((1, 1, d), vec), pl.BlockSpec((1, 1, d), vec),
                  pl.BlockSpec((1, 1, d), vec),
                  pl.BlockSpec(w_pool.shape, lambda bi, i: (0, 0, 0)),
                  pl.BlockSpec(pool_scale.shape, c2),
                  pl.BlockSpec(w_out.shape, c2),
                  pl.BlockSpec(ln_g.shape, c2), pl.BlockSpec(ln_b.shape, c2),
                  pl.BlockSpec(w_r_t.shape, c2), pl.BlockSpec(rb_t.shape, c2),
                  pl.BlockSpec(tri.shape, c2), pl.BlockSpec(ones.shape, c2)],
        out_specs=[pl.BlockSpec((1, tile, d), row),
                   pl.BlockSpec((1, tile, HX_WIDTH), row),
                   pl.BlockSpec((1, N_EXPERTS, tile), lane_row),
                   pl.BlockSpec((seg_per_step, 2, N_EXPERTS, LANES),
                                lambda bi, i: (bi * (s // tile) + i, 0, 0, 0)),
                   pl.BlockSpec((N_EXPERTS, LANES), c2)],
        out_shape=[jax.ShapeDtypeStruct((b, s, d), F32),
                   jax.ShapeDtypeStruct((b, s, HX_WIDTH), BF16),
                   jax.ShapeDtypeStruct((b, N_EXPERTS, s), jnp.int32),
                   jax.ShapeDtypeStruct((b * (s // TOK_TILE), 2, N_EXPERTS, LANES), jnp.int32),
                   jax.ShapeDtypeStruct((N_EXPERTS, LANES), F32)],
        scratch_shapes=[pltpu.VMEM((tile + 2 * POOL_HALO, POOL_WIDTH), F32)],
        compiler_params=pltpu.CompilerParams(dimension_semantics=("arbitrary", "arbitrary"),
                                             vmem_limit_bytes=VMEM_LIMIT),
        name="mix",
    )(attn, u, u, u, x, g1, sc2, sh2, w_pool, pool_scale, w_out, ln_g, ln_b, w_r_t, rb_t,
      tri, ones)


def _one_hot(pos_row, start, window, n_tok):
    rows = lax.broadcasted_iota(jnp.int32, (SEG_WIN, n_tok), 0)
    hit = rows == (pos_row - (start * SEG_ROWS + window * SEG_WIN))
    return jnp.where(hit, 1.0, 0.0).astype(BF16)


def _window_chunks(base_ref, start_ref, step, e, window, n_chunks=WIN_CHUNKS):
    return pl.ds(base_ref[e] + start_ref[step * N_EXPERTS + e] + window * WIN_CHUNKS, n_chunks)


def _start_window(seg_chunks, make_copy):
    @pl.when(seg_chunks <= HALF_CHUNKS)
    def _():
        make_copy(HALF_CHUNKS).start()

    @pl.when(seg_chunks > HALF_CHUNKS)
    def _():
        make_copy(WIN_CHUNKS).start()


def _wait_half_windows(make_copy, units):
    make_copy(N_EXPERTS).wait()
    rest = units - N_EXPERTS
    for bit in range(N_EXPERTS.bit_length()):
        @pl.when((rest >> bit) & 1 == 1)
        def _():
            make_copy(1 << bit).wait()


def _extra_windows(nchunk_ref, step, e):
    rows = nchunk_ref[step * N_EXPERTS + e] * SEG_ROWS
    return jnp.maximum((rows + SEG_WIN - 1) // SEG_WIN, 1)


def _dispatch_kernel(base_ref, start_ref, nchunk_ref, units_ref, over_ref, total_ref, pcnt_ref,
                     nu_ref, hx_ref, pos_ref, xs_hbm, stage, extra, zero_ref, sem):
    i = pl.program_id(0)
    tile_chunks = MOE_TM // SEG_ROWS
    n_tiles = xs_hbm.shape[0] // tile_chunks
    n_tok, width = hx_ref.shape

    def zero_copy(chunk, n_chunks):
        return pltpu.make_async_copy(zero_ref.at[pl.ds(0, n_chunks)],
                                     xs_hbm.at[pl.ds(chunk, n_chunks)], sem.at[1])

    @pl.when(i == 0)
    def _():
        zero_ref[...] = jnp.zeros_like(zero_ref)
        for wait in (False, True):
            def per_tail(j, c):
                cp = zero_copy(j * tile_chunks, tile_chunks)
                cp.wait() if wait else cp.start()
                return c
            lax.fori_loop(nu_ref[0], n_tiles, per_tail, 0)

            def per_expert(e, c):
                n = pcnt_ref[e] - total_ref[e]
                first = base_ref[e] + total_ref[e]
                for bit in range(tile_chunks.bit_length()):
                    size = 1 << bit

                    @pl.when((n >> bit) & 1 == 1)
                    def _():
                        cp = zero_copy(first + (n & (size - 1)), size)
                        cp.wait() if wait else cp.start()
                return c
            lax.fori_loop(0, N_EXPERTS, per_expert, 0)

    slot = i % 2
    hx = hx_ref[...]
    group_chunks = WIN_GROUP * WIN_CHUNKS
    for g in range(N_EXPERTS // WIN_GROUP):
        oh = jnp.concatenate(
            [_one_hot(pos_ref[0, e:e + 1, :], start_ref[i * N_EXPERTS + e], 0, n_tok)
             for e in range(g * WIN_GROUP, (g + 1) * WIN_GROUP)], axis=0)
        rows = jnp.dot(oh, hx, preferred_element_type=F32).astype(BF16)
        stage[slot, pl.ds(g * group_chunks, group_chunks)] = rows.reshape(group_chunks, SEG_ROWS,
                                                                         width)

    def wait_windows(step, which):
        _wait_half_windows(
            lambda k: pltpu.make_async_copy(stage.at[which, pl.ds(0, k * HALF_CHUNKS)],
                                            xs_hbm.at[pl.ds(0, k * HALF_CHUNKS)], sem.at[0]),
            units_ref[step])

    @pl.when(i > 0)
    def _():
        wait_windows(i - 1, 1 - slot)

    for e in range(N_EXPERTS):
        _start_window(nchunk_ref[i * N_EXPERTS + e],
                      lambda n: pltpu.make_async_copy(
                          stage.at[slot, pl.ds(e * WIN_CHUNKS, n)],
                          xs_hbm.at[_window_chunks(base_ref, start_ref, i, e, 0, n)], sem.at[0]))

    @pl.when(i == pl.num_programs(0) - 1)
    def _():
        wait_windows(i, slot)

    @pl.when(over_ref[i] > 0)
    def _():
        def more_windows(e, c):
            def one(window, c2):
                oh = _one_hot(pos_ref[0, pl.ds(e, 1), :], start_ref[i * N_EXPERTS + e], window,
                              n_tok)
                rows = jnp.dot(oh, hx, preferred_element_type=F32).astype(BF16)
                extra[...] = rows.reshape(WIN_CHUNKS, SEG_ROWS, width)
                cp = pltpu.make_async_copy(
                    extra, xs_hbm.at[_window_chunks(base_ref, start_ref, i, e, window)], sem.at[2])
                cp.start()
                cp.wait()
                return c2
            return lax.fori_loop(1, _extra_windows(nchunk_ref, i, e), one, c)
        lax.fori_loop(0, N_EXPERTS, more_windows, 0)


def _dispatch(base, seg_start, seg_chunks, seg_units, seg_over, total, pcnt, n_used, hx, pos,
              n_slots):
    t, width = hx.shape
    tpb = pos.shape[2] // TOK_TILE
    return pl.pallas_call(
        _dispatch_kernel,
        grid_spec=pltpu.PrefetchScalarGridSpec(
            num_scalar_prefetch=8, grid=(t // TOK_TILE,),
            in_specs=[pl.BlockSpec((TOK_TILE, width), lambda i, *_: (i, 0)),
                      pl.BlockSpec((1, N_EXPERTS, TOK_TILE), lambda i, *_: (i // tpb, 0, i % tpb))],
            out_specs=pl.BlockSpec(memory_space=pl.ANY),
            scratch_shapes=[pltpu.VMEM((2, N_EXPERTS * WIN_CHUNKS, SEG_ROWS, width), BF16),
                            pltpu.VMEM((WIN_CHUNKS, SEG_ROWS, width), BF16),
                            pltpu.VMEM((MOE_TM // SEG_ROWS, SEG_ROWS, width), BF16),
                            pltpu.SemaphoreType.DMA((3,))]),
        out_shape=jax.ShapeDtypeStruct((n_slots // SEG_ROWS, SEG_ROWS, width), BF16),
        compiler_params=pltpu.CompilerParams(dimension_semantics=("arbitrary",),
                                             vmem_limit_bytes=VMEM_LIMIT),
        name="dispatch",
    )(base, seg_start, seg_chunks, seg_units, seg_over, total, pcnt, n_used, hx, pos)


def _experts_kernel(te_ref, nu_ref, xs_ref, wg_ref, wu_ref, wd_ref, ys_ref, wg_b, wu_b, wd_b):
    i = pl.program_id(0)

    @pl.when(i < nu_ref[0])
    def _():
        @pl.when((i == 0) | (te_ref[i] != te_ref[jnp.maximum(i - 1, 0)]))
        def _():
            wg_b[...] = wg_ref[0].astype(BF16)
            wu_b[...] = wu_ref[0].astype(BF16)
            wd_b[...] = wd_ref[0].astype(BF16)

        half = MOE_TM // 2
        half_chunks = half // SEG_ROWS
        halves = [pl.ds(k * half_chunks, half_chunks) for k in range(2)]
        hidden = []
        for chunks in halves:
            xg = xs_ref[chunks].reshape(half, xs_ref.shape[2])
            x = xg[:, :D_MODEL]
            hidden.append((xg[:, D_MODEL:].astype(F32),
                           jnp.dot(x, wg_b[...], preferred_element_type=F32),
                           jnp.dot(x, wu_b[...], preferred_element_type=F32)))
        for chunks, (g, hg, hu) in zip(halves, hidden):
            lane = lax.broadcasted_iota(jnp.int32, g.shape, 1)
            mine = (lane == te_ref[i]) | (lane == te_ref[i] + N_EXPERTS)
            gate = jnp.sum(jnp.where(mine, g, 0.0), axis=1, keepdims=True)
            a = (_silu(hg) * hu * gate).astype(BF16)
            y = jnp.dot(a, wd_b[...], preferred_element_type=F32).astype(ys_ref.dtype)
            ys_ref[chunks] = y.reshape(half_chunks, SEG_ROWS, ys_ref.shape[2])

    @pl.when(i >= nu_ref[0])
    def _():
        ys_ref[...] = jnp.zeros_like(ys_ref)


def _experts(tile_expert, n_used, xs, w_e_gate, w_e_up, w_e_down):
    n_chunks, _, width = xs.shape
    _, d, f = w_e_gate.shape
    tile_chunks = MOE_TM // SEG_ROWS
    slot_tile = lambda i, te, nu: (jnp.minimum(i, nu[0] - 1), 0, 0)
    expert = lambda i, te, nu: (te[i], 0, 0)
    return pl.pallas_call(
        _experts_kernel,
        grid_spec=pltpu.PrefetchScalarGridSpec(
            num_scalar_prefetch=2, grid=(n_chunks // tile_chunks,),
            in_specs=[pl.BlockSpec((tile_chunks, SEG_ROWS, width), slot_tile),
                      pl.BlockSpec((1, d, f), expert), pl.BlockSpec((1, d, f), expert),
                      pl.BlockSpec((1, f, d), expert)],
            out_specs=pl.BlockSpec((tile_chunks, SEG_ROWS, d), lambda i, te, nu: (i, 0, 0)),
            scratch_shapes=[pltpu.VMEM((d, f), BF16), pltpu.VMEM((d, f), BF16),
                            pltpu.VMEM((f, d), BF16)]),
        out_shape=jax.ShapeDtypeStruct((n_chunks, SEG_ROWS, d), BF16),
        compiler_params=pltpu.CompilerParams(dimension_semantics=("arbitrary",),
                                             vmem_limit_bytes=VMEM_LIMIT),
        name="experts",
    )(tile_expert, n_used, xs, w_e_gate, w_e_up, w_e_down)


def _dot_tn(a, b):
    return lax.dot_general(a, b, (((0,), (0,)), ((), ())), preferred_element_type=F32)


def _combine_kernel(base_ref, start_ref, nchunk_ref, units_ref, over_ref, hx_ref, pos_ref, x1_ref,
                    g2_ref, wsg_ref, wsu_ref, wsd_ref, ln_g_ref, ln_b_ref, ys_hbm, o_ref, win, extra,
                    acc_ref, sem):
    i = pl.program_id(0)
    n = pl.num_programs(0)
    n_tok = hx_ref.shape[0]

    def fetch(step, slot):
        for e in range(N_EXPERTS):
            _start_window(nchunk_ref[step * N_EXPERTS + e],
                          lambda k: pltpu.make_async_copy(
                              ys_hbm.at[_window_chunks(base_ref, start_ref, step, e, 0, k)],
                              win.at[slot, pl.ds(e * WIN_CHUNKS, k)], sem.at[slot]))

    @pl.when(i == 0)
    def _():
        win[...] = jnp.zeros_like(win)
        fetch(0, 0)

    @pl.when(i + 1 < n)
    def _():
        fetch(i + 1, (i + 1) % 2)

    slot = i % 2
    h = hx_ref[:, :D_MODEL]
    a = _silu(jnp.dot(h, wsg_ref[...], preferred_element_type=F32)) * jnp.dot(
        h, wsu_ref[...], preferred_element_type=F32)
    moe = jnp.dot(a.astype(BF16), wsd_ref[...], preferred_element_type=F32)
    _wait_half_windows(
        lambda k: pltpu.make_async_copy(ys_hbm.at[pl.ds(0, k * HALF_CHUNKS)],
                                        win.at[slot, pl.ds(0, k * HALF_CHUNKS)], sem.at[slot]),
        units_ref[i])
    d = win.shape[3]
    group_chunks = WIN_GROUP * WIN_CHUNKS
    for g in range(N_EXPERTS // WIN_GROUP):
        oh = jnp.concatenate(
            [_one_hot(pos_ref[0, e:e + 1, :], start_ref[i * N_EXPERTS + e], 0, n_tok)
             for e in range(g * WIN_GROUP, (g + 1) * WIN_GROUP)], axis=0)
        rows = win[slot, pl.ds(g * group_chunks, group_chunks)].reshape(WIN_GROUP * SEG_WIN, d)
        moe = moe + _dot_tn(oh, rows)
    acc_ref[...] = moe

    @pl.when(over_ref[i] > 0)
    def _():
        def more_windows(e, c):
            def one(window, c2):
                cp = pltpu.make_async_copy(
                    ys_hbm.at[_window_chunks(base_ref, start_ref, i, e, window)], extra, sem.at[2])
                cp.start()
                cp.wait()
                oh = _one_hot(pos_ref[0, pl.ds(e, 1), :], start_ref[i * N_EXPERTS + e], window,
                              n_tok)
                acc_ref[...] += _dot_tn(oh, extra[...].reshape(SEG_WIN, d))
                return c2
            return lax.fori_loop(1, _extra_windows(nchunk_ref, i, e), one, c)
        lax.fori_loop(0, N_EXPERTS, more_windows, 0)

    z = ALPHA * x1_ref[...] + g2_ref[0] * acc_ref[...]
    o_ref[...] = _layer_norm(z, ln_g_ref[...], ln_b_ref[...])


def _combine(base, seg_start, seg_chunks, seg_units, seg_over, hx, pos, x1, g2, w_s_gate, w_s_up,
             w_s_down, ln_g, ln_b, ys):
    t, d = x1.shape
    tpb = pos.shape[2] // TOK_TILE
    row = lambda i, *_: (i, 0)
    c2 = lambda i, *_: (0, 0)
    return pl.pallas_call(
        _combine_kernel,
        grid_spec=pltpu.PrefetchScalarGridSpec(
            num_scalar_prefetch=5, grid=(t // TOK_TILE,),
            in_specs=[pl.BlockSpec((TOK_TILE, hx.shape[1]), row),
                      pl.BlockSpec((1, N_EXPERTS, TOK_TILE), lambda i, *_: (i // tpb, 0, i % tpb)),
                      pl.BlockSpec((TOK_TILE, d), row),
                      pl.BlockSpec((1, 1, d), lambda i, *_: (i // tpb, 0, 0)),
                      pl.BlockSpec(w_s_gate.shape, c2), pl.BlockSpec(w_s_up.shape, c2),
                      pl.BlockSpec(w_s_down.shape, c2),
                      pl.BlockSpec(ln_g.shape, c2), pl.BlockSpec(ln_b.shape, c2),
                      pl.BlockSpec(memory_space=pl.ANY)],
            out_specs=pl.BlockSpec((TOK_TILE, d), row),
            scratch_shapes=[pltpu.VMEM((2, N_EXPERTS * WIN_CHUNKS, SEG_ROWS, d), BF16),
                            pltpu.VMEM((WIN_CHUNKS, SEG_ROWS, d), BF16),
                            pltpu.VMEM((TOK_TILE, d), F32),
                            pltpu.SemaphoreType.DMA((3,))]),
        out_shape=jax.ShapeDtypeStruct((t, d), F32),
        compiler_params=pltpu.CompilerParams(dimension_semantics=("arbitrary",),
                                             vmem_limit_bytes=VMEM_LIMIT),
        name="combine",
    )(base, seg_start, seg_chunks, seg_units, seg_over, hx, pos, x1, g2, w_s_gate, w_s_up, w_s_down,
      ln_g, ln_b, ys)


def _moe(hx, pos, seg, counts, x1, g2, w_e_gate, w_e_up, w_e_down, w_s_gate, w_s_up, w_s_down,
         ln_g, ln_b):
    b, s, d = x1.shape
    t = b * s
    n_seg = (t // TOK_TILE) * N_EXPERTS
    max_rows = t * TOP_K + n_seg * (SEG_ROWS - 1) + N_EXPERTS * (SEG_WIN + MOE_TM)
    n_tiles = -(-max_rows // MOE_TM)
    total = jnp.round(counts[:, 0]).astype(jnp.int32)
    pcnt = (total + SEG_WIN + MOE_TM - 1) // MOE_TM * MOE_TM
    ends = jnp.cumsum(pcnt)
    base = ends - pcnt
    n_used = (ends[-1] // MOE_TM).reshape(1)
    tile_ids = jnp.arange(n_tiles, dtype=jnp.int32)
    tile_expert = jnp.sum((ends[None, :] <= tile_ids[:, None] * MOE_TM).astype(jnp.int32), axis=1)
    tile_expert = jnp.minimum(tile_expert, N_EXPERTS - 1)
    tile_expert = jnp.where(tile_ids < n_used, tile_expert, tile_expert[n_used[0] - 1])
    base, total, pcnt = base // SEG_ROWS, total // SEG_ROWS, pcnt // SEG_ROWS
    seg_start = seg[:, 0, :, 0].reshape(n_seg) // SEG_ROWS
    seg_chunks = seg[:, 1, :, 0]
    seg_over = (jnp.max(seg_chunks, axis=1) * SEG_ROWS > SEG_WIN).astype(jnp.int32)
    seg_units = jnp.sum(jnp.where(seg_chunks <= HALF_CHUNKS, 1, 2), axis=1).astype(jnp.int32)
    seg_chunks = seg_chunks.reshape(n_seg)

    hxf = hx.reshape(t, hx.shape[2])
    xs = _dispatch(base, seg_start, seg_chunks, seg_units, seg_over, total, pcnt, n_used, hxf, pos,
                   n_tiles * MOE_TM)
    ys = _experts(tile_expert, n_used, xs, w_e_gate, w_e_up, w_e_down)
    out = _combine(base, seg_start, seg_chunks, seg_units, seg_over, hxf, pos, x1.reshape(t, d), g2,
                   w_s_gate, w_s_up, w_s_down, ln_g, ln_b, ys)
    return out.reshape(b, s, d)


def _rope_tables(seq):
    t = jnp.arange(seq)
    pos = jnp.stack([t // GRID_W, t % GRID_W], axis=-1).astype(F32)
    inv_freq = ROPE_THETA ** (-jnp.arange(ROPE_FREQS, dtype=F32) / ROPE_FREQS)
    ang = pos[:, :, None] * inv_freq
    cos, sin = jnp.cos(ang), jnp.sin(ang)
    zero = jnp.zeros_like(sin)
    cos_r = jnp.stack([cos, cos], axis=2).reshape(seq, QK_ROPE)
    sin_lo = jnp.stack([-sin, zero], axis=2).reshape(seq, QK_ROPE)
    sin_hi = jnp.stack([zero, sin], axis=2).reshape(seq, QK_ROPE)
    pads = ((0, 0), (QK_NOPE, HEAD_PAD - QK_NOPE - QK_ROPE))
    lane_tabs = (jnp.pad(cos_r, pads, constant_values=1.0), jnp.pad(sin_lo, pads),
                 jnp.pad(sin_hi, pads))
    row_tabs = (cos.reshape(seq, 2 * ROPE_FREQS).T, sin.reshape(seq, 2 * ROPE_FREQS).T)
    return lane_tabs, row_tabs


def _pad_heads(w, width, padded):
    k = w.shape[0]
    w = jnp.pad(w.reshape(k, N_HEADS, width), ((0, 0), (0, 0), (0, padded - width)))
    return w.reshape(k, N_HEADS * padded)


def kernel(x, c, ctx, c_ctx, w_ada, b_ada, w_in, q_norm_g, w_uq, kv_norm_g, w_ukv, w_pool, pool_scale, w_out, ln1_g, ln1_b, w_router, router_bias, w_e_gate, w_e_up, w_e_down, w_s_gate, w_s_up, w_s_down, ln2_g, ln2_b):
    assert w_ada.shape[0] == 1, "single-layer block"
    b, s, d = x.shape

    cvec = jnp.concatenate([c, c_ctx[None], jnp.zeros((SUBLANES - b - 1, d), F32)], axis=0)
    mod = _ada(cvec, w_ada[0], b_ada)
    sh1, sc1, g1, sh2, sc2, g2 = [mod[:b, k * d:(k + 1) * d][:, None, :] for k in range(6)]
    sh1c, sc1c = [jnp.broadcast_to(mod[b, k * d:(k + 1) * d], (b, 1, d)) for k in range(2)]

    wi = w_in[0]
    kr_cols = jnp.pad(wi[:, Q_LORA + KV_LORA:Q_LORA + KV_LORA + QK_ROPE],
                      ((0, 0), (QK_NOPE, HEAD_PAD - QK_NOPE - QK_ROPE)))
    w_in_r = jnp.concatenate([wi[:, :Q_LORA + KV_LORA], wi[:, Q_LORA + KV_LORA + QK_ROPE:], kr_cols],
                             axis=1).astype(BF16)
    w_uq_t = _pad_heads(w_uq[0], QK_NOPE + QK_ROPE, HEAD_PAD).T.astype(BF16)
    wkv = w_ukv[0].reshape(KV_LORA, N_HEADS, QK_NOPE + V_HEAD)
    w_uk_p = _pad_heads(wkv[:, :, :QK_NOPE].reshape(KV_LORA, -1), QK_NOPE, HEAD_PAD).astype(BF16)
    w_uv_t = _pad_heads(wkv[:, :, QK_NOPE:].reshape(KV_LORA, -1), V_HEAD, V_ROWS).T.astype(BF16)
    tables = _rope_tables(s)

    q_t, k, v_t, u = _proj(x, sc1, sh1, tables, w_in_r, q_norm_g, w_uq_t, kv_norm_g, w_uk_p,
                           w_uv_t, PROJ_TILE)
    kc, vc_t = _proj(ctx, sc1c, sh1c, None, w_in_r, None, None, kv_norm_g, w_uk_p, w_uv_t,
                     ctx.shape[1])
    attn = _attention(q_t, kc, vc_t, k, v_t)

    x1, hx, pos, seg, counts = _mix(attn, u, x, g1, sc2, sh2, w_pool[0].astype(BF16),
                                    pool_scale, w_out[0].astype(BF16), ln1_g, ln1_b,
                                    w_router[0].T, router_bias[0][:, None])
    return _moe(hx, pos, seg, counts, x1, g2, w_e_gate[0], w_e_up[0], w_e_down[0],
                w_s_gate[0].astype(BF16), w_s_up[0].astype(BF16), w_s_down[0].astype(BF16),
                ln2_g, ln2_b)
```
